```python
import jax, jax.numpy as jnp
from jax import lax
import numpy as np

D_MODEL = 1024
BATCH = 8
SEQ = 8192
DEPTH = 1

N_HEADS = 8
HEAD_DIM = 64
N_KV_HEADS = 2
CMP_LEN = 32
CMP_STRIDE = 16
CMP_HID = 256
SEL_BLK = 64
SEL_TOPN = 16
WINDOW = 512
QBLK = 128
CONV_WIDTH = 512
CONV_K = 3
N_EXPERTS = 32
TOP_K = 4
D_FF = 1024
SWIGLU_LIMIT = 7.0
SWIGLU_ALPHA = 1.702
MOE_CHUNK = 512
EPS = 1e-6
NEG = -1e30
POS = 1e30

ATTN_WIDTH = N_HEADS * HEAD_DIM
KV_WIDTH = N_KV_HEADS * HEAD_DIM
IN_SPLITS = (ATTN_WIDTH,
             KV_WIDTH, KV_WIDTH,
             KV_WIDTH, KV_WIDTH,
             KV_WIDTH, KV_WIDTH,
             3 * N_HEADS,
             CONV_WIDTH, CONV_WIDTH, CONV_WIDTH,
             2 * D_MODEL)
IN_COLS = int(sum(IN_SPLITS))
IN_OFFSETS = tuple(int(o) for o in np.cumsum(IN_SPLITS)[:-1])

kernel_name = 'hybrid_nsa_shortconv_moe_block'


def rmsnorm(x, g):
    xf = x.astype(jnp.float32)
    y = xf * lax.rsqrt(jnp.mean(xf * xf, axis=-1, keepdims=True) + EPS)
    return (y * g.astype(jnp.float32)).astype(x.dtype)


def masked_softmax(s, mask):
    p = jax.nn.softmax(jnp.where(mask, s, NEG), axis=-1)
    return jnp.where(mask, p, 0.0)


def alibi_slopes(n_heads):
    return jnp.asarray(2.0 ** (-8.0 * np.arange(1, n_heads + 1) / n_heads), jnp.float32)


def selection_map(n_cmp, n_blk):
    ratio = SEL_BLK // CMP_STRIDE
    span = CMP_LEN // CMP_STRIDE
    j = np.arange(n_blk)[:, None, None]
    i = (ratio * j + np.arange(ratio)[None, :, None] - np.arange(span)[None, None, :]).reshape(n_blk, -1)
    jj = np.broadcast_to(np.arange(n_blk)[:, None], i.shape)
    ok = (i >= 0) & (i < n_cmp)
    m = np.zeros((n_cmp, n_blk), np.float32)
    np.add.at(m, (i[ok], jj[ok]), 1.0)
    return jnp.asarray(m)


def compress(k, pe, w1, b1, w2, b2):
    B, S, G, hd = k.shape
    kk = k.reshape(B, S // CMP_STRIDE, CMP_STRIDE, G, hd)
    blocks = jnp.concatenate([kk[:, :-1], kk[:, 1:]], axis=2) + pe[None, None, :, None, :]
    n_cmp = blocks.shape[1]
    flat = blocks.transpose(0, 1, 3, 2, 4).reshape(B, n_cmp, G, CMP_LEN * hd)
    return jax.nn.gelu(flat @ w1 + b1) @ w2 + b2


def nsa_attention(q, kc, vc, ks, vs, kw, vw, gates):
    B, S, H, hd = q.shape
    G = kc.shape[2]
    hpg = H // G
    n_cmp = kc.shape[1]
    n_blk = S // SEL_BLK
    n_sel = min(SEL_TOPN, n_blk)
    scale = hd ** -0.5
    f32 = jnp.float32
    slopes = alibi_slopes(H).reshape(1, G, hpg, 1, 1)
    sel_map = selection_map(n_cmp, n_blk)
    cmp_end = jnp.arange(n_cmp) * CMP_STRIDE + CMP_LEN - 1
    ksb = ks.reshape(B, n_blk, SEL_BLK, G, hd).transpose(0, 3, 1, 2, 4)
    vsb = vs.reshape(B, n_blk, SEL_BLK, G, hd).transpose(0, 3, 1, 2, 4)
    pad = ((0, 0), (WINDOW, 0), (0, 0), (0, 0))
    kwp = jnp.pad(kw, pad)
    vwp = jnp.pad(vw, pad)
    bi = jnp.arange(B)[:, None, None, None]
    gi = jnp.arange(G)[None, :, None, None]

    def block(i):
        q0 = i * QBLK
        t = q0 + jnp.arange(QBLK)
        qb = lax.dynamic_slice_in_dim(q, q0, QBLK, axis=1).reshape(B, QBLK, G, hpg, hd)
        gb = lax.dynamic_slice_in_dim(gates, q0, QBLK, axis=1).reshape(B, QBLK, G, hpg, 3)
        dist_c = (t[:, None] - cmp_end[None, :]).astype(f32)
        s_c = jnp.einsum('bqghd,bngd->bghqn', qb, kc, preferred_element_type=f32) * scale - slopes * dist_c
        p_c = masked_softmax(s_c, dist_c >= 0)
        o_c = jnp.einsum('bghqn,bngd->bqghd', p_c.astype(vc.dtype), vc)
        imp = jnp.einsum('bghqn,nj->bgqj', p_c, sel_map)
        jb = jnp.arange(n_blk)[None, :]
        cur = (t // SEL_BLK)[:, None]
        forced = (jb == 0) | (jb == cur) | (jb == cur - 1)
        score = jnp.where(jb > cur, NEG, jnp.where(forced, POS, imp))
        _, idx = lax.top_k(score, n_sel)
        k_sel = ksb[bi, gi, idx].reshape(B, G, QBLK, n_sel * SEL_BLK, hd)
        v_sel = vsb[bi, gi, idx].reshape(B, G, QBLK, n_sel * SEL_BLK, hd)
        pos = (idx[..., None] * SEL_BLK + jnp.arange(SEL_BLK)).reshape(B, G, QBLK, n_sel * SEL_BLK)
        dist_s = (t[None, None, :, None] - pos).astype(f32)[:, :, None]
        s_s = jnp.einsum('bqghd,bgqkd->bghqk', qb, k_sel, preferred_element_type=f32) * scale - slopes * dist_s
        p_s = masked_softmax(s_s, dist_s >= 0)
        o_s = jnp.einsum('bghqk,bgqkd->bqghd', p_s.astype(v_sel.dtype), v_sel)
        k_w = lax.dynamic_slice_in_dim(kwp, q0, WINDOW + QBLK, axis=1)
        v_w = lax.dynamic_slice_in_dim(vwp, q0, WINDOW + QBLK, axis=1)
        spos = q0 - WINDOW + jnp.arange(WINDOW + QBLK)
        dist_w = t[:, None] - spos[None, :]
        valid_w = (dist_w >= 0) & (dist_w < WINDOW) & (spos[None, :] >= 0)
        s_w = jnp.einsum('bqghd,bkgd->bghqk', qb, k_w, preferred_element_type=f32) * scale - slopes * dist_w.astype(f32)
        p_w = masked_softmax(s_w, valid_w)
        o_w = jnp.einsum('bghqk,bkgd->bqghd', p_w.astype(v_w.dtype), v_w)
        o = gb[..., 0:1] * o_c + gb[..., 1:2] * o_s + gb[..., 2:3] * o_w
        return o.reshape(B, QBLK, H * hd)

    out = lax.map(block, jnp.arange(S // QBLK))
    return out.transpose(1, 0, 2, 3).reshape(B, S, H * hd)


def short_conv(b_gate, c_gate, u, w):
    v = c_gate * u
    y = lax.conv_general_dilated(v, w[:, None, :].astype(v.dtype), window_strides=(1,),
                                 padding=[(CONV_K - 1, 0)], dimension_numbers=('NWC', 'WIO', 'NWC'),
                                 feature_group_count=v.shape[-1])
    return b_gate * y


def hybrid_mixer(h, w_in, g_q, g_kc, g_ks, g_kw, pe_k, ck_w1, ck_b1, ck_w2, ck_b2,
                 pe_v, cv_w1, cv_b1, cv_w2, cv_b2, conv_w, w_pa, w_pb, w_o):
    B, S, _ = h.shape
    proj = h @ w_in
    (q, kc, vc, ks, vs, kw, vw, ng, cb, cc, cu, mg) = jnp.split(proj, IN_OFFSETS, axis=-1)
    kv_shape = (B, S, N_KV_HEADS, HEAD_DIM)
    q = rmsnorm(q.reshape(B, S, N_HEADS, HEAD_DIM), g_q)
    kc = rmsnorm(compress(kc.reshape(kv_shape), pe_k, ck_w1, ck_b1, ck_w2, ck_b2), g_kc)
    vc = compress(vc.reshape(kv_shape), pe_v, cv_w1, cv_b1, cv_w2, cv_b2)
    ks = rmsnorm(ks.reshape(kv_shape), g_ks)
    kw = rmsnorm(kw.reshape(kv_shape), g_kw)
    nsa_gates = jax.nn.sigmoid(ng).reshape(B, S, N_HEADS, 3)
    o_a = nsa_attention(q, kc, vc, ks, vs.reshape(kv_shape), kw, vw.reshape(kv_shape), nsa_gates)
    y_a = o_a @ w_pa
    y_b = short_conv(cb, cc, cu, conv_w) @ w_pb
    g_a, g_b = jnp.split(jax.nn.sigmoid(mg), 2, axis=-1)
    return (g_a * y_a + g_b * y_b) @ w_o


def moe(h, w_r, b_r, w_gu, b_gu, w_dn, b_dn):
    B, S, D = h.shape
    T = B * S
    hf = h.reshape(T, D)
    logits = (hf @ w_r + b_r).astype(jnp.float32)
    top_v, top_e = lax.top_k(logits, TOP_K)
    gate_w = jax.nn.softmax(top_v, axis=-1).astype(h.dtype)
    e_flat = top_e.reshape(-1)
    tok_flat = jnp.repeat(jnp.arange(T, dtype=jnp.int32), TOP_K)
    w_flat = gate_w.reshape(-1)
    order = jnp.argsort(e_flat, stable=True)
    e_s, tok_s, w_s = e_flat[order], tok_flat[order], w_flat[order]
    counts = jnp.bincount(e_flat, length=N_EXPERTS)
    padded = (counts + MOE_CHUNK - 1) // MOE_CHUNK * MOE_CHUNK
    offs = jnp.cumsum(counts) - counts
    pend = jnp.cumsum(padded)
    poffs = pend - padded
    dest = poffs[e_s] + jnp.arange(T * TOP_K) - offs[e_s]
    n_chunks = (T * TOP_K + MOE_CHUNK - 1) // MOE_CHUNK + N_EXPERTS
    P = n_chunks * MOE_CHUNK
    row_tok = jnp.full((P,), T, jnp.int32).at[dest].set(tok_s)
    row_w = jnp.zeros((P,), h.dtype).at[dest].set(w_s)
    chunk_e = jnp.minimum(jnp.searchsorted(pend, jnp.arange(n_chunks) * MOE_CHUNK, side='right'), N_EXPERTS - 1)
    hpad = jnp.concatenate([hf, jnp.zeros((1, D), hf.dtype)], axis=0)

    def run(args):
        toks, wts, e = args
        gu = hpad[toks] @ w_gu[e] + b_gu[e]
        g, u = jnp.split(gu, 2, axis=-1)
        g = jnp.minimum(g, SWIGLU_LIMIT)
        u = jnp.clip(u, -SWIGLU_LIMIT, SWIGLU_LIMIT)
        y = ((u + 1.0) * (g * jax.nn.sigmoid(SWIGLU_ALPHA * g))) @ w_dn[e] + b_dn[e]
        return y * wts[:, None]

    ys = lax.map(run, (row_tok.reshape(n_chunks, MOE_CHUNK), row_w.reshape(n_chunks, MOE_CHUNK), chunk_e))
    out = jax.ops.segment_sum(ys.reshape(P, D), row_tok, num_segments=T + 1)[:T]
    return out.reshape(B, S, D)


def setup_inputs(seed: int = 0) -> dict:
    key = jax.random.key(seed)
    ks = iter(jax.random.split(key, 40))
    f32 = jnp.float32

    def nrm(shape, fan_in):
        return jax.random.normal(next(ks), shape, f32) * (fan_in ** -0.5)

    def gain(shape):
        return 1.0 + 0.02 * jax.random.normal(next(ks), shape, f32)

    def small(shape, s=0.02):
        return s * jax.random.normal(next(ks), shape, f32)

    L = DEPTH
    return {
        'x': jax.random.normal(next(ks), (BATCH, SEQ, D_MODEL), f32),
        'g_norm1': gain((L, D_MODEL)),
        'w_in': nrm((L, D_MODEL, IN_COLS), D_MODEL),
        'g_q': gain((L, HEAD_DIM)),
        'g_kc': gain((L, HEAD_DIM)),
        'g_ks': gain((L, HEAD_DIM)),
        'g_kw': gain((L, HEAD_DIM)),
        'pe_k': small((L, CMP_LEN, HEAD_DIM)),
        'ck_w1': nrm((L, CMP_LEN * HEAD_DIM, CMP_HID), CMP_LEN * HEAD_DIM),
        'ck_b1': small((L, CMP_HID)),
        'ck_w2': nrm((L, CMP_HID, HEAD_DIM), CMP_HID),
        'ck_b2': small((L, HEAD_DIM)),
        'pe_v': small((L, CMP_LEN, HEAD_DIM)),
        'cv_w1': nrm((L, CMP_LEN * HEAD_DIM, CMP_HID), CMP_LEN * HEAD_DIM),
        'cv_b1': small((L, CMP_HID)),
        'cv_w2': nrm((L, CMP_HID, HEAD_DIM), CMP_HID),
        'cv_b2': small((L, HEAD_DIM)),
        'conv_w': nrm((L, CONV_K, CONV_WIDTH), CONV_K),
        'w_pa': nrm((L, ATTN_WIDTH, D_MODEL), ATTN_WIDTH),
        'w_pb': nrm((L, CONV_WIDTH, D_MODEL), CONV_WIDTH),
        'w_o': nrm((L, D_MODEL, D_MODEL), D_MODEL),
        'g_norm2': gain((L, D_MODEL)),
        'w_r': nrm((L, D_MODEL, N_EXPERTS), D_MODEL),
        'b_r': small((L, N_EXPERTS), 0.01),
        'w_gu': nrm((L, N_EXPERTS, D_MODEL, 2 * D_FF), D_MODEL),
        'b_gu': small((L, N_EXPERTS, 2 * D_FF)),
        'w_dn': nrm((L, N_EXPERTS, D_FF, D_MODEL), D_FF),
        'b_dn': small((L, N_EXPERTS, D_MODEL)),
    }


def reference(x, g_norm1, w_in, g_q, g_kc, g_ks, g_kw, pe_k, ck_w1, ck_b1, ck_w2, ck_b2,
              pe_v, cv_w1, cv_b1, cv_w2, cv_b2, conv_w, w_pa, w_pb, w_o, g_norm2,
              w_r, b_r, w_gu, b_gu, w_dn, b_dn):
    for l in range(DEPTH):
        h = rmsnorm(x, g_norm1[l])
        x = x + hybrid_mixer(h, w_in[l], g_q[l], g_kc[l], g_ks[l], g_kw[l], pe_k[l], ck_w1[l], ck_b1[l],
                             ck_w2[l], ck_b2[l], pe_v[l], cv_w1[l], cv_b1[l], cv_w2[l], cv_b2[l],
                             conv_w[l], w_pa[l], w_pb[l], w_o[l])
        h = rmsnorm(x, g_norm2[l])
        x = x + moe(h, w_r[l], b_r[l], w_gu[l], b_gu[l], w_dn[l], b_dn[l])
    return x
```

```python
import functools

import numpy as np
import jax
import jax.numpy as jnp
from jax import lax
from jax.experimental import pallas as pl
from jax.experimental.pallas import tpu as pltpu

F32 = jnp.float32
BF16 = jnp.bfloat16
I32 = jnp.int32

N_HEADS = 8
HEAD_DIM = 64
N_KV_HEADS = 2
HEADS_PER_GROUP = N_HEADS // N_KV_HEADS
CMP_LEN = 32
CMP_STRIDE = 16
CMP_HID = 256
SEL_BLK = 64
SEL_TOPN = 16
WINDOW = 512
CONV_WIDTH = 512
CONV_K = 3
N_EXPERTS = 32
TOP_K = 4
SWIGLU_LIMIT = 7.0
SWIGLU_ALPHA = 1.702
MOE_CHUNK = 512
EPS = 1e-6
NEG = -1e30
POS = 1e30
MASK_BIG = 2.0 ** 100

LANES = 128
Q_TILE = 128
KEY_TILE = 512
N_AUG = 5
ROW_TILE = 512
VMEM_LIMIT = 56 * 1024 * 1024


def _cparams(n_axes):
    return pltpu.CompilerParams(dimension_semantics=("arbitrary",) * n_axes,
                                vmem_limit_bytes=VMEM_LIMIT)


def _dot(a, b):
    return jnp.dot(a, b, preferred_element_type=F32)


def _dot_nt(a, b):
    return lax.dot_general(a, b, (((1,), (1,)), ((), ())), preferred_element_type=F32)


def _rms_pairs(v, bd):
    ss = _dot((v * v).astype(BF16), bd)
    return v * lax.rsqrt(ss + EPS)


def _inproj_body(x_ref, g1_ref, wq_ref, wkv_ref, wng_ref, wcv_ref, wmg_ref, gq_ref, gk_ref, bd_ref,
                 q_out, kv_out, gate_out, cbv_out, gab_out):
    x = x_ref[...]
    ms = jnp.mean(x * x, axis=-1, keepdims=True)
    h = (x * lax.rsqrt(ms + EPS) * g1_ref[...]).astype(BF16)
    bd = bd_ref[...]
    q = _dot(h, wq_ref[...])
    for c in range(N_HEADS * HEAD_DIM // LANES):
        sl = slice(c * LANES, (c + 1) * LANES)
        q_out[:, sl] = (_rms_pairs(q[:, sl], bd) * gq_ref[:, sl]).astype(BF16)
    kv = _dot(h, wkv_ref[...])
    kv_out[:, 0:256] = kv[:, 0:256].astype(BF16)
    kv_out[:, 256:384] = (_rms_pairs(kv[:, 256:384], bd) * gk_ref[:, 0:128]).astype(BF16)
    kv_out[:, 384:512] = kv[:, 384:512].astype(BF16)
    kv_out[:, 512:640] = (_rms_pairs(kv[:, 512:640], bd) * gk_ref[:, 128:256]).astype(BF16)
    kv_out[:, 640:768] = kv[:, 640:768].astype(BF16)
    gate_out[...] = jax.nn.sigmoid(_dot(h, wng_ref[...]))
    cv = _dot(h, wcv_ref[...])
    cw = CONV_WIDTH
    cbv_out[:, 0:cw] = cv[:, 0:cw].astype(BF16)
    cbv_out[:, cw:2 * cw] = (cv[:, cw:2 * cw] * cv[:, 2 * cw:3 * cw]).astype(BF16)
    gab_out[...] = jax.nn.sigmoid(_dot(h, wmg_ref[...])).astype(BF16)


def _inproj(x2, g1, w_in, g_q, g_ks, g_kw):
    T, D = x2.shape
    aw = N_HEADS * HEAD_DIM
    kvw = N_KV_HEADS * HEAD_DIM
    o = 0
    wq = w_in[:, o:o + aw]; o += aw
    wkv = w_in[:, o:o + 6 * kvw]; o += 6 * kvw
    wng = w_in[:, o:o + 3 * N_HEADS]; o += 3 * N_HEADS
    wcv = w_in[:, o:o + 3 * CONV_WIDTH]; o += 3 * CONV_WIDTH
    wmg = w_in[:, o:o + 2 * D]
    wng = jnp.pad(wng, ((0, 0), (0, LANES - 3 * N_HEADS)))
    wq, wkv, wng, wcv, wmg = (w.astype(BF16) for w in (wq, wkv, wng, wcv, wmg))
    gq = (jnp.tile(g_q, N_HEADS) * (HEAD_DIM ** -0.5)).reshape(1, aw)
    gk = jnp.concatenate([jnp.tile(g_ks, N_KV_HEADS), jnp.tile(g_kw, N_KV_HEADS)]).reshape(1, 2 * kvw)
    idx = np.arange(LANES) // HEAD_DIM
    bd = jnp.asarray((idx[:, None] == idx[None, :]).astype(np.float32) / HEAD_DIM, BF16)
    tm = ROW_TILE
    row = lambda w: pl.BlockSpec((tm, w), lambda i: (i, 0))
    full = lambda a: pl.BlockSpec(a.shape, lambda i: (0,) * a.ndim)
    ins = (x2, g1.reshape(1, D), wq, wkv, wng, wcv, wmg, gq, gk, bd)
    return pl.pallas_call(
        _inproj_body,
        grid=(T // tm,),
        in_specs=[row(D)] + [full(a) for a in ins[1:]],
        out_specs=[row(aw), row(6 * kvw), row(LANES), row(2 * CONV_WIDTH), row(2 * D)],
        out_shape=[jax.ShapeDtypeStruct((T, aw), BF16), jax.ShapeDtypeStruct((T, 6 * kvw), BF16),
                   jax.ShapeDtypeStruct((T, LANES), F32), jax.ShapeDtypeStruct((T, 2 * CONV_WIDTH), BF16),
                   jax.ShapeDtypeStruct((T, 2 * D), BF16)],
        compiler_params=_cparams(1),
        name="inproj",
    )(*ins)


def _compress_body(h_ref, w1_ref, pe_ref, b1_ref, w2_ref, b2_ref, g_ref, o_ref, *, normalize):
    hb = h_ref[0, 0]
    nc = hb.shape[0]
    a = _dot(hb, w1_ref[0])
    b = _dot(hb, w1_ref[1])
    c = _dot(pe_ref[0], w1_ref[0]) + _dot(pe_ref[1], w1_ref[1])
    pre = a + pltpu.roll(b, nc - 1, 0) + c[0:1, :] + b1_ref[...]
    hid = jax.nn.gelu(pre)
    out = _dot(hid.astype(BF16), w2_ref[...]) + b2_ref[...]
    if normalize:
        ms = jnp.mean(out * out, axis=-1, keepdims=True)
        out = out * lax.rsqrt(ms + EPS) * g_ref[...]
    o_ref[0, 0] = out


def _compress(hh, pe, w1, b1, w2, b2, gain, normalize):
    B, G, NC, HW = hh.shape
    w1s = w1.reshape(2, HW, CMP_HID).astype(BF16)
    pes = jnp.broadcast_to(pe.reshape(2, 1, HW), (2, 8, HW)).astype(BF16)
    full = lambda a: pl.BlockSpec(a.shape, lambda b, g: (0,) * a.ndim)
    ins = (hh, w1s, pes, b1.reshape(1, CMP_HID), w2.astype(BF16), b2.reshape(1, HEAD_DIM),
           gain.reshape(1, HEAD_DIM))
    return pl.pallas_call(
        functools.partial(_compress_body, normalize=normalize),
        grid=(B, G),
        in_specs=[pl.BlockSpec((1, 1, NC, HW), lambda b, g: (b, g, 0, 0))] + [full(a) for a in ins[1:]],
        out_specs=pl.BlockSpec((1, 1, NC, HEAD_DIM), lambda b, g: (b, g, 0, 0)),
        out_shape=jax.ShapeDtypeStruct((B, G, NC, HEAD_DIM), F32),
        compiler_params=_cparams(2),
        name="compress_norm" if normalize else "compress",
    )(*ins)


def _attn_body(qa_ref, g_ref, kca_ref, vc_ref, kas_ref, vs_ref, kaw_ref, vw_ref, selmap_ref, wbias_ref,
               o_ref, qaug_ref, m_ref, l_ref, acc_ref, *, n_sel):
    i = pl.program_id(2)
    q0 = i * Q_TILE
    rows = HEADS_PER_GROUP * Q_TILE
    qa = qa_ref[0].reshape(rows, LANES)

    nc = kca_ref.shape[2]
    s = _dot_nt(qa, kca_ref[0, 0])
    r_i = lax.broadcasted_iota(I32, (rows, nc), 0)
    c_i = lax.broadcasted_iota(I32, (rows, nc), 1)
    valid = (q0 + (r_i & (Q_TILE - 1))) >= c_i * CMP_STRIDE + (CMP_LEN - 1)
    s = jnp.where(valid, s, NEG)
    m = jnp.max(s, axis=-1, keepdims=True)
    p = jnp.where(valid, jnp.exp(s - m), 0.0)
    l = jnp.sum(p, axis=-1, keepdims=True)
    pc = p * jnp.where(l > 0.0, 1.0 / l, 0.0)
    o_c = _dot(pc.astype(BF16), vc_ref[0, 0])

    ps = pc[0:Q_TILE]
    for h in range(1, HEADS_PER_GROUP):
        ps = ps + pc[h * Q_TILE:(h + 1) * Q_TILE]
    ps_hi = ps.astype(BF16)
    ps_lo = (ps - ps_hi.astype(F32)).astype(BF16)
    imp = _dot(ps_hi, selmap_ref[...]) + _dot(ps_lo, selmap_ref[...])
    jb = lax.broadcasted_iota(I32, (Q_TILE, LANES), 1)
    cur = (q0 + lax.broadcasted_iota(I32, (Q_TILE, LANES), 0)) // SEL_BLK
    forced = (jb == 0) | (jb == cur) | (jb == cur - 1)
    score = jnp.where(jb > cur, NEG, jnp.where(forced, POS, imp))
    jbf = jb.astype(F32)
    sel = jnp.zeros((Q_TILE, LANES), F32)
    for _ in range(n_sel):
        mx = jnp.max(score, axis=-1, keepdims=True)
        first = jnp.min(jnp.where(score == mx, jbf, float(LANES)), axis=-1, keepdims=True)
        hit = jbf == first
        sel = jnp.where(hit, 1.0, sel)
        score = jnp.where(hit, -jnp.inf, score)
    selbias = jnp.where(sel > 0.0, 0.0, -MASK_BIG).astype(BF16)
    qaug_ref[:, 0:LANES] = qa
    for h in range(HEADS_PER_GROUP):
        qaug_ref[h * Q_TILE:(h + 1) * Q_TILE, LANES:2 * LANES] = selbias

    m_ref[...] = jnp.full(m_ref.shape, -3.0e38, F32)
    l_ref[...] = jnp.zeros(l_ref.shape, F32)
    acc_ref[...] = jnp.zeros(acc_ref.shape, F32)
    qaug = qaug_ref[...]
    rel = (q0 + (lax.broadcasted_iota(I32, (rows, KEY_TILE), 0) & (Q_TILE - 1))
           - lax.broadcasted_iota(I32, (rows, KEY_TILE), 1))

    def tile_step(kt, causal):
        k0 = pl.multiple_of(kt * KEY_TILE, KEY_TILE)
        sc = _dot_nt(qaug, kas_ref[0, 0, pl.ds(k0, KEY_TILE), :])
        if causal:
            sc = jnp.where(rel >= k0, sc, NEG)
        m_old = m_ref[...]
        m_new = jnp.maximum(m_old, jnp.max(sc, axis=-1, keepdims=True))
        alpha = jnp.exp(m_old - m_new)
        pp = jnp.exp(sc - m_new)
        l_ref[...] = alpha * l_ref[...] + jnp.sum(pp, axis=-1, keepdims=True)
        acc_ref[...] = alpha * acc_ref[...] + _dot(pp.astype(BF16), vs_ref[0, 0, pl.ds(k0, KEY_TILE), :])
        m_ref[...] = m_new

    n_full = q0 // KEY_TILE

    def loop_body(kt, carry):
        tile_step(kt, False)
        return carry

    lax.fori_loop(0, n_full, loop_body, 0)
    tile_step(n_full, True)
    o_s = acc_ref[...] * (1.0 / l_ref[...])

    wk = WINDOW + Q_TILE
    w0 = pl.multiple_of(q0, Q_TILE)
    sw = _dot_nt(qa, kaw_ref[0, 0, pl.ds(w0, wk), :])
    sw = (sw.reshape(HEADS_PER_GROUP, Q_TILE, wk) + wbias_ref[...][None]).reshape(rows, wk)
    mw = jnp.max(sw, axis=-1, keepdims=True)
    pw = jnp.exp(sw - mw)
    lw = jnp.sum(pw, axis=-1, keepdims=True)
    o_w = _dot(pw.astype(BF16), vw_ref[0, 0, pl.ds(w0, wk), :]) * (1.0 / lw)

    g = g_ref[0].reshape(rows, 3)
    o = g[:, 0:1] * o_c + g[:, 1:2] * o_s + g[:, 2:3] * o_w
    o_ref[0] = o.reshape(HEADS_PER_GROUP, Q_TILE, HEAD_DIM)


def _key_aug(pos, vbias):
    one = jnp.ones_like(pos, F32)
    return jnp.stack([one, one, (pos // 64 * 64).astype(F32), (pos % 64).astype(F32), vbias], axis=-1)


def _attention(qh, gates, kc, vc, ks, vs, kw, vw):
    B, H, S, hd = qh.shape
    G = N_KV_HEADS
    NC = kc.shape[2]
    n_blk = S // SEL_BLK
    assert n_blk <= LANES and S % KEY_TILE == 0
    n_sel = min(SEL_TOPN, n_blk)
    pad = LANES - hd - N_AUG
    slopes = jnp.asarray(2.0 ** (-8.0 * np.arange(1, H + 1) / H), F32)
    pos = jnp.arange(S, dtype=I32)
    sl = slopes[:, None]
    aq = jnp.stack([-sl * (pos // 64 * 64).astype(F32)[None], -sl * (pos % 64).astype(F32)[None],
                    jnp.broadcast_to(sl, (H, S)), jnp.broadcast_to(sl, (H, S)), jnp.ones((H, S), F32)], axis=-1)
    qa = jnp.concatenate([qh, jnp.broadcast_to(aq.astype(BF16)[None], (B, H, S, N_AUG)),
                          jnp.zeros((B, H, S, pad), BF16)], axis=-1)

    def with_aug(k, ak, extra=None):
        parts = [k.astype(BF16), jnp.broadcast_to(ak.astype(BF16)[None, None], k.shape[:3] + (N_AUG,)),
                 jnp.zeros(k.shape[:3] + (pad,), BF16)]
        if extra is not None:
            parts.append(jnp.broadcast_to(extra[None, None], k.shape[:3] + (extra.shape[-1],)))
        return jnp.concatenate(parts, axis=-1)

    zero_s = jnp.zeros((S,), F32)
    onehot = (pos[:, None] // SEL_BLK == jnp.arange(LANES, dtype=I32)[None, :]).astype(BF16)
    kas = with_aug(ks, _key_aug(pos, zero_s), onehot)
    cpos = jnp.arange(NC, dtype=I32) * CMP_STRIDE + (CMP_LEN - 1)
    kca = with_aug(kc, _key_aug(cpos, jnp.zeros((NC,), F32)))
    front = ((0, 0), (0, 0), (WINDOW, 0), (0, 0))
    wpos = jnp.concatenate([jnp.zeros((WINDOW,), I32), pos])
    wvb = jnp.concatenate([jnp.full((WINDOW,), -MASK_BIG, F32), zero_s])
    kaw = with_aug(jnp.pad(kw, front), _key_aug(wpos, wvb))
    vwp = jnp.pad(vw, front)
    ratio, span = SEL_BLK // CMP_STRIDE, CMP_LEN // CMP_STRIDE
    sm = np.zeros((NC, LANES), np.float32)
    for j in range(n_blk):
        for a in range(ratio):
            for b in range(span):
                n = ratio * j + a - b
                if 0 <= n < NC - 1:
                    sm[n, j] += 1.0
    selmap = jnp.asarray(sm, BF16)
    r = np.arange(Q_TILE)[:, None]
    c = np.arange(WINDOW + Q_TILE)[None, :]
    wbias = jnp.asarray(np.where((c > r) & (c <= r + WINDOW), 0.0, NEG), F32)

    hpg, rows = HEADS_PER_GROUP, HEADS_PER_GROUP * Q_TILE
    grp = lambda n, w: pl.BlockSpec((1, 1, n, w), lambda b, g, i: (b, g, 0, 0))
    const = lambda a: pl.BlockSpec(a.shape, lambda b, g, i: (0,) * a.ndim)
    return pl.pallas_call(
        functools.partial(_attn_body, n_sel=n_sel),
        grid=(B, G, S // Q_TILE),
        in_specs=[pl.BlockSpec((1, hpg, Q_TILE, LANES), lambda b, g, i: (b, g, i, 0)),
                  pl.BlockSpec((1, hpg, Q_TILE, 3), lambda b, g, i: (b, g, i, 0)),
                  grp(NC, LANES), grp(NC, hd), grp(S, 2 * LANES), grp(S, hd),
                  grp(S + WINDOW, LANES), grp(S + WINDOW, hd), const(selmap), const(wbias)],
        out_specs=pl.BlockSpec((1, hpg, Q_TILE, hd), lambda b, g, i: (b, g, i, 0)),
        out_shape=jax.ShapeDtypeStruct((B, H, S, hd), F32),
        scratch_shapes=[pltpu.VMEM((rows, 2 * LANES), BF16), pltpu.VMEM((rows, 1), F32),
                        pltpu.VMEM((rows, 1), F32), pltpu.VMEM((rows, hd), F32)],
        compiler_params=_cparams(3),
        name="nsa_attention",
    )(qa, gates, kca, vc.astype(BF16), kas, vs, kaw, vwp, selmap, wbias)


def _mixer_out_body(x_ref, oa_ref, cbv_ref, halo_ref, gab_ref, cw_ref, wpa_ref, wpb_ref, wo_ref, o_ref,
                    *, seq_len):
    i = pl.program_id(0)
    tm = x_ref.shape[0]
    cwd = CONV_WIDTH
    d = x_ref.shape[1]
    v = cbv_ref[:, cwd:2 * cwd].astype(F32)
    prev = halo_ref[:, cwd:2 * cwd].astype(F32)
    keep = ((i * tm) % seq_len != 0).astype(F32)
    p1 = prev[7:8, :] * keep
    p2 = prev[6:7, :] * keep
    ridx = lax.broadcasted_iota(I32, (tm, cwd), 0)
    v1 = jnp.where(ridx == 0, p1, pltpu.roll(v, 1, 0))
    v2 = jnp.where(ridx == 0, p2, jnp.where(ridx == 1, p1, pltpu.roll(v, 2, 0)))
    y = cw_ref[0:1, :] * v2 + cw_ref[1:2, :] * v1 + cw_ref[2:3, :] * v
    yb_in = (cbv_ref[:, 0:cwd].astype(F32) * y).astype(BF16)
    y_a = _dot(oa_ref[...], wpa_ref[...])
    y_b = _dot(yb_in, wpb_ref[...])
    merged = gab_ref[:, 0:d].astype(F32) * y_a + gab_ref[:, d:2 * d].astype(F32) * y_b
    o_ref[...] = x_ref[...] + _dot(merged.astype(BF16), wo_ref[...])


def _mixer_out(x2, oa, cbv, gab, conv_w, w_pa, w_pb, w_o, seq_len):
    T, D = x2.shape
    tm = ROW_TILE
    cw8 = jnp.pad(conv_w, ((0, 8 - CONV_K), (0, 0)))
    row = lambda w: pl.BlockSpec((tm, w), lambda i: (i, 0))
    full = lambda a: pl.BlockSpec(a.shape, lambda i: (0,) * a.ndim)
    halo = pl.BlockSpec((8, cbv.shape[1]), lambda i: (jnp.maximum(i * (tm // 8) - 1, 0), 0))
    wts = (cw8, w_pa.astype(BF16), w_pb.astype(BF16), w_o.astype(BF16))
    return pl.pallas_call(
        functools.partial(_mixer_out_body, seq_len=seq_len),
        grid=(T // tm,),
        in_specs=[row(D), row(oa.shape[1]), row(cbv.shape[1]), halo, row(gab.shape[1])] + [full(a) for a in wts],
        out_specs=row(D),
        out_shape=jax.ShapeDtypeStruct((T, D), F32),
        compiler_params=_cparams(1),
        name="mixer_out",
    )(x2, oa, cbv, cbv, gab, *wts)


def _router_body(x_ref, g2_ref, whi_ref, wlo_ref, br_ref, tri_ref, h_out, mi_out, mf_out, cnt_out):
    i = pl.program_id(0)

    @pl.when(i == 0)
    def _():
        cnt_out[...] = jnp.zeros(cnt_out.shape, F32)

    x = x_ref[...]
    tm = x.shape[0]
    ms = jnp.mean(x * x, axis=-1, keepdims=True)
    h = x * lax.rsqrt(ms + EPS) * g2_ref[...]
    h_out[...] = h
    h_hi = h.astype(BF16)
    h_lo = (h - h_hi.astype(F32)).astype(BF16)
    logits = (_dot(h_hi, whi_ref[...]) + _dot(h_lo, whi_ref[...]) + _dot(h_hi, wlo_ref[...])) + br_ref[...]
    lane = lax.broadcasted_iota(I32, (tm, LANES), 1)
    lanef = lane.astype(F32)
    work = jnp.where(lane < N_EXPERTS, logits, -jnp.inf)
    vals, hits = [], []
    for _ in range(TOP_K):
        mx = jnp.max(work, axis=-1, keepdims=True)
        first = jnp.min(jnp.where(work == mx, lanef, float(LANES)), axis=-1, keepdims=True)
        hit = lanef == first
        vals.append(mx)
        hits.append(hit)
        work = jnp.where(hit, -jnp.inf, work)
    ex = [jnp.exp(v - vals[0]) for v in vals]
    den = ex[0]
    for e in ex[1:]:
        den = den + e
    cnt = jnp.zeros((tm, LANES), F32)
    for hit in hits:
        cnt = cnt + hit.astype(F32)
    before = _dot(tri_ref[...], cnt.astype(BF16)) + cnt_out[0:1, :]
    mi = jnp.zeros((tm, LANES), F32)
    mf = jnp.zeros((tm, LANES), F32)
    for k, hit in enumerate(hits):
        e_k = jnp.sum(jnp.where(hit, lanef, 0.0), axis=-1, keepdims=True)
        r_k = jnp.sum(jnp.where(hit, before, 0.0), axis=-1, keepdims=True)
        mi = jnp.where(lane == k, e_k, jnp.where(lane == TOP_K + k, r_k, mi))
        mf = jnp.where(lane == k, ex[k] / den, mf)
    mi_out[...] = mi.astype(I32)
    mf_out[...] = mf
    cnt_out[...] = cnt_out[...] + jnp.sum(cnt, axis=0, keepdims=True)


def _router(x1, g2, w_r, b_r):
    T, D = x1.shape
    tm = ROW_TILE
    wpad = jnp.pad(w_r, ((0, 0), (0, LANES - N_EXPERTS)))
    whi = wpad.astype(BF16)
    wlo = (wpad - whi.astype(F32)).astype(BF16)
    br = jnp.pad(b_r, (0, LANES - N_EXPERTS)).reshape(1, LANES)
    tri = jnp.asarray(np.tril(np.ones((tm, tm), np.float32), -1), BF16)
    row = lambda w: pl.BlockSpec((tm, w), lambda i: (i, 0))
    full = lambda a: pl.BlockSpec(a.shape, lambda i: (0,) * a.ndim)
    ins = (x1, g2.reshape(1, D), whi, wlo, br, tri)
    return pl.pallas_call(
        _router_body,
        grid=(T // tm,),
        in_specs=[row(D)] + [full(a) for a in ins[1:]],
        out_specs=[row(D), row(LANES), row(LANES), pl.BlockSpec((8, LANES), lambda i: (0, 0))],
        out_shape=[jax.ShapeDtypeStruct((T, D), F32), jax.ShapeDtypeStruct((T, LANES), I32),
                   jax.ShapeDtypeStruct((T, LANES), F32), jax.ShapeDtypeStruct((8, LANES), F32)],
        compiler_params=_cparams(1),
        name="router",
    )(*ins)


DISPATCH_TILE = 256


def _dispatch_body(dest_ref, h_hbm, zero_hbm, o_hbm, sem):
    del zero_hbm
    i = pl.program_id(0)
    n = DISPATCH_TILE * TOP_K

    def row_copy(j):
        t = i * DISPATCH_TILE + j // TOP_K
        return pltpu.make_async_copy(h_hbm.at[pl.ds(t, 1)], o_hbm.at[pl.ds(dest_ref[0, 0, j], 1)], sem)

    def start(j, c):
        row_copy(j).start()
        return c

    def wait(j, c):
        row_copy(j).wait()
        return c

    lax.fori_loop(0, n, start, 0)
    lax.fori_loop(0, n, wait, 0)


def _dispatch(h2, dest, n_rows):
    T, D = h2.shape
    td = DISPATCH_TILE
    dest3 = dest.reshape(T // td, 1, td * TOP_K)
    zeros = jnp.zeros((n_rows, D), h2.dtype)
    return pl.pallas_call(
        _dispatch_body,
        grid=(T // td,),
        in_specs=[pl.BlockSpec((1, 1, td * TOP_K), lambda i: (i, 0, 0), memory_space=pltpu.SMEM),
                  pl.BlockSpec(memory_space=pl.ANY), pl.BlockSpec(memory_space=pl.ANY)],
        out_specs=pl.BlockSpec(memory_space=pl.ANY),
        out_shape=jax.ShapeDtypeStruct((n_rows, D), h2.dtype),
        scratch_shapes=[pltpu.SemaphoreType.DMA(())],
        input_output_aliases={2: 0},
        compiler_params=_cparams(1),
        name="dispatch",
    )(dest3, h2, zeros)


def _expert_body(ce_ref, nu_ref, x_ref, wgu_ref, bgu_ref, wdn_ref, bdn_ref, o_ref):
    c = pl.program_id(0)
    dff = wdn_ref.shape[1]

    @pl.when(c < nu_ref[0])
    def _():
        gu = _dot(x_ref[...].astype(BF16), wgu_ref[0]) + bgu_ref[0]
        g = jnp.minimum(gu[:, 0:dff], SWIGLU_LIMIT)
        u = jnp.clip(gu[:, dff:2 * dff], -SWIGLU_LIMIT, SWIGLU_LIMIT)
        act = (u + 1.0) * (g * jax.nn.sigmoid(SWIGLU_ALPHA * g))
        o_ref[...] = _dot(act.astype(BF16), wdn_ref[0]) + bdn_ref[0]

    @pl.when(c >= nu_ref[0])
    def _():
        o_ref[...] = jnp.zeros(o_ref.shape, F32)


def _experts(hperm, chunk_e, n_used, w_gu, b_gu, w_dn, b_dn):
    P, D = hperm.shape
    E, _, F2 = w_gu.shape
    dff = F2 // 2
    n_chunks = P // MOE_CHUNK
    grid_spec = pltpu.PrefetchScalarGridSpec(
        num_scalar_prefetch=2,
        grid=(n_chunks,),
        in_specs=[pl.BlockSpec((MOE_CHUNK, D), lambda c, ce, nu: (c, 0)),
                  pl.BlockSpec((1, D, F2), lambda c, ce, nu: (ce[c], 0, 0)),
                  pl.BlockSpec((1, 1, F2), lambda c, ce, nu: (ce[c], 0, 0)),
                  pl.BlockSpec((1, dff, D), lambda c, ce, nu: (ce[c], 0, 0)),
                  pl.BlockSpec((1, 1, D), lambda c, ce, nu: (ce[c], 0, 0))],
        out_specs=pl.BlockSpec((MOE_CHUNK, D), lambda c, ce, nu: (c, 0)),
    )
    return pl.pallas_call(
        _expert_body,
        grid_spec=grid_spec,
        out_shape=jax.ShapeDtypeStruct((P, D), F32),
        compiler_params=_cparams(1),
        name="experts",
    )(chunk_e, n_used, hperm, w_gu.astype(BF16), b_gu.reshape(E, 1, F2), w_dn.astype(BF16),
      b_dn.reshape(E, 1, D))


COMBINE_TILE = 256


def _combine_body(dest_ref, x_ref, w_ref, y_hbm, o_ref, buf_ref, sem):
    n = COMBINE_TILE * TOP_K

    def row_copy(j):
        return pltpu.make_async_copy(y_hbm.at[pl.ds(dest_ref[0, 0, j], 1)],
                                     buf_ref.at[j % TOP_K, pl.ds(j // TOP_K, 1)], sem)

    def start(j, c):
        row_copy(j).start()
        return c

    def wait(j, c):
        row_copy(j).wait()
        return c

    lax.fori_loop(0, n, start, 0)
    lax.fori_loop(0, n, wait, 0)
    out = x_ref[...]
    for k in range(TOP_K):
        out = out + w_ref[:, k:k + 1] * buf_ref[k]
    o_ref[...] = out


def _combine(x1, gate_w, dest, ys):
    T, D = x1.shape
    tc = COMBINE_TILE
    dest3 = dest.reshape(T // tc, 1, tc * TOP_K)
    row = lambda w: pl.BlockSpec((tc, w), lambda i: (i, 0))
    return pl.pallas_call(
        _combine_body,
        grid=(T // tc,),
        in_specs=[pl.BlockSpec((1, 1, tc * TOP_K), lambda i: (i, 0, 0), memory_space=pltpu.SMEM),
                  row(D), row(LANES), pl.BlockSpec(memory_space=pl.ANY)],
        out_specs=row(D),
        out_shape=jax.ShapeDtypeStruct((T, D), F32),
        scratch_shapes=[pltpu.VMEM((TOP_K, tc, D), F32), pltpu.SemaphoreType.DMA(())],
        compiler_params=_cparams(1),
        name="combine",
    )(dest3, x1, gate_w, ys)


def _mixer(x2, B, S, g_norm1, w_in, g_q, g_kc, g_ks, g_kw, pe_k, ck_w1, ck_b1, ck_w2, ck_b2,
           pe_v, cv_w1, cv_b1, cv_w2, cv_b2, conv_w, w_pa, w_pb, w_o):
    T, D = x2.shape
    G, H, hd = N_KV_HEADS, N_HEADS, HEAD_DIM
    q, kv, gates, cbv, gab = _inproj(x2, g_norm1, w_in, g_q, g_ks, g_kw)
    qh = q.reshape(B, S, H, hd).transpose(0, 2, 1, 3)
    kv6 = kv.reshape(B, S, 6, G, hd)
    heads = lambda j: kv6[:, :, j].transpose(0, 2, 1, 3)
    half = lambda j: (kv6[:, :, j].reshape(B, S // CMP_STRIDE, CMP_STRIDE, G, hd)
                      .transpose(0, 3, 1, 2, 4).reshape(B, G, S // CMP_STRIDE, CMP_STRIDE * hd))
    kc = _compress(half(0), pe_k, ck_w1, ck_b1, ck_w2, ck_b2, g_kc, True)
    vc = _compress(half(1), pe_v, cv_w1, cv_b1, cv_w2, cv_b2, jnp.ones((hd,), F32), False)
    gat = gates[:, :3 * H].reshape(B, S, H, 3).transpose(0, 2, 1, 3)
    o = _attention(qh, gat, kc, vc, heads(2), heads(3), heads(4), heads(5))
    oa = o.transpose(0, 2, 1, 3).reshape(T, H * hd).astype(BF16)
    return _mixer_out(x2, oa, cbv, gab, conv_w, w_pa, w_pb, w_o, S)


def _moe(x1, g_norm2, w_r, b_r, w_gu, b_gu, w_dn, b_dn):
    T, D = x1.shape
    h2, mi, mf, cnt = _router(x1, g_norm2, w_r, b_r)
    top_e = mi[:, 0:TOP_K]
    rank = mi[:, TOP_K:2 * TOP_K]
    counts = cnt[0, :N_EXPERTS].astype(I32)
    padded = (counts + MOE_CHUNK - 1) // MOE_CHUNK * MOE_CHUNK
    pend = jnp.cumsum(padded)
    poffs = pend - padded
    dest = (poffs[top_e] + rank).reshape(-1)
    n_chunks = (T * TOP_K + MOE_CHUNK - 1) // MOE_CHUNK + N_EXPERTS
    chunk_e = jnp.minimum(jnp.searchsorted(pend, jnp.arange(n_chunks, dtype=I32) * MOE_CHUNK, side='right'),
                          N_EXPERTS - 1).astype(I32)
    n_used = (pend[-1:] // MOE_CHUNK).astype(I32)
    hperm = _dispatch(h2, dest, n_chunks * MOE_CHUNK)
    ys = _experts(hperm, chunk_e, n_used, w_gu, b_gu, w_dn, b_dn)
    return _combine(x1, mf, dest, ys)


def kernel(x, g_norm1, w_in, g_q, g_kc, g_ks, g_kw, pe_k, ck_w1, ck_b1, ck_w2, ck_b2, pe_v, cv_w1, cv_b1,
           cv_w2, cv_b2, conv_w, w_pa, w_pb, w_o, g_norm2, w_r, b_r, w_gu, b_gu, w_dn, b_dn):
    B, S, D = x.shape
    x2 = x.reshape(B * S, D)
    for l in range(g_norm1.shape[0]):
        x2 = _mixer(x2, B, S, g_norm1[l], w_in[l], g_q[l], g_kc[l], g_ks[l], g_kw[l], pe_k[l], ck_w1[l],
                    ck_b1[l], ck_w2[l], ck_b2[l], pe_v[l], cv_w1[l], cv_b1[l], cv_w2[l], cv_b2[l],
                    conv_w[l], w_pa[l], w_pb[l], w_o[l])
        x2 = _moe(x2, g_norm2[l], w_r[l], b_r[l], w_gu[l], b_gu[l], w_dn[l], b_dn[l])
    return x2.reshape(B, S, D)
```

```python
import functools

import numpy as np
import jax
import jax.numpy as jnp
from jax import lax
from jax.experimental import pallas as pl
from jax.experimental.pallas import tpu as pltpu

F32 = jnp.float32
BF16 = jnp.bfloat16
I32 = jnp.int32

N_HEADS = 8
HEAD_DIM = 64
N_KV_HEADS = 2
HEADS_PER_GROUP = N_HEADS // N_KV_HEADS
CMP_LEN = 32
CMP_STRIDE = 16
CMP_HID = 256
SEL_BLK = 64
SEL_TOPN = 16
WINDOW = 512
CONV_WIDTH = 512
CONV_K = 3
N_EXPERTS = 32
TOP_K = 4
SWIGLU_LIMIT = 7.0
SWIGLU_ALPHA = 1.702
MOE_CHUNK = 512
EPS = 1e-6
NEG = -1e30
POS = 1e30
MASK_BIG = 2.0 ** 100

LANES = 128
Q_TILE = 128
KEY_TILE = 512
N_AUG = 5
ROW_TILE = 512
VMEM_LIMIT = 56 * 1024 * 1024


def _cparams(n_axes):
    return pltpu.CompilerParams(dimension_semantics=("arbitrary",) * n_axes,
                                vmem_limit_bytes=VMEM_LIMIT)


def _dot(a, b):
    return jnp.dot(a, b, preferred_element_type=F32)


def _dot_nt(a, b):
    return lax.dot_general(a, b, (((1,), (1,)), ((), ())), preferred_element_type=F32)


def _rms_pairs(v, bd):
    ss = _dot((v * v).astype(BF16), bd)
    return v * lax.rsqrt(ss + EPS)


def _inproj_body(x_ref, g1_ref, wq_ref, wkv_ref, wng_ref, wcv_ref, wmg_ref, gq_ref, gk_ref, bd_ref,
                 q_out, kv_out, gate_out, cbv_out, gab_out):
    x = x_ref[...]
    ms = jnp.mean(x * x, axis=-1, keepdims=True)
    h = (x * lax.rsqrt(ms + EPS) * g1_ref[...]).astype(BF16)
    bd = bd_ref[...]
    q = _dot(h, wq_ref[...])
    for c in range(N_HEADS * HEAD_DIM // LANES):
        sl = slice(c * LANES, (c + 1) * LANES)
        q_out[:, sl] = (_rms_pairs(q[:, sl], bd) * gq_ref[:, sl]).astype(BF16)
    kv = _dot(h, wkv_ref[...])
    kv_out[:, 0:256] = kv[:, 0:256].astype(BF16)
    kv_out[:, 256:384] = (_rms_pairs(kv[:, 256:384], bd) * gk_ref[:, 0:128]).astype(BF16)
    kv_out[:, 384:512] = kv[:, 384:512].astype(BF16)
    kv_out[:, 512:640] = (_rms_pairs(kv[:, 512:640], bd) * gk_ref[:, 128:256]).astype(BF16)
    kv_out[:, 640:768] = kv[:, 640:768].astype(BF16)
    gate_out[...] = jax.nn.sigmoid(_dot(h, wng_ref[...]))
    cv = _dot(h, wcv_ref[...])
    cw = CONV_WIDTH
    cbv_out[:, 0:cw] = cv[:, 0:cw].astype(BF16)
    cbv_out[:, cw:2 * cw] = (cv[:, cw:2 * cw] * cv[:, 2 * cw:3 * cw]).astype(BF16)
    gab_out[...] = jax.nn.sigmoid(_dot(h, wmg_ref[...])).astype(BF16)


def _inproj(x2, g1, w_in, g_q, g_ks, g_kw):
    T, D = x2.shape
    aw = N_HEADS * HEAD_DIM
    kvw = N_KV_HEADS * HEAD_DIM
    o = 0
    wq = w_in[:, o:o + aw]; o += aw
    wkv = w_in[:, o:o + 6 * kvw]; o += 6 * kvw
    wng = w_in[:, o:o + 3 * N_HEADS]; o += 3 * N_HEADS
    wcv = w_in[:, o:o + 3 * CONV_WIDTH]; o += 3 * CONV_WIDTH
    wmg = w_in[:, o:o + 2 * D]
    wng = jnp.pad(wng, ((0, 0), (0, LANES - 3 * N_HEADS)))
    wq, wkv, wng, wcv, wmg = (w.astype(BF16) for w in (wq, wkv, wng, wcv, wmg))
    gq = (jnp.tile(g_q, N_HEADS) * (HEAD_DIM ** -0.5)).reshape(1, aw)
    gk = jnp.concatenate([jnp.tile(g_ks, N_KV_HEADS), jnp.tile(g_kw, N_KV_HEADS)]).reshape(1, 2 * kvw)
    idx = np.arange(LANES) // HEAD_DIM
    bd = jnp.asarray((idx[:, None] == idx[None, :]).astype(np.float32) / HEAD_DIM, BF16)
    tm = ROW_TILE
    row = lambda w: pl.BlockSpec((tm, w), lambda i: (i, 0))
    full = lambda a: pl.BlockSpec(a.shape, lambda i: (0,) * a.ndim)
    ins = (x2, g1.reshape(1, D), wq, wkv, wng, wcv, wmg, gq, gk, bd)
    return pl.pallas_call(
        _inproj_body,
        grid=(T // tm,),
        in_specs=[row(D)] + [full(a) for a in ins[1:]],
        out_specs=[row(aw), row(6 * kvw), row(LANES), row(2 * CONV_WIDTH), row(2 * D)],
        out_shape=[jax.ShapeDtypeStruct((T, aw), BF16), jax.ShapeDtypeStruct((T, 6 * kvw), BF16),
                   jax.ShapeDtypeStruct((T, LANES), F32), jax.ShapeDtypeStruct((T, 2 * CONV_WIDTH), BF16),
                   jax.ShapeDtypeStruct((T, 2 * D), BF16)],
        compiler_params=_cparams(1),
        name="inproj",
    )(*ins)


def _compress_body(h_ref, w1_ref, pe_ref, b1_ref, w2_ref, b2_ref, g_ref, o_ref, *, normalize):
    hb = h_ref[0, 0]
    nc = hb.shape[0]
    a = _dot(hb, w1_ref[0])
    b = _dot(hb, w1_ref[1])
    c = _dot(pe_ref[0], w1_ref[0]) + _dot(pe_ref[1], w1_ref[1])
    pre = a + pltpu.roll(b, nc - 1, 0) + c[0:1, :] + b1_ref[...]
    hid = jax.nn.gelu(pre)
    out = _dot(hid.astype(BF16), w2_ref[...]) + b2_ref[...]
    if normalize:
        ms = jnp.mean(out * out, axis=-1, keepdims=True)
        out = out * lax.rsqrt(ms + EPS) * g_ref[...]
    o_ref[0, 0] = out


def _compress(hh, pe, w1, b1, w2, b2, gain, normalize):
    B, G, NC, HW = hh.shape
    w1s = w1.reshape(2, HW, CMP_HID).astype(BF16)
    pes = jnp.broadcast_to(pe.reshape(2, 1, HW), (2, 8, HW)).astype(BF16)
    full = lambda a: pl.BlockSpec(a.shape, lambda b, g: (0,) * a.ndim)
    ins = (hh, w1s, pes, b1.reshape(1, CMP_HID), w2.astype(BF16), b2.reshape(1, HEAD_DIM),
           gain.reshape(1, HEAD_DIM))
    return pl.pallas_call(
        functools.partial(_compress_body, normalize=normalize),
        grid=(B, G),
        in_specs=[pl.BlockSpec((1, 1, NC, HW), lambda b, g: (b, g, 0, 0))] + [full(a) for a in ins[1:]],
        out_specs=pl.BlockSpec((1, 1, NC, HEAD_DIM), lambda b, g: (b, g, 0, 0)),
        out_shape=jax.ShapeDtypeStruct((B, G, NC, HEAD_DIM), F32),
        compiler_params=_cparams(2),
        name="compress_norm" if normalize else "compress",
    )(*ins)


def _attn_body(qa_ref, g_ref, kca_ref, vc_ref, kas_ref, vs_ref, kaw_ref, vw_ref, selmap_ref, wbias_ref,
               o_ref, qaug_ref, m_ref, l_ref, acc_ref, *, n_sel):
    i = pl.program_id(2)
    q0 = i * Q_TILE
    rows = HEADS_PER_GROUP * Q_TILE
    qa = qa_ref[0].reshape(rows, LANES)

    nc = kca_ref.shape[2]
    s = _dot_nt(qa, kca_ref[0, 0])
    r_i = lax.broadcasted_iota(I32, (rows, nc), 0)
    c_i = lax.broadcasted_iota(I32, (rows, nc), 1)
    valid = (q0 + (r_i & (Q_TILE - 1))) >= c_i * CMP_STRIDE + (CMP_LEN - 1)
    s = jnp.where(valid, s, NEG)
    m = jnp.max(s, axis=-1, keepdims=True)
    p = jnp.where(valid, jnp.exp(s - m), 0.0)
    l = jnp.sum(p, axis=-1, keepdims=True)
    pc = p * jnp.where(l > 0.0, 1.0 / l, 0.0)
    o_c = _dot(pc.astype(BF16), vc_ref[0, 0])

    ps = pc[0:Q_TILE]
    for h in range(1, HEADS_PER_GROUP):
        ps = ps + pc[h * Q_TILE:(h + 1) * Q_TILE]
    ps_hi = ps.astype(BF16)
    ps_lo = (ps - ps_hi.astype(F32)).astype(BF16)
    imp = _dot(ps_hi, selmap_ref[...]) + _dot(ps_lo, selmap_ref[...])
    jb = lax.broadcasted_iota(I32, (Q_TILE, LANES), 1)
    cur = (q0 + lax.broadcasted_iota(I32, (Q_TILE, LANES), 0)) // SEL_BLK
    forced = (jb == 0) | (jb == cur) | (jb == cur - 1)
    score = jnp.where(jb > cur, NEG, jnp.where(forced, POS, imp))
    jbf = jb.astype(F32)
    sel = jnp.zeros((Q_TILE, LANES), F32)
    for _ in range(n_sel):
        mx = jnp.max(score, axis=-1, keepdims=True)
        first = jnp.min(jnp.where(score == mx, jbf, float(LANES)), axis=-1, keepdims=True)
        hit = jbf == first
        sel = jnp.where(hit, 1.0, sel)
        score = jnp.where(hit, -jnp.inf, score)
    selbias = jnp.where(sel > 0.0, 0.0, -MASK_BIG).astype(BF16)
    qaug_ref[:, 0:LANES] = qa
    for h in range(HEADS_PER_GROUP):
        qaug_ref[h * Q_TILE:(h + 1) * Q_TILE, LANES:2 * LANES] = selbias

    m_ref[...] = jnp.full(m_ref.shape, -3.0e38, F32)
    l_ref[...] = jnp.zeros(l_ref.shape, F32)
    acc_ref[...] = jnp.zeros(acc_ref.shape, F32)
    qaug = qaug_ref[...]
    rel = (q0 + (lax.broadcasted_iota(I32, (rows, KEY_TILE), 0) & (Q_TILE - 1))
           - lax.broadcasted_iota(I32, (rows, KEY_TILE), 1))

    def tile_step(kt, causal):
        k0 = pl.multiple_of(kt * KEY_TILE, KEY_TILE)
        sc = _dot_nt(qaug, kas_ref[0, 0, pl.ds(k0, KEY_TILE), :])
        if causal:
            sc = jnp.where(rel >= k0, sc, NEG)
        m_old = m_ref[...]
        m_new = jnp.maximum(m_old, jnp.max(sc, axis=-1, keepdims=True))
        alpha = jnp.exp(m_old - m_new)
        pp = jnp.exp(sc - m_new)
        l_ref[...] = alpha * l_ref[...] + jnp.sum(pp, axis=-1, keepdims=True)
        acc_ref[...] = alpha * acc_ref[...] + _dot(pp.astype(BF16), vs_ref[0, 0, pl.ds(k0, KEY_TILE), :])
        m_ref[...] = m_new

    n_full = q0 // KEY_TILE

    def loop_body(kt, carry):
        tile_step(kt, False)
        return carry

    lax.fori_loop(0, n_full, loop_body, 0)
    tile_step(n_full, True)
    o_s = acc_ref[...] * (1.0 / l_ref[...])

    wk = WINDOW + Q_TILE
    w0 = pl.multiple_of(q0, Q_TILE)
    sw = _dot_nt(qa, kaw_ref[0, 0, pl.ds(w0, wk), :])
    sw = (sw.reshape(HEADS_PER_GROUP, Q_TILE, wk) + wbias_ref[...][None]).reshape(rows, wk)
    mw = jnp.max(sw, axis=-1, keepdims=True)
    pw = jnp.exp(sw - mw)
    lw = jnp.sum(pw, axis=-1, keepdims=True)
    o_w = _dot(pw.astype(BF16), vw_ref[0, 0, pl.ds(w0, wk), :]) * (1.0 / lw)

    g = g_ref[0].reshape(rows, 3)
    o = g[:, 0:1] * o_c + g[:, 1:2] * o_s + g[:, 2:3] * o_w
    o_ref[0] = o.reshape(HEADS_PER_GROUP, Q_TILE, HEAD_DIM)


def _key_aug(pos, vbias):
    one = jnp.ones_like(pos, F32)
    return jnp.stack([one, one, (pos // 64 * 64).astype(F32), (pos % 64).astype(F32), vbias], axis=-1)


def _attention(qh, gates, kc, vc, ks, vs, kw, vw):
    B, H, S, hd = qh.shape
    G = N_KV_HEADS
    NC = kc.shape[2]
    n_blk = S // SEL_BLK
    assert n_blk <= LANES and S % KEY_TILE == 0
    n_sel = min(SEL_TOPN, n_blk)
    pad = LANES - hd - N_AUG
    slopes = jnp.asarray(2.0 ** (-8.0 * np.arange(1, H + 1) / H), F32)
    pos = jnp.arange(S, dtype=I32)
    sl = slopes[:, None]
    aq = jnp.stack([-sl * (pos // 64 * 64).astype(F32)[None], -sl * (pos % 64).astype(F32)[None],
                    jnp.broadcast_to(sl, (H, S)), jnp.broadcast_to(sl, (H, S)), jnp.ones((H, S), F32)], axis=-1)
    qa = jnp.concatenate([qh, jnp.broadcast_to(aq.astype(BF16)[None], (B, H, S, N_AUG)),
                          jnp.zeros((B, H, S, pad), BF16)], axis=-1)

    def with_aug(k, ak, extra=None):
        parts = [k.astype(BF16), jnp.broadcast_to(ak.astype(BF16)[None, None], k.shape[:3] + (N_AUG,)),
                 jnp.zeros(k.shape[:3] + (pad,), BF16)]
        if extra is not None:
            parts.append(jnp.broadcast_to(extra[None, None], k.shape[:3] + (extra.shape[-1],)))
        return jnp.concatenate(parts, axis=-1)

    zero_s = jnp.zeros((S,), F32)
    onehot = (pos[:, None] // SEL_BLK == jnp.arange(LANES, dtype=I32)[None, :]).astype(BF16)
    kas = with_aug(ks, _key_aug(pos, zero_s), onehot)
    cpos = jnp.arange(NC, dtype=I32) * CMP_STRIDE + (CMP_LEN - 1)
    kca = with_aug(kc, _key_aug(cpos, jnp.zeros((NC,), F32)))
    front = ((0, 0), (0, 0), (WINDOW, 0), (0, 0))
    wpos = jnp.concatenate([jnp.zeros((WINDOW,), I32), pos])
    wvb = jnp.concatenate([jnp.full((WINDOW,), -MASK_BIG, F32), zero_s])
    kaw = with_aug(jnp.pad(kw, front), _key_aug(wpos, wvb))
    vwp = jnp.pad(vw, front)
    ratio, span = SEL_BLK // CMP_STRIDE, CMP_LEN // CMP_STRIDE
    sm = np.zeros((NC, LANES), np.float32)
    for j in range(n_blk):
        for a in range(ratio):
            for b in range(span):
                n = ratio * j + a - b
                if 0 <= n < NC - 1:
                    sm[n, j] += 1.0
    selmap = jnp.asarray(sm, BF16)
    r = np.arange(Q_TILE)[:, None]
    c = np.arange(WINDOW + Q_TILE)[None, :]
    wbias = jnp.asarray(np.where((c > r) & (c <= r + WINDOW), 0.0, NEG), F32)

    hpg, rows = HEADS_PER_GROUP, HEADS_PER_GROUP * Q_TILE
    grp = lambda n, w: pl.BlockSpec((1, 1, n, w), lambda b, g, i: (b, g, 0, 0))
    const = lambda a: pl.BlockSpec(a.shape, lambda b, g, i: (0,) * a.ndim)
    return pl.pallas_call(
        functools.partial(_attn_body, n_sel=n_sel),
        grid=(B, G, S // Q_TILE),
        in_specs=[pl.BlockSpec((1, hpg, Q_TILE, LANES), lambda b, g, i: (b, g, i, 0)),
                  pl.BlockSpec((1, hpg, Q_TILE, 3), lambda b, g, i: (b, g, i, 0)),
                  grp(NC, LANES), grp(NC, hd), grp(S, 2 * LANES), grp(S, hd),
                  grp(S + WINDOW, LANES), grp(S + WINDOW, hd), const(selmap), const(wbias)],
        out_specs=pl.BlockSpec((1, hpg, Q_TILE, hd), lambda b, g, i: (b, g, i, 0)),
        out_shape=jax.ShapeDtypeStruct((B, H, S, hd), F32),
        scratch_shapes=[pltpu.VMEM((rows, 2 * LANES), BF16), pltpu.VMEM((rows, 1), F32),
                        pltpu.VMEM((rows, 1), F32), pltpu.VMEM((rows, hd), F32)],
        compiler_params=_cparams(3),
        name="nsa_attention",
    )(qa, gates, kca, vc.astype(BF16), kas, vs, kaw, vwp, selmap, wbias)


def _mixer_out_body(x_ref, oa_ref, cbv_ref, halo_ref, gab_ref, cw_ref, wpa_ref, wpb_ref, wo_ref, o_ref,
                    *, seq_len):
    i = pl.program_id(0)
    tm = x_ref.shape[0]
    cwd = CONV_WIDTH
    d = x_ref.shape[1]
    v = cbv_ref[:, cwd:2 * cwd].astype(F32)
    prev = halo_ref[:, cwd:2 * cwd].astype(F32)
    keep = ((i * tm) % seq_len != 0).astype(F32)
    p1 = prev[7:8, :] * keep
    p2 = prev[6:7, :] * keep
    ridx = lax.broadcasted_iota(I32, (tm, cwd), 0)
    v1 = jnp.where(ridx == 0, p1, pltpu.roll(v, 1, 0))
    v2 = jnp.where(ridx == 0, p2, jnp.where(ridx == 1, p1, pltpu.roll(v, 2, 0)))
    y = cw_ref[0:1, :] * v2 + cw_ref[1:2, :] * v1 + cw_ref[2:3, :] * v
    yb_in = (cbv_ref[:, 0:cwd].astype(F32) * y).astype(BF16)
    y_a = _dot(oa_ref[...], wpa_ref[...])
    y_b = _dot(yb_in, wpb_ref[...])
    merged = gab_ref[:, 0:d].astype(F32) * y_a + gab_ref[:, d:2 * d].astype(F32) * y_b
    o_ref[...] = x_ref[...] + _dot(merged.astype(BF16), wo_ref[...])


def _mixer_out(x2, oa, cbv, gab, conv_w, w_pa, w_pb, w_o, seq_len):
    T, D = x2.shape
    tm = ROW_TILE
    cw8 = jnp.pad(conv_w, ((0, 8 - CONV_K), (0, 0)))
    row = lambda w: pl.BlockSpec((tm, w), lambda i: (i, 0))
    full = lambda a: pl.BlockSpec(a.shape, lambda i: (0,) * a.ndim)
    halo = pl.BlockSpec((8, cbv.shape[1]), lambda i: (jnp.maximum(i * (tm // 8) - 1, 0), 0))
    wts = (cw8, w_pa.astype(BF16), w_pb.astype(BF16), w_o.astype(BF16))
    return pl.pallas_call(
        functools.partial(_mixer_out_body, seq_len=seq_len),
        grid=(T // tm,),
        in_specs=[row(D), row(oa.shape[1]), row(cbv.shape[1]), halo, row(gab.shape[1])] + [full(a) for a in wts],
        out_specs=row(D),
        out_shape=jax.ShapeDtypeStruct((T, D), F32),
        compiler_params=_cparams(1),
        name="mixer_out",
    )(x2, oa, cbv, cbv, gab, *wts)


def _router_body(x_ref, g2_ref, whi_ref, wlo_ref, br_ref, tri_ref, h_out, mi_out, mf_out, cnt_out):
    i = pl.program_id(0)

    @pl.when(i == 0)
    def _():
        cnt_out[...] = jnp.zeros(cnt_out.shape, F32)

    x = x_ref[...]
    tm = x.shape[0]
    ms = jnp.mean(x * x, axis=-1, keepdims=True)
    h = x * lax.rsqrt(ms + EPS) * g2_ref[...]
    h_out[...] = h
    h_hi = h.astype(BF16)
    h_lo = (h - h_hi.astype(F32)).astype(BF16)
    logits = (_dot(h_hi, whi_ref[...]) + _dot(h_lo, whi_ref[...]) + _dot(h_hi, wlo_ref[...])) + br_ref[...]
    lane = lax.broadcasted_iota(I32, (tm, LANES), 1)
    lanef = lane.astype(F32)
    work = jnp.where(lane < N_EXPERTS, logits, -jnp.inf)
    vals, hits = [], []
    for _ in range(TOP_K):
        mx = jnp.max(work, axis=-1, keepdims=True)
        first = jnp.min(jnp.where(work == mx, lanef, float(LANES)), axis=-1, keepdims=True)
        hit = lanef == first
        vals.append(mx)
        hits.append(hit)
        work = jnp.where(hit, -jnp.inf, work)
    ex = [jnp.exp(v - vals[0]) for v in vals]
    den = ex[0]
    for e in ex[1:]:
        den = den + e
    cnt = jnp.zeros((tm, LANES), F32)
    for hit in hits:
        cnt = cnt + hit.astype(F32)
    before = _dot(tri_ref[...], cnt.astype(BF16)) + cnt_out[0:1, :]
    mi = jnp.zeros((tm, LANES), F32)
    mf = jnp.zeros((tm, LANES), F32)
    for k, hit in enumerate(hits):
        e_k = jnp.sum(jnp.where(hit, lanef, 0.0), axis=-1, keepdims=True)
        r_k = jnp.sum(jnp.where(hit, before, 0.0), axis=-1, keepdims=True)
        mi = jnp.where(lane == k, e_k, jnp.where(lane == TOP_K + k, r_k, mi))
        mf = jnp.where(lane == k, ex[k] / den, mf)
    mi_out[...] = mi.astype(I32)
    mf_out[...] = mf
    cnt_out[...] = cnt_out[...] + jnp.sum(cnt, axis=0, keepdims=True)


def _router(x1, g2, w_r, b_r):
    T, D = x1.shape
    tm = ROW_TILE
    wpad = jnp.pad(w_r, ((0, 0), (0, LANES - N_EXPERTS)))
    whi = wpad.astype(BF16)
    wlo = (wpad - whi.astype(F32)).astype(BF16)
    br = jnp.pad(b_r, (0, LANES - N_EXPERTS)).reshape(1, LANES)
    tri = jnp.asarray(np.tril(np.ones((tm, tm), np.float32), -1), BF16)
    row = lambda w: pl.BlockSpec((tm, w), lambda i: (i, 0))
    full = lambda a: pl.BlockSpec(a.shape, lambda i: (0,) * a.ndim)
    ins = (x1, g2.reshape(1, D), whi, wlo, br, tri)
    return pl.pallas_call(
        _router_body,
        grid=(T // tm,),
        in_specs=[row(D)] + [full(a) for a in ins[1:]],
        out_specs=[row(D), row(LANES), row(LANES), pl.BlockSpec((8, LANES), lambda i: (0, 0))],
        out_shape=[jax.ShapeDtypeStruct((T, D), F32), jax.ShapeDtypeStruct((T, LANES), I32),
                   jax.ShapeDtypeStruct((T, LANES), F32), jax.ShapeDtypeStruct((8, LANES), F32)],
        compiler_params=_cparams(1),
        name="router",
    )(*ins)


DISPATCH_TILE = 512
DMA_UNROLL = 8


def _dispatch_body(dest_ref, h_ref, zero_hbm, o_hbm, sem):
    del zero_hbm

    def row_copy(r, d):
        return pltpu.make_async_copy(h_ref.at[pl.ds(r, 1)], o_hbm.at[pl.ds(d, 1)], sem)

    def start(r, c):
        for k in range(TOP_K):
            row_copy(r, dest_ref[0, 0, r * TOP_K + k]).start()
        return c

    def wait(r, c):
        for k in range(TOP_K):
            row_copy(0, 0).wait()
        return c

    lax.fori_loop(0, DISPATCH_TILE, start, 0, unroll=DMA_UNROLL)
    lax.fori_loop(0, DISPATCH_TILE, wait, 0, unroll=DMA_UNROLL)


def _dispatch(h2, dest, n_rows):
    T, D = h2.shape
    td = DISPATCH_TILE
    dest3 = dest.reshape(T // td, 1, td * TOP_K)
    zeros = jnp.zeros((n_rows, D), h2.dtype)
    return pl.pallas_call(
        _dispatch_body,
        grid=(T // td,),
        in_specs=[pl.BlockSpec((1, 1, td * TOP_K), lambda i: (i, 0, 0), memory_space=pltpu.SMEM),
                  pl.BlockSpec((td, D), lambda i: (i, 0)), pl.BlockSpec(memory_space=pl.ANY)],
        out_specs=pl.BlockSpec(memory_space=pl.ANY),
        out_shape=jax.ShapeDtypeStruct((n_rows, D), h2.dtype),
        scratch_shapes=[pltpu.SemaphoreType.DMA(())],
        input_output_aliases={2: 0},
        compiler_params=_cparams(1),
        name="dispatch",
    )(dest3, h2, zeros)


def _expert_body(ce_ref, nu_ref, x_ref, wgu_ref, bgu_ref, wdn_ref, bdn_ref, o_ref):
    c = pl.program_id(0)
    dff = wdn_ref.shape[1]

    @pl.when(c < nu_ref[0])
    def _():
        gu = _dot(x_ref[...].astype(BF16), wgu_ref[0]) + bgu_ref[0]
        g = jnp.minimum(gu[:, 0:dff], SWIGLU_LIMIT)
        u = jnp.clip(gu[:, dff:2 * dff], -SWIGLU_LIMIT, SWIGLU_LIMIT)
        act = (u + 1.0) * (g * jax.nn.sigmoid(SWIGLU_ALPHA * g))
        o_ref[...] = _dot(act.astype(BF16), wdn_ref[0]) + bdn_ref[0]

    @pl.when(c >= nu_ref[0])
    def _():
        o_ref[...] = jnp.zeros(o_ref.shape, F32)


def _experts(hperm, chunk_e, n_used, w_gu, b_gu, w_dn, b_dn):
    P, D = hperm.shape
    E, _, F2 = w_gu.shape
    dff = F2 // 2
    n_chunks = P // MOE_CHUNK
    grid_spec = pltpu.PrefetchScalarGridSpec(
        num_scalar_prefetch=2,
        grid=(n_chunks,),
        in_specs=[pl.BlockSpec((MOE_CHUNK, D), lambda c, ce, nu: (c, 0)),
                  pl.BlockSpec((1, D, F2), lambda c, ce, nu: (ce[c], 0, 0)),
                  pl.BlockSpec((1, 1, F2), lambda c, ce, nu: (ce[c], 0, 0)),
                  pl.BlockSpec((1, dff, D), lambda c, ce, nu: (ce[c], 0, 0)),
                  pl.BlockSpec((1, 1, D), lambda c, ce, nu: (ce[c], 0, 0))],
        out_specs=pl.BlockSpec((MOE_CHUNK, D), lambda c, ce, nu: (c, 0)),
    )
    return pl.pallas_call(
        _expert_body,
        grid_spec=grid_spec,
        out_shape=jax.ShapeDtypeStruct((P, D), F32),
        compiler_params=_cparams(1),
        name="experts",
    )(chunk_e, n_used, hperm, w_gu.astype(BF16), b_gu.reshape(E, 1, F2), w_dn.astype(BF16),
      b_dn.reshape(E, 1, D))


COMBINE_TILE = 256


def _combine_body(dest_ref, x_ref, w_ref, y_hbm, o_ref, buf_ref, sem):
    def row_copy(r, k, d):
        return pltpu.make_async_copy(y_hbm.at[pl.ds(d, 1)], buf_ref.at[k, pl.ds(r, 1)], sem)

    def start(r, c):
        for k in range(TOP_K):
            row_copy(r, k, dest_ref[0, 0, r * TOP_K + k]).start()
        return c

    def wait(r, c):
        for k in range(TOP_K):
            row_copy(0, 0, 0).wait()
        return c

    lax.fori_loop(0, COMBINE_TILE, start, 0, unroll=DMA_UNROLL)
    lax.fori_loop(0, COMBINE_TILE, wait, 0, unroll=DMA_UNROLL)
    out = x_ref[...]
    for k in range(TOP_K):
        out = out + w_ref[:, k:k + 1] * buf_ref[k]
    o_ref[...] = out


def _combine(x1, gate_w, dest, ys):
    T, D = x1.shape
    tc = COMBINE_TILE
    dest3 = dest.reshape(T // tc, 1, tc * TOP_K)
    row = lambda w: pl.BlockSpec((tc, w), lambda i: (i, 0))
    return pl.pallas_call(
        _combine_body,
        grid=(T // tc,),
        in_specs=[pl.BlockSpec((1, 1, tc * TOP_K), lambda i: (i, 0, 0), memory_space=pltpu.SMEM),
                  row(D), row(LANES), pl.BlockSpec(memory_space=pl.ANY)],
        out_specs=row(D),
        out_shape=jax.ShapeDtypeStruct((T, D), F32),
        scratch_shapes=[pltpu.VMEM((TOP_K, tc, D), F32), pltpu.SemaphoreType.DMA(())],
        compiler_params=_cparams(1),
        name="combine",
    )(dest3, x1, gate_w, ys)


def _mixer(x2, B, S, g_norm1, w_in, g_q, g_kc, g_ks, g_kw, pe_k, ck_w1, ck_b1, ck_w2, ck_b2,
           pe_v, cv_w1, cv_b1, cv_w2, cv_b2, conv_w, w_pa, w_pb, w_o):
    T, D = x2.shape
    G, H, hd = N_KV_HEADS, N_HEADS, HEAD_DIM
    q, kv, gates, cbv, gab = _inproj(x2, g_norm1, w_in, g_q, g_ks, g_kw)
    qh = q.reshape(B, S, H, hd).transpose(0, 2, 1, 3)
    kv6 = kv.reshape(B, S, 6, G, hd)
    heads = lambda j: kv6[:, :, j].transpose(0, 2, 1, 3)
    half = lambda j: (kv6[:, :, j].reshape(B, S // CMP_STRIDE, CMP_STRIDE, G, hd)
                      .transpose(0, 3, 1, 2, 4).reshape(B, G, S // CMP_STRIDE, CMP_STRIDE * hd))
    kc = _compress(half(0), pe_k, ck_w1, ck_b1, ck_w2, ck_b2, g_kc, True)
    vc = _compress(half(1), pe_v, cv_w1, cv_b1, cv_w2, cv_b2, jnp.ones((hd,), F32), False)
    gat = gates[:, :3 * H].reshape(B, S, H, 3).transpose(0, 2, 1, 3)
    o = _attention(qh, gat, kc, vc, heads(2), heads(3), heads(4), heads(5))
    oa = o.transpose(0, 2, 1, 3).reshape(T, H * hd).astype(BF16)
    return _mixer_out(x2, oa, cbv, gab, conv_w, w_pa, w_pb, w_o, S)


def _moe(x1, g_norm2, w_r, b_r, w_gu, b_gu, w_dn, b_dn):
    T, D = x1.shape
    h2, mi, mf, cnt = _router(x1, g_norm2, w_r, b_r)
    top_e = mi[:, 0:TOP_K]
    rank = mi[:, TOP_K:2 * TOP_K]
    counts = cnt[0, :N_EXPERTS].astype(I32)
    padded = (counts + MOE_CHUNK - 1) // MOE_CHUNK * MOE_CHUNK
    pend = jnp.cumsum(padded)
    poffs = pend - padded
    dest = (poffs[top_e] + rank).reshape(-1)
    n_chunks = (T * TOP_K + MOE_CHUNK - 1) // MOE_CHUNK + N_EXPERTS
    chunk_e = jnp.minimum(jnp.searchsorted(pend, jnp.arange(n_chunks, dtype=I32) * MOE_CHUNK, side='right'),
                          N_EXPERTS - 1).astype(I32)
    n_used = (pend[-1:] // MOE_CHUNK).astype(I32)
    hperm = _dispatch(h2, dest, n_chunks * MOE_CHUNK)
    ys = _experts(hperm, chunk_e, n_used, w_gu, b_gu, w_dn, b_dn)
    return _combine(x1, mf, dest, ys)


def kernel(x, g_norm1, w_in, g_q, g_kc, g_ks, g_kw, pe_k, ck_w1, ck_b1, ck_w2, ck_b2, pe_v, cv_w1, cv_b1,
           cv_w2, cv_b2, conv_w, w_pa, w_pb, w_o, g_norm2, w_r, b_r, w_gu, b_gu, w_dn, b_dn):
    B, S, D = x.shape
    x2 = x.reshape(B * S, D)
    for l in range(g_norm1.shape[0]):
        x2 = _mixer(x2, B, S, g_norm1[l], w_in[l], g_q[l], g_kc[l], g_ks[l], g_kw[l], pe_k[l], ck_w1[l],
                    ck_b1[l], ck_w2[l], ck_b2[l], pe_v[l], cv_w1[l], cv_b1[l], cv_w2[l], cv_b2[l],
                    conv_w[l], w_pa[l], w_pb[l], w_o[l])
        x2 = _moe(x2, g_norm2[l], w_r[l], b_r[l], w_gu[l], b_gu[l], w_dn[l], b_dn[l])
    return x2.reshape(B, S, D)
```

```python
import functools

import numpy as np
import jax
import jax.numpy as jnp
from jax import lax
from jax.experimental import pallas as pl
from jax.experimental.pallas import tpu as pltpu

F32 = jnp.float32
BF16 = jnp.bfloat16
I32 = jnp.int32

N_HEADS = 8
HEAD_DIM = 64
N_KV_HEADS = 2
HEADS_PER_GROUP = N_HEADS // N_KV_HEADS
CMP_LEN = 32
CMP_STRIDE = 16
CMP_HID = 256
SEL_BLK = 64
SEL_TOPN = 16
WINDOW = 512
CONV_WIDTH = 512
CONV_K = 3
N_EXPERTS = 32
TOP_K = 4
SWIGLU_LIMIT = 7.0
SWIGLU_ALPHA = 1.702
MOE_CHUNK = 512
EPS = 1e-6
NEG = -1e30
POS = 1e30
MASK_BIG = 2.0 ** 100

LANES = 128
Q_TILE = 128
KEY_TILE = 512
N_AUG = 5
ROW_TILE = 512
VMEM_LIMIT = 56 * 1024 * 1024
GROUP_LANES = HEADS_PER_GROUP * Q_TILE
WIN_KEYS = WINDOW + Q_TILE


def _cparams(n_axes):
    return pltpu.CompilerParams(dimension_semantics=("arbitrary",) * n_axes,
                                vmem_limit_bytes=VMEM_LIMIT)


def _dot(a, b):
    return jnp.dot(a, b, preferred_element_type=F32)


def _dot_nt(a, b):
    return lax.dot_general(a, b, (((1,), (1,)), ((), ())), preferred_element_type=F32)


def _rms_pairs(v, bd):
    ss = _dot((v * v).astype(BF16), bd)
    return v * lax.rsqrt(ss + EPS)


def _inproj_body(x_ref, g1_ref, wq_ref, wkv_ref, wng_ref, wcv_ref, wmg_ref, gq_ref, gk_ref, bd_ref,
                 qtab_ref, kstab_ref, kwtab_ref,
                 qa_out, kcv_out, kas_out, kaw_out, vsw_out, gate_out, cbv_out, gab_out):
    x = x_ref[...]
    tm = x.shape[0]
    ms = jnp.mean(x * x, axis=-1, keepdims=True)
    h = (x * lax.rsqrt(ms + EPS) * g1_ref[...]).astype(BF16)
    bd = bd_ref[...]
    low = lax.broadcasted_iota(I32, (tm, LANES), 1) < HEAD_DIM

    def place(pair, tab_ref, out_ref, base, slot):
        for j, src in enumerate((pair, pltpu.roll(pair, HEAD_DIM, 1))):
            o = base + j * slot
            out_ref[:, o:o + LANES] = jnp.where(low, src, tab_ref[:, o:o + LANES].astype(F32)).astype(BF16)

    q = _dot(h, wq_ref[...])
    for c in range(N_HEADS * HEAD_DIM // LANES):
        sl = slice(c * LANES, (c + 1) * LANES)
        place(_rms_pairs(q[:, sl], bd) * gq_ref[:, sl], qtab_ref, qa_out, 2 * c * LANES, LANES)
    kv = _dot(h, wkv_ref[...])
    kcv_out[...] = kv[:, 0:256].astype(BF16)
    place(_rms_pairs(kv[:, 256:384], bd) * gk_ref[:, 0:128], kstab_ref, kas_out, 0, 2 * LANES)
    for g in range(N_KV_HEADS):
        o = (2 * g + 1) * LANES
        kas_out[:, o:o + LANES] = kstab_ref[:, o:o + LANES]
    place(_rms_pairs(kv[:, 512:640], bd) * gk_ref[:, 128:256], kwtab_ref, kaw_out, 0, LANES)
    vsw_out[:, 0:128] = kv[:, 384:512].astype(BF16)
    vsw_out[:, 128:256] = kv[:, 640:768].astype(BF16)
    gate_out[...] = jax.nn.sigmoid(_dot(h, wng_ref[...]))
    cv = _dot(h, wcv_ref[...])
    cw = CONV_WIDTH
    cbv_out[:, 0:cw] = cv[:, 0:cw].astype(BF16)
    cbv_out[:, cw:2 * cw] = (cv[:, cw:2 * cw] * cv[:, 2 * cw:3 * cw]).astype(BF16)
    gab_out[...] = jax.nn.sigmoid(_dot(h, wmg_ref[...])).astype(BF16)


def _key_aug(pos, vbias):
    one = jnp.ones_like(pos, F32)
    return jnp.stack([one, one, (pos // 64 * 64).astype(F32), (pos % 64).astype(F32), vbias], axis=-1)


def _slot_table(aug, slot, extra=None):
    S, n, _ = aug.shape
    parts = [jnp.zeros((S, n, HEAD_DIM), F32), aug, jnp.zeros((S, n, LANES - HEAD_DIM - N_AUG), F32)]
    if extra is not None:
        parts.append(jnp.broadcast_to(extra[:, None, :], (S, n, extra.shape[-1])))
    return jnp.concatenate(parts, axis=-1).reshape(S, n * slot).astype(BF16)


def _inproj(x2, g1, w_in, g_q, g_ks, g_kw, S):
    T, D = x2.shape
    H, G = N_HEADS, N_KV_HEADS
    aw = H * HEAD_DIM
    kvw = G * HEAD_DIM
    o = 0
    wq = w_in[:, o:o + aw]; o += aw
    wkv = w_in[:, o:o + 6 * kvw]; o += 6 * kvw
    wng = w_in[:, o:o + 3 * H]; o += 3 * H
    wcv = w_in[:, o:o + 3 * CONV_WIDTH]; o += 3 * CONV_WIDTH
    wmg = w_in[:, o:o + 2 * D]
    wng = jnp.pad(wng, ((0, 0), (0, LANES - 3 * H)))
    wq, wkv, wng, wcv, wmg = (w.astype(BF16) for w in (wq, wkv, wng, wcv, wmg))
    gq = (jnp.tile(g_q, H) * (HEAD_DIM ** -0.5)).reshape(1, aw)
    gk = jnp.concatenate([jnp.tile(g_ks, G), jnp.tile(g_kw, G)]).reshape(1, 2 * kvw)
    idx = np.arange(LANES) // HEAD_DIM
    bd = jnp.asarray((idx[:, None] == idx[None, :]).astype(np.float32) / HEAD_DIM, BF16)
    pos = jnp.arange(S, dtype=I32)
    hi = (pos // 64 * 64).astype(F32)[:, None]
    lo = (pos % 64).astype(F32)[:, None]
    sl = jnp.asarray(2.0 ** (-8.0 * np.arange(1, H + 1) / H), F32)[None, :]
    aq = jnp.stack([-sl * hi, -sl * lo, jnp.broadcast_to(sl, (S, H)), jnp.broadcast_to(sl, (S, H)),
                    jnp.ones((S, H), F32)], axis=-1)
    ak = jnp.broadcast_to(_key_aug(pos, jnp.zeros((S,), F32))[:, None, :], (S, G, N_AUG))
    onehot = (pos[:, None] // SEL_BLK == jnp.arange(LANES, dtype=I32)[None, :]).astype(F32)
    qtab = _slot_table(aq, LANES)
    kstab = _slot_table(ak, 2 * LANES, onehot)
    kwtab = _slot_table(ak, LANES)
    tm = ROW_TILE
    nst = S // tm
    row = lambda w: pl.BlockSpec((tm, w), lambda i: (i, 0))
    full = lambda a: pl.BlockSpec(a.shape, lambda i: (0,) * a.ndim)
    tab = lambda a: pl.BlockSpec((tm, a.shape[1]), lambda i: (i % nst, 0))
    ins = (x2, g1.reshape(1, D), wq, wkv, wng, wcv, wmg, gq, gk, bd, qtab, kstab, kwtab)
    widths = (H * LANES, 2 * kvw, G * 2 * LANES, G * LANES, 2 * kvw, LANES, 2 * CONV_WIDTH, 2 * D)
    dtypes = (BF16, BF16, BF16, BF16, BF16, F32, BF16, BF16)
    return pl.pallas_call(
        _inproj_body,
        grid=(T // tm,),
        in_specs=[row(D)] + [full(a) for a in ins[1:10]] + [tab(a) for a in ins[10:]],
        out_specs=[row(w) for w in widths],
        out_shape=[jax.ShapeDtypeStruct((T, w), dt) for w, dt in zip(widths, dtypes)],
        compiler_params=_cparams(1),
        name="inproj",
    )(*ins)


def _compress_body(h_ref, w1_ref, pe_ref, b1_ref, w2_ref, b2_ref, g_ref, tab_ref, o_ref, *, for_keys):
    hb = h_ref[0, 0]
    nc = hb.shape[0]
    a = _dot(hb, w1_ref[0])
    b = _dot(hb, w1_ref[1])
    c = _dot(pe_ref[0], w1_ref[0]) + _dot(pe_ref[1], w1_ref[1])
    pre = a + pltpu.roll(b, nc - 1, 0) + c[0:1, :] + b1_ref[...]
    hid = jax.nn.gelu(pre)
    out = _dot(hid.astype(BF16), w2_ref[...]) + b2_ref[...]
    if for_keys:
        ms = jnp.sum(out * out, axis=-1, keepdims=True) * (1.0 / HEAD_DIM)
        out = out * lax.rsqrt(ms + EPS) * g_ref[...]
        low = lax.broadcasted_iota(I32, out.shape, 1) < HEAD_DIM
        o_ref[0, 0] = jnp.where(low, out, tab_ref[...]).astype(BF16)
    else:
        o_ref[0, 0] = out.T[0:HEAD_DIM, :].astype(BF16)


def _compress(hh, pe, w1, b1, w2, b2, gain, for_keys):
    B, G, NC, HW = hh.shape
    w1s = w1.reshape(2, HW, CMP_HID).astype(BF16)
    pes = jnp.broadcast_to(pe.reshape(2, 1, HW), (2, 8, HW)).astype(BF16)
    padl = lambda a: jnp.pad(a, ((0, 0), (0, LANES - HEAD_DIM)))
    cpos = jnp.arange(NC, dtype=I32) * CMP_STRIDE + (CMP_LEN - 1)
    tab = jnp.concatenate([jnp.zeros((NC, HEAD_DIM), F32), _key_aug(cpos, jnp.zeros((NC,), F32)),
                           jnp.zeros((NC, LANES - HEAD_DIM - N_AUG), F32)], axis=-1)
    full = lambda a: pl.BlockSpec(a.shape, lambda b, g: (0,) * a.ndim)
    ins = (hh, w1s, pes, b1.reshape(1, CMP_HID), padl(w2).astype(BF16), padl(b2.reshape(1, HEAD_DIM)),
           padl(gain.reshape(1, HEAD_DIM)), tab)
    oshape = (B, G, NC, LANES) if for_keys else (B, G, HEAD_DIM, NC)
    return pl.pallas_call(
        functools.partial(_compress_body, for_keys=for_keys),
        grid=(B, G),
        in_specs=[pl.BlockSpec((1, 1, NC, HW), lambda b, g: (b, g, 0, 0))] + [full(a) for a in ins[1:]],
        out_specs=pl.BlockSpec((1, 1) + oshape[2:], lambda b, g: (b, g, 0, 0)),
        out_shape=jax.ShapeDtypeStruct(oshape, BF16),
        compiler_params=_cparams(2),
        name="compress_keys" if for_keys else "compress_values",
    )(*ins)


def _attn_body(qa_ref, g_ref, kca_ref, vct_ref, kas_ref, vst_ref, kaw_ref, vwt_ref, selmapt_ref, wbias_ref,
               o_ref, qaug_ref, m_ref, l_ref, acc_ref, *, n_sel):
    i = pl.program_id(2)
    q0 = i * Q_TILE
    gl = GROUP_LANES
    for h in range(HEADS_PER_GROUP):
        qaug_ref[h * Q_TILE:(h + 1) * Q_TILE, 0:LANES] = qa_ref[0, :, h * LANES:(h + 1) * LANES]
    qa = qaug_ref[:, 0:LANES]

    nc = kca_ref.shape[2]
    s = _dot_nt(kca_ref[0, 0], qa)
    n_i = lax.broadcasted_iota(I32, (nc, gl), 0)
    t_i = q0 + (lax.broadcasted_iota(I32, (nc, gl), 1) & (Q_TILE - 1))
    valid = t_i >= n_i * CMP_STRIDE + (CMP_LEN - 1)
    s = jnp.where(valid, s, NEG)
    m = jnp.max(s, axis=0, keepdims=True)
    p = jnp.where(valid, jnp.exp(s - m), 0.0)
    l = jnp.sum(p, axis=0, keepdims=True)
    pc = p * jnp.where(l > 0.0, 1.0 / l, 0.0)
    o_c = _dot(vct_ref[0, 0], pc.astype(BF16))

    ps = pc[:, 0:Q_TILE]
    for h in range(1, HEADS_PER_GROUP):
        ps = ps + pc[:, h * Q_TILE:(h + 1) * Q_TILE]
    ps_hi = ps.astype(BF16)
    ps_lo = (ps - ps_hi.astype(F32)).astype(BF16)
    imp = _dot(selmapt_ref[...], ps_hi) + _dot(selmapt_ref[...], ps_lo)
    jb = lax.broadcasted_iota(I32, (LANES, Q_TILE), 0)
    cur = (q0 + lax.broadcasted_iota(I32, (LANES, Q_TILE), 1)) // SEL_BLK
    forced = (jb == 0) | (jb == cur) | (jb == cur - 1)
    score = jnp.where(jb > cur, NEG, jnp.where(forced, POS, imp))
    jbf = jb.astype(F32)
    sel = jnp.zeros((LANES, Q_TILE), F32)
    for _ in range(n_sel):
        mx = jnp.max(score, axis=0, keepdims=True)
        first = jnp.min(jnp.where(score == mx, jbf, float(LANES)), axis=0, keepdims=True)
        hit = jbf == first
        sel = jnp.where(hit, 1.0, sel)
        score = jnp.where(hit, -jnp.inf, score)
    selbias = jnp.where(sel > 0.0, 0.0, -MASK_BIG).T.astype(BF16)
    for h in range(HEADS_PER_GROUP):
        qaug_ref[h * Q_TILE:(h + 1) * Q_TILE, LANES:2 * LANES] = selbias

    w0 = pl.multiple_of(jnp.maximum(q0 - WINDOW, 0), Q_TILE)
    sw = _dot_nt(kaw_ref[0, pl.ds(w0, WIN_KEYS), :], qa)
    wb = wbias_ref[jnp.minimum(i, WINDOW // Q_TILE)]
    sw = sw + jnp.concatenate([wb] * HEADS_PER_GROUP, axis=1)
    mw = jnp.max(sw, axis=0, keepdims=True)
    pw = jnp.exp(sw - mw)
    lw = jnp.sum(pw, axis=0, keepdims=True)
    c0 = w0 // Q_TILE
    vw = jnp.concatenate([vwt_ref[0, 0, c0 + j] for j in range(WIN_KEYS // Q_TILE)], axis=1)
    o_w = _dot(vw, pw.astype(BF16)) * (1.0 / lw)

    m_ref[...] = jnp.full(m_ref.shape, -3.0e38, F32)
    l_ref[...] = jnp.zeros(l_ref.shape, F32)
    acc_ref[...] = jnp.zeros(acc_ref.shape, F32)
    qaug = qaug_ref[...]

    def tile_step(kt, causal):
        k0 = pl.multiple_of(kt * KEY_TILE, KEY_TILE)
        sc = _dot_nt(kas_ref[0, pl.ds(k0, KEY_TILE), :], qaug)
        if causal:
            key = k0 + lax.broadcasted_iota(I32, (KEY_TILE, gl), 0)
            qry = q0 + (lax.broadcasted_iota(I32, (KEY_TILE, gl), 1) & (Q_TILE - 1))
            sc = jnp.where(key <= qry, sc, NEG)
        m_old = m_ref[...]
        m_new = jnp.maximum(m_old, jnp.max(sc, axis=0, keepdims=True))
        alpha = jnp.exp(m_old - m_new)
        pp = jnp.exp(sc - m_new)
        l_ref[...] = alpha * l_ref[...] + jnp.sum(pp, axis=0, keepdims=True)
        acc_ref[...] = alpha * acc_ref[...] + _dot(vst_ref[0, 0, kt], pp.astype(BF16))
        m_ref[...] = m_new

    n_full = q0 // KEY_TILE

    def loop_body(kt, carry):
        tile_step(kt, False)
        return carry

    lax.fori_loop(0, n_full, loop_body, 0)
    tile_step(n_full, True)
    o_s = acc_ref[...] * (1.0 / l_ref[...])

    g = g_ref[0, 0]
    outs = []
    for h in range(HEADS_PER_GROUP):
        sl = slice(h * Q_TILE, (h + 1) * Q_TILE)
        outs.append(g[3 * h:3 * h + 1, :] * o_c[:, sl] + g[3 * h + 1:3 * h + 2, :] * o_s[:, sl]
                    + g[3 * h + 2:3 * h + 3, :] * o_w[:, sl])
    o_ref[0] = jnp.concatenate(outs, axis=0).T.astype(BF16)


def _attention(qa, gates_t, kca, vct, kas, vst, kaw, vwt, B, S):
    G, hd = N_KV_HEADS, HEAD_DIM
    NC = kca.shape[2]
    n_blk = S // SEL_BLK
    assert n_blk <= LANES and S % KEY_TILE == 0 and S >= WIN_KEYS
    n_sel = min(SEL_TOPN, n_blk)
    ratio, span = SEL_BLK // CMP_STRIDE, CMP_LEN // CMP_STRIDE
    sm = np.zeros((LANES, NC), np.float32)
    for j in range(n_blk):
        for a in range(ratio):
            for b in range(span):
                n = ratio * j + a - b
                if 0 <= n < NC - 1:
                    sm[j, n] += 1.0
    selmapt = jnp.asarray(sm, BF16)
    c = np.arange(WIN_KEYS)[:, None]
    r = np.arange(Q_TILE)[None, :]
    offs = np.arange(WINDOW // Q_TILE + 1)[:, None, None] * Q_TILE
    wbias = jnp.asarray(np.where((c - r <= offs) & (c - r > offs - WINDOW), 0.0, NEG), F32)

    hpg = HEADS_PER_GROUP
    grp = lambda *blk: pl.BlockSpec((1, 1) + blk, lambda b, g, i: (b, g) + (0,) * len(blk))
    seq = lambda w: pl.BlockSpec((1, S, w), lambda b, g, i: (b, 0, g))
    const = lambda a: pl.BlockSpec(a.shape, lambda b, g, i: (0,) * a.ndim)
    return pl.pallas_call(
        functools.partial(_attn_body, n_sel=n_sel),
        grid=(B, G, S // Q_TILE),
        in_specs=[pl.BlockSpec((1, Q_TILE, hpg * LANES), lambda b, g, i: (b, i, g)),
                  pl.BlockSpec((1, 1, 16, Q_TILE), lambda b, g, i: (b, g, 0, i)),
                  grp(NC, LANES), grp(hd, NC), seq(2 * LANES), grp(S // KEY_TILE, hd, KEY_TILE),
                  seq(LANES), grp(S // Q_TILE, hd, Q_TILE), const(selmapt), const(wbias)],
        out_specs=pl.BlockSpec((1, Q_TILE, hpg * hd), lambda b, g, i: (b, i, g)),
        out_shape=jax.ShapeDtypeStruct((B, S, N_HEADS * hd), BF16),
        scratch_shapes=[pltpu.VMEM((GROUP_LANES, 2 * LANES), BF16), pltpu.VMEM((1, GROUP_LANES), F32),
                        pltpu.VMEM((1, GROUP_LANES), F32), pltpu.VMEM((hd, GROUP_LANES), F32)],
        compiler_params=_cparams(3),
        name="nsa_attention",
    )(qa, gates_t, kca, vct, kas, vst, kaw, vwt, selmapt, wbias)


def _mixer_out_body(x_ref, oa_ref, cbv_ref, halo_ref, gab_ref, cw_ref, wpa_ref, wpb_ref, wo_ref, o_ref,
                    *, seq_len):
    i = pl.program_id(0)
    tm = x_ref.shape[0]
    cwd = CONV_WIDTH
    d = x_ref.shape[1]
    v = cbv_ref[:, cwd:2 * cwd].astype(F32)
    prev = halo_ref[:, cwd:2 * cwd].astype(F32)
    keep = ((i * tm) % seq_len != 0).astype(F32)
    p1 = prev[7:8, :] * keep
    p2 = prev[6:7, :] * keep
    ridx = lax.broadcasted_iota(I32, (tm, cwd), 0)
    v1 = jnp.where(ridx == 0, p1, pltpu.roll(v, 1, 0))
    v2 = jnp.where(ridx == 0, p2, jnp.where(ridx == 1, p1, pltpu.roll(v, 2, 0)))
    y = cw_ref[0:1, :] * v2 + cw_ref[1:2, :] * v1 + cw_ref[2:3, :] * v
    yb_in = (cbv_ref[:, 0:cwd].astype(F32) * y).astype(BF16)
    y_a = _dot(oa_ref[...], wpa_ref[...])
    y_b = _dot(yb_in, wpb_ref[...])
    merged = gab_ref[:, 0:d].astype(F32) * y_a + gab_ref[:, d:2 * d].astype(F32) * y_b
    o_ref[...] = x_ref[...] + _dot(merged.astype(BF16), wo_ref[...])


def _mixer_out(x2, oa, cbv, gab, conv_w, w_pa, w_pb, w_o, seq_len):
    T, D = x2.shape
    tm = ROW_TILE
    cw8 = jnp.pad(conv_w, ((0, 8 - CONV_K), (0, 0)))
    row = lambda w: pl.BlockSpec((tm, w), lambda i: (i, 0))
    full = lambda a: pl.BlockSpec(a.shape, lambda i: (0,) * a.ndim)
    halo = pl.BlockSpec((8, cbv.shape[1]), lambda i: (jnp.maximum(i * (tm // 8) - 1, 0), 0))
    wts = (cw8, w_pa.astype(BF16), w_pb.astype(BF16), w_o.astype(BF16))
    return pl.pallas_call(
        functools.partial(_mixer_out_body, seq_len=seq_len),
        grid=(T // tm,),
        in_specs=[row(D), row(oa.shape[1]), row(cbv.shape[1]), halo, row(gab.shape[1])] + [full(a) for a in wts],
        out_specs=row(D),
        out_shape=jax.ShapeDtypeStruct((T, D), F32),
        compiler_params=_cparams(1),
        name="mixer_out",
    )(x2, oa, cbv, cbv, gab, *wts)


def _router_body(x_ref, g2_ref, whi_ref, wlo_ref, br_ref, tri_ref, h_out, mi_out, mf_out, cnt_out):
    i = pl.program_id(0)

    @pl.when(i == 0)
    def _():
        cnt_out[...] = jnp.zeros(cnt_out.shape, F32)

    x = x_ref[...]
    tm = x.shape[0]
    ms = jnp.mean(x * x, axis=-1, keepdims=True)
    h = x * lax.rsqrt(ms + EPS) * g2_ref[...]
    h_out[...] = h
    h_hi = h.astype(BF16)
    h_lo = (h - h_hi.astype(F32)).astype(BF16)
    logits = (_dot(h_hi, whi_ref[...]) + _dot(h_lo, whi_ref[...]) + _dot(h_hi, wlo_ref[...])) + br_ref[...]
    lane = lax.broadcasted_iota(I32, (tm, LANES), 1)
    lanef = lane.astype(F32)
    work = jnp.where(lane < N_EXPERTS, logits, -jnp.inf)
    vals, hits = [], []
    for _ in range(TOP_K):
        mx = jnp.max(work, axis=-1, keepdims=True)
        first = jnp.min(jnp.where(work == mx, lanef, float(LANES)), axis=-1, keepdims=True)
        hit = lanef == first
        vals.append(mx)
        hits.append(hit)
        work = jnp.where(hit, -jnp.inf, work)
    ex = [jnp.exp(v - vals[0]) for v in vals]
    den = ex[0]
    for e in ex[1:]:
        den = den + e
    cnt = jnp.zeros((tm, LANES), F32)
    for hit in hits:
        cnt = cnt + hit.astype(F32)
    before = _dot(tri_ref[...], cnt.astype(BF16)) + cnt_out[0:1, :]
    mi = jnp.zeros((tm, LANES), F32)
    mf = jnp.zeros((tm, LANES), F32)
    for k, hit in enumerate(hits):
        e_k = jnp.sum(jnp.where(hit, lanef, 0.0), axis=-1, keepdims=True)
        r_k = jnp.sum(jnp.where(hit, before, 0.0), axis=-1, keepdims=True)
        mi = jnp.where(lane == k, e_k, jnp.where(lane == TOP_K + k, r_k, mi))
        mf = jnp.where(lane == k, ex[k] / den, mf)
    mi_out[...] = mi.astype(I32)
    mf_out[...] = mf
    cnt_out[...] = cnt_out[...] + jnp.sum(cnt, axis=0, keepdims=True)


def _router(x1, g2, w_r, b_r):
    T, D = x1.shape
    tm = ROW_TILE
    wpad = jnp.pad(w_r, ((0, 0), (0, LANES - N_EXPERTS)))
    whi = wpad.astype(BF16)
    wlo = (wpad - whi.astype(F32)).astype(BF16)
    br = jnp.pad(b_r, (0, LANES - N_EXPERTS)).reshape(1, LANES)
    tri = jnp.asarray(np.tril(np.ones((tm, tm), np.float32), -1), BF16)
    row = lambda w: pl.BlockSpec((tm, w), lambda i: (i, 0))
    full = lambda a: pl.BlockSpec(a.shape, lambda i: (0,) * a.ndim)
    ins = (x1, g2.reshape(1, D), whi, wlo, br, tri)
    return pl.pallas_call(
        _router_body,
        grid=(T // tm,),
        in_specs=[row(D)] + [full(a) for a in ins[1:]],
        out_specs=[row(D), row(LANES), row(LANES), pl.BlockSpec((8, LANES), lambda i: (0, 0))],
        out_shape=[jax.ShapeDtypeStruct((T, D), F32), jax.ShapeDtypeStruct((T, LANES), I32),
                   jax.ShapeDtypeStruct((T, LANES), F32), jax.ShapeDtypeStruct((8, LANES), F32)],
        compiler_params=_cparams(1),
        name="router",
    )(*ins)


DISPATCH_TILE = 512
DMA_UNROLL = 8


def _dispatch_body(dest_ref, h_ref, zero_hbm, o_hbm, sem):
    del zero_hbm

    def row_copy(r, d):
        return pltpu.make_async_copy(h_ref.at[pl.ds(r, 1)], o_hbm.at[pl.ds(d, 1)], sem)

    def start(r, c):
        for k in range(TOP_K):
            row_copy(r, dest_ref[0, 0, r * TOP_K + k]).start()
        return c

    def wait(r, c):
        for k in range(TOP_K):
            row_copy(0, 0).wait()
        return c

    lax.fori_loop(0, DISPATCH_TILE, start, 0, unroll=DMA_UNROLL)
    lax.fori_loop(0, DISPATCH_TILE, wait, 0, unroll=DMA_UNROLL)


def _dispatch(h2, dest, n_rows):
    T, D = h2.shape
    td = DISPATCH_TILE
    dest3 = dest.reshape(T // td, 1, td * TOP_K)
    zeros = jnp.zeros((n_rows, D), h2.dtype)
    return pl.pallas_call(
        _dispatch_body,
        grid=(T // td,),
        in_specs=[pl.BlockSpec((1, 1, td * TOP_K), lambda i: (i, 0, 0), memory_space=pltpu.SMEM),
                  pl.BlockSpec((td, D), lambda i: (i, 0)), pl.BlockSpec(memory_space=pl.ANY)],
        out_specs=pl.BlockSpec(memory_space=pl.ANY),
        out_shape=jax.ShapeDtypeStruct((n_rows, D), h2.dtype),
        scratch_shapes=[pltpu.SemaphoreType.DMA(())],
        input_output_aliases={2: 0},
        compiler_params=_cparams(1),
        name="dispatch",
    )(dest3, h2, zeros)


def _expert_body(ce_ref, nu_ref, x_ref, wgu_ref, bgu_ref, wdn_ref, bdn_ref, o_ref):
    c = pl.program_id(0)
    dff = wdn_ref.shape[1]

    @pl.when(c < nu_ref[0])
    def _():
        gu = _dot(x_ref[...].astype(BF16), wgu_ref[0]) + bgu_ref[0]
        g = jnp.minimum(gu[:, 0:dff], SWIGLU_LIMIT)
        u = jnp.clip(gu[:, dff:2 * dff], -SWIGLU_LIMIT, SWIGLU_LIMIT)
        act = (u + 1.0) * (g * jax.nn.sigmoid(SWIGLU_ALPHA * g))
        o_ref[...] = _dot(act.astype(BF16), wdn_ref[0]) + bdn_ref[0]

    @pl.when(c >= nu_ref[0])
    def _():
        o_ref[...] = jnp.zeros(o_ref.shape, F32)


def _experts(hperm, chunk_e, n_used, w_gu, b_gu, w_dn, b_dn):
    P, D = hperm.shape
    E, _, F2 = w_gu.shape
    dff = F2 // 2
    n_chunks = P // MOE_CHUNK
    grid_spec = pltpu.PrefetchScalarGridSpec(
        num_scalar_prefetch=2,
        grid=(n_chunks,),
        in_specs=[pl.BlockSpec((MOE_CHUNK, D), lambda c, ce, nu: (c, 0)),
                  pl.BlockSpec((1, D, F2), lambda c, ce, nu: (ce[c], 0, 0)),
                  pl.BlockSpec((1, 1, F2), lambda c, ce, nu: (ce[c], 0, 0)),
                  pl.BlockSpec((1, dff, D), lambda c, ce, nu: (ce[c], 0, 0)),
                  pl.BlockSpec((1, 1, D), lambda c, ce, nu: (ce[c], 0, 0))],
        out_specs=pl.BlockSpec((MOE_CHUNK, D), lambda c, ce, nu: (c, 0)),
    )
    return pl.pallas_call(
        _expert_body,
        grid_spec=grid_spec,
        out_shape=jax.ShapeDtypeStruct((P, D), F32),
        compiler_params=_cparams(1),
        name="experts",
    )(chunk_e, n_used, hperm, w_gu.astype(BF16), b_gu.reshape(E, 1, F2), w_dn.astype(BF16),
      b_dn.reshape(E, 1, D))


COMBINE_TILE = 256


def _combine_body(dest_ref, x_ref, w_ref, y_hbm, o_ref, buf_ref, sem):
    def row_copy(r, k, d):
        return pltpu.make_async_copy(y_hbm.at[pl.ds(d, 1)], buf_ref.at[k, pl.ds(r, 1)], sem)

    def start(r, c):
        for k in range(TOP_K):
            row_copy(r, k, dest_ref[0, 0, r * TOP_K + k]).start()
        return c

    def wait(r, c):
        for k in range(TOP_K):
            row_copy(0, 0, 0).wait()
        return c

    lax.fori_loop(0, COMBINE_TILE, start, 0, unroll=DMA_UNROLL)
    lax.fori_loop(0, COMBINE_TILE, wait, 0, unroll=DMA_UNROLL)
    out = x_ref[...]
    for k in range(TOP_K):
        out = out + w_ref[:, k:k + 1] * buf_ref[k]
    o_ref[...] = out


def _combine(x1, gate_w, dest, ys):
    T, D = x1.shape
    tc = COMBINE_TILE
    dest3 = dest.reshape(T // tc, 1, tc * TOP_K)
    row = lambda w: pl.BlockSpec((tc, w), lambda i: (i, 0))
    return pl.pallas_call(
        _combine_body,
        grid=(T // tc,),
        in_specs=[pl.BlockSpec((1, 1, tc * TOP_K), lambda i: (i, 0, 0), memory_space=pltpu.SMEM),
                  row(D), row(LANES), pl.BlockSpec(memory_space=pl.ANY)],
        out_specs=row(D),
        out_shape=jax.ShapeDtypeStruct((T, D), F32),
        scratch_shapes=[pltpu.VMEM((TOP_K, tc, D), F32), pltpu.SemaphoreType.DMA(())],
        compiler_params=_cparams(1),
        name="combine",
    )(dest3, x1, gate_w, ys)


def _mixer(x2, B, S, g_norm1, w_in, g_q, g_kc, g_ks, g_kw, pe_k, ck_w1, ck_b1, ck_w2, ck_b2,
           pe_v, cv_w1, cv_b1, cv_w2, cv_b2, conv_w, w_pa, w_pb, w_o):
    T, D = x2.shape
    G, H, hd = N_KV_HEADS, N_HEADS, HEAD_DIM
    qa, kcv, kas, kaw, vsw, gates, cbv, gab = _inproj(x2, g_norm1, w_in, g_q, g_ks, g_kw, S)
    nh = S // CMP_STRIDE
    half = (kcv.reshape(B, nh, CMP_STRIDE, 2, G, hd).transpose(3, 0, 4, 1, 2, 5)
            .reshape(2, B, G, nh, CMP_STRIDE * hd))
    kca = _compress(half[0], pe_k, ck_w1, ck_b1, ck_w2, ck_b2, g_kc, True)
    vct = _compress(half[1], pe_v, cv_w1, cv_b1, cv_w2, cv_b2, jnp.ones((hd,), F32), False)
    vsw5 = vsw.reshape(B, S, 2, G, hd)
    vst = (vsw5[:, :, 0].reshape(B, S // KEY_TILE, KEY_TILE, G, hd).transpose(0, 3, 1, 4, 2))
    vwt = (vsw5[:, :, 1].reshape(B, S // Q_TILE, Q_TILE, G, hd).transpose(0, 3, 1, 4, 2))
    gat = gates[:, :3 * H].reshape(B, S, G, 3 * HEADS_PER_GROUP).transpose(0, 2, 3, 1)
    gat = jnp.pad(gat, ((0, 0), (0, 0), (0, 16 - 3 * HEADS_PER_GROUP), (0, 0)))
    o = _attention(qa.reshape(B, S, -1), gat, kca, vct, kas.reshape(B, S, -1), vst, kaw.reshape(B, S, -1),
                   vwt, B, S)
    return _mixer_out(x2, o.reshape(T, H * hd), cbv, gab, conv_w, w_pa, w_pb, w_o, S)


def _moe(x1, g_norm2, w_r, b_r, w_gu, b_gu, w_dn, b_dn):
    T, D = x1.shape
    h2, mi, mf, cnt = _router(x1, g_norm2, w_r, b_r)
    top_e = mi[:, 0:TOP_K]
    rank = mi[:, TOP_K:2 * TOP_K]
    counts = cnt[0, :N_EXPERTS].astype(I32)
    padded = (counts + MOE_CHUNK - 1) // MOE_CHUNK * MOE_CHUNK
    pend = jnp.cumsum(padded)
    poffs = pend - padded
    dest = (poffs[top_e] + rank).reshape(-1)
    n_chunks = (T * TOP_K + MOE_CHUNK - 1) // MOE_CHUNK + N_EXPERTS
    chunk_e = jnp.minimum(jnp.searchsorted(pend, jnp.arange(n_chunks, dtype=I32) * MOE_CHUNK, side='right'),
                          N_EXPERTS - 1).astype(I32)
    n_used = (pend[-1:] // MOE_CHUNK).astype(I32)
    hperm = _dispatch(h2, dest, n_chunks * MOE_CHUNK)
    ys = _experts(hperm, chunk_e, n_used, w_gu, b_gu, w_dn, b_dn)
    return _combine(x1, mf, dest, ys)


def kernel(x, g_norm1, w_in, g_q, g_kc, g_ks, g_kw, pe_k, ck_w1, ck_b1, ck_w2, ck_b2, pe_v, cv_w1, cv_b1,
           cv_w2, cv_b2, conv_w, w_pa, w_pb, w_o, g_norm2, w_r, b_r, w_gu, b_gu, w_dn, b_dn):
    B, S, D = x.shape
    x2 = x.reshape(B * S, D)
    for l in range(g_norm1.shape[0]):
        x2 = _mixer(x2, B, S, g_norm1[l], w_in[l], g_q[l], g_kc[l], g_ks[l], g_kw[l], pe_k[l], ck_w1[l],
                    ck_b1[l], ck_w2[l], ck_b2[l], pe_v[l], cv_w1[l], cv_b1[l], cv_w2[l], cv_b2[l],
                    conv_w[l], w_pa[l], w_pb[l], w_o[l])
        x2 = _moe(x2, g_norm2[l], w_r[l], b_r[l], w_gu[l], b_gu[l], w_dn[l], b_dn[l])
    return x2.reshape(B, S, D)
```

```python
import functools

import numpy as np
import jax
import jax.numpy as jnp
from jax import lax
from jax.experimental import pallas as pl
from jax.experimental.pallas import tpu as pltpu

F32 = jnp.float32
BF16 = jnp.bfloat16
I32 = jnp.int32

N_HEADS = 8
HEAD_DIM = 64
N_KV_HEADS = 2
HEADS_PER_GROUP = N_HEADS // N_KV_HEADS
CMP_LEN = 32
CMP_STRIDE = 16
CMP_HID = 256
SEL_BLK = 64
SEL_TOPN = 16
WINDOW = 512
CONV_WIDTH = 512
CONV_K = 3
N_EXPERTS = 32
TOP_K = 4
SWIGLU_LIMIT = 7.0
SWIGLU_ALPHA = 1.702
MOE_CHUNK = 512
EPS = 1e-6
NEG = -1e30
POS = 1e30
MASK_BIG = 2.0 ** 100

LANES = 128
Q_TILE = 128
KEY_TILE = 512
N_AUG = 5
ROW_TILE = 512
VMEM_LIMIT = 56 * 1024 * 1024
GROUP_LANES = HEADS_PER_GROUP * Q_TILE
WIN_KEYS = WINDOW + Q_TILE
N_TILE_ROWS = 16


def _cparams(n_axes):
    return pltpu.CompilerParams(dimension_semantics=("arbitrary",) * n_axes,
                                vmem_limit_bytes=VMEM_LIMIT)


def _dot(a, b):
    return jnp.dot(a, b, preferred_element_type=F32)


def _dot_nt(a, b):
    return lax.dot_general(a, b, (((1,), (1,)), ((), ())), preferred_element_type=F32)


def _rms_pairs(v, bd):
    ss = _dot((v * v).astype(BF16), bd)
    return v * lax.rsqrt(ss + EPS)


def _inproj_body(x_ref, g1_ref, wq_ref, wkv_ref, wng_ref, wcv_ref, wmg_ref, gq_ref, gk_ref, bd_ref,
                 qtab_ref, kstab_ref, kwtab_ref,
                 qa_out, kcv_out, kas_out, kaw_out, vsw_out, gate_out, cbv_out, gab_out):
    x = x_ref[...]
    tm = x.shape[0]
    ms = jnp.mean(x * x, axis=-1, keepdims=True)
    h = (x * lax.rsqrt(ms + EPS) * g1_ref[...]).astype(BF16)
    bd = bd_ref[...]
    low = lax.broadcasted_iota(I32, (tm, LANES), 1) < HEAD_DIM

    def place(pair, tab_ref, out_ref, base, slot):
        for j, src in enumerate((pair, pltpu.roll(pair, HEAD_DIM, 1))):
            o = base + j * slot
            out_ref[:, o:o + LANES] = jnp.where(low, src, tab_ref[:, o:o + LANES].astype(F32)).astype(BF16)

    q = _dot(h, wq_ref[...])
    for c in range(N_HEADS * HEAD_DIM // LANES):
        sl = slice(c * LANES, (c + 1) * LANES)
        place(_rms_pairs(q[:, sl], bd) * gq_ref[:, sl], qtab_ref, qa_out, 2 * c * LANES, LANES)
    kv = _dot(h, wkv_ref[...])
    kcv_out[...] = kv[:, 0:256].astype(BF16)
    place(_rms_pairs(kv[:, 256:384], bd) * gk_ref[:, 0:128], kstab_ref, kas_out, 0, 2 * LANES)
    for g in range(N_KV_HEADS):
        o = (2 * g + 1) * LANES
        kas_out[:, o:o + LANES] = kstab_ref[:, o:o + LANES]
    place(_rms_pairs(kv[:, 512:640], bd) * gk_ref[:, 128:256], kwtab_ref, kaw_out, 0, LANES)
    vsw_out[:, 0:128] = kv[:, 384:512].astype(BF16)
    vsw_out[:, 128:256] = kv[:, 640:768].astype(BF16)
    gate_out[...] = jax.nn.sigmoid(_dot(h, wng_ref[...]))
    cv = _dot(h, wcv_ref[...])
    cw = CONV_WIDTH
    cbv_out[:, 0:cw] = cv[:, 0:cw].astype(BF16)
    cbv_out[:, cw:2 * cw] = (cv[:, cw:2 * cw] * cv[:, 2 * cw:3 * cw]).astype(BF16)
    gab_out[...] = jax.nn.sigmoid(_dot(h, wmg_ref[...])).astype(BF16)


def _key_aug(pos, vbias):
    one = jnp.ones_like(pos, F32)
    return jnp.stack([one, one, (pos // 64 * 64).astype(F32), (pos % 64).astype(F32), vbias], axis=-1)


def _slot_table(aug, slot, extra=None):
    S, n, _ = aug.shape
    parts = [jnp.zeros((S, n, HEAD_DIM), F32), aug, jnp.zeros((S, n, LANES - HEAD_DIM - N_AUG), F32)]
    if extra is not None:
        parts.append(jnp.broadcast_to(extra[:, None, :], (S, n, extra.shape[-1])))
    return jnp.concatenate(parts, axis=-1).reshape(S, n * slot).astype(BF16)


def _inproj(x2, g1, w_in, g_q, g_ks, g_kw, S):
    T, D = x2.shape
    H, G = N_HEADS, N_KV_HEADS
    aw = H * HEAD_DIM
    kvw = G * HEAD_DIM
    o = 0
    wq = w_in[:, o:o + aw]; o += aw
    wkv = w_in[:, o:o + 6 * kvw]; o += 6 * kvw
    wng = w_in[:, o:o + 3 * H]; o += 3 * H
    wcv = w_in[:, o:o + 3 * CONV_WIDTH]; o += 3 * CONV_WIDTH
    wmg = w_in[:, o:o + 2 * D]
    wng = jnp.pad(wng, ((0, 0), (0, LANES - 3 * H)))
    wq, wkv, wng, wcv, wmg = (w.astype(BF16) for w in (wq, wkv, wng, wcv, wmg))
    gq = (jnp.tile(g_q, H) * (HEAD_DIM ** -0.5)).reshape(1, aw)
    gk = jnp.concatenate([jnp.tile(g_ks, G), jnp.tile(g_kw, G)]).reshape(1, 2 * kvw)
    idx = np.arange(LANES) // HEAD_DIM
    bd = jnp.asarray((idx[:, None] == idx[None, :]).astype(np.float32) / HEAD_DIM, BF16)
    pos = jnp.arange(S, dtype=I32)
    hi = (pos // 64 * 64).astype(F32)[:, None]
    lo = (pos % 64).astype(F32)[:, None]
    sl = jnp.asarray(2.0 ** (-8.0 * np.arange(1, H + 1) / H), F32)[None, :]
    aq = jnp.stack([-sl * hi, -sl * lo, jnp.broadcast_to(sl, (S, H)), jnp.broadcast_to(sl, (S, H)),
                    jnp.ones((S, H), F32)], axis=-1)
    ak = jnp.broadcast_to(_key_aug(pos, jnp.zeros((S,), F32))[:, None, :], (S, G, N_AUG))
    onehot = (pos[:, None] // SEL_BLK == jnp.arange(LANES, dtype=I32)[None, :]).astype(F32)
    qtab = _slot_table(aq, LANES)
    kstab = _slot_table(ak, 2 * LANES, onehot)
    kwtab = _slot_table(ak, LANES)
    tm = ROW_TILE
    nst = S // tm
    row = lambda w: pl.BlockSpec((tm, w), lambda i: (i, 0))
    full = lambda a: pl.BlockSpec(a.shape, lambda i: (0,) * a.ndim)
    tab = lambda a: pl.BlockSpec((tm, a.shape[1]), lambda i: (i % nst, 0))
    ins = (x2, g1.reshape(1, D), wq, wkv, wng, wcv, wmg, gq, gk, bd, qtab, kstab, kwtab)
    widths = (H * LANES, 2 * kvw, G * 2 * LANES, G * LANES, 2 * kvw, LANES, 2 * CONV_WIDTH, 2 * D)
    dtypes = (BF16, BF16, BF16, BF16, BF16, F32, BF16, BF16)
    return pl.pallas_call(
        _inproj_body,
        grid=(T // tm,),
        in_specs=[row(D)] + [full(a) for a in ins[1:10]] + [tab(a) for a in ins[10:]],
        out_specs=[row(w) for w in widths],
        out_shape=[jax.ShapeDtypeStruct((T, w), dt) for w, dt in zip(widths, dtypes)],
        compiler_params=_cparams(1),
        name="inproj",
    )(*ins)


def _compress_body(h_ref, w1_ref, pe_ref, b1_ref, w2_ref, b2_ref, g_ref, tab_ref, o_ref, *, for_keys):
    hb = h_ref[0, 0]
    nc = hb.shape[0]
    a = _dot(hb, w1_ref[0])
    b = _dot(hb, w1_ref[1])
    c = _dot(pe_ref[0], w1_ref[0]) + _dot(pe_ref[1], w1_ref[1])
    pre = a + pltpu.roll(b, nc - 1, 0) + c[0:1, :] + b1_ref[...]
    hid = jax.nn.gelu(pre)
    out = _dot(hid.astype(BF16), w2_ref[...]) + b2_ref[...]
    if for_keys:
        ms = jnp.sum(out * out, axis=-1, keepdims=True) * (1.0 / HEAD_DIM)
        out = out * lax.rsqrt(ms + EPS) * g_ref[...]
        low = lax.broadcasted_iota(I32, out.shape, 1) < HEAD_DIM
        o_ref[0, 0] = jnp.where(low, out, tab_ref[...]).astype(BF16)
    else:
        o_ref[0, 0] = out.T[0:HEAD_DIM, :].astype(BF16)


def _compress(hh, pe, w1, b1, w2, b2, gain, for_keys):
    B, G, NC, HW = hh.shape
    w1s = w1.reshape(2, HW, CMP_HID).astype(BF16)
    pes = jnp.broadcast_to(pe.reshape(2, 1, HW), (2, 8, HW)).astype(BF16)
    padl = lambda a: jnp.pad(a, ((0, 0), (0, LANES - HEAD_DIM)))
    cpos = jnp.arange(NC, dtype=I32) * CMP_STRIDE + (CMP_LEN - 1)
    tab = jnp.concatenate([jnp.zeros((NC, HEAD_DIM), F32), _key_aug(cpos, jnp.zeros((NC,), F32)),
                           jnp.zeros((NC, LANES - HEAD_DIM - N_AUG), F32)], axis=-1)
    full = lambda a: pl.BlockSpec(a.shape, lambda b, g: (0,) * a.ndim)
    ins = (hh, w1s, pes, b1.reshape(1, CMP_HID), padl(w2).astype(BF16), padl(b2.reshape(1, HEAD_DIM)),
           padl(gain.reshape(1, HEAD_DIM)), tab)
    oshape = (B, G, NC, LANES) if for_keys else (B, G, HEAD_DIM, NC)
    return pl.pallas_call(
        functools.partial(_compress_body, for_keys=for_keys),
        grid=(B, G),
        in_specs=[pl.BlockSpec((1, 1, NC, HW), lambda b, g: (b, g, 0, 0))] + [full(a) for a in ins[1:]],
        out_specs=pl.BlockSpec((1, 1) + oshape[2:], lambda b, g: (b, g, 0, 0)),
        out_shape=jax.ShapeDtypeStruct(oshape, BF16),
        compiler_params=_cparams(2),
        name="compress_keys" if for_keys else "compress_values",
    )(*ins)


def _attn_body(qa_ref, g_ref, kca_ref, vct_ref, kas_ref, vst_ref, kaw_ref, vwt_ref, selmapt_ref, wbias_ref,
               tilemap_ref, o_ref, qaug_ref, m_ref, l_ref, acc_ref, flagv_ref, flags_ref, list_ref, sem,
               *, n_sel):
    i = pl.program_id(2)
    q0 = i * Q_TILE
    gl = GROUP_LANES
    for h in range(HEADS_PER_GROUP):
        qaug_ref[h * Q_TILE:(h + 1) * Q_TILE, 0:LANES] = qa_ref[0, :, h * LANES:(h + 1) * LANES]
    qa = qaug_ref[:, 0:LANES]

    nc = kca_ref.shape[2]
    s = _dot_nt(kca_ref[0, 0], qa)
    n_i = lax.broadcasted_iota(I32, (nc, gl), 0)
    t_i = q0 + (lax.broadcasted_iota(I32, (nc, gl), 1) & (Q_TILE - 1))
    valid = t_i >= n_i * CMP_STRIDE + (CMP_LEN - 1)
    s = jnp.where(valid, s, NEG)
    m = jnp.max(s, axis=0, keepdims=True)
    p = jnp.where(valid, jnp.exp(s - m), 0.0)
    l = jnp.sum(p, axis=0, keepdims=True)
    pc = p * jnp.where(l > 0.0, 1.0 / l, 0.0)
    o_c = _dot(vct_ref[0, 0], pc.astype(BF16))

    ps = pc[:, 0:Q_TILE]
    for h in range(1, HEADS_PER_GROUP):
        ps = ps + pc[:, h * Q_TILE:(h + 1) * Q_TILE]
    ps_hi = ps.astype(BF16)
    ps_lo = (ps - ps_hi.astype(F32)).astype(BF16)
    imp = _dot(selmapt_ref[...], ps_hi) + _dot(selmapt_ref[...], ps_lo)
    jb = lax.broadcasted_iota(I32, (LANES, Q_TILE), 0)
    cur = (q0 + lax.broadcasted_iota(I32, (LANES, Q_TILE), 1)) // SEL_BLK
    forced = (jb == 0) | (jb == cur) | (jb == cur - 1)
    score = jnp.where(jb > cur, NEG, jnp.where(forced, POS, imp))
    jbf = jb.astype(F32)
    sel = jnp.zeros((LANES, Q_TILE), F32)
    for _ in range(n_sel):
        mx = jnp.max(score, axis=0, keepdims=True)
        first = jnp.min(jnp.where(score == mx, jbf, float(LANES)), axis=0, keepdims=True)
        hit = jbf == first
        sel = jnp.where(hit, 1.0, sel)
        score = jnp.where(hit, -jnp.inf, score)
    selbias = jnp.where(sel > 0.0, 0.0, -MASK_BIG).T.astype(BF16)
    for h in range(HEADS_PER_GROUP):
        qaug_ref[h * Q_TILE:(h + 1) * Q_TILE, LANES:2 * LANES] = selbias
    tile_hits = jnp.max(_dot(tilemap_ref[...], sel.astype(BF16)), axis=1, keepdims=True)
    flagv_ref[...] = jnp.broadcast_to(tile_hits, flagv_ref.shape).astype(I32)
    flag_copy = pltpu.make_async_copy(flagv_ref, flags_ref, sem)
    flag_copy.start()

    w0 = pl.multiple_of(jnp.maximum(q0 - WINDOW, 0), Q_TILE)
    sw = _dot_nt(kaw_ref[0, pl.ds(w0, WIN_KEYS), :], qa)
    wb = wbias_ref[jnp.minimum(i, WINDOW // Q_TILE)]
    sw = sw + jnp.concatenate([wb] * HEADS_PER_GROUP, axis=1)
    mw = jnp.max(sw, axis=0, keepdims=True)
    pw = jnp.exp(sw - mw)
    lw = jnp.sum(pw, axis=0, keepdims=True)
    c0 = w0 // Q_TILE
    vw = jnp.concatenate([vwt_ref[0, 0, c0 + j] for j in range(WIN_KEYS // Q_TILE)], axis=1)
    o_w = _dot(vw, pw.astype(BF16)) * (1.0 / lw)

    m_ref[...] = jnp.full(m_ref.shape, -3.0e38, F32)
    l_ref[...] = jnp.zeros(l_ref.shape, F32)
    acc_ref[...] = jnp.zeros(acc_ref.shape, F32)
    qaug = qaug_ref[...]

    def tile_step(kt, causal):
        k0 = pl.multiple_of(kt * KEY_TILE, KEY_TILE)
        sc = _dot_nt(kas_ref[0, pl.ds(k0, KEY_TILE), :], qaug)
        if causal:
            key = k0 + lax.broadcasted_iota(I32, (KEY_TILE, gl), 0)
            qry = q0 + (lax.broadcasted_iota(I32, (KEY_TILE, gl), 1) & (Q_TILE - 1))
            sc = jnp.where(key <= qry, sc, NEG)
        m_old = m_ref[...]
        m_new = jnp.maximum(m_old, jnp.max(sc, axis=0, keepdims=True))
        alpha = jnp.exp(m_old - m_new)
        pp = jnp.exp(sc - m_new)
        l_ref[...] = alpha * l_ref[...] + jnp.sum(pp, axis=0, keepdims=True)
        acc_ref[...] = alpha * acc_ref[...] + _dot(vst_ref[0, 0, kt], pp.astype(BF16))
        m_ref[...] = m_new

    n_full = q0 // KEY_TILE
    flag_copy.wait()

    def compact(kt, n):
        active = flags_ref[kt, 0] > 0

        @pl.when(active)
        def _():
            list_ref[n] = kt

        return n + active.astype(I32)

    n_active = lax.fori_loop(0, n_full, compact, 0)

    def loop_body(j, carry):
        tile_step(list_ref[j], False)
        return carry

    lax.fori_loop(0, n_active, loop_body, 0)
    tile_step(n_full, True)
    o_s = acc_ref[...] * (1.0 / l_ref[...])

    g = g_ref[0, 0]
    outs = []
    for h in range(HEADS_PER_GROUP):
        sl = slice(h * Q_TILE, (h + 1) * Q_TILE)
        outs.append(g[3 * h:3 * h + 1, :] * o_c[:, sl] + g[3 * h + 1:3 * h + 2, :] * o_s[:, sl]
                    + g[3 * h + 2:3 * h + 3, :] * o_w[:, sl])
    o_ref[0] = jnp.concatenate(outs, axis=0).T.astype(BF16)


def _attention(qa, gates_t, kca, vct, kas, vst, kaw, vwt, B, S):
    G, hd = N_KV_HEADS, HEAD_DIM
    NC = kca.shape[2]
    n_blk = S // SEL_BLK
    assert n_blk <= LANES and S % KEY_TILE == 0 and S >= WIN_KEYS
    n_sel = min(SEL_TOPN, n_blk)
    ratio, span = SEL_BLK // CMP_STRIDE, CMP_LEN // CMP_STRIDE
    sm = np.zeros((LANES, NC), np.float32)
    for j in range(n_blk):
        for a in range(ratio):
            for b in range(span):
                n = ratio * j + a - b
                if 0 <= n < NC - 1:
                    sm[j, n] += 1.0
    selmapt = jnp.asarray(sm, BF16)
    c = np.arange(WIN_KEYS)[:, None]
    r = np.arange(Q_TILE)[None, :]
    offs = np.arange(WINDOW // Q_TILE + 1)[:, None, None] * Q_TILE
    wbias = jnp.asarray(np.where((c - r <= offs) & (c - r > offs - WINDOW), 0.0, NEG), F32)
    tilemap = jnp.asarray(np.arange(LANES)[None, :] // (KEY_TILE // SEL_BLK) == np.arange(N_TILE_ROWS)[:, None],
                          BF16)
    assert S // KEY_TILE <= N_TILE_ROWS

    hpg = HEADS_PER_GROUP
    grp = lambda *blk: pl.BlockSpec((1, 1) + blk, lambda b, g, i: (b, g) + (0,) * len(blk))
    seq = lambda w: pl.BlockSpec((1, S, w), lambda b, g, i: (b, 0, g))
    const = lambda a: pl.BlockSpec(a.shape, lambda b, g, i: (0,) * a.ndim)
    return pl.pallas_call(
        functools.partial(_attn_body, n_sel=n_sel),
        grid=(B, G, S // Q_TILE),
        in_specs=[pl.BlockSpec((1, Q_TILE, hpg * LANES), lambda b, g, i: (b, i, g)),
                  pl.BlockSpec((1, 1, 16, Q_TILE), lambda b, g, i: (b, g, 0, i)),
                  grp(NC, LANES), grp(hd, NC), seq(2 * LANES), grp(S // KEY_TILE, hd, KEY_TILE),
                  seq(LANES), grp(S // Q_TILE, hd, Q_TILE), const(selmapt), const(wbias), const(tilemap)],
        out_specs=pl.BlockSpec((1, Q_TILE, hpg * hd), lambda b, g, i: (b, i, g)),
        out_shape=jax.ShapeDtypeStruct((B, S, N_HEADS * hd), BF16),
        scratch_shapes=[pltpu.VMEM((GROUP_LANES, 2 * LANES), BF16), pltpu.VMEM((1, GROUP_LANES), F32),
                        pltpu.VMEM((1, GROUP_LANES), F32), pltpu.VMEM((hd, GROUP_LANES), F32),
                        pltpu.VMEM((N_TILE_ROWS, LANES), I32), pltpu.SMEM((N_TILE_ROWS, LANES), I32),
                        pltpu.SMEM((N_TILE_ROWS,), I32), pltpu.SemaphoreType.DMA(())],
        compiler_params=_cparams(3),
        name="nsa_attention",
    )(qa, gates_t, kca, vct, kas, vst, kaw, vwt, selmapt, wbias, tilemap)


def _mixer_out_body(x_ref, oa_ref, cbv_ref, halo_ref, gab_ref, cw_ref, wpa_ref, wpb_ref, wo_ref, o_ref,
                    *, seq_len):
    i = pl.program_id(0)
    tm = x_ref.shape[0]
    cwd = CONV_WIDTH
    d = x_ref.shape[1]
    v = cbv_ref[:, cwd:2 * cwd].astype(F32)
    prev = halo_ref[:, cwd:2 * cwd].astype(F32)
    keep = ((i * tm) % seq_len != 0).astype(F32)
    p1 = prev[7:8, :] * keep
    p2 = prev[6:7, :] * keep
    ridx = lax.broadcasted_iota(I32, (tm, cwd), 0)
    v1 = jnp.where(ridx == 0, p1, pltpu.roll(v, 1, 0))
    v2 = jnp.where(ridx == 0, p2, jnp.where(ridx == 1, p1, pltpu.roll(v, 2, 0)))
    y = cw_ref[0:1, :] * v2 + cw_ref[1:2, :] * v1 + cw_ref[2:3, :] * v
    yb_in = (cbv_ref[:, 0:cwd].astype(F32) * y).astype(BF16)
    y_a = _dot(oa_ref[...], wpa_ref[...])
    y_b = _dot(yb_in, wpb_ref[...])
    merged = gab_ref[:, 0:d].astype(F32) * y_a + gab_ref[:, d:2 * d].astype(F32) * y_b
    o_ref[...] = x_ref[...] + _dot(merged.astype(BF16), wo_ref[...])


def _mixer_out(x2, oa, cbv, gab, conv_w, w_pa, w_pb, w_o, seq_len):
    T, D = x2.shape
    tm = ROW_TILE
    cw8 = jnp.pad(conv_w, ((0, 8 - CONV_K), (0, 0)))
    row = lambda w: pl.BlockSpec((tm, w), lambda i: (i, 0))
    full = lambda a: pl.BlockSpec(a.shape, lambda i: (0,) * a.ndim)
    halo = pl.BlockSpec((8, cbv.shape[1]), lambda i: (jnp.maximum(i * (tm // 8) - 1, 0), 0))
    wts = (cw8, w_pa.astype(BF16), w_pb.astype(BF16), w_o.astype(BF16))
    return pl.pallas_call(
        functools.partial(_mixer_out_body, seq_len=seq_len),
        grid=(T // tm,),
        in_specs=[row(D), row(oa.shape[1]), row(cbv.shape[1]), halo, row(gab.shape[1])] + [full(a) for a in wts],
        out_specs=row(D),
        out_shape=jax.ShapeDtypeStruct((T, D), F32),
        compiler_params=_cparams(1),
        name="mixer_out",
    )(x2, oa, cbv, cbv, gab, *wts)


def _router_body(x_ref, g2_ref, whi_ref, wlo_ref, br_ref, tri_ref, h_out, mi_out, mf_out, cnt_out):
    i = pl.program_id(0)

    @pl.when(i == 0)
    def _():
        cnt_out[...] = jnp.zeros(cnt_out.shape, F32)

    x = x_ref[...]
    tm = x.shape[0]
    ms = jnp.mean(x * x, axis=-1, keepdims=True)
    h = x * lax.rsqrt(ms + EPS) * g2_ref[...]
    h_out[...] = h
    h_hi = h.astype(BF16)
    h_lo = (h - h_hi.astype(F32)).astype(BF16)
    logits = (_dot(h_hi, whi_ref[...]) + _dot(h_lo, whi_ref[...]) + _dot(h_hi, wlo_ref[...])) + br_ref[...]
    lane = lax.broadcasted_iota(I32, (tm, LANES), 1)
    lanef = lane.astype(F32)
    work = jnp.where(lane < N_EXPERTS, logits, -jnp.inf)
    vals, hits = [], []
    for _ in range(TOP_K):
        mx = jnp.max(work, axis=-1, keepdims=True)
        first = jnp.min(jnp.where(work == mx, lanef, float(LANES)), axis=-1, keepdims=True)
        hit = lanef == first
        vals.append(mx)
        hits.append(hit)
        work = jnp.where(hit, -jnp.inf, work)
    ex = [jnp.exp(v - vals[0]) for v in vals]
    den = ex[0]
    for e in ex[1:]:
        den = den + e
    cnt = jnp.zeros((tm, LANES), F32)
    for hit in hits:
        cnt = cnt + hit.astype(F32)
    before = _dot(tri_ref[...], cnt.astype(BF16)) + cnt_out[0:1, :]
    mi = jnp.zeros((tm, LANES), F32)
    mf = jnp.zeros((tm, LANES), F32)
    for k, hit in enumerate(hits):
        e_k = jnp.sum(jnp.where(hit, lanef, 0.0), axis=-1, keepdims=True)
        r_k = jnp.sum(jnp.where(hit, before, 0.0), axis=-1, keepdims=True)
        mi = jnp.where(lane == k, e_k, jnp.where(lane == TOP_K + k, r_k, mi))
        mf = jnp.where(lane == k, ex[k] / den, mf)
    mi_out[...] = mi.astype(I32)
    mf_out[...] = mf
    cnt_out[...] = cnt_out[...] + jnp.sum(cnt, axis=0, keepdims=True)


def _router(x1, g2, w_r, b_r):
    T, D = x1.shape
    tm = ROW_TILE
    wpad = jnp.pad(w_r, ((0, 0), (0, LANES - N_EXPERTS)))
    whi = wpad.astype(BF16)
    wlo = (wpad - whi.astype(F32)).astype(BF16)
    br = jnp.pad(b_r, (0, LANES - N_EXPERTS)).reshape(1, LANES)
    tri = jnp.asarray(np.tril(np.ones((tm, tm), np.float32), -1), BF16)
    row = lambda w: pl.BlockSpec((tm, w), lambda i: (i, 0))
    full = lambda a: pl.BlockSpec(a.shape, lambda i: (0,) * a.ndim)
    ins = (x1, g2.reshape(1, D), whi, wlo, br, tri)
    return pl.pallas_call(
        _router_body,
        grid=(T // tm,),
        in_specs=[row(D)] + [full(a) for a in ins[1:]],
        out_specs=[row(D), row(LANES), row(LANES), pl.BlockSpec((8, LANES), lambda i: (0, 0))],
        out_shape=[jax.ShapeDtypeStruct((T, D), F32), jax.ShapeDtypeStruct((T, LANES), I32),
                   jax.ShapeDtypeStruct((T, LANES), F32), jax.ShapeDtypeStruct((8, LANES), F32)],
        compiler_params=_cparams(1),
        name="router",
    )(*ins)


DISPATCH_TILE = 512
DMA_UNROLL = 8


def _dispatch_body(dest_ref, last_ref, h_ref, o_hbm, zero_ref, sem, zsem):
    @pl.when(pl.program_id(0) == 0)
    def _():
        zero_ref[...] = jnp.zeros(zero_ref.shape, zero_ref.dtype)

        def clear(e):
            start = pl.multiple_of(last_ref[e], MOE_CHUNK)
            return pltpu.make_async_copy(zero_ref, o_hbm.at[pl.ds(start, MOE_CHUNK)], zsem)

        for e in range(N_EXPERTS):
            @pl.when(last_ref[e] >= 0)
            def _():
                clear(e).start()
        for e in range(N_EXPERTS):
            @pl.when(last_ref[e] >= 0)
            def _():
                clear(e).wait()

    def row_copy(r, d):
        return pltpu.make_async_copy(h_ref.at[pl.ds(r, 1)], o_hbm.at[pl.ds(d, 1)], sem)

    def start(r, c):
        for k in range(TOP_K):
            row_copy(r, dest_ref[0, 0, r * TOP_K + k]).start()
        return c

    def wait(r, c):
        for k in range(TOP_K):
            row_copy(0, 0).wait()
        return c

    lax.fori_loop(0, DISPATCH_TILE, start, 0, unroll=DMA_UNROLL)
    lax.fori_loop(0, DISPATCH_TILE, wait, 0, unroll=DMA_UNROLL)


def _dispatch(h2, dest, last_chunk, n_rows):
    T, D = h2.shape
    td = DISPATCH_TILE
    dest3 = dest.reshape(T // td, 1, td * TOP_K)
    return pl.pallas_call(
        _dispatch_body,
        grid=(T // td,),
        in_specs=[pl.BlockSpec((1, 1, td * TOP_K), lambda i: (i, 0, 0), memory_space=pltpu.SMEM),
                  pl.BlockSpec(memory_space=pltpu.SMEM), pl.BlockSpec((td, D), lambda i: (i, 0))],
        out_specs=pl.BlockSpec(memory_space=pl.ANY),
        out_shape=jax.ShapeDtypeStruct((n_rows, D), h2.dtype),
        scratch_shapes=[pltpu.VMEM((MOE_CHUNK, D), h2.dtype), pltpu.SemaphoreType.DMA(()),
                        pltpu.SemaphoreType.DMA(())],
        compiler_params=_cparams(1),
        name="dispatch",
    )(dest3, last_chunk, h2)


def _expert_body(ce_ref, nu_ref, x_ref, wgu_ref, bgu_ref, wdn_ref, bdn_ref, o_ref):
    c = pl.program_id(0)
    dff = wdn_ref.shape[1]

    @pl.when(c < nu_ref[0])
    def _():
        gu = _dot(x_ref[...].astype(BF16), wgu_ref[0]) + bgu_ref[0]
        g = jnp.minimum(gu[:, 0:dff], SWIGLU_LIMIT)
        u = jnp.clip(gu[:, dff:2 * dff], -SWIGLU_LIMIT, SWIGLU_LIMIT)
        act = (u + 1.0) * (g * jax.nn.sigmoid(SWIGLU_ALPHA * g))
        o_ref[...] = _dot(act.astype(BF16), wdn_ref[0]) + bdn_ref[0]

    @pl.when(c >= nu_ref[0])
    def _():
        o_ref[...] = jnp.zeros(o_ref.shape, F32)


def _experts(hperm, chunk_e, n_used, w_gu, b_gu, w_dn, b_dn):
    P, D = hperm.shape
    E, _, F2 = w_gu.shape
    dff = F2 // 2
    n_chunks = P // MOE_CHUNK
    grid_spec = pltpu.PrefetchScalarGridSpec(
        num_scalar_prefetch=2,
        grid=(n_chunks,),
        in_specs=[pl.BlockSpec((MOE_CHUNK, D), lambda c, ce, nu: (jnp.minimum(c, nu[0] - 1), 0)),
                  pl.BlockSpec((1, D, F2), lambda c, ce, nu: (ce[c], 0, 0)),
                  pl.BlockSpec((1, 1, F2), lambda c, ce, nu: (ce[c], 0, 0)),
                  pl.BlockSpec((1, dff, D), lambda c, ce, nu: (ce[c], 0, 0)),
                  pl.BlockSpec((1, 1, D), lambda c, ce, nu: (ce[c], 0, 0))],
        out_specs=pl.BlockSpec((MOE_CHUNK, D), lambda c, ce, nu: (c, 0)),
    )
    return pl.pallas_call(
        _expert_body,
        grid_spec=grid_spec,
        out_shape=jax.ShapeDtypeStruct((P, D), F32),
        compiler_params=_cparams(1),
        name="experts",
    )(chunk_e, n_used, hperm, w_gu.astype(BF16), b_gu.reshape(E, 1, F2), w_dn.astype(BF16),
      b_dn.reshape(E, 1, D))


COMBINE_TILE = 256


def _combine_body(dest_ref, x_ref, w_ref, y_hbm, o_ref, buf_ref, sem):
    def row_copy(r, k, d):
        return pltpu.make_async_copy(y_hbm.at[pl.ds(d, 1)], buf_ref.at[k, pl.ds(r, 1)], sem)

    def start(r, c):
        for k in range(TOP_K):
            row_copy(r, k, dest_ref[0, 0, r * TOP_K + k]).start()
        return c

    def wait(r, c):
        for k in range(TOP_K):
            row_copy(0, 0, 0).wait()
        return c

    lax.fori_loop(0, COMBINE_TILE, start, 0, unroll=DMA_UNROLL)
    lax.fori_loop(0, COMBINE_TILE, wait, 0, unroll=DMA_UNROLL)
    out = x_ref[...]
    for k in range(TOP_K):
        out = out + w_ref[:, k:k + 1] * buf_ref[k]
    o_ref[...] = out


def _combine(x1, gate_w, dest, ys):
    T, D = x1.shape
    tc = COMBINE_TILE
    dest3 = dest.reshape(T // tc, 1, tc * TOP_K)
    row = lambda w: pl.BlockSpec((tc, w), lambda i: (i, 0))
    return pl.pallas_call(
        _combine_body,
        grid=(T // tc,),
        in_specs=[pl.BlockSpec((1, 1, tc * TOP_K), lambda i: (i, 0, 0), memory_space=pltpu.SMEM),
                  row(D), row(LANES), pl.BlockSpec(memory_space=pl.ANY)],
        out_specs=row(D),
        out_shape=jax.ShapeDtypeStruct((T, D), F32),
        scratch_shapes=[pltpu.VMEM((TOP_K, tc, D), F32), pltpu.SemaphoreType.DMA(())],
        compiler_params=_cparams(1),
        name="combine",
    )(dest3, x1, gate_w, ys)


def _mixer(x2, B, S, g_norm1, w_in, g_q, g_kc, g_ks, g_kw, pe_k, ck_w1, ck_b1, ck_w2, ck_b2,
           pe_v, cv_w1, cv_b1, cv_w2, cv_b2, conv_w, w_pa, w_pb, w_o):
    T, D = x2.shape
    G, H, hd = N_KV_HEADS, N_HEADS, HEAD_DIM
    qa, kcv, kas, kaw, vsw, gates, cbv, gab = _inproj(x2, g_norm1, w_in, g_q, g_ks, g_kw, S)
    nh = S // CMP_STRIDE
    half = (kcv.reshape(B, nh, CMP_STRIDE, 2, G, hd).transpose(3, 0, 4, 1, 2, 5)
            .reshape(2, B, G, nh, CMP_STRIDE * hd))
    kca = _compress(half[0], pe_k, ck_w1, ck_b1, ck_w2, ck_b2, g_kc, True)
    vct = _compress(half[1], pe_v, cv_w1, cv_b1, cv_w2, cv_b2, jnp.ones((hd,), F32), False)
    vsw5 = vsw.reshape(B, S, 2, G, hd)
    vst = (vsw5[:, :, 0].reshape(B, S // KEY_TILE, KEY_TILE, G, hd).transpose(0, 3, 1, 4, 2))
    vwt = (vsw5[:, :, 1].reshape(B, S // Q_TILE, Q_TILE, G, hd).transpose(0, 3, 1, 4, 2))
    gat = gates[:, :3 * H].reshape(B, S, G, 3 * HEADS_PER_GROUP).transpose(0, 2, 3, 1)
    gat = jnp.pad(gat, ((0, 0), (0, 0), (0, 16 - 3 * HEADS_PER_GROUP), (0, 0)))
    o = _attention(qa.reshape(B, S, -1), gat, kca, vct, kas.reshape(B, S, -1), vst, kaw.reshape(B, S, -1),
                   vwt, B, S)
    return _mixer_out(x2, o.reshape(T, H * hd), cbv, gab, conv_w, w_pa, w_pb, w_o, S)


def _moe(x1, g_norm2, w_r, b_r, w_gu, b_gu, w_dn, b_dn):
    T, D = x1.shape
    h2, mi, mf, cnt = _router(x1, g_norm2, w_r, b_r)
    top_e = mi[:, 0:TOP_K]
    rank = mi[:, TOP_K:2 * TOP_K]
    counts = cnt[0, :N_EXPERTS].astype(I32)
    padded = (counts + MOE_CHUNK - 1) // MOE_CHUNK * MOE_CHUNK
    pend = jnp.cumsum(padded)
    poffs = pend - padded
    dest = (poffs[top_e] + rank).reshape(-1)
    n_chunks = (T * TOP_K + MOE_CHUNK - 1) // MOE_CHUNK + N_EXPERTS
    chunk_start = jnp.arange(n_chunks, dtype=I32) * MOE_CHUNK
    chunk_e = jnp.minimum(jnp.sum((pend[None, :] <= chunk_start[:, None]).astype(I32), axis=1), N_EXPERTS - 1)
    n_used = (pend[-1:] // MOE_CHUNK).astype(I32)
    last_chunk = jnp.where(padded > 0, pend - MOE_CHUNK, -1).astype(I32)
    hperm = _dispatch(h2, dest, last_chunk, n_chunks * MOE_CHUNK)
    ys = _experts(hperm, chunk_e, n_used, w_gu, b_gu, w_dn, b_dn)
    return _combine(x1, mf, dest, ys)


def kernel(x, g_norm1, w_in, g_q, g_kc, g_ks, g_kw, pe_k, ck_w1, ck_b1, ck_w2, ck_b2, pe_v, cv_w1, cv_b1,
           cv_w2, cv_b2, conv_w, w_pa, w_pb, w_o, g_norm2, w_r, b_r, w_gu, b_gu, w_dn, b_dn):
    B, S, D = x.shape
    x2 = x.reshape(B * S, D)
    for l in range(g_norm1.shape[0]):
        x2 = _mixer(x2, B, S, g_norm1[l], w_in[l], g_q[l], g_kc[l], g_ks[l], g_kw[l], pe_k[l], ck_w1[l],
                    ck_b1[l], ck_w2[l], ck_b2[l], pe_v[l], cv_w1[l], cv_b1[l], cv_w2[l], cv_b2[l],
                    conv_w[l], w_pa[l], w_pb[l], w_o[l])
        x2 = _moe(x2, g_norm2[l], w_r[l], b_r[l], w_gu[l], b_gu[l], w_dn[l], b_dn[l])
    return x2.reshape(B, S, D)
```

```python
import functools

import numpy as np
import jax
import jax.numpy as jnp
from jax import lax
from jax.experimental import pallas as pl
from jax.experimental.pallas import tpu as pltpu

F32 = jnp.float32
BF16 = jnp.bfloat16
I32 = jnp.int32

N_HEADS = 8
HEAD_DIM = 64
N_KV_HEADS = 2
HEADS_PER_GROUP = N_HEADS // N_KV_HEADS
CMP_LEN = 32
CMP_STRIDE = 16
CMP_HID = 256
SEL_BLK = 64
SEL_TOPN = 16
WINDOW = 512
CONV_WIDTH = 512
CONV_K = 3
N_EXPERTS = 32
TOP_K = 4
SWIGLU_LIMIT = 7.0
SWIGLU_ALPHA = 1.702
MOE_CHUNK = 512
EPS = 1e-6
NEG = -1e30
POS = 1e30
MASK_BIG = 2.0 ** 100

LANES = 128
Q_TILE = 128
KEY_TILE = 512
N_AUG = 5
ROW_TILE = 512
VMEM_LIMIT = 56 * 1024 * 1024
GROUP_LANES = HEADS_PER_GROUP * Q_TILE
WIN_KEYS = WINDOW + Q_TILE
N_TILE_ROWS = 16
V_ROWS = HEAD_DIM + 8


def _cparams(n_axes):
    return pltpu.CompilerParams(dimension_semantics=("arbitrary",) * n_axes,
                                vmem_limit_bytes=VMEM_LIMIT)


def _dot(a, b):
    return jnp.dot(a, b, preferred_element_type=F32)


def _dot_nt(a, b):
    return lax.dot_general(a, b, (((1,), (1,)), ((), ())), preferred_element_type=F32)


def _rms_pairs(v, bd):
    ss = _dot((v * v).astype(BF16), bd)
    return v * lax.rsqrt(ss + EPS)


def _inproj_body(x_ref, g1_ref, wq_ref, wkv_ref, wng_ref, wcv_ref, wmg_ref, gq_ref, gk_ref, bd_ref,
                 qtab_ref, kstab_ref, kwtab_ref,
                 qa_out, hk_out, hv_out, kas_out, kaw_out, vsw_out, gate_out, cbv_out, gab_out, raw_ref):
    x = x_ref[...]
    tm = x.shape[0]
    ms = jnp.mean(x * x, axis=-1, keepdims=True)
    h = (x * lax.rsqrt(ms + EPS) * g1_ref[...]).astype(BF16)
    bd = bd_ref[...]
    low = lax.broadcasted_iota(I32, (tm, LANES), 1) < HEAD_DIM

    def place(pair, tab_ref, out_ref, base, slot):
        for j, src in enumerate((pair, pltpu.roll(pair, HEAD_DIM, 1))):
            o = base + j * slot
            out_ref[:, o:o + LANES] = jnp.where(low, src, tab_ref[:, o:o + LANES].astype(F32)).astype(BF16)

    q = _dot(h, wq_ref[...])
    for c in range(N_HEADS * HEAD_DIM // LANES):
        sl = slice(c * LANES, (c + 1) * LANES)
        place(_rms_pairs(q[:, sl], bd) * gq_ref[:, sl], qtab_ref, qa_out, 2 * c * LANES, LANES)
    kv = _dot(h, wkv_ref[...])

    def emit_half_blocks(c, out_ref):
        nb = tm // CMP_STRIDE
        hw = CMP_STRIDE * HEAD_DIM
        raw_ref[...] = kv[:, c * LANES:(c + 1) * LANES]
        lo = lax.broadcasted_iota(I32, (nb, LANES), 1) < HEAD_DIM
        for u in range(CMP_STRIDE // 2):
            t0 = raw_ref[pl.ds(2 * u, nb, stride=CMP_STRIDE), :]
            t1 = raw_ref[pl.ds(2 * u + 1, nb, stride=CMP_STRIDE), :]
            out_ref[:, u * LANES:(u + 1) * LANES] = jnp.where(lo, t0, pltpu.roll(t1, HEAD_DIM, 1)).astype(BF16)
            out_ref[:, hw + u * LANES:hw + (u + 1) * LANES] = (
                jnp.where(lo, pltpu.roll(t0, HEAD_DIM, 1), t1).astype(BF16))

    emit_half_blocks(0, hk_out)
    emit_half_blocks(1, hv_out)
    place(_rms_pairs(kv[:, 256:384], bd) * gk_ref[:, 0:128], kstab_ref, kas_out, 0, 2 * LANES)
    for g in range(N_KV_HEADS):
        o = (2 * g + 1) * LANES
        kas_out[:, o:o + LANES] = kstab_ref[:, o:o + LANES]
    place(_rms_pairs(kv[:, 512:640], bd) * gk_ref[:, 128:256], kwtab_ref, kaw_out, 0, LANES)
    vsw_out[:, 0:128] = kv[:, 384:512].astype(BF16)
    vsw_out[:, 128:256] = kv[:, 640:768].astype(BF16)
    gate_out[...] = jax.nn.sigmoid(_dot(h, wng_ref[...]))
    cv = _dot(h, wcv_ref[...])
    cw = CONV_WIDTH
    cbv_out[:, 0:cw] = cv[:, 0:cw].astype(BF16)
    cbv_out[:, cw:2 * cw] = (cv[:, cw:2 * cw] * cv[:, 2 * cw:3 * cw]).astype(BF16)
    gab_out[...] = jax.nn.sigmoid(_dot(h, wmg_ref[...])).astype(BF16)


def _key_aug(pos):
    one = np.ones_like(pos, np.float32)
    return np.stack([one, one, (pos // 64 * 64).astype(np.float32), (pos % 64).astype(np.float32), 0 * one],
                    axis=-1)


def _slot_table(aug, slot, extra=None):
    S, n, _ = aug.shape
    tab = np.zeros((S, n, slot), np.float32)
    tab[:, :, HEAD_DIM:HEAD_DIM + N_AUG] = aug
    if extra is not None:
        tab[:, :, LANES:] = extra[:, None, :]
    return jnp.asarray(tab.reshape(S, n * slot), BF16)


def _inproj(x2, g1, w_in, g_q, g_ks, g_kw, S):
    T, D = x2.shape
    H, G = N_HEADS, N_KV_HEADS
    aw = H * HEAD_DIM
    kvw = G * HEAD_DIM
    o = 0
    wq = w_in[:, o:o + aw]; o += aw
    wkv = w_in[:, o:o + 6 * kvw]; o += 6 * kvw
    wng = w_in[:, o:o + 3 * H]; o += 3 * H
    wcv = w_in[:, o:o + 3 * CONV_WIDTH]; o += 3 * CONV_WIDTH
    wmg = w_in[:, o:o + 2 * D]
    wng = jnp.pad(wng, ((0, 0), (0, LANES - 3 * H)))
    wq, wkv, wng, wcv, wmg = (w.astype(BF16) for w in (wq, wkv, wng, wcv, wmg))
    gq = (jnp.tile(g_q, H) * (HEAD_DIM ** -0.5)).reshape(1, aw)
    gk = jnp.concatenate([jnp.tile(g_ks, G), jnp.tile(g_kw, G)]).reshape(1, 2 * kvw)
    idx = np.arange(LANES) // HEAD_DIM
    bd = jnp.asarray((idx[:, None] == idx[None, :]).astype(np.float32) / HEAD_DIM, BF16)
    pos = np.arange(S)
    hi = (pos // 64 * 64).astype(np.float32)[:, None]
    lo = (pos % 64).astype(np.float32)[:, None]
    sl = (2.0 ** (-8.0 * np.arange(1, H + 1) / H)).astype(np.float32)[None, :]
    aq = np.stack([-sl * hi, -sl * lo, np.broadcast_to(sl, (S, H)), np.broadcast_to(sl, (S, H)),
                   np.ones((S, H), np.float32)], axis=-1)
    ak = np.broadcast_to(_key_aug(pos)[:, None, :], (S, G, N_AUG))
    onehot = (pos[:, None] // SEL_BLK == np.arange(LANES)[None, :]).astype(np.float32)
    qtab = _slot_table(aq, LANES)
    kstab = _slot_table(ak, 2 * LANES, onehot)
    kwtab = _slot_table(ak, LANES)
    tm = ROW_TILE
    nst = S // tm
    row = lambda w: pl.BlockSpec((tm, w), lambda i: (i, 0))
    full = lambda a: pl.BlockSpec(a.shape, lambda i: (0,) * a.ndim)
    tab = lambda a: pl.BlockSpec((tm, a.shape[1]), lambda i: (i % nst, 0))
    ins = (x2, g1.reshape(1, D), wq, wkv, wng, wcv, wmg, gq, gk, bd, qtab, kstab, kwtab)
    widths = (H * LANES, G * 2 * LANES, G * LANES, 2 * kvw, LANES, 2 * CONV_WIDTH, 2 * D)
    dtypes = (BF16, BF16, BF16, BF16, F32, BF16, BF16)
    hw = G * CMP_STRIDE * HEAD_DIM
    nb = tm // CMP_STRIDE
    half = pl.BlockSpec((nb, hw), lambda i: (i, 0))
    half_shape = jax.ShapeDtypeStruct((T // CMP_STRIDE, hw), BF16)
    rows = [(row(w), jax.ShapeDtypeStruct((T, w), dt)) for w, dt in zip(widths, dtypes)]
    outs = [rows[0], (half, half_shape), (half, half_shape)] + rows[1:]
    return pl.pallas_call(
        _inproj_body,
        grid=(T // tm,),
        in_specs=[row(D)] + [full(a) for a in ins[1:10]] + [tab(a) for a in ins[10:]],
        out_specs=[o[0] for o in outs],
        out_shape=[o[1] for o in outs],
        scratch_shapes=[pltpu.VMEM((tm, LANES), F32)],
        compiler_params=_cparams(1),
        name="inproj",
    )(*ins)


def _compress_body(h_ref, w1_ref, pe_ref, b1_ref, w2_ref, b2_ref, g_ref, tab_ref, o_ref, *, for_keys):
    hb = h_ref[0]
    nc = hb.shape[0]
    a = _dot(hb, w1_ref[0])
    b = _dot(hb, w1_ref[1])
    c = _dot(pe_ref[0], w1_ref[0]) + _dot(pe_ref[1], w1_ref[1])
    pre = a + pltpu.roll(b, nc - 1, 0) + c[0:1, :] + b1_ref[...]
    hid = jax.nn.gelu(pre)
    out = _dot(hid.astype(BF16), w2_ref[...]) + b2_ref[...]
    if for_keys:
        ms = jnp.sum(out * out, axis=-1, keepdims=True) * (1.0 / HEAD_DIM)
        out = out * lax.rsqrt(ms + EPS) * g_ref[...]
        low = lax.broadcasted_iota(I32, out.shape, 1) < HEAD_DIM
        o_ref[0, 0] = jnp.where(low, out, tab_ref[...]).astype(BF16)
    else:
        o_ref[0, 0] = out.T[0:HEAD_DIM, :].astype(BF16)


def _compress(hh, pe, w1, b1, w2, b2, gain, for_keys):
    B, NC, _ = hh.shape
    G, HW = N_KV_HEADS, CMP_STRIDE * HEAD_DIM
    w1s = w1.reshape(2, HW, CMP_HID).astype(BF16)
    pes = jnp.broadcast_to(pe.reshape(2, 1, HW), (2, 8, HW)).astype(BF16)
    padl = lambda a: jnp.pad(a, ((0, 0), (0, LANES - HEAD_DIM)))
    tabn = np.zeros((NC, LANES), np.float32)
    tabn[:, HEAD_DIM:HEAD_DIM + N_AUG] = _key_aug(np.arange(NC) * CMP_STRIDE + (CMP_LEN - 1))
    tab = jnp.asarray(tabn)
    full = lambda a: pl.BlockSpec(a.shape, lambda b, g: (0,) * a.ndim)
    ins = (hh, w1s, pes, b1.reshape(1, CMP_HID), padl(w2).astype(BF16), padl(b2.reshape(1, HEAD_DIM)),
           padl(gain.reshape(1, HEAD_DIM)), tab)
    oshape = (B, G, NC, LANES) if for_keys else (B, G, HEAD_DIM, NC)
    return pl.pallas_call(
        functools.partial(_compress_body, for_keys=for_keys),
        grid=(B, G),
        in_specs=[pl.BlockSpec((1, NC, HW), lambda b, g: (b, 0, g))] + [full(a) for a in ins[1:]],
        out_specs=pl.BlockSpec((1, 1) + oshape[2:], lambda b, g: (b, g, 0, 0)),
        out_shape=jax.ShapeDtypeStruct(oshape, BF16),
        compiler_params=_cparams(2),
        name="compress_keys" if for_keys else "compress_values",
    )(*ins)


def _attn_body(qa_ref, g_ref, kca_ref, vct_ref, kas_ref, vst_ref, kaw_ref, vwt_ref, selmapt_ref, wbias_ref,
               tilemap_ref, o_ref, qaug_ref, m_ref, acc_ref, flagv_ref, flags_ref, list_ref, sem,
               *, n_sel):
    i = pl.program_id(2)
    q0 = i * Q_TILE
    gl = GROUP_LANES
    for h in range(HEADS_PER_GROUP):
        qaug_ref[h * Q_TILE:(h + 1) * Q_TILE, 0:LANES] = qa_ref[0, :, h * LANES:(h + 1) * LANES]
    qa = qaug_ref[:, 0:LANES]

    nc = kca_ref.shape[2]
    s = _dot_nt(kca_ref[0, 0], qa)
    n_i = lax.broadcasted_iota(I32, (nc, gl), 0)
    t_i = q0 + (lax.broadcasted_iota(I32, (nc, gl), 1) & (Q_TILE - 1))
    valid = t_i >= n_i * CMP_STRIDE + (CMP_LEN - 1)
    s = jnp.where(valid, s, NEG)
    m = jnp.max(s, axis=0, keepdims=True)
    p = jnp.where(valid, jnp.exp(s - m), 0.0)
    l = jnp.sum(p, axis=0, keepdims=True)
    pc = p * jnp.where(l > 0.0, 1.0 / l, 0.0)
    o_c = _dot(vct_ref[0, 0], pc.astype(BF16))

    ps = pc[:, 0:Q_TILE]
    for h in range(1, HEADS_PER_GROUP):
        ps = ps + pc[:, h * Q_TILE:(h + 1) * Q_TILE]
    ps_hi = ps.astype(BF16)
    ps_lo = (ps - ps_hi.astype(F32)).astype(BF16)
    imp = _dot(selmapt_ref[...], ps_hi) + _dot(selmapt_ref[...], ps_lo)
    jb = lax.broadcasted_iota(I32, (LANES, Q_TILE), 0)
    cur = (q0 + lax.broadcasted_iota(I32, (LANES, Q_TILE), 1)) // SEL_BLK
    forced = (jb == 0) | (jb == cur) | (jb == cur - 1)
    score = jnp.where(jb > cur, NEG, jnp.where(forced, POS, imp))
    jbf = jb.astype(F32)
    sel = jnp.zeros((LANES, Q_TILE), F32)
    for _ in range(n_sel):
        mx = jnp.max(score, axis=0, keepdims=True)
        first = jnp.min(jnp.where(score == mx, jbf, float(LANES)), axis=0, keepdims=True)
        hit = jbf == first
        sel = jnp.where(hit, 1.0, sel)
        score = jnp.where(hit, -jnp.inf, score)
    selbias = jnp.where(sel > 0.0, 0.0, -MASK_BIG).T.astype(BF16)
    for h in range(HEADS_PER_GROUP):
        qaug_ref[h * Q_TILE:(h + 1) * Q_TILE, LANES:2 * LANES] = selbias
    tile_hits = jnp.max(_dot(tilemap_ref[...], sel.astype(BF16)), axis=1, keepdims=True)
    flagv_ref[...] = jnp.broadcast_to(tile_hits, flagv_ref.shape).astype(I32)
    flag_copy = pltpu.make_async_copy(flagv_ref, flags_ref, sem)
    flag_copy.start()

    w0 = pl.multiple_of(jnp.maximum(q0 - WINDOW, 0), Q_TILE)
    sw = _dot_nt(kaw_ref[0, pl.ds(w0, WIN_KEYS), :], qa)
    wb = wbias_ref[jnp.minimum(i, WINDOW // Q_TILE)]
    sw = sw + jnp.concatenate([wb] * HEADS_PER_GROUP, axis=1)
    mw = jnp.max(sw, axis=0, keepdims=True)
    pw = jnp.exp(sw - mw)
    c0 = w0 // Q_TILE
    vw = jnp.concatenate([vwt_ref[0, 0, c0 + j] for j in range(WIN_KEYS // Q_TILE)], axis=1)
    aw = _dot(vw, pw.astype(BF16))
    o_w = aw[0:HEAD_DIM] * (1.0 / aw[HEAD_DIM:HEAD_DIM + 1])

    qaug = qaug_ref[...]
    n_full = q0 // KEY_TILE
    kd = pl.multiple_of(n_full * KEY_TILE, KEY_TILE)
    sd = _dot_nt(kas_ref[0, pl.ds(kd, KEY_TILE), :], qaug)
    key = kd + lax.broadcasted_iota(I32, (KEY_TILE, gl), 0)
    qry = q0 + (lax.broadcasted_iota(I32, (KEY_TILE, gl), 1) & (Q_TILE - 1))
    sd = jnp.where(key <= qry, sd, NEG)
    md = jnp.max(sd, axis=0, keepdims=True)
    m_ref[...] = md
    acc_ref[...] = _dot(vst_ref[0, 0, n_full], jnp.exp(sd - md).astype(BF16))

    def tile_step(kt):
        k0 = pl.multiple_of(kt * KEY_TILE, KEY_TILE)
        sc = _dot_nt(kas_ref[0, pl.ds(k0, KEY_TILE), :], qaug)
        m_old = m_ref[...]
        m_new = jnp.maximum(m_old, jnp.max(sc, axis=0, keepdims=True))
        pp = jnp.exp(sc - m_new)
        acc_ref[...] = jnp.exp(m_old - m_new) * acc_ref[...] + _dot(vst_ref[0, 0, kt], pp.astype(BF16))
        m_ref[...] = m_new

    flag_copy.wait()

    def compact(kt, n):
        active = flags_ref[kt, 0] > 0

        @pl.when(active)
        def _():
            list_ref[n] = kt

        return n + active.astype(I32)

    n_active = lax.fori_loop(0, n_full, compact, 0)

    def loop_body(j, carry):
        tile_step(list_ref[j])
        return carry

    lax.fori_loop(0, n_active, loop_body, 0)
    o_s = acc_ref[0:HEAD_DIM, :] * (1.0 / acc_ref[HEAD_DIM:HEAD_DIM + 1, :])

    g = g_ref[0, 0]
    outs = []
    for h in range(HEADS_PER_GROUP):
        sl = slice(h * Q_TILE, (h + 1) * Q_TILE)
        outs.append(g[3 * h:3 * h + 1, :] * o_c[:, sl] + g[3 * h + 1:3 * h + 2, :] * o_s[:, sl]
                    + g[3 * h + 2:3 * h + 3, :] * o_w[:, sl])
    o_ref[0] = jnp.concatenate(outs, axis=0).T.astype(BF16)


def _attention(qa, gates_t, kca, vct, kas, vst, kaw, vwt, B, S):
    G, hd = N_KV_HEADS, HEAD_DIM
    NC = kca.shape[2]
    n_blk = S // SEL_BLK
    assert n_blk <= LANES and S % KEY_TILE == 0 and S >= WIN_KEYS
    n_sel = min(SEL_TOPN, n_blk)
    ratio, span = SEL_BLK // CMP_STRIDE, CMP_LEN // CMP_STRIDE
    sm = np.zeros((LANES, NC), np.float32)
    for j in range(n_blk):
        for a in range(ratio):
            for b in range(span):
                n = ratio * j + a - b
                if 0 <= n < NC - 1:
                    sm[j, n] += 1.0
    selmapt = jnp.asarray(sm, BF16)
    c = np.arange(WIN_KEYS)[:, None]
    r = np.arange(Q_TILE)[None, :]
    offs = np.arange(WINDOW // Q_TILE + 1)[:, None, None] * Q_TILE
    wbias = jnp.asarray(np.where((c - r <= offs) & (c - r > offs - WINDOW), 0.0, NEG), F32)
    tilemap = jnp.asarray(np.arange(LANES)[None, :] // (KEY_TILE // SEL_BLK) == np.arange(N_TILE_ROWS)[:, None],
                          BF16)
    assert S // KEY_TILE <= N_TILE_ROWS

    hpg = HEADS_PER_GROUP
    grp = lambda *blk: pl.BlockSpec((1, 1) + blk, lambda b, g, i: (b, g) + (0,) * len(blk))
    seq = lambda w: pl.BlockSpec((1, S, w), lambda b, g, i: (b, 0, g))
    const = lambda a: pl.BlockSpec(a.shape, lambda b, g, i: (0,) * a.ndim)
    return pl.pallas_call(
        functools.partial(_attn_body, n_sel=n_sel),
        grid=(B, G, S // Q_TILE),
        in_specs=[pl.BlockSpec((1, Q_TILE, hpg * LANES), lambda b, g, i: (b, i, g)),
                  pl.BlockSpec((1, 1, 16, Q_TILE), lambda b, g, i: (b, g, 0, i)),
                  grp(NC, LANES), grp(hd, NC), seq(2 * LANES), grp(S // KEY_TILE, V_ROWS, KEY_TILE),
                  seq(LANES), grp(S // Q_TILE, V_ROWS, Q_TILE), const(selmapt), const(wbias), const(tilemap)],
        out_specs=pl.BlockSpec((1, Q_TILE, hpg * hd), lambda b, g, i: (b, i, g)),
        out_shape=jax.ShapeDtypeStruct((B, S, N_HEADS * hd), BF16),
        scratch_shapes=[pltpu.VMEM((GROUP_LANES, 2 * LANES), BF16), pltpu.VMEM((1, GROUP_LANES), F32),
                        pltpu.VMEM((V_ROWS, GROUP_LANES), F32),
                        pltpu.VMEM((N_TILE_ROWS, LANES), I32), pltpu.SMEM((N_TILE_ROWS, LANES), I32),
                        pltpu.SMEM((N_TILE_ROWS,), I32), pltpu.SemaphoreType.DMA(())],
        compiler_params=_cparams(3),
        name="nsa_attention",
    )(qa, gates_t, kca, vct, kas, vst, kaw, vwt, selmapt, wbias, tilemap)


def _mixer_out_body(x_ref, oa_ref, cbv_ref, halo_ref, gab_ref, cw_ref, wpa_ref, wpb_ref, wo_ref, o_ref,
                    *, seq_len):
    i = pl.program_id(0)
    tm = x_ref.shape[0]
    cwd = CONV_WIDTH
    d = x_ref.shape[1]
    v = cbv_ref[:, cwd:2 * cwd].astype(F32)
    prev = halo_ref[:, cwd:2 * cwd].astype(F32)
    keep = ((i * tm) % seq_len != 0).astype(F32)
    p1 = prev[7:8, :] * keep
    p2 = prev[6:7, :] * keep
    ridx = lax.broadcasted_iota(I32, (tm, cwd), 0)
    v1 = jnp.where(ridx == 0, p1, pltpu.roll(v, 1, 0))
    v2 = jnp.where(ridx == 0, p2, jnp.where(ridx == 1, p1, pltpu.roll(v, 2, 0)))
    y = cw_ref[0:1, :] * v2 + cw_ref[1:2, :] * v1 + cw_ref[2:3, :] * v
    yb_in = (cbv_ref[:, 0:cwd].astype(F32) * y).astype(BF16)
    y_a = _dot(oa_ref[...], wpa_ref[...])
    y_b = _dot(yb_in, wpb_ref[...])
    merged = gab_ref[:, 0:d].astype(F32) * y_a + gab_ref[:, d:2 * d].astype(F32) * y_b
    o_ref[...] = x_ref[...] + _dot(merged.astype(BF16), wo_ref[...])


def _mixer_out(x2, oa, cbv, gab, conv_w, w_pa, w_pb, w_o, seq_len):
    T, D = x2.shape
    tm = ROW_TILE
    cw8 = jnp.pad(conv_w, ((0, 8 - CONV_K), (0, 0)))
    row = lambda w: pl.BlockSpec((tm, w), lambda i: (i, 0))
    full = lambda a: pl.BlockSpec(a.shape, lambda i: (0,) * a.ndim)
    halo = pl.BlockSpec((8, cbv.shape[1]), lambda i: (jnp.maximum(i * (tm // 8) - 1, 0), 0))
    wts = (cw8, w_pa.astype(BF16), w_pb.astype(BF16), w_o.astype(BF16))
    return pl.pallas_call(
        functools.partial(_mixer_out_body, seq_len=seq_len),
        grid=(T // tm,),
        in_specs=[row(D), row(oa.shape[1]), row(cbv.shape[1]), halo, row(gab.shape[1])] + [full(a) for a in wts],
        out_specs=row(D),
        out_shape=jax.ShapeDtypeStruct((T, D), F32),
        compiler_params=_cparams(1),
        name="mixer_out",
    )(x2, oa, cbv, cbv, gab, *wts)


def _router_body(x_ref, g2_ref, whi_ref, wlo_ref, br_ref, tri_ref, h_out, mi_out, mf_out, cnt_out):
    i = pl.program_id(0)

    @pl.when(i == 0)
    def _():
        cnt_out[...] = jnp.zeros(cnt_out.shape, F32)

    x = x_ref[...]
    tm = x.shape[0]
    ms = jnp.mean(x * x, axis=-1, keepdims=True)
    h = x * lax.rsqrt(ms + EPS) * g2_ref[...]
    h_out[...] = h
    h_hi = h.astype(BF16)
    h_lo = (h - h_hi.astype(F32)).astype(BF16)
    logits = (_dot(h_hi, whi_ref[...]) + _dot(h_lo, whi_ref[...]) + _dot(h_hi, wlo_ref[...])) + br_ref[...]
    lane = lax.broadcasted_iota(I32, (tm, LANES), 1)
    lanef = lane.astype(F32)
    work = jnp.where(lane < N_EXPERTS, logits, -jnp.inf)
    vals, hits = [], []
    for _ in range(TOP_K):
        mx = jnp.max(work, axis=-1, keepdims=True)
        first = jnp.min(jnp.where(work == mx, lanef, float(LANES)), axis=-1, keepdims=True)
        hit = lanef == first
        vals.append(mx)
        hits.append(hit)
        work = jnp.where(hit, -jnp.inf, work)
    ex = [jnp.exp(v - vals[0]) for v in vals]
    den = ex[0]
    for e in ex[1:]:
        den = den + e
    cnt = jnp.zeros((tm, LANES), F32)
    for hit in hits:
        cnt = cnt + hit.astype(F32)
    before = _dot(tri_ref[...], cnt.astype(BF16)) + cnt_out[0:1, :]
    mi = jnp.zeros((tm, LANES), F32)
    mf = jnp.zeros((tm, LANES), F32)
    for k, hit in enumerate(hits):
        e_k = jnp.sum(jnp.where(hit, lanef, 0.0), axis=-1, keepdims=True)
        r_k = jnp.sum(jnp.where(hit, before, 0.0), axis=-1, keepdims=True)
        mi = jnp.where(lane == k, e_k, jnp.where(lane == TOP_K + k, r_k, mi))
        mf = jnp.where(lane == k, ex[k] / den, mf)
    mi_out[...] = mi.astype(I32)
    mf_out[...] = mf
    cnt_out[...] = cnt_out[...] + jnp.sum(cnt, axis=0, keepdims=True)


def _router(x1, g2, w_r, b_r):
    T, D = x1.shape
    tm = ROW_TILE
    wpad = jnp.pad(w_r, ((0, 0), (0, LANES - N_EXPERTS)))
    whi = wpad.astype(BF16)
    wlo = (wpad - whi.astype(F32)).astype(BF16)
    br = jnp.pad(b_r, (0, LANES - N_EXPERTS)).reshape(1, LANES)
    tri = jnp.asarray(np.tril(np.ones((tm, tm), np.float32), -1), BF16)
    row = lambda w: pl.BlockSpec((tm, w), lambda i: (i, 0))
    full = lambda a: pl.BlockSpec(a.shape, lambda i: (0,) * a.ndim)
    ins = (x1, g2.reshape(1, D), whi, wlo, br, tri)
    return pl.pallas_call(
        _router_body,
        grid=(T // tm,),
        in_specs=[row(D)] + [full(a) for a in ins[1:]],
        out_specs=[row(D), row(LANES), row(LANES), pl.BlockSpec((8, LANES), lambda i: (0, 0))],
        out_shape=[jax.ShapeDtypeStruct((T, D), F32), jax.ShapeDtypeStruct((T, LANES), I32),
                   jax.ShapeDtypeStruct((T, LANES), F32), jax.ShapeDtypeStruct((8, LANES), F32)],
        compiler_params=_cparams(1),
        name="router",
    )(*ins)


DISPATCH_TILE = 512
DMA_UNROLL = 8


def _dispatch_body(dest_ref, last_ref, h_ref, o_hbm, zero_ref, sem, zsem):
    @pl.when(pl.program_id(0) == 0)
    def _():
        zero_ref[...] = jnp.zeros(zero_ref.shape, zero_ref.dtype)

        def clear(e):
            start = pl.multiple_of(last_ref[e], MOE_CHUNK)
            return pltpu.make_async_copy(zero_ref, o_hbm.at[pl.ds(start, MOE_CHUNK)], zsem)

        for e in range(N_EXPERTS):
            @pl.when(last_ref[e] >= 0)
            def _():
                clear(e).start()
        for e in range(N_EXPERTS):
            @pl.when(last_ref[e] >= 0)
            def _():
                clear(e).wait()

    def row_copy(r, d):
        return pltpu.make_async_copy(h_ref.at[pl.ds(r, 1)], o_hbm.at[pl.ds(d, 1)], sem)

    def start(r, c):
        for k in range(TOP_K):
            row_copy(r, dest_ref[0, 0, r * TOP_K + k]).start()
        return c

    def wait(r, c):
        for k in range(TOP_K):
            row_copy(0, 0).wait()
        return c

    lax.fori_loop(0, DISPATCH_TILE, start, 0, unroll=DMA_UNROLL)
    lax.fori_loop(0, DISPATCH_TILE, wait, 0, unroll=DMA_UNROLL)


def _dispatch(h2, dest, last_chunk, n_rows):
    T, D = h2.shape
    td = DISPATCH_TILE
    dest3 = dest.reshape(T // td, 1, td * TOP_K)
    return pl.pallas_call(
        _dispatch_body,
        grid=(T // td,),
        in_specs=[pl.BlockSpec((1, 1, td * TOP_K), lambda i: (i, 0, 0), memory_space=pltpu.SMEM),
                  pl.BlockSpec(memory_space=pltpu.SMEM), pl.BlockSpec((td, D), lambda i: (i, 0))],
        out_specs=pl.BlockSpec(memory_space=pl.ANY),
        out_shape=jax.ShapeDtypeStruct((n_rows, D), h2.dtype),
        scratch_shapes=[pltpu.VMEM((MOE_CHUNK, D), h2.dtype), pltpu.SemaphoreType.DMA(()),
                        pltpu.SemaphoreType.DMA(())],
        compiler_params=_cparams(1),
        name="dispatch",
    )(dest3, last_chunk, h2)


def _expert_body(ce_ref, nu_ref, x_ref, wgu_ref, bgu_ref, wdn_ref, bdn_ref, o_ref, wgu_bf, wdn_bf):
    c = pl.program_id(0)
    dff = wdn_ref.shape[1]

    @pl.when((c == 0) | (ce_ref[c] != ce_ref[jnp.maximum(c - 1, 0)]))
    def _():
        wgu_bf[...] = wgu_ref[0].astype(BF16)
        wdn_bf[...] = wdn_ref[0].astype(BF16)

    @pl.when(c < nu_ref[0])
    def _():
        gu = _dot(x_ref[...].astype(BF16), wgu_bf[...]) + bgu_ref[0]
        g = jnp.minimum(gu[:, 0:dff], SWIGLU_LIMIT)
        u = jnp.clip(gu[:, dff:2 * dff], -SWIGLU_LIMIT, SWIGLU_LIMIT)
        act = (u + 1.0) * (g * jax.nn.sigmoid(SWIGLU_ALPHA * g))
        o_ref[...] = _dot(act.astype(BF16), wdn_bf[...]) + bdn_ref[0]

    @pl.when(c >= nu_ref[0])
    def _():
        o_ref[...] = jnp.zeros(o_ref.shape, F32)


def _experts(hperm, chunk_e, n_used, w_gu, b_gu, w_dn, b_dn):
    P, D = hperm.shape
    E, _, F2 = w_gu.shape
    dff = F2 // 2
    n_chunks = P // MOE_CHUNK
    grid_spec = pltpu.PrefetchScalarGridSpec(
        num_scalar_prefetch=2,
        grid=(n_chunks,),
        in_specs=[pl.BlockSpec((MOE_CHUNK, D), lambda c, ce, nu: (jnp.minimum(c, nu[0] - 1), 0)),
                  pl.BlockSpec((1, D, F2), lambda c, ce, nu: (ce[c], 0, 0)),
                  pl.BlockSpec((1, 1, F2), lambda c, ce, nu: (ce[c], 0, 0)),
                  pl.BlockSpec((1, dff, D), lambda c, ce, nu: (ce[c], 0, 0)),
                  pl.BlockSpec((1, 1, D), lambda c, ce, nu: (ce[c], 0, 0))],
        out_specs=pl.BlockSpec((MOE_CHUNK, D), lambda c, ce, nu: (c, 0)),
        scratch_shapes=[pltpu.VMEM((D, F2), BF16), pltpu.VMEM((dff, D), BF16)],
    )
    return pl.pallas_call(
        _expert_body,
        grid_spec=grid_spec,
        out_shape=jax.ShapeDtypeStruct((P, D), F32),
        compiler_params=_cparams(1),
        name="experts",
    )(chunk_e, n_used, hperm, w_gu, b_gu.reshape(E, 1, F2), w_dn, b_dn.reshape(E, 1, D))


COMBINE_TILE = 256


def _combine_body(dest_ref, x_ref, w_ref, y_hbm, o_ref, buf_ref, sem):
    def row_copy(r, k, d):
        return pltpu.make_async_copy(y_hbm.at[pl.ds(d, 1)], buf_ref.at[k, pl.ds(r, 1)], sem)

    def start(r, c):
        for k in range(TOP_K):
            row_copy(r, k, dest_ref[0, 0, r * TOP_K + k]).start()
        return c

    def wait(r, c):
        for k in range(TOP_K):
            row_copy(0, 0, 0).wait()
        return c

    lax.fori_loop(0, COMBINE_TILE, start, 0, unroll=DMA_UNROLL)
    lax.fori_loop(0, COMBINE_TILE, wait, 0, unroll=DMA_UNROLL)
    out = x_ref[...]
    for k in range(TOP_K):
        out = out + w_ref[:, k:k + 1] * buf_ref[k]
    o_ref[...] = out


def _combine(x1, gate_w, dest, ys):
    T, D = x1.shape
    tc = COMBINE_TILE
    dest3 = dest.reshape(T // tc, 1, tc * TOP_K)
    row = lambda w: pl.BlockSpec((tc, w), lambda i: (i, 0))
    return pl.pallas_call(
        _combine_body,
        grid=(T // tc,),
        in_specs=[pl.BlockSpec((1, 1, tc * TOP_K), lambda i: (i, 0, 0), memory_space=pltpu.SMEM),
                  row(D), row(LANES), pl.BlockSpec(memory_space=pl.ANY)],
        out_specs=row(D),
        out_shape=jax.ShapeDtypeStruct((T, D), F32),
        scratch_shapes=[pltpu.VMEM((TOP_K, tc, D), F32), pltpu.SemaphoreType.DMA(())],
        compiler_params=_cparams(1),
        name="combine",
    )(dest3, x1, gate_w, ys)


def _mixer(x2, B, S, g_norm1, w_in, g_q, g_kc, g_ks, g_kw, pe_k, ck_w1, ck_b1, ck_w2, ck_b2,
           pe_v, cv_w1, cv_b1, cv_w2, cv_b2, conv_w, w_pa, w_pb, w_o):
    T, D = x2.shape
    G, H, hd = N_KV_HEADS, N_HEADS, HEAD_DIM
    qa, hk, hv, kas, kaw, vsw, gates, cbv, gab = _inproj(x2, g_norm1, w_in, g_q, g_ks, g_kw, S)
    nh = S // CMP_STRIDE
    kca = _compress(hk.reshape(B, nh, -1), pe_k, ck_w1, ck_b1, ck_w2, ck_b2, g_kc, True)
    vct = _compress(hv.reshape(B, nh, -1), pe_v, cv_w1, cv_b1, cv_w2, cv_b2, jnp.ones((hd,), F32), False)
    vsw5 = vsw.reshape(B, S, 2, G, hd)
    ones_rows = jnp.concatenate([jnp.ones((1,), BF16), jnp.zeros((V_ROWS - hd - 1,), BF16)])

    def key_major(v, tile):
        vt = v.reshape(B, S // tile, tile, G, hd).transpose(0, 3, 1, 4, 2)
        extra = jnp.broadcast_to(ones_rows[None, None, None, :, None], vt.shape[:3] + (V_ROWS - hd, tile))
        return jnp.concatenate([vt, extra], axis=3)

    vst = key_major(vsw5[:, :, 0], KEY_TILE)
    vwt = key_major(vsw5[:, :, 1], Q_TILE)
    gat = gates[:, :3 * H].reshape(B, S, G, 3 * HEADS_PER_GROUP).transpose(0, 2, 3, 1)
    gat = jnp.pad(gat, ((0, 0), (0, 0), (0, 16 - 3 * HEADS_PER_GROUP), (0, 0)))
    o = _attention(qa.reshape(B, S, -1), gat, kca, vct, kas.reshape(B, S, -1), vst, kaw.reshape(B, S, -1),
                   vwt, B, S)
    return _mixer_out(x2, o.reshape(T, H * hd), cbv, gab, conv_w, w_pa, w_pb, w_o, S)


def _moe(x1, g_norm2, w_r, b_r, w_gu, b_gu, w_dn, b_dn):
    T, D = x1.shape
    h2, mi, mf, cnt = _router(x1, g_norm2, w_r, b_r)
    top_e = mi[:, 0:TOP_K]
    rank = mi[:, TOP_K:2 * TOP_K]
    counts = cnt[0, :N_EXPERTS].astype(I32)
    padded = (counts + MOE_CHUNK - 1) // MOE_CHUNK * MOE_CHUNK
    pend = jnp.cumsum(padded)
    poffs = pend - padded
    dest = (poffs[top_e] + rank).reshape(-1)
    n_chunks = (T * TOP_K + MOE_CHUNK - 1) // MOE_CHUNK + N_EXPERTS
    chunk_start = jnp.arange(n_chunks, dtype=I32) * MOE_CHUNK
    chunk_e = jnp.minimum(jnp.sum((pend[None, :] <= chunk_start[:, None]).astype(I32), axis=1), N_EXPERTS - 1)
    n_used = (pend[-1:] // MOE_CHUNK).astype(I32)
    last_chunk = jnp.where(padded > 0, pend - MOE_CHUNK, -1).astype(I32)
    hperm = _dispatch(h2, dest, last_chunk, n_chunks * MOE_CHUNK)
    ys = _experts(hperm, chunk_e, n_used, w_gu, b_gu, w_dn, b_dn)
    return _combine(x1, mf, dest, ys)


def kernel(x, g_norm1, w_in, g_q, g_kc, g_ks, g_kw, pe_k, ck_w1, ck_b1, ck_w2, ck_b2, pe_v, cv_w1, cv_b1,
           cv_w2, cv_b2, conv_w, w_pa, w_pb, w_o, g_norm2, w_r, b_r, w_gu, b_gu, w_dn, b_dn):
    B, S, D = x.shape
    x2 = x.reshape(B * S, D)
    for l in range(g_norm1.shape[0]):
        x2 = _mixer(x2, B, S, g_norm1[l], w_in[l], g_q[l], g_kc[l], g_ks[l], g_kw[l], pe_k[l], ck_w1[l],
                    ck_b1[l], ck_w2[l], ck_b2[l], pe_v[l], cv_w1[l], cv_b1[l], cv_w2[l], cv_b2[l],
                    conv_w[l], w_pa[l], w_pb[l], w_o[l])
        x2 = _moe(x2, g_norm2[l], w_r[l], b_r[l], w_gu[l], b_gu[l], w_dn[l], b_dn[l])
    return x2.reshape(B, S, D)
```

```python
import functools

import numpy as np
import jax
import jax.numpy as jnp
from jax import lax
from jax.experimental import pallas as pl
from jax.experimental.pallas import tpu as pltpu

F32 = jnp.float32
BF16 = jnp.bfloat16
I32 = jnp.int32

N_HEADS = 8
HEAD_DIM = 64
N_KV_HEADS = 2
HEADS_PER_GROUP = N_HEADS // N_KV_HEADS
CMP_LEN = 32
CMP_STRIDE = 16
CMP_HID = 256
SEL_BLK = 64
SEL_TOPN = 16
WINDOW = 512
CONV_WIDTH = 512
CONV_K = 3
N_EXPERTS = 32
TOP_K = 4
SWIGLU_LIMIT = 7.0
SWIGLU_ALPHA = 1.702
MOE_CHUNK = 512
EPS = 1e-6
NEG = -1e30
POS = 1e30
MASK_BIG = 2.0 ** 100

LANES = 128
Q_TILE = 128
KEY_TILE = 512
N_AUG = 5
ROW_TILE = 512
VMEM_LIMIT = 56 * 1024 * 1024
GROUP_LANES = HEADS_PER_GROUP * Q_TILE
WIN_KEYS = WINDOW + Q_TILE
N_TILE_ROWS = 16
V_ROWS = HEAD_DIM + 8


def _cparams(n_axes):
    return pltpu.CompilerParams(dimension_semantics=("arbitrary",) * n_axes,
                                vmem_limit_bytes=VMEM_LIMIT)


def _dot(a, b):
    return jnp.dot(a, b, preferred_element_type=F32)


def _dot_nt(a, b):
    return lax.dot_general(a, b, (((1,), (1,)), ((), ())), preferred_element_type=F32)


def _rms_pairs(v, bd):
    ss = _dot((v * v).astype(BF16), bd)
    return v * lax.rsqrt(ss + EPS)


def _inproj_body(x_ref, g1_ref, wq_ref, wkv_ref, wng_ref, wcv_ref, wmg_ref, gq_ref, gk_ref, bd_ref,
                 qtab_ref, kstab_ref, kwtab_ref,
                 qa_out, hk_out, hv_out, kas_out, kaw_out, vsw_out, gate_out, cbv_out, gab_out, raw_ref):
    x = x_ref[...]
    tm = x.shape[0]
    ms = jnp.mean(x * x, axis=-1, keepdims=True)
    h = (x * lax.rsqrt(ms + EPS) * g1_ref[...]).astype(BF16)
    bd = bd_ref[...]
    low = lax.broadcasted_iota(I32, (tm, LANES), 1) < HEAD_DIM

    def place(pair, tab_ref, out_ref, base, slot):
        for j, src in enumerate((pair, pltpu.roll(pair, HEAD_DIM, 1))):
            o = base + j * slot
            out_ref[:, o:o + LANES] = jnp.where(low, src, tab_ref[:, o:o + LANES].astype(F32)).astype(BF16)

    q = _dot(h, wq_ref[...])
    for c in range(N_HEADS * HEAD_DIM // LANES):
        sl = slice(c * LANES, (c + 1) * LANES)
        place(_rms_pairs(q[:, sl], bd) * gq_ref[:, sl], qtab_ref, qa_out, 2 * c * LANES, LANES)
    kv = _dot(h, wkv_ref[...])

    def emit_half_blocks(c, out_ref):
        nb = tm // CMP_STRIDE
        hw = CMP_STRIDE * HEAD_DIM
        raw_ref[...] = kv[:, c * LANES:(c + 1) * LANES]
        lo = lax.broadcasted_iota(I32, (nb, LANES), 1) < HEAD_DIM
        for u in range(CMP_STRIDE // 2):
            t0 = raw_ref[pl.ds(2 * u, nb, stride=CMP_STRIDE), :]
            t1 = raw_ref[pl.ds(2 * u + 1, nb, stride=CMP_STRIDE), :]
            out_ref[:, u * LANES:(u + 1) * LANES] = jnp.where(lo, t0, pltpu.roll(t1, HEAD_DIM, 1)).astype(BF16)
            out_ref[:, hw + u * LANES:hw + (u + 1) * LANES] = (
                jnp.where(lo, pltpu.roll(t0, HEAD_DIM, 1), t1).astype(BF16))

    emit_half_blocks(0, hk_out)
    emit_half_blocks(1, hv_out)
    place(_rms_pairs(kv[:, 256:384], bd) * gk_ref[:, 0:128], kstab_ref, kas_out, 0, 2 * LANES)
    for g in range(N_KV_HEADS):
        o = (2 * g + 1) * LANES
        kas_out[:, o:o + LANES] = kstab_ref[:, o:o + LANES]
    place(_rms_pairs(kv[:, 512:640], bd) * gk_ref[:, 128:256], kwtab_ref, kaw_out, 0, LANES)
    vsw_out[:, 0:128] = kv[:, 384:512].astype(BF16)
    vsw_out[:, 128:256] = kv[:, 640:768].astype(BF16)
    gate_out[...] = jax.nn.sigmoid(_dot(h, wng_ref[...]))
    cv = _dot(h, wcv_ref[...])
    cw = CONV_WIDTH
    cbv_out[:, 0:cw] = cv[:, 0:cw].astype(BF16)
    cbv_out[:, cw:2 * cw] = (cv[:, cw:2 * cw] * cv[:, 2 * cw:3 * cw]).astype(BF16)
    gab_out[...] = jax.nn.sigmoid(_dot(h, wmg_ref[...])).astype(BF16)


def _key_aug(pos):
    one = np.ones_like(pos, np.float32)
    return np.stack([one, one, (pos // 64 * 64).astype(np.float32), (pos % 64).astype(np.float32), 0 * one],
                    axis=-1)


def _slot_table(aug, slot, extra=None):
    S, n, _ = aug.shape
    tab = np.zeros((S, n, slot), np.float32)
    tab[:, :, HEAD_DIM:HEAD_DIM + N_AUG] = aug
    if extra is not None:
        tab[:, :, LANES:] = extra[:, None, :]
    return jnp.asarray(tab.reshape(S, n * slot), BF16)


def _inproj(x2, g1, w_in, g_q, g_ks, g_kw, S):
    T, D = x2.shape
    H, G = N_HEADS, N_KV_HEADS
    aw = H * HEAD_DIM
    kvw = G * HEAD_DIM
    o = 0
    wq = w_in[:, o:o + aw]; o += aw
    wkv = w_in[:, o:o + 6 * kvw]; o += 6 * kvw
    wng = w_in[:, o:o + 3 * H]; o += 3 * H
    wcv = w_in[:, o:o + 3 * CONV_WIDTH]; o += 3 * CONV_WIDTH
    wmg = w_in[:, o:o + 2 * D]
    wng = jnp.pad(wng, ((0, 0), (0, LANES - 3 * H)))
    wq, wkv, wng, wcv, wmg = (w.astype(BF16) for w in (wq, wkv, wng, wcv, wmg))
    gq = (jnp.tile(g_q, H) * (HEAD_DIM ** -0.5)).reshape(1, aw)
    gk = jnp.concatenate([jnp.tile(g_ks, G), jnp.tile(g_kw, G)]).reshape(1, 2 * kvw)
    idx = np.arange(LANES) // HEAD_DIM
    bd = jnp.asarray((idx[:, None] == idx[None, :]).astype(np.float32) / HEAD_DIM, BF16)
    pos = np.arange(S)
    hi = (pos // 64 * 64).astype(np.float32)[:, None]
    lo = (pos % 64).astype(np.float32)[:, None]
    sl = (2.0 ** (-8.0 * np.arange(1, H + 1) / H)).astype(np.float32)[None, :]
    aq = np.stack([-sl * hi, -sl * lo, np.broadcast_to(sl, (S, H)), np.broadcast_to(sl, (S, H)),
                   np.ones((S, H), np.float32)], axis=-1)
    ak = np.broadcast_to(_key_aug(pos)[:, None, :], (S, G, N_AUG))
    onehot = (pos[:, None] // SEL_BLK == np.arange(LANES)[None, :]).astype(np.float32)
    qtab = _slot_table(aq, LANES)
    kstab = _slot_table(ak, 2 * LANES, onehot)
    kwtab = _slot_table(ak, LANES)
    tm = ROW_TILE
    nst = S // tm
    row = lambda w: pl.BlockSpec((tm, w), lambda i: (i, 0))
    full = lambda a: pl.BlockSpec(a.shape, lambda i: (0,) * a.ndim)
    tab = lambda a: pl.BlockSpec((tm, a.shape[1]), lambda i: (i % nst, 0))
    ins = (x2, g1.reshape(1, D), wq, wkv, wng, wcv, wmg, gq, gk, bd, qtab, kstab, kwtab)
    widths = (H * LANES, G * 2 * LANES, G * LANES, 2 * kvw, LANES, 2 * CONV_WIDTH, 2 * D)
    dtypes = (BF16, BF16, BF16, BF16, F32, BF16, BF16)
    hw = G * CMP_STRIDE * HEAD_DIM
    nb = tm // CMP_STRIDE
    half = pl.BlockSpec((nb, hw), lambda i: (i, 0))
    half_shape = jax.ShapeDtypeStruct((T // CMP_STRIDE, hw), BF16)
    rows = [(row(w), jax.ShapeDtypeStruct((T, w), dt)) for w, dt in zip(widths, dtypes)]
    outs = [rows[0], (half, half_shape), (half, half_shape)] + rows[1:]
    return pl.pallas_call(
        _inproj_body,
        grid=(T // tm,),
        in_specs=[row(D)] + [full(a) for a in ins[1:10]] + [tab(a) for a in ins[10:]],
        out_specs=[o[0] for o in outs],
        out_shape=[o[1] for o in outs],
        scratch_shapes=[pltpu.VMEM((tm, LANES), F32)],
        compiler_params=_cparams(1),
        name="inproj",
    )(*ins)


def _compress_body(h_ref, w1_ref, pe_ref, b1_ref, w2_ref, b2_ref, g_ref, tab_ref, o_ref, *, for_keys):
    hb = h_ref[0]
    nc = hb.shape[0]
    a = _dot(hb, w1_ref[0])
    b = _dot(hb, w1_ref[1])
    c = _dot(pe_ref[0], w1_ref[0]) + _dot(pe_ref[1], w1_ref[1])
    pre = a + pltpu.roll(b, nc - 1, 0) + c[0:1, :] + b1_ref[...]
    hid = jax.nn.gelu(pre)
    out = _dot(hid.astype(BF16), w2_ref[...]) + b2_ref[...]
    if for_keys:
        ms = jnp.sum(out * out, axis=-1, keepdims=True) * (1.0 / HEAD_DIM)
        out = out * lax.rsqrt(ms + EPS) * g_ref[...]
        low = lax.broadcasted_iota(I32, out.shape, 1) < HEAD_DIM
        o_ref[0, 0] = jnp.where(low, out, tab_ref[...]).astype(BF16)
    else:
        o_ref[0, 0] = out.T[0:HEAD_DIM, :].astype(BF16)


def _compress(hh, pe, w1, b1, w2, b2, gain, for_keys):
    B, NC, _ = hh.shape
    G, HW = N_KV_HEADS, CMP_STRIDE * HEAD_DIM
    w1s = w1.reshape(2, HW, CMP_HID).astype(BF16)
    pes = jnp.broadcast_to(pe.reshape(2, 1, HW), (2, 8, HW)).astype(BF16)
    padl = lambda a: jnp.pad(a, ((0, 0), (0, LANES - HEAD_DIM)))
    tabn = np.zeros((NC, LANES), np.float32)
    tabn[:, HEAD_DIM:HEAD_DIM + N_AUG] = _key_aug(np.arange(NC) * CMP_STRIDE + (CMP_LEN - 1))
    tab = jnp.asarray(tabn)
    full = lambda a: pl.BlockSpec(a.shape, lambda b, g: (0,) * a.ndim)
    ins = (hh, w1s, pes, b1.reshape(1, CMP_HID), padl(w2).astype(BF16), padl(b2.reshape(1, HEAD_DIM)),
           padl(gain.reshape(1, HEAD_DIM)), tab)
    oshape = (B, G, NC, LANES) if for_keys else (B, G, HEAD_DIM, NC)
    return pl.pallas_call(
        functools.partial(_compress_body, for_keys=for_keys),
        grid=(B, G),
        in_specs=[pl.BlockSpec((1, NC, HW), lambda b, g: (b, 0, g))] + [full(a) for a in ins[1:]],
        out_specs=pl.BlockSpec((1, 1) + oshape[2:], lambda b, g: (b, g, 0, 0)),
        out_shape=jax.ShapeDtypeStruct(oshape, BF16),
        compiler_params=_cparams(2),
        name="compress_keys" if for_keys else "compress_values",
    )(*ins)


def _attn_body(qa_ref, g_ref, kca_ref, vct_ref, kas_ref, vst_ref, kaw_ref, vwt_ref, selmapt_ref, wbias_ref,
               tilemap_ref, cbias_ref, o_ref, qaug_ref, m_ref, acc_ref, flagv_ref, flags_ref, list_ref, sem,
               *, n_sel):
    i = pl.program_id(2)
    q0 = i * Q_TILE
    gl = GROUP_LANES
    for h in range(HEADS_PER_GROUP):
        qaug_ref[h * Q_TILE:(h + 1) * Q_TILE, 0:LANES] = qa_ref[0, :, h * LANES:(h + 1) * LANES]
    qa = qaug_ref[:, 0:LANES]

    nc = kca_ref.shape[2]
    s = _dot_nt(kca_ref[0, 0], qa)
    cb = cbias_ref[pl.ds(pl.multiple_of(nc - i * (Q_TILE // CMP_STRIDE), 8), nc), :]
    s = s + jnp.concatenate([cb] * HEADS_PER_GROUP, axis=1)
    m = jnp.max(s, axis=0, keepdims=True)
    p = jnp.exp(s - m)
    l = jnp.sum(p, axis=0, keepdims=True)
    has_entry = (q0 + (lax.broadcasted_iota(I32, (1, gl), 1) & (Q_TILE - 1))) >= CMP_LEN - 1
    pc = p * jnp.where(has_entry, 1.0 / l, 0.0)
    o_c = _dot(vct_ref[0, 0], pc.astype(BF16))

    ps = pc[:, 0:Q_TILE]
    for h in range(1, HEADS_PER_GROUP):
        ps = ps + pc[:, h * Q_TILE:(h + 1) * Q_TILE]
    ps_hi = ps.astype(BF16)
    ps_lo = (ps - ps_hi.astype(F32)).astype(BF16)
    imp = _dot(selmapt_ref[...], ps_hi) + _dot(selmapt_ref[...], ps_lo)
    jb = lax.broadcasted_iota(I32, (LANES, Q_TILE), 0)
    cur = (q0 + lax.broadcasted_iota(I32, (LANES, Q_TILE), 1)) // SEL_BLK
    forced = (jb == 0) | (jb == cur) | (jb == cur - 1)
    score = jnp.where(jb > cur, NEG, jnp.where(forced, POS, imp))
    jbf = jb.astype(F32)
    for _ in range(n_sel):
        mx = jnp.max(score, axis=0, keepdims=True)
        first = jnp.min(jnp.where(score == mx, jbf, float(LANES)), axis=0, keepdims=True)
        score = jnp.where(jbf == first, -jnp.inf, score)
    picked = score == -jnp.inf
    sel = jnp.where(picked, 1.0, 0.0)
    selbias = jnp.where(picked, 0.0, -MASK_BIG).T.astype(BF16)
    for h in range(HEADS_PER_GROUP):
        qaug_ref[h * Q_TILE:(h + 1) * Q_TILE, LANES:2 * LANES] = selbias
    tile_hits = jnp.max(_dot(tilemap_ref[...], sel.astype(BF16)), axis=1, keepdims=True)
    flagv_ref[...] = jnp.broadcast_to(tile_hits, flagv_ref.shape).astype(I32)
    flag_copy = pltpu.make_async_copy(flagv_ref, flags_ref, sem)
    flag_copy.start()

    w0 = pl.multiple_of(jnp.maximum(q0 - WINDOW, 0), Q_TILE)
    sw = _dot_nt(kaw_ref[0, pl.ds(w0, WIN_KEYS), :], qa)
    wb = wbias_ref[jnp.minimum(i, WINDOW // Q_TILE)]
    sw = sw + jnp.concatenate([wb] * HEADS_PER_GROUP, axis=1)
    mw = jnp.max(sw, axis=0, keepdims=True)
    pw = jnp.exp(sw - mw)
    c0 = w0 // Q_TILE
    vw = jnp.concatenate([vwt_ref[0, 0, c0 + j] for j in range(WIN_KEYS // Q_TILE)], axis=1)
    aw = _dot(vw, pw.astype(BF16))
    o_w = aw[0:HEAD_DIM] * (1.0 / aw[HEAD_DIM:HEAD_DIM + 1])

    qaug = qaug_ref[...]
    n_full = q0 // KEY_TILE
    kd = pl.multiple_of(n_full * KEY_TILE, KEY_TILE)
    sd = _dot_nt(kas_ref[0, pl.ds(kd, KEY_TILE), :], qaug)
    key = kd + lax.broadcasted_iota(I32, (KEY_TILE, gl), 0)
    qry = q0 + (lax.broadcasted_iota(I32, (KEY_TILE, gl), 1) & (Q_TILE - 1))
    sd = jnp.where(key <= qry, sd, NEG)
    md = jnp.max(sd, axis=0, keepdims=True)
    m_ref[...] = md
    acc_ref[...] = _dot(vst_ref[0, 0, n_full], jnp.exp(sd - md).astype(BF16))

    def tile_step(tiles):
        scs = [_dot_nt(kas_ref[0, pl.ds(pl.multiple_of(kt * KEY_TILE, KEY_TILE), KEY_TILE), :], qaug)
               for kt in tiles]
        m_old = m_ref[...]
        m_new = m_old
        for sc in scs:
            m_new = jnp.maximum(m_new, jnp.max(sc, axis=0, keepdims=True))
        upd = jnp.exp(m_old - m_new) * acc_ref[...]
        for kt, sc in zip(tiles, scs):
            upd = upd + _dot(vst_ref[0, 0, kt], jnp.exp(sc - m_new).astype(BF16))
        acc_ref[...] = upd
        m_ref[...] = m_new

    flag_copy.wait()

    def compact(kt, n):
        active = flags_ref[kt, 0] > 0

        @pl.when(active)
        def _():
            list_ref[n] = kt

        return n + active.astype(I32)

    n_active = lax.fori_loop(0, n_full, compact, 0)

    def pair_body(j, carry):
        tile_step((list_ref[2 * j], list_ref[2 * j + 1]))
        return carry

    lax.fori_loop(0, n_active // 2, pair_body, 0)

    @pl.when(n_active % 2 == 1)
    def _():
        tile_step((list_ref[n_active - 1],))

    o_s = acc_ref[0:HEAD_DIM, :] * (1.0 / acc_ref[HEAD_DIM:HEAD_DIM + 1, :])

    g = g_ref[0, 0]
    outs = []
    for h in range(HEADS_PER_GROUP):
        sl = slice(h * Q_TILE, (h + 1) * Q_TILE)
        outs.append(g[3 * h:3 * h + 1, :] * o_c[:, sl] + g[3 * h + 1:3 * h + 2, :] * o_s[:, sl]
                    + g[3 * h + 2:3 * h + 3, :] * o_w[:, sl])
    o_ref[0] = jnp.concatenate(outs, axis=0).T.astype(BF16)


def _attention(qa, gates_t, kca, vct, kas, vst, kaw, vwt, B, S):
    G, hd = N_KV_HEADS, HEAD_DIM
    NC = kca.shape[2]
    n_blk = S // SEL_BLK
    assert n_blk <= LANES and S % KEY_TILE == 0 and S >= WIN_KEYS
    n_sel = min(SEL_TOPN, n_blk)
    ratio, span = SEL_BLK // CMP_STRIDE, CMP_LEN // CMP_STRIDE
    sm = np.zeros((LANES, NC), np.float32)
    for j in range(n_blk):
        for a in range(ratio):
            for b in range(span):
                n = ratio * j + a - b
                if 0 <= n < NC - 1:
                    sm[j, n] += 1.0
    selmapt = jnp.asarray(sm, BF16)
    c = np.arange(WIN_KEYS)[:, None]
    r = np.arange(Q_TILE)[None, :]
    offs = np.arange(WINDOW // Q_TILE + 1)[:, None, None] * Q_TILE
    wbias = jnp.asarray(np.where((c - r <= offs) & (c - r > offs - WINDOW), 0.0, NEG), F32)
    tilemap = jnp.asarray(np.arange(LANES)[None, :] // (KEY_TILE // SEL_BLK) == np.arange(N_TILE_ROWS)[:, None],
                          BF16)
    assert S // KEY_TILE <= N_TILE_ROWS
    u = np.arange(2 * NC)[:, None] - NC
    cbias = jnp.asarray(np.where(CMP_STRIDE * u + (CMP_LEN - 1) <= np.arange(Q_TILE)[None, :], 0.0, NEG), F32)

    hpg = HEADS_PER_GROUP
    grp = lambda *blk: pl.BlockSpec((1, 1) + blk, lambda b, g, i: (b, g) + (0,) * len(blk))
    seq = lambda w: pl.BlockSpec((1, S, w), lambda b, g, i: (b, 0, g))
    const = lambda a: pl.BlockSpec(a.shape, lambda b, g, i: (0,) * a.ndim)
    return pl.pallas_call(
        functools.partial(_attn_body, n_sel=n_sel),
        grid=(B, G, S // Q_TILE),
        in_specs=[pl.BlockSpec((1, Q_TILE, hpg * LANES), lambda b, g, i: (b, i, g)),
                  pl.BlockSpec((1, 1, 16, Q_TILE), lambda b, g, i: (b, g, 0, i)),
                  grp(NC, LANES), grp(hd, NC), seq(2 * LANES), grp(S // KEY_TILE, V_ROWS, KEY_TILE),
                  seq(LANES), grp(S // Q_TILE, V_ROWS, Q_TILE), const(selmapt), const(wbias), const(tilemap),
                  const(cbias)],
        out_specs=pl.BlockSpec((1, Q_TILE, hpg * hd), lambda b, g, i: (b, i, g)),
        out_shape=jax.ShapeDtypeStruct((B, S, N_HEADS * hd), BF16),
        scratch_shapes=[pltpu.VMEM((GROUP_LANES, 2 * LANES), BF16), pltpu.VMEM((1, GROUP_LANES), F32),
                        pltpu.VMEM((V_ROWS, GROUP_LANES), F32),
                        pltpu.VMEM((N_TILE_ROWS, LANES), I32), pltpu.SMEM((N_TILE_ROWS, LANES), I32),
                        pltpu.SMEM((N_TILE_ROWS,), I32), pltpu.SemaphoreType.DMA(())],
        compiler_params=_cparams(3),
        name="nsa_attention",
    )(qa, gates_t, kca, vct, kas, vst, kaw, vwt, selmapt, wbias, tilemap, cbias)


def _mixer_out_body(x_ref, oa_ref, cbv_ref, halo_ref, gab_ref, cw_ref, wpa_ref, wpb_ref, wo_ref, o_ref,
                    *, seq_len):
    i = pl.program_id(0)
    tm = x_ref.shape[0]
    cwd = CONV_WIDTH
    d = x_ref.shape[1]
    v = cbv_ref[:, cwd:2 * cwd].astype(F32)
    prev = halo_ref[:, cwd:2 * cwd].astype(F32)
    keep = ((i * tm) % seq_len != 0).astype(F32)
    p1 = prev[7:8, :] * keep
    p2 = prev[6:7, :] * keep
    ridx = lax.broadcasted_iota(I32, (tm, cwd), 0)
    v1 = jnp.where(ridx == 0, p1, pltpu.roll(v, 1, 0))
    v2 = jnp.where(ridx == 0, p2, jnp.where(ridx == 1, p1, pltpu.roll(v, 2, 0)))
    y = cw_ref[0:1, :] * v2 + cw_ref[1:2, :] * v1 + cw_ref[2:3, :] * v
    yb_in = (cbv_ref[:, 0:cwd].astype(F32) * y).astype(BF16)
    y_a = _dot(oa_ref[...], wpa_ref[...])
    y_b = _dot(yb_in, wpb_ref[...])
    merged = gab_ref[:, 0:d].astype(F32) * y_a + gab_ref[:, d:2 * d].astype(F32) * y_b
    o_ref[...] = x_ref[...] + _dot(merged.astype(BF16), wo_ref[...])


def _mixer_out(x2, oa, cbv, gab, conv_w, w_pa, w_pb, w_o, seq_len):
    T, D = x2.shape
    tm = ROW_TILE
    cw8 = jnp.pad(conv_w, ((0, 8 - CONV_K), (0, 0)))
    row = lambda w: pl.BlockSpec((tm, w), lambda i: (i, 0))
    full = lambda a: pl.BlockSpec(a.shape, lambda i: (0,) * a.ndim)
    halo = pl.BlockSpec((8, cbv.shape[1]), lambda i: (jnp.maximum(i * (tm // 8) - 1, 0), 0))
    wts = (cw8, w_pa.astype(BF16), w_pb.astype(BF16), w_o.astype(BF16))
    return pl.pallas_call(
        functools.partial(_mixer_out_body, seq_len=seq_len),
        grid=(T // tm,),
        in_specs=[row(D), row(oa.shape[1]), row(cbv.shape[1]), halo, row(gab.shape[1])] + [full(a) for a in wts],
        out_specs=row(D),
        out_shape=jax.ShapeDtypeStruct((T, D), F32),
        compiler_params=_cparams(1),
        name="mixer_out",
    )(x2, oa, cbv, cbv, gab, *wts)


def _router_body(x_ref, g2_ref, whi_ref, wlo_ref, br_ref, tri_ref, h_out, mi_out, mf_out, cnt_out):
    i = pl.program_id(0)

    @pl.when(i == 0)
    def _():
        cnt_out[...] = jnp.zeros(cnt_out.shape, F32)

    x = x_ref[...]
    tm = x.shape[0]
    ms = jnp.mean(x * x, axis=-1, keepdims=True)
    h = x * lax.rsqrt(ms + EPS) * g2_ref[...]
    h_out[...] = h
    h_hi = h.astype(BF16)
    h_lo = (h - h_hi.astype(F32)).astype(BF16)
    logits = (_dot(h_hi, whi_ref[...]) + _dot(h_lo, whi_ref[...]) + _dot(h_hi, wlo_ref[...])) + br_ref[...]
    lane = lax.broadcasted_iota(I32, (tm, LANES), 1)
    lanef = lane.astype(F32)
    work = jnp.where(lane < N_EXPERTS, logits, -jnp.inf)
    vals, hits = [], []
    for _ in range(TOP_K):
        mx = jnp.max(work, axis=-1, keepdims=True)
        first = jnp.min(jnp.where(work == mx, lanef, float(LANES)), axis=-1, keepdims=True)
        hit = lanef == first
        vals.append(mx)
        hits.append(hit)
        work = jnp.where(hit, -jnp.inf, work)
    ex = [jnp.exp(v - vals[0]) for v in vals]
    den = ex[0]
    for e in ex[1:]:
        den = den + e
    cnt = jnp.zeros((tm, LANES), F32)
    for hit in hits:
        cnt = cnt + hit.astype(F32)
    before = _dot(tri_ref[...], cnt.astype(BF16)) + cnt_out[0:1, :]
    mi = jnp.zeros((tm, LANES), F32)
    mf = jnp.zeros((tm, LANES), F32)
    for k, hit in enumerate(hits):
        e_k = jnp.sum(jnp.where(hit, lanef, 0.0), axis=-1, keepdims=True)
        r_k = jnp.sum(jnp.where(hit, before, 0.0), axis=-1, keepdims=True)
        mi = jnp.where(lane == k, e_k, jnp.where(lane == TOP_K + k, r_k, mi))
        mf = jnp.where(lane == k, ex[k] / den, mf)
    mi_out[...] = mi.astype(I32)
    mf_out[...] = mf
    cnt_out[...] = cnt_out[...] + jnp.sum(cnt, axis=0, keepdims=True)


def _router(x1, g2, w_r, b_r):
    T, D = x1.shape
    tm = ROW_TILE
    wpad = jnp.pad(w_r, ((0, 0), (0, LANES - N_EXPERTS)))
    whi = wpad.astype(BF16)
    wlo = (wpad - whi.astype(F32)).astype(BF16)
    br = jnp.pad(b_r, (0, LANES - N_EXPERTS)).reshape(1, LANES)
    tri = jnp.asarray(np.tril(np.ones((tm, tm), np.float32), -1), BF16)
    row = lambda w: pl.BlockSpec((tm, w), lambda i: (i, 0))
    full = lambda a: pl.BlockSpec(a.shape, lambda i: (0,) * a.ndim)
    ins = (x1, g2.reshape(1, D), whi, wlo, br, tri)
    return pl.pallas_call(
        _router_body,
        grid=(T // tm,),
        in_specs=[row(D)] + [full(a) for a in ins[1:]],
        out_specs=[row(D), row(LANES), row(LANES), pl.BlockSpec((8, LANES), lambda i: (0, 0))],
        out_shape=[jax.ShapeDtypeStruct((T, D), F32), jax.ShapeDtypeStruct((T, LANES), I32),
                   jax.ShapeDtypeStruct((T, LANES), F32), jax.ShapeDtypeStruct((8, LANES), F32)],
        compiler_params=_cparams(1),
        name="router",
    )(*ins)


DISPATCH_TILE = 512
DMA_UNROLL = 8


def _dispatch_body(dest_ref, last_ref, h_ref, o_hbm, zero_ref, sem, zsem):
    @pl.when(pl.program_id(0) == 0)
    def _():
        zero_ref[...] = jnp.zeros(zero_ref.shape, zero_ref.dtype)

        def clear(e):
            start = pl.multiple_of(last_ref[e], MOE_CHUNK)
            return pltpu.make_async_copy(zero_ref, o_hbm.at[pl.ds(start, MOE_CHUNK)], zsem)

        for e in range(N_EXPERTS):
            @pl.when(last_ref[e] >= 0)
            def _():
                clear(e).start()
        for e in range(N_EXPERTS):
            @pl.when(last_ref[e] >= 0)
            def _():
                clear(e).wait()

    def row_copy(r, d):
        return pltpu.make_async_copy(h_ref.at[pl.ds(r, 1)], o_hbm.at[pl.ds(d, 1)], sem)

    def start(r, c):
        for k in range(TOP_K):
            row_copy(r, dest_ref[0, 0, r * TOP_K + k]).start(priority=k % 2)
        return c

    def wait(r, c):
        for k in range(TOP_K):
            row_copy(0, 0).wait()
        return c

    lax.fori_loop(0, DISPATCH_TILE, start, 0, unroll=DMA_UNROLL)
    lax.fori_loop(0, DISPATCH_TILE, wait, 0, unroll=DMA_UNROLL)


def _dispatch(h2, dest, last_chunk, n_rows):
    T, D = h2.shape
    td = DISPATCH_TILE
    dest3 = dest.reshape(T // td, 1, td * TOP_K)
    return pl.pallas_call(
        _dispatch_body,
        grid=(T // td,),
        in_specs=[pl.BlockSpec((1, 1, td * TOP_K), lambda i: (i, 0, 0), memory_space=pltpu.SMEM),
                  pl.BlockSpec(memory_space=pltpu.SMEM), pl.BlockSpec((td, D), lambda i: (i, 0))],
        out_specs=pl.BlockSpec(memory_space=pl.ANY),
        out_shape=jax.ShapeDtypeStruct((n_rows, D), h2.dtype),
        scratch_shapes=[pltpu.VMEM((MOE_CHUNK, D), h2.dtype), pltpu.SemaphoreType.DMA(()),
                        pltpu.SemaphoreType.DMA(())],
        compiler_params=_cparams(1),
        name="dispatch",
    )(dest3, last_chunk, h2)


def _expert_body(ce_ref, nu_ref, x_ref, wgu_ref, bgu_ref, wdn_ref, bdn_ref, o_ref, wgu_bf, wdn_bf):
    c = pl.program_id(0)
    dff = wdn_ref.shape[1]

    @pl.when((c == 0) | (ce_ref[c] != ce_ref[jnp.maximum(c - 1, 0)]))
    def _():
        wgu_bf[...] = wgu_ref[0].astype(BF16)
        wdn_bf[...] = wdn_ref[0].astype(BF16)

    @pl.when(c < nu_ref[0])
    def _():
        gu = _dot(x_ref[...].astype(BF16), wgu_bf[...]) + bgu_ref[0]
        g = jnp.minimum(gu[:, 0:dff], SWIGLU_LIMIT)
        u = jnp.clip(gu[:, dff:2 * dff], -SWIGLU_LIMIT, SWIGLU_LIMIT)
        act = (u + 1.0) * (g * jax.nn.sigmoid(SWIGLU_ALPHA * g))
        o_ref[...] = _dot(act.astype(BF16), wdn_bf[...]) + bdn_ref[0]

    @pl.when(c >= nu_ref[0])
    def _():
        o_ref[...] = jnp.zeros(o_ref.shape, F32)


def _experts(hperm, chunk_e, n_used, w_gu, b_gu, w_dn, b_dn):
    P, D = hperm.shape
    E, _, F2 = w_gu.shape
    dff = F2 // 2
    n_chunks = P // MOE_CHUNK
    grid_spec = pltpu.PrefetchScalarGridSpec(
        num_scalar_prefetch=2,
        grid=(n_chunks,),
        in_specs=[pl.BlockSpec((MOE_CHUNK, D), lambda c, ce, nu: (jnp.minimum(c, nu[0] - 1), 0)),
                  pl.BlockSpec((1, D, F2), lambda c, ce, nu: (ce[c], 0, 0)),
                  pl.BlockSpec((1, 1, F2), lambda c, ce, nu: (ce[c], 0, 0)),
                  pl.BlockSpec((1, dff, D), lambda c, ce, nu: (ce[c], 0, 0)),
                  pl.BlockSpec((1, 1, D), lambda c, ce, nu: (ce[c], 0, 0))],
        out_specs=pl.BlockSpec((MOE_CHUNK, D), lambda c, ce, nu: (c, 0)),
        scratch_shapes=[pltpu.VMEM((D, F2), BF16), pltpu.VMEM((dff, D), BF16)],
    )
    return pl.pallas_call(
        _expert_body,
        grid_spec=grid_spec,
        out_shape=jax.ShapeDtypeStruct((P, D), F32),
        compiler_params=_cparams(1),
        name="experts",
    )(chunk_e, n_used, hperm, w_gu, b_gu.reshape(E, 1, F2), w_dn, b_dn.reshape(E, 1, D))


COMBINE_TILE = 256


def _combine_body(dest_ref, x_ref, w_ref, y_hbm, o_ref, buf_ref, sem):
    def row_copy(r, k, d):
        return pltpu.make_async_copy(y_hbm.at[pl.ds(d, 1)], buf_ref.at[k, pl.ds(r, 1)], sem)

    def start(r, c):
        for k in range(TOP_K):
            row_copy(r, k, dest_ref[0, 0, r * TOP_K + k]).start(priority=k % 2)
        return c

    def wait(r, c):
        for k in range(TOP_K):
            row_copy(0, 0, 0).wait()
        return c

    lax.fori_loop(0, COMBINE_TILE, start, 0, unroll=DMA_UNROLL)
    lax.fori_loop(0, COMBINE_TILE, wait, 0, unroll=DMA_UNROLL)
    out = x_ref[...]
    for k in range(TOP_K):
        out = out + w_ref[:, k:k + 1] * buf_ref[k]
    o_ref[...] = out


def _combine(x1, gate_w, dest, ys):
    T, D = x1.shape
    tc = COMBINE_TILE
    dest3 = dest.reshape(T // tc, 1, tc * TOP_K)
    row = lambda w: pl.BlockSpec((tc, w), lambda i: (i, 0))
    return pl.pallas_call(
        _combine_body,
        grid=(T // tc,),
        in_specs=[pl.BlockSpec((1, 1, tc * TOP_K), lambda i: (i, 0, 0), memory_space=pltpu.SMEM),
                  row(D), row(LANES), pl.BlockSpec(memory_space=pl.ANY)],
        out_specs=row(D),
        out_shape=jax.ShapeDtypeStruct((T, D), F32),
        scratch_shapes=[pltpu.VMEM((TOP_K, tc, D), F32), pltpu.SemaphoreType.DMA(())],
        compiler_params=_cparams(1),
        name="combine",
    )(dest3, x1, gate_w, ys)


def _mixer(x2, B, S, g_norm1, w_in, g_q, g_kc, g_ks, g_kw, pe_k, ck_w1, ck_b1, ck_w2, ck_b2,
           pe_v, cv_w1, cv_b1, cv_w2, cv_b2, conv_w, w_pa, w_pb, w_o):
    T, D = x2.shape
    G, H, hd = N_KV_HEADS, N_HEADS, HEAD_DIM
    qa, hk, hv, kas, kaw, vsw, gates, cbv, gab = _inproj(x2, g_norm1, w_in, g_q, g_ks, g_kw, S)
    nh = S // CMP_STRIDE
    kca = _compress(hk.reshape(B, nh, -1), pe_k, ck_w1, ck_b1, ck_w2, ck_b2, g_kc, True)
    vct = _compress(hv.reshape(B, nh, -1), pe_v, cv_w1, cv_b1, cv_w2, cv_b2, jnp.ones((hd,), F32), False)
    vsw5 = vsw.reshape(B, S, 2, G, hd)
    ones_rows = jnp.concatenate([jnp.ones((1,), BF16), jnp.zeros((V_ROWS - hd - 1,), BF16)])

    def key_major(v, tile):
        vt = v.reshape(B, S // tile, tile, G, hd).transpose(0, 3, 1, 4, 2)
        extra = jnp.broadcast_to(ones_rows[None, None, None, :, None], vt.shape[:3] + (V_ROWS - hd, tile))
        return jnp.concatenate([vt, extra], axis=3)

    vst = key_major(vsw5[:, :, 0], KEY_TILE)
    vwt = key_major(vsw5[:, :, 1], Q_TILE)
    gat = gates[:, :3 * H].reshape(B, S, G, 3 * HEADS_PER_GROUP).transpose(0, 2, 3, 1)
    gat = jnp.pad(gat, ((0, 0), (0, 0), (0, 16 - 3 * HEADS_PER_GROUP), (0, 0)))
    o = _attention(qa.reshape(B, S, -1), gat, kca, vct, kas.reshape(B, S, -1), vst, kaw.reshape(B, S, -1),
                   vwt, B, S)
    return _mixer_out(x2, o.reshape(T, H * hd), cbv, gab, conv_w, w_pa, w_pb, w_o, S)


def _moe(x1, g_norm2, w_r, b_r, w_gu, b_gu, w_dn, b_dn):
    T, D = x1.shape
    h2, mi, mf, cnt = _router(x1, g_norm2, w_r, b_r)
    top_e = mi[:, 0:TOP_K]
    rank = mi[:, TOP_K:2 * TOP_K]
    counts = cnt[0, :N_EXPERTS].astype(I32)
    padded = (counts + MOE_CHUNK - 1) // MOE_CHUNK * MOE_CHUNK
    pend = jnp.cumsum(padded)
    poffs = pend - padded
    dest = (poffs[top_e] + rank).reshape(-1)
    n_chunks = (T * TOP_K + MOE_CHUNK - 1) // MOE_CHUNK + N_EXPERTS
    chunk_start = jnp.arange(n_chunks, dtype=I32) * MOE_CHUNK
    chunk_e = jnp.minimum(jnp.sum((pend[None, :] <= chunk_start[:, None]).astype(I32), axis=1), N_EXPERTS - 1)
    n_used = (pend[-1:] // MOE_CHUNK).astype(I32)
    last_chunk = jnp.where(padded > 0, pend - MOE_CHUNK, -1).astype(I32)
    hperm = _dispatch(h2, dest, last_chunk, n_chunks * MOE_CHUNK)
    ys = _experts(hperm, chunk_e, n_used, w_gu, b_gu, w_dn, b_dn)
    return _combine(x1, mf, dest, ys)


def kernel(x, g_norm1, w_in, g_q, g_kc, g_ks, g_kw, pe_k, ck_w1, ck_b1, ck_w2, ck_b2, pe_v, cv_w1, cv_b1,
           cv_w2, cv_b2, conv_w, w_pa, w_pb, w_o, g_norm2, w_r, b_r, w_gu, b_gu, w_dn, b_dn):
    B, S, D = x.shape
    x2 = x.reshape(B * S, D)
    for l in range(g_norm1.shape[0]):
        x2 = _mixer(x2, B, S, g_norm1[l], w_in[l], g_q[l], g_kc[l], g_ks[l], g_kw[l], pe_k[l], ck_w1[l],
                    ck_b1[l], ck_w2[l], ck_b2[l], pe_v[l], cv_w1[l], cv_b1[l], cv_w2[l], cv_b2[l],
                    conv_w[l], w_pa[l], w_pb[l], w_o[l])
        x2 = _moe(x2, g_norm2[l], w_r[l], b_r[l], w_gu[l], b_gu[l], w_dn[l], b_dn[l])
    return x2.reshape(B, S, D)
```

```python
import functools

import numpy as np
import jax
import jax.numpy as jnp
from jax import lax
from jax.experimental import pallas as pl
from jax.experimental.pallas import tpu as pltpu

F32 = jnp.float32
BF16 = jnp.bfloat16
I32 = jnp.int32

N_HEADS = 8
HEAD_DIM = 64
N_KV_HEADS = 2
HEADS_PER_GROUP = N_HEADS // N_KV_HEADS
CMP_LEN = 32
CMP_STRIDE = 16
CMP_HID = 256
SEL_BLK = 64
SEL_TOPN = 16
WINDOW = 512
CONV_WIDTH = 512
CONV_K = 3
N_EXPERTS = 32
TOP_K = 4
SWIGLU_LIMIT = 7.0
SWIGLU_ALPHA = 1.702
MOE_CHUNK = 512
EPS = 1e-6
NEG = -1e30
POS = 1e30
MASK_BIG = 2.0 ** 100

LANES = 128
Q_TILE = 128
KEY_TILE = 512
N_AUG = 5
ROW_TILE = 512
VMEM_LIMIT = 56 * 1024 * 1024
GROUP_LANES = HEADS_PER_GROUP * Q_TILE
WIN_KEYS = WINDOW + Q_TILE
N_TILE_ROWS = 16
V_ROWS = HEAD_DIM + 8


def _cparams(n_axes):
    return pltpu.CompilerParams(dimension_semantics=("arbitrary",) * n_axes,
                                vmem_limit_bytes=VMEM_LIMIT)


def _dot(a, b):
    return jnp.dot(a, b, preferred_element_type=F32)


def _dot_nt(a, b):
    return lax.dot_general(a, b, (((1,), (1,)), ((), ())), preferred_element_type=F32)


ROW_TILES = 8


def _store_row_tiles(ref, val):
    n = val.shape[0]
    for s in range(ROW_TILES):
        ref[pl.ds(s, n, stride=ROW_TILES), :] = val[:, s * LANES:(s + 1) * LANES]


def _load_row_tiles(ref, n):
    return jnp.concatenate([ref[pl.ds(s, n, stride=ROW_TILES), :] for s in range(ROW_TILES)], axis=1)


def _rms_pairs(v, bd):
    ss = _dot((v * v).astype(BF16), bd)
    return v * lax.rsqrt(ss + EPS)


def _inproj_body(x_ref, g1_ref, wq_ref, wkv_ref, wng_ref, wcv_ref, wmg_ref, gq_ref, gk_ref, bd_ref,
                 qtab_ref, kstab_ref, kwtab_ref,
                 qa_out, hk_out, hv_out, kas_out, kaw_out, vsw_out, gate_out, cbv_out, gab_out, raw_ref):
    x = x_ref[...]
    tm = x.shape[0]
    ms = jnp.mean(x * x, axis=-1, keepdims=True)
    h = (x * lax.rsqrt(ms + EPS) * g1_ref[...]).astype(BF16)
    bd = bd_ref[...]
    low = lax.broadcasted_iota(I32, (tm, LANES), 1) < HEAD_DIM

    def place(pair, tab_ref, out_ref, base, slot):
        for j, src in enumerate((pair, pltpu.roll(pair, HEAD_DIM, 1))):
            o = base + j * slot
            out_ref[:, o:o + LANES] = jnp.where(low, src, tab_ref[:, o:o + LANES].astype(F32)).astype(BF16)

    q = _dot(h, wq_ref[...])
    for c in range(N_HEADS * HEAD_DIM // LANES):
        sl = slice(c * LANES, (c + 1) * LANES)
        place(_rms_pairs(q[:, sl], bd) * gq_ref[:, sl], qtab_ref, qa_out, 2 * c * LANES, LANES)
    kv = _dot(h, wkv_ref[...])

    def emit_half_blocks(c, out_ref):
        nb = tm // CMP_STRIDE
        hw = CMP_STRIDE * HEAD_DIM
        raw_ref[...] = kv[:, c * LANES:(c + 1) * LANES]
        lo = lax.broadcasted_iota(I32, (nb, LANES), 1) < HEAD_DIM
        for u in range(CMP_STRIDE // 2):
            t0 = raw_ref[pl.ds(2 * u, nb, stride=CMP_STRIDE), :]
            t1 = raw_ref[pl.ds(2 * u + 1, nb, stride=CMP_STRIDE), :]
            out_ref[:, u * LANES:(u + 1) * LANES] = jnp.where(lo, t0, pltpu.roll(t1, HEAD_DIM, 1)).astype(BF16)
            out_ref[:, hw + u * LANES:hw + (u + 1) * LANES] = (
                jnp.where(lo, pltpu.roll(t0, HEAD_DIM, 1), t1).astype(BF16))

    emit_half_blocks(0, hk_out)
    emit_half_blocks(1, hv_out)
    place(_rms_pairs(kv[:, 256:384], bd) * gk_ref[:, 0:128], kstab_ref, kas_out, 0, 2 * LANES)
    for g in range(N_KV_HEADS):
        o = (2 * g + 1) * LANES
        kas_out[:, o:o + LANES] = kstab_ref[:, o:o + LANES]
    place(_rms_pairs(kv[:, 512:640], bd) * gk_ref[:, 128:256], kwtab_ref, kaw_out, 0, LANES)
    vsw_out[:, 0:128] = kv[:, 384:512].astype(BF16)
    vsw_out[:, 128:256] = kv[:, 640:768].astype(BF16)
    gate_out[...] = jax.nn.sigmoid(_dot(h, wng_ref[...]))
    cv = _dot(h, wcv_ref[...])
    cw = CONV_WIDTH
    cbv_out[:, 0:cw] = cv[:, 0:cw].astype(BF16)
    cbv_out[:, cw:2 * cw] = (cv[:, cw:2 * cw] * cv[:, 2 * cw:3 * cw]).astype(BF16)
    gab_out[...] = jax.nn.sigmoid(_dot(h, wmg_ref[...])).astype(BF16)


def _key_aug(pos):
    one = np.ones_like(pos, np.float32)
    return np.stack([one, one, (pos // 64 * 64).astype(np.float32), (pos % 64).astype(np.float32), 0 * one],
                    axis=-1)


def _slot_table(aug, slot, extra=None):
    S, n, _ = aug.shape
    tab = np.zeros((S, n, slot), np.float32)
    tab[:, :, HEAD_DIM:HEAD_DIM + N_AUG] = aug
    if extra is not None:
        tab[:, :, LANES:] = extra[:, None, :]
    return jnp.asarray(tab.reshape(S, n * slot), BF16)


def _inproj(x2, g1, w_in, g_q, g_ks, g_kw, S):
    T, D = x2.shape
    H, G = N_HEADS, N_KV_HEADS
    aw = H * HEAD_DIM
    kvw = G * HEAD_DIM
    o = 0
    wq = w_in[:, o:o + aw]; o += aw
    wkv = w_in[:, o:o + 6 * kvw]; o += 6 * kvw
    wng = w_in[:, o:o + 3 * H]; o += 3 * H
    wcv = w_in[:, o:o + 3 * CONV_WIDTH]; o += 3 * CONV_WIDTH
    wmg = w_in[:, o:o + 2 * D]
    wng = jnp.pad(wng, ((0, 0), (0, LANES - 3 * H)))
    wq, wkv, wng, wcv, wmg = (w.astype(BF16) for w in (wq, wkv, wng, wcv, wmg))
    gq = (jnp.tile(g_q, H) * (HEAD_DIM ** -0.5)).reshape(1, aw)
    gk = jnp.concatenate([jnp.tile(g_ks, G), jnp.tile(g_kw, G)]).reshape(1, 2 * kvw)
    idx = np.arange(LANES) // HEAD_DIM
    bd = jnp.asarray((idx[:, None] == idx[None, :]).astype(np.float32) / HEAD_DIM, BF16)
    pos = np.arange(S)
    hi = (pos // 64 * 64).astype(np.float32)[:, None]
    lo = (pos % 64).astype(np.float32)[:, None]
    sl = (2.0 ** (-8.0 * np.arange(1, H + 1) / H)).astype(np.float32)[None, :]
    aq = np.stack([-sl * hi, -sl * lo, np.broadcast_to(sl, (S, H)), np.broadcast_to(sl, (S, H)),
                   np.ones((S, H), np.float32)], axis=-1)
    ak = np.broadcast_to(_key_aug(pos)[:, None, :], (S, G, N_AUG))
    onehot = (pos[:, None] // SEL_BLK == np.arange(LANES)[None, :]).astype(np.float32)
    qtab = _slot_table(aq, LANES)
    kstab = _slot_table(ak, 2 * LANES, onehot)
    kwtab = _slot_table(ak, LANES)
    tm = ROW_TILE
    nst = S // tm
    row = lambda w: pl.BlockSpec((tm, w), lambda i: (i, 0))
    full = lambda a: pl.BlockSpec(a.shape, lambda i: (0,) * a.ndim)
    tab = lambda a: pl.BlockSpec((tm, a.shape[1]), lambda i: (i % nst, 0))
    ins = (x2, g1.reshape(1, D), wq, wkv, wng, wcv, wmg, gq, gk, bd, qtab, kstab, kwtab)
    widths = (H * LANES, G * 2 * LANES, G * LANES, 2 * kvw, LANES, 2 * CONV_WIDTH, 2 * D)
    dtypes = (BF16, BF16, BF16, BF16, F32, BF16, BF16)
    hw = G * CMP_STRIDE * HEAD_DIM
    nb = tm // CMP_STRIDE
    half = pl.BlockSpec((nb, hw), lambda i: (i, 0))
    half_shape = jax.ShapeDtypeStruct((T // CMP_STRIDE, hw), BF16)
    rows = [(row(w), jax.ShapeDtypeStruct((T, w), dt)) for w, dt in zip(widths, dtypes)]
    outs = [rows[0], (half, half_shape), (half, half_shape)] + rows[1:]
    return pl.pallas_call(
        _inproj_body,
        grid=(T // tm,),
        in_specs=[row(D)] + [full(a) for a in ins[1:10]] + [tab(a) for a in ins[10:]],
        out_specs=[o[0] for o in outs],
        out_shape=[o[1] for o in outs],
        scratch_shapes=[pltpu.VMEM((tm, LANES), F32)],
        compiler_params=_cparams(1),
        name="inproj",
    )(*ins)


def _compress_body(h_ref, w1_ref, pe_ref, b1_ref, w2_ref, b2_ref, g_ref, tab_ref, o_ref, *, for_keys):
    hb = h_ref[0]
    nc = hb.shape[0]
    a = _dot(hb, w1_ref[0])
    b = _dot(hb, w1_ref[1])
    c = _dot(pe_ref[0], w1_ref[0]) + _dot(pe_ref[1], w1_ref[1])
    pre = a + pltpu.roll(b, nc - 1, 0) + c[0:1, :] + b1_ref[...]
    hid = jax.nn.gelu(pre)
    out = _dot(hid.astype(BF16), w2_ref[...]) + b2_ref[...]
    if for_keys:
        ms = jnp.sum(out * out, axis=-1, keepdims=True) * (1.0 / HEAD_DIM)
        out = out * lax.rsqrt(ms + EPS) * g_ref[...]
        low = lax.broadcasted_iota(I32, out.shape, 1) < HEAD_DIM
        o_ref[0, 0] = jnp.where(low, out, tab_ref[...]).astype(BF16)
    else:
        o_ref[0, 0] = out.T[0:HEAD_DIM, :].astype(BF16)


def _compress(hh, pe, w1, b1, w2, b2, gain, for_keys):
    B, NC, _ = hh.shape
    G, HW = N_KV_HEADS, CMP_STRIDE * HEAD_DIM
    w1s = w1.reshape(2, HW, CMP_HID).astype(BF16)
    pes = jnp.broadcast_to(pe.reshape(2, 1, HW), (2, 8, HW)).astype(BF16)
    padl = lambda a: jnp.pad(a, ((0, 0), (0, LANES - HEAD_DIM)))
    tabn = np.zeros((NC, LANES), np.float32)
    tabn[:, HEAD_DIM:HEAD_DIM + N_AUG] = _key_aug(np.arange(NC) * CMP_STRIDE + (CMP_LEN - 1))
    tab = jnp.asarray(tabn)
    full = lambda a: pl.BlockSpec(a.shape, lambda b, g: (0,) * a.ndim)
    ins = (hh, w1s, pes, b1.reshape(1, CMP_HID), padl(w2).astype(BF16), padl(b2.reshape(1, HEAD_DIM)),
           padl(gain.reshape(1, HEAD_DIM)), tab)
    oshape = (B, G, NC, LANES) if for_keys else (B, G, HEAD_DIM, NC)
    return pl.pallas_call(
        functools.partial(_compress_body, for_keys=for_keys),
        grid=(B, G),
        in_specs=[pl.BlockSpec((1, NC, HW), lambda b, g: (b, 0, g))] + [full(a) for a in ins[1:]],
        out_specs=pl.BlockSpec((1, 1) + oshape[2:], lambda b, g: (b, g, 0, 0)),
        out_shape=jax.ShapeDtypeStruct(oshape, BF16),
        compiler_params=_cparams(2),
        name="compress_keys" if for_keys else "compress_values",
    )(*ins)


def _attn_body(qa_ref, g_ref, kca_ref, vct_ref, kas_ref, vst_ref, kaw_ref, vwt_ref, selmapt_ref, wbias_ref,
               tilemap_ref, cbias_ref, o_ref, qaug_ref, m_ref, acc_ref, flagv_ref, flags_ref, list_ref, sem,
               *, n_sel):
    i = pl.program_id(2)
    q0 = i * Q_TILE
    gl = GROUP_LANES
    for h in range(HEADS_PER_GROUP):
        qaug_ref[h * Q_TILE:(h + 1) * Q_TILE, 0:LANES] = qa_ref[0, :, h * LANES:(h + 1) * LANES]
    qa = qaug_ref[:, 0:LANES]

    nc = kca_ref.shape[2]
    s = _dot_nt(kca_ref[0, 0], qa)
    cb = cbias_ref[pl.ds(pl.multiple_of(nc - i * (Q_TILE // CMP_STRIDE), 8), nc), :]
    s = s + jnp.concatenate([cb] * HEADS_PER_GROUP, axis=1)
    m = jnp.max(s, axis=0, keepdims=True)
    p = jnp.exp(s - m)
    l = jnp.sum(p, axis=0, keepdims=True)
    has_entry = (q0 + (lax.broadcasted_iota(I32, (1, gl), 1) & (Q_TILE - 1))) >= CMP_LEN - 1
    pc = p * jnp.where(has_entry, 1.0 / l, 0.0)
    o_c = _dot(vct_ref[0, 0], pc.astype(BF16))

    ps = pc[:, 0:Q_TILE]
    for h in range(1, HEADS_PER_GROUP):
        ps = ps + pc[:, h * Q_TILE:(h + 1) * Q_TILE]
    ps_hi = ps.astype(BF16)
    ps_lo = (ps - ps_hi.astype(F32)).astype(BF16)
    imp = _dot(selmapt_ref[...], ps_hi) + _dot(selmapt_ref[...], ps_lo)
    jb = lax.broadcasted_iota(I32, (LANES, Q_TILE), 0)
    cur = (q0 + lax.broadcasted_iota(I32, (LANES, Q_TILE), 1)) // SEL_BLK
    forced = (jb == 0) | (jb == cur) | (jb == cur - 1)
    score = jnp.where(jb > cur, NEG, jnp.where(forced, POS, imp))
    jbf = jb.astype(F32)
    for _ in range(n_sel):
        mx = jnp.max(score, axis=0, keepdims=True)
        first = jnp.min(jnp.where(score == mx, jbf, float(LANES)), axis=0, keepdims=True)
        score = jnp.where(jbf == first, -jnp.inf, score)
    picked = score == -jnp.inf
    sel = jnp.where(picked, 1.0, 0.0)
    selbias = jnp.where(picked, 0.0, -MASK_BIG).T.astype(BF16)
    for h in range(HEADS_PER_GROUP):
        qaug_ref[h * Q_TILE:(h + 1) * Q_TILE, LANES:2 * LANES] = selbias
    tile_hits = jnp.max(_dot(tilemap_ref[...], sel.astype(BF16)), axis=1, keepdims=True)
    flagv_ref[...] = jnp.broadcast_to(tile_hits, flagv_ref.shape).astype(I32)
    flag_copy = pltpu.make_async_copy(flagv_ref, flags_ref, sem)
    flag_copy.start()

    w0 = pl.multiple_of(jnp.maximum(q0 - WINDOW, 0), Q_TILE)
    sw = _dot_nt(kaw_ref[0, pl.ds(w0, WIN_KEYS), :], qa)
    wb = wbias_ref[jnp.minimum(i, WINDOW // Q_TILE)]
    sw = sw + jnp.concatenate([wb] * HEADS_PER_GROUP, axis=1)
    mw = jnp.max(sw, axis=0, keepdims=True)
    pw = jnp.exp(sw - mw)
    c0 = w0 // Q_TILE
    vw = jnp.concatenate([vwt_ref[0, 0, c0 + j] for j in range(WIN_KEYS // Q_TILE)], axis=1)
    aw = _dot(vw, pw.astype(BF16))
    o_w = aw[0:HEAD_DIM] * (1.0 / aw[HEAD_DIM:HEAD_DIM + 1])

    qaug = qaug_ref[...]
    n_full = q0 // KEY_TILE
    kd = pl.multiple_of(n_full * KEY_TILE, KEY_TILE)
    sd = _dot_nt(kas_ref[0, pl.ds(kd, KEY_TILE), :], qaug)
    key = kd + lax.broadcasted_iota(I32, (KEY_TILE, gl), 0)
    qry = q0 + (lax.broadcasted_iota(I32, (KEY_TILE, gl), 1) & (Q_TILE - 1))
    sd = jnp.where(key <= qry, sd, NEG)
    md = jnp.max(sd, axis=0, keepdims=True)
    m_ref[...] = md
    acc_ref[...] = _dot(vst_ref[0, 0, n_full], jnp.exp(sd - md).astype(BF16))

    def tile_step(tiles):
        scs = [_dot_nt(kas_ref[0, pl.ds(pl.multiple_of(kt * KEY_TILE, KEY_TILE), KEY_TILE), :], qaug)
               for kt in tiles]
        m_old = m_ref[...]
        m_new = m_old
        for sc in scs:
            m_new = jnp.maximum(m_new, jnp.max(sc, axis=0, keepdims=True))
        upd = jnp.exp(m_old - m_new) * acc_ref[...]
        for kt, sc in zip(tiles, scs):
            upd = upd + _dot(vst_ref[0, 0, kt], jnp.exp(sc - m_new).astype(BF16))
        acc_ref[...] = upd
        m_ref[...] = m_new

    flag_copy.wait()

    def compact(kt, n):
        active = flags_ref[kt, 0] > 0

        @pl.when(active)
        def _():
            list_ref[n] = kt

        return n + active.astype(I32)

    n_active = lax.fori_loop(0, n_full, compact, 0)

    def pair_body(j, carry):
        tile_step((list_ref[2 * j], list_ref[2 * j + 1]))
        return carry

    lax.fori_loop(0, n_active // 2, pair_body, 0)

    @pl.when(n_active % 2 == 1)
    def _():
        tile_step((list_ref[n_active - 1],))

    o_s = acc_ref[0:HEAD_DIM, :] * (1.0 / acc_ref[HEAD_DIM:HEAD_DIM + 1, :])

    g = g_ref[0, 0]
    outs = []
    for h in range(HEADS_PER_GROUP):
        sl = slice(h * Q_TILE, (h + 1) * Q_TILE)
        outs.append(g[3 * h:3 * h + 1, :] * o_c[:, sl] + g[3 * h + 1:3 * h + 2, :] * o_s[:, sl]
                    + g[3 * h + 2:3 * h + 3, :] * o_w[:, sl])
    o_ref[0] = jnp.concatenate(outs, axis=0).T.astype(BF16)


def _attention(qa, gates_t, kca, vct, kas, vst, kaw, vwt, B, S):
    G, hd = N_KV_HEADS, HEAD_DIM
    NC = kca.shape[2]
    n_blk = S // SEL_BLK
    assert n_blk <= LANES and S % KEY_TILE == 0 and S >= WIN_KEYS
    n_sel = min(SEL_TOPN, n_blk)
    ratio, span = SEL_BLK // CMP_STRIDE, CMP_LEN // CMP_STRIDE
    sm = np.zeros((LANES, NC), np.float32)
    for j in range(n_blk):
        for a in range(ratio):
            for b in range(span):
                n = ratio * j + a - b
                if 0 <= n < NC - 1:
                    sm[j, n] += 1.0
    selmapt = jnp.asarray(sm, BF16)
    c = np.arange(WIN_KEYS)[:, None]
    r = np.arange(Q_TILE)[None, :]
    offs = np.arange(WINDOW // Q_TILE + 1)[:, None, None] * Q_TILE
    wbias = jnp.asarray(np.where((c - r <= offs) & (c - r > offs - WINDOW), 0.0, NEG), F32)
    tilemap = jnp.asarray(np.arange(LANES)[None, :] // (KEY_TILE // SEL_BLK) == np.arange(N_TILE_ROWS)[:, None],
                          BF16)
    assert S // KEY_TILE <= N_TILE_ROWS
    u = np.arange(2 * NC)[:, None] - NC
    cbias = jnp.asarray(np.where(CMP_STRIDE * u + (CMP_LEN - 1) <= np.arange(Q_TILE)[None, :], 0.0, NEG), F32)

    hpg = HEADS_PER_GROUP
    grp = lambda *blk: pl.BlockSpec((1, 1) + blk, lambda b, g, i: (b, g) + (0,) * len(blk))
    seq = lambda w: pl.BlockSpec((1, S, w), lambda b, g, i: (b, 0, g))
    const = lambda a: pl.BlockSpec(a.shape, lambda b, g, i: (0,) * a.ndim)
    return pl.pallas_call(
        functools.partial(_attn_body, n_sel=n_sel),
        grid=(B, G, S // Q_TILE),
        in_specs=[pl.BlockSpec((1, Q_TILE, hpg * LANES), lambda b, g, i: (b, i, g)),
                  pl.BlockSpec((1, 1, 16, Q_TILE), lambda b, g, i: (b, g, 0, i)),
                  grp(NC, LANES), grp(hd, NC), seq(2 * LANES), grp(S // KEY_TILE, V_ROWS, KEY_TILE),
                  seq(LANES), grp(S // Q_TILE, V_ROWS, Q_TILE), const(selmapt), const(wbias), const(tilemap),
                  const(cbias)],
        out_specs=pl.BlockSpec((1, Q_TILE, hpg * hd), lambda b, g, i: (b, i, g)),
        out_shape=jax.ShapeDtypeStruct((B, S, N_HEADS * hd), BF16),
        scratch_shapes=[pltpu.VMEM((GROUP_LANES, 2 * LANES), BF16), pltpu.VMEM((1, GROUP_LANES), F32),
                        pltpu.VMEM((V_ROWS, GROUP_LANES), F32),
                        pltpu.VMEM((N_TILE_ROWS, LANES), I32), pltpu.SMEM((N_TILE_ROWS, LANES), I32),
                        pltpu.SMEM((N_TILE_ROWS,), I32), pltpu.SemaphoreType.DMA(())],
        compiler_params=_cparams(3),
        name="nsa_attention",
    )(qa, gates_t, kca, vct, kas, vst, kaw, vwt, selmapt, wbias, tilemap, cbias)


def _mixer_out_body(x_ref, oa_ref, cbv_ref, halo_ref, gab_ref, cw_ref, wpa_ref, wpb_ref, wo_ref, o_ref,
                    *, seq_len):
    i = pl.program_id(0)
    tm = x_ref.shape[0]
    cwd = CONV_WIDTH
    d = x_ref.shape[1]
    v = cbv_ref[:, cwd:2 * cwd].astype(F32)
    prev = halo_ref[:, cwd:2 * cwd].astype(F32)
    keep = ((i * tm) % seq_len != 0).astype(F32)
    p1 = prev[7:8, :] * keep
    p2 = prev[6:7, :] * keep
    ridx = lax.broadcasted_iota(I32, (tm, cwd), 0)
    v1 = jnp.where(ridx == 0, p1, pltpu.roll(v, 1, 0))
    v2 = jnp.where(ridx == 0, p2, jnp.where(ridx == 1, p1, pltpu.roll(v, 2, 0)))
    y = cw_ref[0:1, :] * v2 + cw_ref[1:2, :] * v1 + cw_ref[2:3, :] * v
    yb_in = (cbv_ref[:, 0:cwd].astype(F32) * y).astype(BF16)
    y_a = _dot(oa_ref[...], wpa_ref[...])
    y_b = _dot(yb_in, wpb_ref[...])
    merged = gab_ref[:, 0:d].astype(F32) * y_a + gab_ref[:, d:2 * d].astype(F32) * y_b
    o_ref[...] = x_ref[...] + _dot(merged.astype(BF16), wo_ref[...])


def _mixer_out(x2, oa, cbv, gab, conv_w, w_pa, w_pb, w_o, seq_len):
    T, D = x2.shape
    tm = ROW_TILE
    cw8 = jnp.pad(conv_w, ((0, 8 - CONV_K), (0, 0)))
    row = lambda w: pl.BlockSpec((tm, w), lambda i: (i, 0))
    full = lambda a: pl.BlockSpec(a.shape, lambda i: (0,) * a.ndim)
    halo = pl.BlockSpec((8, cbv.shape[1]), lambda i: (jnp.maximum(i * (tm // 8) - 1, 0), 0))
    wts = (cw8, w_pa.astype(BF16), w_pb.astype(BF16), w_o.astype(BF16))
    return pl.pallas_call(
        functools.partial(_mixer_out_body, seq_len=seq_len),
        grid=(T // tm,),
        in_specs=[row(D), row(oa.shape[1]), row(cbv.shape[1]), halo, row(gab.shape[1])] + [full(a) for a in wts],
        out_specs=row(D),
        out_shape=jax.ShapeDtypeStruct((T, D), F32),
        compiler_params=_cparams(1),
        name="mixer_out",
    )(x2, oa, cbv, cbv, gab, *wts)


def _router_body(x_ref, g2_ref, whi_ref, wlo_ref, br_ref, tri_ref, h_out, mi_out, mf_out, cnt_out):
    i = pl.program_id(0)

    @pl.when(i == 0)
    def _():
        cnt_out[...] = jnp.zeros(cnt_out.shape, F32)

    x = x_ref[...]
    tm = x.shape[0]
    ms = jnp.mean(x * x, axis=-1, keepdims=True)
    h = x * lax.rsqrt(ms + EPS) * g2_ref[...]
    _store_row_tiles(h_out, h)
    h_hi = h.astype(BF16)
    h_lo = (h - h_hi.astype(F32)).astype(BF16)
    logits = (_dot(h_hi, whi_ref[...]) + _dot(h_lo, whi_ref[...]) + _dot(h_hi, wlo_ref[...])) + br_ref[...]
    lane = lax.broadcasted_iota(I32, (tm, LANES), 1)
    lanef = lane.astype(F32)
    work = jnp.where(lane < N_EXPERTS, logits, -jnp.inf)
    vals, hits = [], []
    for _ in range(TOP_K):
        mx = jnp.max(work, axis=-1, keepdims=True)
        first = jnp.min(jnp.where(work == mx, lanef, float(LANES)), axis=-1, keepdims=True)
        hit = lanef == first
        vals.append(mx)
        hits.append(hit)
        work = jnp.where(hit, -jnp.inf, work)
    ex = [jnp.exp(v - vals[0]) for v in vals]
    den = ex[0]
    for e in ex[1:]:
        den = den + e
    cnt = jnp.zeros((tm, LANES), F32)
    for hit in hits:
        cnt = cnt + hit.astype(F32)
    before = _dot(tri_ref[...], cnt.astype(BF16)) + cnt_out[0:1, :]
    mi = jnp.zeros((tm, LANES), F32)
    mf = jnp.zeros((tm, LANES), F32)
    for k, hit in enumerate(hits):
        e_k = jnp.sum(jnp.where(hit, lanef, 0.0), axis=-1, keepdims=True)
        r_k = jnp.sum(jnp.where(hit, before, 0.0), axis=-1, keepdims=True)
        mi = jnp.where(lane == k, e_k, jnp.where(lane == TOP_K + k, r_k, mi))
        mf = jnp.where(lane == k, ex[k] / den, mf)
    mi_out[...] = mi.astype(I32)
    mf_out[...] = mf
    cnt_out[...] = cnt_out[...] + jnp.sum(cnt, axis=0, keepdims=True)


def _router(x1, g2, w_r, b_r):
    T, D = x1.shape
    assert D == ROW_TILES * LANES
    tm = ROW_TILE
    wpad = jnp.pad(w_r, ((0, 0), (0, LANES - N_EXPERTS)))
    whi = wpad.astype(BF16)
    wlo = (wpad - whi.astype(F32)).astype(BF16)
    br = jnp.pad(b_r, (0, LANES - N_EXPERTS)).reshape(1, LANES)
    tri = jnp.asarray(np.tril(np.ones((tm, tm), np.float32), -1), BF16)
    row = lambda w: pl.BlockSpec((tm, w), lambda i: (i, 0))
    full = lambda a: pl.BlockSpec(a.shape, lambda i: (0,) * a.ndim)
    ins = (x1, g2.reshape(1, D), whi, wlo, br, tri)
    return pl.pallas_call(
        _router_body,
        grid=(T // tm,),
        in_specs=[row(D)] + [full(a) for a in ins[1:]],
        out_specs=[pl.BlockSpec((tm * ROW_TILES, LANES), lambda i: (i, 0)), row(LANES), row(LANES),
                   pl.BlockSpec((8, LANES), lambda i: (0, 0))],
        out_shape=[jax.ShapeDtypeStruct((T * ROW_TILES, LANES), F32), jax.ShapeDtypeStruct((T, LANES), I32),
                   jax.ShapeDtypeStruct((T, LANES), F32), jax.ShapeDtypeStruct((8, LANES), F32)],
        compiler_params=_cparams(1),
        name="router",
    )(*ins)


DISPATCH_TILE = 512
DMA_UNROLL = 8


def _dispatch_body(dest_ref, last_ref, h_ref, o_hbm, zero_ref, sem, zsem):
    @pl.when(pl.program_id(0) == 0)
    def _():
        zero_ref[...] = jnp.zeros(zero_ref.shape, zero_ref.dtype)

        def clear(e):
            start = pl.multiple_of(last_ref[e] * ROW_TILES, MOE_CHUNK * ROW_TILES)
            return pltpu.make_async_copy(zero_ref, o_hbm.at[pl.ds(start, MOE_CHUNK * ROW_TILES)], zsem)

        for e in range(N_EXPERTS):
            @pl.when(last_ref[e] >= 0)
            def _():
                clear(e).start()
        for e in range(N_EXPERTS):
            @pl.when(last_ref[e] >= 0)
            def _():
                clear(e).wait()

    def row_copy(r, d):
        return pltpu.make_async_copy(h_ref.at[pl.ds(pl.multiple_of(r * ROW_TILES, ROW_TILES), ROW_TILES)],
                                     o_hbm.at[pl.ds(pl.multiple_of(d * ROW_TILES, ROW_TILES), ROW_TILES)], sem)

    def start(r, c):
        for k in range(TOP_K):
            row_copy(r, dest_ref[0, 0, r * TOP_K + k]).start(priority=k % 2)
        return c

    def wait(r, c):
        for k in range(TOP_K):
            row_copy(0, 0).wait()
        return c

    lax.fori_loop(0, DISPATCH_TILE, start, 0, unroll=DMA_UNROLL)
    lax.fori_loop(0, DISPATCH_TILE, wait, 0, unroll=DMA_UNROLL)


def _dispatch(h2, dest, last_chunk, n_rows):
    T = h2.shape[0] // ROW_TILES
    td = DISPATCH_TILE
    dest3 = dest.reshape(T // td, 1, td * TOP_K)
    return pl.pallas_call(
        _dispatch_body,
        grid=(T // td,),
        in_specs=[pl.BlockSpec((1, 1, td * TOP_K), lambda i: (i, 0, 0), memory_space=pltpu.SMEM),
                  pl.BlockSpec(memory_space=pltpu.SMEM),
                  pl.BlockSpec((td * ROW_TILES, LANES), lambda i: (i, 0))],
        out_specs=pl.BlockSpec(memory_space=pl.ANY),
        out_shape=jax.ShapeDtypeStruct((n_rows * ROW_TILES, LANES), h2.dtype),
        scratch_shapes=[pltpu.VMEM((MOE_CHUNK * ROW_TILES, LANES), h2.dtype), pltpu.SemaphoreType.DMA(()),
                        pltpu.SemaphoreType.DMA(())],
        compiler_params=_cparams(1),
        name="dispatch",
    )(dest3, last_chunk, h2)


def _expert_body(ce_ref, nu_ref, x_ref, wgu_ref, bgu_ref, wdn_ref, bdn_ref, o_ref, wgu_bf, wdn_bf):
    c = pl.program_id(0)
    dff = wdn_ref.shape[1]

    @pl.when((c == 0) | (ce_ref[c] != ce_ref[jnp.maximum(c - 1, 0)]))
    def _():
        wgu_bf[...] = wgu_ref[0].astype(BF16)
        wdn_bf[...] = wdn_ref[0].astype(BF16)

    @pl.when(c < nu_ref[0])
    def _():
        x = _load_row_tiles(x_ref, MOE_CHUNK)
        gu = _dot(x.astype(BF16), wgu_bf[...]) + bgu_ref[0]
        g = jnp.minimum(gu[:, 0:dff], SWIGLU_LIMIT)
        u = jnp.clip(gu[:, dff:2 * dff], -SWIGLU_LIMIT, SWIGLU_LIMIT)
        act = (u + 1.0) * (g * jax.nn.sigmoid(SWIGLU_ALPHA * g))
        _store_row_tiles(o_ref, _dot(act.astype(BF16), wdn_bf[...]) + bdn_ref[0])

    @pl.when(c >= nu_ref[0])
    def _():
        o_ref[...] = jnp.zeros(o_ref.shape, F32)


def _experts(hperm, chunk_e, n_used, w_gu, b_gu, w_dn, b_dn):
    E, D, F2 = w_gu.shape
    assert D == ROW_TILES * LANES
    P = hperm.shape[0] // ROW_TILES
    dff = F2 // 2
    n_chunks = P // MOE_CHUNK
    chunk = (MOE_CHUNK * ROW_TILES, LANES)
    grid_spec = pltpu.PrefetchScalarGridSpec(
        num_scalar_prefetch=2,
        grid=(n_chunks,),
        in_specs=[pl.BlockSpec(chunk, lambda c, ce, nu: (jnp.minimum(c, nu[0] - 1), 0)),
                  pl.BlockSpec((1, D, F2), lambda c, ce, nu: (ce[c], 0, 0)),
                  pl.BlockSpec((1, 1, F2), lambda c, ce, nu: (ce[c], 0, 0)),
                  pl.BlockSpec((1, dff, D), lambda c, ce, nu: (ce[c], 0, 0)),
                  pl.BlockSpec((1, 1, D), lambda c, ce, nu: (ce[c], 0, 0))],
        out_specs=pl.BlockSpec(chunk, lambda c, ce, nu: (c, 0)),
        scratch_shapes=[pltpu.VMEM((D, F2), BF16), pltpu.VMEM((dff, D), BF16)],
    )
    return pl.pallas_call(
        _expert_body,
        grid_spec=grid_spec,
        out_shape=jax.ShapeDtypeStruct(hperm.shape, F32),
        compiler_params=_cparams(1),
        name="experts",
    )(chunk_e, n_used, hperm, w_gu, b_gu.reshape(E, 1, F2), w_dn, b_dn.reshape(E, 1, D))


COMBINE_TILE = 256


def _combine_body(dest_ref, next_ref, x_ref, w_ref, y_hbm, o_ref, buf_ref, sems):
    i = pl.program_id(0)
    slot = i % 2

    def row_copy(s, r, k, d):
        return pltpu.make_async_copy(y_hbm.at[pl.ds(pl.multiple_of(d * ROW_TILES, ROW_TILES), ROW_TILES)],
                                     buf_ref.at[s, k, pl.ds(pl.multiple_of(r * ROW_TILES, ROW_TILES), ROW_TILES)],
                                     sems.at[s])

    def fetch(idx_ref, s):
        def start(r, c):
            for k in range(TOP_K):
                row_copy(s, r, k, idx_ref[0, 0, r * TOP_K + k]).start(priority=k % 2)
            return c

        lax.fori_loop(0, COMBINE_TILE, start, 0, unroll=DMA_UNROLL)

    @pl.when(i == 0)
    def _():
        fetch(dest_ref, slot)

    @pl.when(i + 1 < pl.num_programs(0))
    def _():
        fetch(next_ref, 1 - slot)

    def wait(r, c):
        for k in range(TOP_K):
            row_copy(slot, 0, 0, 0).wait()
        return c

    lax.fori_loop(0, COMBINE_TILE, wait, 0, unroll=DMA_UNROLL)
    gate = [jnp.broadcast_to(w_ref[:, k:k + 1], (COMBINE_TILE, LANES)) for k in range(TOP_K)]
    for s in range(ROW_TILES):
        sl = slice(s * LANES, (s + 1) * LANES)
        out = x_ref[:, sl]
        for k in range(TOP_K):
            out = out + gate[k] * buf_ref[slot, k, pl.ds(s, COMBINE_TILE, stride=ROW_TILES), :]
        o_ref[:, sl] = out


def _combine(x1, gate_w, dest, ys):
    T, D = x1.shape
    tc = COMBINE_TILE
    n = T // tc
    dest3 = dest.reshape(n, 1, tc * TOP_K)
    row = lambda w: pl.BlockSpec((tc, w), lambda i: (i, 0))
    idx = lambda f: pl.BlockSpec((1, 1, tc * TOP_K), f, memory_space=pltpu.SMEM)
    return pl.pallas_call(
        _combine_body,
        grid=(n,),
        in_specs=[idx(lambda i: (i, 0, 0)), idx(lambda i: (jnp.minimum(i + 1, n - 1), 0, 0)),
                  row(D), row(LANES), pl.BlockSpec(memory_space=pl.ANY)],
        out_specs=row(D),
        out_shape=jax.ShapeDtypeStruct((T, D), F32),
        scratch_shapes=[pltpu.VMEM((2, TOP_K, tc * ROW_TILES, LANES), F32), pltpu.SemaphoreType.DMA((2,))],
        compiler_params=_cparams(1),
        name="combine",
    )(dest3, dest3, x1, gate_w, ys)


def _mixer(x2, B, S, g_norm1, w_in, g_q, g_kc, g_ks, g_kw, pe_k, ck_w1, ck_b1, ck_w2, ck_b2,
           pe_v, cv_w1, cv_b1, cv_w2, cv_b2, conv_w, w_pa, w_pb, w_o):
    T, D = x2.shape
    G, H, hd = N_KV_HEADS, N_HEADS, HEAD_DIM
    qa, hk, hv, kas, kaw, vsw, gates, cbv, gab = _inproj(x2, g_norm1, w_in, g_q, g_ks, g_kw, S)
    nh = S // CMP_STRIDE
    kca = _compress(hk.reshape(B, nh, -1), pe_k, ck_w1, ck_b1, ck_w2, ck_b2, g_kc, True)
    vct = _compress(hv.reshape(B, nh, -1), pe_v, cv_w1, cv_b1, cv_w2, cv_b2, jnp.ones((hd,), F32), False)
    vsw5 = vsw.reshape(B, S, 2, G, hd)
    ones_rows = jnp.concatenate([jnp.ones((1,), BF16), jnp.zeros((V_ROWS - hd - 1,), BF16)])

    def key_major(v, tile):
        vt = v.reshape(B, S // tile, tile, G, hd).transpose(0, 3, 1, 4, 2)
        extra = jnp.broadcast_to(ones_rows[None, None, None, :, None], vt.shape[:3] + (V_ROWS - hd, tile))
        return jnp.concatenate([vt, extra], axis=3)

    vst = key_major(vsw5[:, :, 0], KEY_TILE)
    vwt = key_major(vsw5[:, :, 1], Q_TILE)
    gat = gates[:, :3 * H].reshape(B, S, G, 3 * HEADS_PER_GROUP).transpose(0, 2, 3, 1)
    gat = jnp.pad(gat, ((0, 0), (0, 0), (0, 16 - 3 * HEADS_PER_GROUP), (0, 0)))
    o = _attention(qa.reshape(B, S, -1), gat, kca, vct, kas.reshape(B, S, -1), vst, kaw.reshape(B, S, -1),
                   vwt, B, S)
    return _mixer_out(x2, o.reshape(T, H * hd), cbv, gab, conv_w, w_pa, w_pb, w_o, S)


def _moe(x1, g_norm2, w_r, b_r, w_gu, b_gu, w_dn, b_dn):
    T, D = x1.shape
    h2, mi, mf, cnt = _router(x1, g_norm2, w_r, b_r)
    top_e = mi[:, 0:TOP_K]
    rank = mi[:, TOP_K:2 * TOP_K]
    counts = cnt[0, :N_EXPERTS].astype(I32)
    padded = (counts + MOE_CHUNK - 1) // MOE_CHUNK * MOE_CHUNK
    pend = jnp.cumsum(padded)
    poffs = pend - padded
    dest = (poffs[top_e] + rank).reshape(-1)
    n_chunks = (T * TOP_K + MOE_CHUNK - 1) // MOE_CHUNK + N_EXPERTS
    chunk_start = jnp.arange(n_chunks, dtype=I32) * MOE_CHUNK
    chunk_e = jnp.minimum(jnp.sum((pend[None, :] <= chunk_start[:, None]).astype(I32), axis=1), N_EXPERTS - 1)
    n_used = (pend[-1:] // MOE_CHUNK).astype(I32)
    last_chunk = jnp.where(padded > 0, pend - MOE_CHUNK, -1).astype(I32)
    hperm = _dispatch(h2, dest, last_chunk, n_chunks * MOE_CHUNK)
    ys = _experts(hperm, chunk_e, n_used, w_gu, b_gu, w_dn, b_dn)
    return _combine(x1, mf, dest, ys)


def kernel(x, g_norm1, w_in, g_q, g_kc, g_ks, g_kw, pe_k, ck_w1, ck_b1, ck_w2, ck_b2, pe_v, cv_w1, cv_b1,
           cv_w2, cv_b2, conv_w, w_pa, w_pb, w_o, g_norm2, w_r, b_r, w_gu, b_gu, w_dn, b_dn):
    B, S, D = x.shape
    x2 = x.reshape(B * S, D)
    for l in range(g_norm1.shape[0]):
        x2 = _mixer(x2, B, S, g_norm1[l], w_in[l], g_q[l], g_kc[l], g_ks[l], g_kw[l], pe_k[l], ck_w1[l],
                    ck_b1[l], ck_w2[l], ck_b2[l], pe_v[l], cv_w1[l], cv_b1[l], cv_w2[l], cv_b2[l],
                    conv_w[l], w_pa[l], w_pb[l], w_o[l])
        x2 = _moe(x2, g_norm2[l], w_r[l], b_r[l], w_gu[l], b_gu[l], w_dn[l], b_dn[l])
    return x2.reshape(B, S, D)
```

```python
import functools

import numpy as np
import jax
import jax.numpy as jnp
from jax import lax
from jax.experimental import pallas as pl
from jax.experimental.pallas import tpu as pltpu

F32 = jnp.float32
BF16 = jnp.bfloat16
I32 = jnp.int32

N_HEADS = 8
HEAD_DIM = 64
N_KV_HEADS = 2
HEADS_PER_GROUP = N_HEADS // N_KV_HEADS
CMP_LEN = 32
CMP_STRIDE = 16
CMP_HID = 256
SEL_BLK = 64
SEL_TOPN = 16
WINDOW = 512
CONV_WIDTH = 512
CONV_K = 3
N_EXPERTS = 32
TOP_K = 4
SWIGLU_LIMIT = 7.0
SWIGLU_ALPHA = 1.702
MOE_CHUNK = 512
EPS = 1e-6
NEG = -1e30
POS = 1e30
MASK_BIG = 2.0 ** 100

LANES = 128
Q_TILE = 128
KEY_TILE = 512
N_AUG = 5
ROW_TILE = 512
VMEM_LIMIT = 56 * 1024 * 1024
GROUP_LANES = HEADS_PER_GROUP * Q_TILE
WIN_KEYS = WINDOW + Q_TILE
N_TILE_ROWS = 16
V_ROWS = HEAD_DIM + 8


def _cparams(n_axes):
    return pltpu.CompilerParams(dimension_semantics=("arbitrary",) * n_axes,
                                vmem_limit_bytes=VMEM_LIMIT)


def _dot(a, b):
    return jnp.dot(a, b, preferred_element_type=F32)


def _dot_nt(a, b):
    return lax.dot_general(a, b, (((1,), (1,)), ((), ())), preferred_element_type=F32)


ROW_TILES = 8


def _store_row_tiles(ref, val, first=0):
    n = val.shape[0]
    for s in range(ROW_TILES):
        ref[pl.ds(first * ROW_TILES + s, n, stride=ROW_TILES), :] = val[:, s * LANES:(s + 1) * LANES]


def _load_row_tiles(ref, n, first=0):
    return jnp.concatenate([ref[pl.ds(first * ROW_TILES + s, n, stride=ROW_TILES), :] for s in range(ROW_TILES)],
                           axis=1)


def _rms_pairs(v, bd):
    ss = _dot((v * v).astype(BF16), bd)
    return v * lax.rsqrt(ss + EPS)


def _inproj_body(x_ref, g1_ref, wq_ref, wkv_ref, wng_ref, wcv_ref, wmg_ref, gq_ref, gk_ref, bd_ref,
                 qtab_ref, kstab_ref, kwtab_ref,
                 qa_out, hk_out, hv_out, kas_out, kaw_out, vsw_out, gate_out, cbv_out, gab_out, raw_ref):
    x = x_ref[...]
    tm = x.shape[0]
    ms = jnp.mean(x * x, axis=-1, keepdims=True)
    h = (x * lax.rsqrt(ms + EPS) * g1_ref[...]).astype(BF16)
    bd = bd_ref[...]
    low = lax.broadcasted_iota(I32, (tm, LANES), 1) < HEAD_DIM

    def place(pair, tab_ref, out_ref, base, slot):
        for j, src in enumerate((pair, pltpu.roll(pair, HEAD_DIM, 1))):
            o = base + j * slot
            out_ref[:, o:o + LANES] = jnp.where(low, src, tab_ref[:, o:o + LANES].astype(F32)).astype(BF16)

    q = _dot(h, wq_ref[...])
    for c in range(N_HEADS * HEAD_DIM // LANES):
        sl = slice(c * LANES, (c + 1) * LANES)
        place(_rms_pairs(q[:, sl], bd) * gq_ref[:, sl], qtab_ref, qa_out, 2 * c * LANES, LANES)
    kv = _dot(h, wkv_ref[...])

    def emit_half_blocks(c, out_ref):
        nb = tm // CMP_STRIDE
        hw = CMP_STRIDE * HEAD_DIM
        raw_ref[...] = kv[:, c * LANES:(c + 1) * LANES]
        lo = lax.broadcasted_iota(I32, (nb, LANES), 1) < HEAD_DIM
        for u in range(CMP_STRIDE // 2):
            t0 = raw_ref[pl.ds(2 * u, nb, stride=CMP_STRIDE), :]
            t1 = raw_ref[pl.ds(2 * u + 1, nb, stride=CMP_STRIDE), :]
            out_ref[:, u * LANES:(u + 1) * LANES] = jnp.where(lo, t0, pltpu.roll(t1, HEAD_DIM, 1)).astype(BF16)
            out_ref[:, hw + u * LANES:hw + (u + 1) * LANES] = (
                jnp.where(lo, pltpu.roll(t0, HEAD_DIM, 1), t1).astype(BF16))

    emit_half_blocks(0, hk_out)
    emit_half_blocks(1, hv_out)
    place(_rms_pairs(kv[:, 256:384], bd) * gk_ref[:, 0:128], kstab_ref, kas_out, 0, 2 * LANES)
    for g in range(N_KV_HEADS):
        o = (2 * g + 1) * LANES
        kas_out[:, o:o + LANES] = kstab_ref[:, o:o + LANES]
    place(_rms_pairs(kv[:, 512:640], bd) * gk_ref[:, 128:256], kwtab_ref, kaw_out, 0, LANES)
    vsw_out[:, 0:128] = kv[:, 384:512].astype(BF16)
    vsw_out[:, 128:256] = kv[:, 640:768].astype(BF16)
    gate_out[...] = jax.nn.sigmoid(_dot(h, wng_ref[...]))
    cv = _dot(h, wcv_ref[...])
    cw = CONV_WIDTH
    cbv_out[:, 0:cw] = cv[:, 0:cw].astype(BF16)
    cbv_out[:, cw:2 * cw] = (cv[:, cw:2 * cw] * cv[:, 2 * cw:3 * cw]).astype(BF16)
    gab_out[...] = jax.nn.sigmoid(_dot(h, wmg_ref[...])).astype(BF16)


def _key_aug(pos):
    one = np.ones_like(pos, np.float32)
    return np.stack([one, one, (pos // 64 * 64).astype(np.float32), (pos % 64).astype(np.float32), 0 * one],
                    axis=-1)


def _slot_table(aug, slot, extra=None):
    S, n, _ = aug.shape
    tab = np.zeros((S, n, slot), np.float32)
    tab[:, :, HEAD_DIM:HEAD_DIM + N_AUG] = aug
    if extra is not None:
        tab[:, :, LANES:] = extra[:, None, :]
    return jnp.asarray(tab.reshape(S, n * slot), BF16)


def _inproj(x2, g1, w_in, g_q, g_ks, g_kw, S):
    T, D = x2.shape
    H, G = N_HEADS, N_KV_HEADS
    aw = H * HEAD_DIM
    kvw = G * HEAD_DIM
    o = 0
    wq = w_in[:, o:o + aw]; o += aw
    wkv = w_in[:, o:o + 6 * kvw]; o += 6 * kvw
    wng = w_in[:, o:o + 3 * H]; o += 3 * H
    wcv = w_in[:, o:o + 3 * CONV_WIDTH]; o += 3 * CONV_WIDTH
    wmg = w_in[:, o:o + 2 * D]
    wng = jnp.pad(wng, ((0, 0), (0, LANES - 3 * H)))
    wq, wkv, wng, wcv, wmg = (w.astype(BF16) for w in (wq, wkv, wng, wcv, wmg))
    gq = (jnp.tile(g_q, H) * (HEAD_DIM ** -0.5)).reshape(1, aw)
    gk = jnp.concatenate([jnp.tile(g_ks, G), jnp.tile(g_kw, G)]).reshape(1, 2 * kvw)
    idx = np.arange(LANES) // HEAD_DIM
    bd = jnp.asarray((idx[:, None] == idx[None, :]).astype(np.float32) / HEAD_DIM, BF16)
    pos = np.arange(S)
    hi = (pos // 64 * 64).astype(np.float32)[:, None]
    lo = (pos % 64).astype(np.float32)[:, None]
    sl = (2.0 ** (-8.0 * np.arange(1, H + 1) / H)).astype(np.float32)[None, :]
    aq = np.stack([-sl * hi, -sl * lo, np.broadcast_to(sl, (S, H)), np.broadcast_to(sl, (S, H)),
                   np.ones((S, H), np.float32)], axis=-1)
    ak = np.broadcast_to(_key_aug(pos)[:, None, :], (S, G, N_AUG))
    onehot = (pos[:, None] // SEL_BLK == np.arange(LANES)[None, :]).astype(np.float32)
    qtab = _slot_table(aq, LANES)
    kstab = _slot_table(ak, 2 * LANES, onehot)
    kwtab = _slot_table(ak, LANES)
    tm = ROW_TILE
    nst = S // tm
    row = lambda w: pl.BlockSpec((tm, w), lambda i: (i, 0))
    full = lambda a: pl.BlockSpec(a.shape, lambda i: (0,) * a.ndim)
    tab = lambda a: pl.BlockSpec((tm, a.shape[1]), lambda i: (i % nst, 0))
    ins = (x2, g1.reshape(1, D), wq, wkv, wng, wcv, wmg, gq, gk, bd, qtab, kstab, kwtab)
    widths = (H * LANES, G * 2 * LANES, G * LANES, 2 * kvw, LANES, 2 * CONV_WIDTH, 2 * D)
    dtypes = (BF16, BF16, BF16, BF16, F32, BF16, BF16)
    hw = G * CMP_STRIDE * HEAD_DIM
    nb = tm // CMP_STRIDE
    half = pl.BlockSpec((nb, hw), lambda i: (i, 0))
    half_shape = jax.ShapeDtypeStruct((T // CMP_STRIDE, hw), BF16)
    rows = [(row(w), jax.ShapeDtypeStruct((T, w), dt)) for w, dt in zip(widths, dtypes)]
    outs = [rows[0], (half, half_shape), (half, half_shape)] + rows[1:]
    return pl.pallas_call(
        _inproj_body,
        grid=(T // tm,),
        in_specs=[row(D)] + [full(a) for a in ins[1:10]] + [tab(a) for a in ins[10:]],
        out_specs=[o[0] for o in outs],
        out_shape=[o[1] for o in outs],
        scratch_shapes=[pltpu.VMEM((tm, LANES), F32)],
        compiler_params=_cparams(1),
        name="inproj",
    )(*ins)


def _compress_body(h_ref, w1_ref, pe_ref, b1_ref, w2_ref, b2_ref, g_ref, tab_ref, o_ref, *, for_keys):
    hb = h_ref[0]
    nc = hb.shape[0]
    a = _dot(hb, w1_ref[0])
    b = _dot(hb, w1_ref[1])
    c = _dot(pe_ref[0], w1_ref[0]) + _dot(pe_ref[1], w1_ref[1])
    pre = a + pltpu.roll(b, nc - 1, 0) + c[0:1, :] + b1_ref[...]
    hid = jax.nn.gelu(pre)
    out = _dot(hid.astype(BF16), w2_ref[...]) + b2_ref[...]
    if for_keys:
        ms = jnp.sum(out * out, axis=-1, keepdims=True) * (1.0 / HEAD_DIM)
        out = out * lax.rsqrt(ms + EPS) * g_ref[...]
        low = lax.broadcasted_iota(I32, out.shape, 1) < HEAD_DIM
        o_ref[0, 0] = jnp.where(low, out, tab_ref[...]).astype(BF16)
    else:
        o_ref[0, 0] = out.T[0:HEAD_DIM, :].astype(BF16)


def _compress(hh, pe, w1, b1, w2, b2, gain, for_keys):
    B, NC, _ = hh.shape
    G, HW = N_KV_HEADS, CMP_STRIDE * HEAD_DIM
    w1s = w1.reshape(2, HW, CMP_HID).astype(BF16)
    pes = jnp.broadcast_to(pe.reshape(2, 1, HW), (2, 8, HW)).astype(BF16)
    padl = lambda a: jnp.pad(a, ((0, 0), (0, LANES - HEAD_DIM)))
    tabn = np.zeros((NC, LANES), np.float32)
    tabn[:, HEAD_DIM:HEAD_DIM + N_AUG] = _key_aug(np.arange(NC) * CMP_STRIDE + (CMP_LEN - 1))
    tab = jnp.asarray(tabn)
    full = lambda a: pl.BlockSpec(a.shape, lambda b, g: (0,) * a.ndim)
    ins = (hh, w1s, pes, b1.reshape(1, CMP_HID), padl(w2).astype(BF16), padl(b2.reshape(1, HEAD_DIM)),
           padl(gain.reshape(1, HEAD_DIM)), tab)
    oshape = (B, G, NC, LANES) if for_keys else (B, G, HEAD_DIM, NC)
    return pl.pallas_call(
        functools.partial(_compress_body, for_keys=for_keys),
        grid=(B, G),
        in_specs=[pl.BlockSpec((1, NC, HW), lambda b, g: (b, 0, g))] + [full(a) for a in ins[1:]],
        out_specs=pl.BlockSpec((1, 1) + oshape[2:], lambda b, g: (b, g, 0, 0)),
        out_shape=jax.ShapeDtypeStruct(oshape, BF16),
        compiler_params=_cparams(2),
        name="compress_keys" if for_keys else "compress_values",
    )(*ins)


def _attn_body(qa_ref, g_ref, kca_ref, vct_ref, kas_ref, vst_ref, kaw_ref, vwt_ref, selmapt_ref, wbias_ref,
               tilemap_ref, cbias_ref, o_ref, qaug_ref, m_ref, acc_ref, s0_ref, s1_ref, flagv_ref, flags_ref,
               list_ref, sem,
               *, n_sel):
    i = pl.program_id(2)
    q0 = i * Q_TILE
    gl = GROUP_LANES
    for h in range(HEADS_PER_GROUP):
        qaug_ref[h * Q_TILE:(h + 1) * Q_TILE, 0:LANES] = qa_ref[0, :, h * LANES:(h + 1) * LANES]
    qa = qaug_ref[:, 0:LANES]

    nc = kca_ref.shape[2]
    s = _dot_nt(kca_ref[0, 0], qa)
    cb = cbias_ref[pl.ds(pl.multiple_of(nc - i * (Q_TILE // CMP_STRIDE), 8), nc), :]
    s = s + jnp.concatenate([cb] * HEADS_PER_GROUP, axis=1)
    m = jnp.max(s, axis=0, keepdims=True)
    p = jnp.exp(s - m)
    l = jnp.sum(p, axis=0, keepdims=True)
    has_entry = (q0 + (lax.broadcasted_iota(I32, (1, gl), 1) & (Q_TILE - 1))) >= CMP_LEN - 1
    pc = p * jnp.where(has_entry, 1.0 / l, 0.0)
    o_c = _dot(vct_ref[0, 0], pc.astype(BF16))

    ps = pc[:, 0:Q_TILE]
    for h in range(1, HEADS_PER_GROUP):
        ps = ps + pc[:, h * Q_TILE:(h + 1) * Q_TILE]
    ps_hi = ps.astype(BF16)
    ps_lo = (ps - ps_hi.astype(F32)).astype(BF16)
    imp = _dot(selmapt_ref[...], ps_hi) + _dot(selmapt_ref[...], ps_lo)
    jb = lax.broadcasted_iota(I32, (LANES, Q_TILE), 0)
    cur = (q0 + lax.broadcasted_iota(I32, (LANES, Q_TILE), 1)) // SEL_BLK
    forced = (jb == 0) | (jb == cur) | (jb == cur - 1)
    score = jnp.where(jb > cur, NEG, jnp.where(forced, POS, imp))
    jbf = jb.astype(F32)
    for _ in range(n_sel):
        mx = jnp.max(score, axis=0, keepdims=True)
        first = jnp.min(jnp.where(score == mx, jbf, float(LANES)), axis=0, keepdims=True)
        score = jnp.where(jbf == first, -jnp.inf, score)
    picked = score == -jnp.inf
    sel = jnp.where(picked, 1.0, 0.0)
    selbias = jnp.where(picked, 0.0, -MASK_BIG).T.astype(BF16)
    for h in range(HEADS_PER_GROUP):
        qaug_ref[h * Q_TILE:(h + 1) * Q_TILE, LANES:2 * LANES] = selbias
    tile_hits = jnp.max(_dot(tilemap_ref[...], sel.astype(BF16)), axis=1, keepdims=True)
    flagv_ref[...] = jnp.broadcast_to(tile_hits, flagv_ref.shape).astype(I32)
    flag_copy = pltpu.make_async_copy(flagv_ref, flags_ref, sem)
    flag_copy.start()

    w0 = pl.multiple_of(jnp.maximum(q0 - WINDOW, 0), Q_TILE)
    sw = _dot_nt(kaw_ref[0, pl.ds(w0, WIN_KEYS), :], qa)
    wb = wbias_ref[jnp.minimum(i, WINDOW // Q_TILE)]
    sw = sw + jnp.concatenate([wb] * HEADS_PER_GROUP, axis=1)
    mw = jnp.max(sw, axis=0, keepdims=True)
    pw = jnp.exp(sw - mw)
    c0 = w0 // Q_TILE
    vw = jnp.concatenate([vwt_ref[0, 0, c0 + j] for j in range(WIN_KEYS // Q_TILE)], axis=1)
    aw = _dot(vw, pw.astype(BF16))
    o_w = aw[0:HEAD_DIM] * (1.0 / aw[HEAD_DIM:HEAD_DIM + 1])

    qaug = qaug_ref[...]
    n_full = q0 // KEY_TILE
    kd = pl.multiple_of(n_full * KEY_TILE, KEY_TILE)
    sd = _dot_nt(kas_ref[0, pl.ds(kd, KEY_TILE), :], qaug)
    key = kd + lax.broadcasted_iota(I32, (KEY_TILE, gl), 0)
    qry = q0 + (lax.broadcasted_iota(I32, (KEY_TILE, gl), 1) & (Q_TILE - 1))
    s0_ref[...] = jnp.where(key <= qry, sd, NEG)
    m_ref[...] = jnp.full(m_ref.shape, -3.0e38, F32)
    acc_ref[...] = jnp.zeros(acc_ref.shape, F32)
    list_ref[0] = n_full

    def scores(kt):
        return _dot_nt(kas_ref[0, pl.ds(pl.multiple_of(kt * KEY_TILE, KEY_TILE), KEY_TILE), :], qaug)

    def absorb(s_ref, kt, live):
        sc = s_ref[...]
        m_old = m_ref[...]
        m_new = jnp.where(live, jnp.maximum(m_old, jnp.max(sc, axis=0, keepdims=True)), m_old)
        pv = _dot(vst_ref[0, 0, kt], jnp.exp(sc - m_new).astype(BF16))
        acc_ref[...] = jnp.exp(m_old - m_new) * acc_ref[...] + jnp.where(live, pv, 0.0)
        m_ref[...] = m_new

    flag_copy.wait()

    def compact(kt, n):
        active = flags_ref[kt, 0] > 0

        @pl.when(active)
        def _():
            list_ref[n] = kt

        return n + active.astype(I32)

    n_items = lax.fori_loop(0, n_full, compact, 1)

    last = n_items - 1

    def pair_body(j, carry):
        a = list_ref[2 * j]
        b = list_ref[jnp.minimum(2 * j + 1, last)]
        nxt = list_ref[jnp.minimum(2 * j + 2, last)]
        s1_ref[...] = scores(b)
        absorb(s0_ref, a, True)
        s0_ref[...] = scores(nxt)
        absorb(s1_ref, b, 2 * j + 1 <= last)
        return carry

    lax.fori_loop(0, (n_items + 1) // 2, pair_body, 0)

    o_s = acc_ref[0:HEAD_DIM, :] * (1.0 / acc_ref[HEAD_DIM:HEAD_DIM + 1, :])

    g = g_ref[0, 0]
    outs = []
    for h in range(HEADS_PER_GROUP):
        sl = slice(h * Q_TILE, (h + 1) * Q_TILE)
        outs.append(g[3 * h:3 * h + 1, :] * o_c[:, sl] + g[3 * h + 1:3 * h + 2, :] * o_s[:, sl]
                    + g[3 * h + 2:3 * h + 3, :] * o_w[:, sl])
    o_ref[0] = jnp.concatenate(outs, axis=0).T.astype(BF16)


def _attention(qa, gates_t, kca, vct, kas, vst, kaw, vwt, B, S):
    G, hd = N_KV_HEADS, HEAD_DIM
    NC = kca.shape[2]
    n_blk = S // SEL_BLK
    assert n_blk <= LANES and S % KEY_TILE == 0 and S >= WIN_KEYS
    n_sel = min(SEL_TOPN, n_blk)
    ratio, span = SEL_BLK // CMP_STRIDE, CMP_LEN // CMP_STRIDE
    sm = np.zeros((LANES, NC), np.float32)
    for j in range(n_blk):
        for a in range(ratio):
            for b in range(span):
                n = ratio * j + a - b
                if 0 <= n < NC - 1:
                    sm[j, n] += 1.0
    selmapt = jnp.asarray(sm, BF16)
    c = np.arange(WIN_KEYS)[:, None]
    r = np.arange(Q_TILE)[None, :]
    offs = np.arange(WINDOW // Q_TILE + 1)[:, None, None] * Q_TILE
    wbias = jnp.asarray(np.where((c - r <= offs) & (c - r > offs - WINDOW), 0.0, NEG), F32)
    tilemap = jnp.asarray(np.arange(LANES)[None, :] // (KEY_TILE // SEL_BLK) == np.arange(N_TILE_ROWS)[:, None],
                          BF16)
    assert S // KEY_TILE <= N_TILE_ROWS
    u = np.arange(2 * NC)[:, None] - NC
    cbias = jnp.asarray(np.where(CMP_STRIDE * u + (CMP_LEN - 1) <= np.arange(Q_TILE)[None, :], 0.0, NEG), F32)

    hpg = HEADS_PER_GROUP
    grp = lambda *blk: pl.BlockSpec((1, 1) + blk, lambda b, g, i: (b, g) + (0,) * len(blk))
    seq = lambda w: pl.BlockSpec((1, S, w), lambda b, g, i: (b, 0, g))
    const = lambda a: pl.BlockSpec(a.shape, lambda b, g, i: (0,) * a.ndim)
    return pl.pallas_call(
        functools.partial(_attn_body, n_sel=n_sel),
        grid=(B, G, S // Q_TILE),
        in_specs=[pl.BlockSpec((1, Q_TILE, hpg * LANES), lambda b, g, i: (b, i, g)),
                  pl.BlockSpec((1, 1, 16, Q_TILE), lambda b, g, i: (b, g, 0, i)),
                  grp(NC, LANES), grp(hd, NC), seq(2 * LANES), grp(S // KEY_TILE, V_ROWS, KEY_TILE),
                  seq(LANES), grp(S // Q_TILE, V_ROWS, Q_TILE), const(selmapt), const(wbias), const(tilemap),
                  const(cbias)],
        out_specs=pl.BlockSpec((1, Q_TILE, hpg * hd), lambda b, g, i: (b, i, g)),
        out_shape=jax.ShapeDtypeStruct((B, S, N_HEADS * hd), BF16),
        scratch_shapes=[pltpu.VMEM((GROUP_LANES, 2 * LANES), BF16), pltpu.VMEM((1, GROUP_LANES), F32),
                        pltpu.VMEM((V_ROWS, GROUP_LANES), F32), pltpu.VMEM((KEY_TILE, GROUP_LANES), F32),
                        pltpu.VMEM((KEY_TILE, GROUP_LANES), F32),
                        pltpu.VMEM((N_TILE_ROWS, LANES), I32), pltpu.SMEM((N_TILE_ROWS, LANES), I32),
                        pltpu.SMEM((N_TILE_ROWS,), I32), pltpu.SemaphoreType.DMA(())],
        compiler_params=_cparams(3),
        name="nsa_attention",
    )(qa, gates_t, kca, vct, kas, vst, kaw, vwt, selmapt, wbias, tilemap, cbias)


def _mixer_out_body(x_ref, oa_ref, cbv_ref, halo_ref, gab_ref, cw_ref, wpa_ref, wpb_ref, wo_ref, o_ref,
                    *, seq_len):
    i = pl.program_id(0)
    tm = x_ref.shape[0]
    cwd = CONV_WIDTH
    d = x_ref.shape[1]
    v = cbv_ref[:, cwd:2 * cwd].astype(F32)
    prev = halo_ref[:, cwd:2 * cwd].astype(F32)
    keep = ((i * tm) % seq_len != 0).astype(F32)
    p1 = prev[7:8, :] * keep
    p2 = prev[6:7, :] * keep
    ridx = lax.broadcasted_iota(I32, (tm, cwd), 0)
    v1 = jnp.where(ridx == 0, p1, pltpu.roll(v, 1, 0))
    v2 = jnp.where(ridx == 0, p2, jnp.where(ridx == 1, p1, pltpu.roll(v, 2, 0)))
    y = cw_ref[0:1, :] * v2 + cw_ref[1:2, :] * v1 + cw_ref[2:3, :] * v
    yb_in = (cbv_ref[:, 0:cwd].astype(F32) * y).astype(BF16)
    y_a = _dot(oa_ref[...], wpa_ref[...])
    y_b = _dot(yb_in, wpb_ref[...])
    merged = gab_ref[:, 0:d].astype(F32) * y_a + gab_ref[:, d:2 * d].astype(F32) * y_b
    o_ref[...] = x_ref[...] + _dot(merged.astype(BF16), wo_ref[...])


def _mixer_out(x2, oa, cbv, gab, conv_w, w_pa, w_pb, w_o, seq_len):
    T, D = x2.shape
    tm = ROW_TILE
    cw8 = jnp.pad(conv_w, ((0, 8 - CONV_K), (0, 0)))
    row = lambda w: pl.BlockSpec((tm, w), lambda i: (i, 0))
    full = lambda a: pl.BlockSpec(a.shape, lambda i: (0,) * a.ndim)
    halo = pl.BlockSpec((8, cbv.shape[1]), lambda i: (jnp.maximum(i * (tm // 8) - 1, 0), 0))
    wts = (cw8, w_pa.astype(BF16), w_pb.astype(BF16), w_o.astype(BF16))
    return pl.pallas_call(
        functools.partial(_mixer_out_body, seq_len=seq_len),
        grid=(T // tm,),
        in_specs=[row(D), row(oa.shape[1]), row(cbv.shape[1]), halo, row(gab.shape[1])] + [full(a) for a in wts],
        out_specs=row(D),
        out_shape=jax.ShapeDtypeStruct((T, D), F32),
        compiler_params=_cparams(1),
        name="mixer_out",
    )(x2, oa, cbv, cbv, gab, *wts)


def _router_body(x_ref, g2_ref, whi_ref, wlo_ref, br_ref, tri_ref, h_out, mi_out, mf_out, cnt_out):
    i = pl.program_id(0)

    @pl.when(i == 0)
    def _():
        cnt_out[...] = jnp.zeros(cnt_out.shape, F32)

    x = x_ref[...]
    tm = x.shape[0]
    ms = jnp.mean(x * x, axis=-1, keepdims=True)
    h = x * lax.rsqrt(ms + EPS) * g2_ref[...]
    _store_row_tiles(h_out, h)
    h_hi = h.astype(BF16)
    h_lo = (h - h_hi.astype(F32)).astype(BF16)
    logits = (_dot(h_hi, whi_ref[...]) + _dot(h_lo, whi_ref[...]) + _dot(h_hi, wlo_ref[...])) + br_ref[...]
    lane = lax.broadcasted_iota(I32, (tm, LANES), 1)
    lanef = lane.astype(F32)
    work = jnp.where(lane < N_EXPERTS, logits, -jnp.inf)
    vals, hits = [], []
    for _ in range(TOP_K):
        mx = jnp.max(work, axis=-1, keepdims=True)
        first = jnp.min(jnp.where(work == mx, lanef, float(LANES)), axis=-1, keepdims=True)
        hit = lanef == first
        vals.append(mx)
        hits.append(hit)
        work = jnp.where(hit, -jnp.inf, work)
    ex = [jnp.exp(v - vals[0]) for v in vals]
    den = ex[0]
    for e in ex[1:]:
        den = den + e
    cnt = jnp.zeros((tm, LANES), F32)
    for hit in hits:
        cnt = cnt + hit.astype(F32)
    before = _dot(tri_ref[...], cnt.astype(BF16)) + cnt_out[0:1, :]
    mi = jnp.zeros((tm, LANES), F32)
    mf = jnp.zeros((tm, LANES), F32)
    for k, hit in enumerate(hits):
        e_k = jnp.sum(jnp.where(hit, lanef, 0.0), axis=-1, keepdims=True)
        r_k = jnp.sum(jnp.where(hit, before, 0.0), axis=-1, keepdims=True)
        mi = jnp.where(lane == k, e_k, jnp.where(lane == TOP_K + k, r_k, mi))
        mf = jnp.where(lane == k, ex[k] / den, mf)
    mi_out[...] = mi.astype(I32)
    mf_out[...] = mf
    cnt_out[...] = cnt_out[...] + jnp.sum(cnt, axis=0, keepdims=True)


def _router(x1, g2, w_r, b_r):
    T, D = x1.shape
    assert D == ROW_TILES * LANES
    tm = ROW_TILE
    wpad = jnp.pad(w_r, ((0, 0), (0, LANES - N_EXPERTS)))
    whi = wpad.astype(BF16)
    wlo = (wpad - whi.astype(F32)).astype(BF16)
    br = jnp.pad(b_r, (0, LANES - N_EXPERTS)).reshape(1, LANES)
    tri = jnp.asarray(np.tril(np.ones((tm, tm), np.float32), -1), BF16)
    row = lambda w: pl.BlockSpec((tm, w), lambda i: (i, 0))
    full = lambda a: pl.BlockSpec(a.shape, lambda i: (0,) * a.ndim)
    ins = (x1, g2.reshape(1, D), whi, wlo, br, tri)
    return pl.pallas_call(
        _router_body,
        grid=(T // tm,),
        in_specs=[row(D)] + [full(a) for a in ins[1:]],
        out_specs=[pl.BlockSpec((tm * ROW_TILES, LANES), lambda i: (i, 0)), row(LANES), row(LANES),
                   pl.BlockSpec((8, LANES), lambda i: (0, 0))],
        out_shape=[jax.ShapeDtypeStruct((T * ROW_TILES, LANES), F32), jax.ShapeDtypeStruct((T, LANES), I32),
                   jax.ShapeDtypeStruct((T, LANES), F32), jax.ShapeDtypeStruct((8, LANES), F32)],
        compiler_params=_cparams(1),
        name="router",
    )(*ins)


DISPATCH_TILE = 512
DMA_UNROLL = 8


def _dispatch_body(dest_ref, last_ref, h_ref, o_hbm, zero_ref, sem, zsem):
    @pl.when(pl.program_id(0) == 0)
    def _():
        zero_ref[...] = jnp.zeros(zero_ref.shape, zero_ref.dtype)

        def clear(e):
            start = pl.multiple_of(last_ref[e] * ROW_TILES, MOE_CHUNK * ROW_TILES)
            return pltpu.make_async_copy(zero_ref, o_hbm.at[pl.ds(start, MOE_CHUNK * ROW_TILES)], zsem)

        for e in range(N_EXPERTS):
            @pl.when(last_ref[e] >= 0)
            def _():
                clear(e).start()
        for e in range(N_EXPERTS):
            @pl.when(last_ref[e] >= 0)
            def _():
                clear(e).wait()

    def row_copy(r, d):
        return pltpu.make_async_copy(h_ref.at[pl.ds(pl.multiple_of(r * ROW_TILES, ROW_TILES), ROW_TILES)],
                                     o_hbm.at[pl.ds(pl.multiple_of(d * ROW_TILES, ROW_TILES), ROW_TILES)], sem)

    def start(r, c):
        for k in range(TOP_K):
            row_copy(r, dest_ref[0, 0, r * TOP_K + k]).start(priority=k % 2)
        return c

    def wait(r, c):
        for k in range(TOP_K):
            row_copy(0, 0).wait()
        return c

    lax.fori_loop(0, DISPATCH_TILE, start, 0, unroll=DMA_UNROLL)
    lax.fori_loop(0, DISPATCH_TILE, wait, 0, unroll=DMA_UNROLL)


def _dispatch(h2, dest, last_chunk, n_rows):
    T = h2.shape[0] // ROW_TILES
    td = DISPATCH_TILE
    dest3 = dest.reshape(T // td, 1, td * TOP_K)
    return pl.pallas_call(
        _dispatch_body,
        grid=(T // td,),
        in_specs=[pl.BlockSpec((1, 1, td * TOP_K), lambda i: (i, 0, 0), memory_space=pltpu.SMEM),
                  pl.BlockSpec(memory_space=pltpu.SMEM),
                  pl.BlockSpec((td * ROW_TILES, LANES), lambda i: (i, 0))],
        out_specs=pl.BlockSpec(memory_space=pl.ANY),
        out_shape=jax.ShapeDtypeStruct((n_rows * ROW_TILES, LANES), h2.dtype),
        scratch_shapes=[pltpu.VMEM((MOE_CHUNK * ROW_TILES, LANES), h2.dtype), pltpu.SemaphoreType.DMA(()),
                        pltpu.SemaphoreType.DMA(())],
        compiler_params=_cparams(1),
        name="dispatch",
    )(dest3, last_chunk, h2)


EXPERT_SPLIT = 2


def _expert_body(ce_ref, nu_ref, x_ref, wgu_ref, bgu_ref, wdn_ref, bdn_ref, o_ref, wgu_bf, wdn_bf):
    c = pl.program_id(0)
    dff = wdn_ref.shape[1]

    @pl.when((c == 0) | (ce_ref[c] != ce_ref[jnp.maximum(c - 1, 0)]))
    def _():
        wgu_bf[...] = wgu_ref[0].astype(BF16)
        wdn_bf[...] = wdn_ref[0].astype(BF16)

    @pl.when(c < nu_ref[0])
    def _():
        for part in range(EXPERT_SPLIT):
            n = MOE_CHUNK // EXPERT_SPLIT
            x = _load_row_tiles(x_ref, n, part * n)
            gu = _dot(x.astype(BF16), wgu_bf[...]) + bgu_ref[0]
            g = jnp.minimum(gu[:, 0:dff], SWIGLU_LIMIT)
            u = jnp.clip(gu[:, dff:2 * dff], -SWIGLU_LIMIT, SWIGLU_LIMIT)
            act = (u + 1.0) * (g * jax.nn.sigmoid(SWIGLU_ALPHA * g))
            _store_row_tiles(o_ref, _dot(act.astype(BF16), wdn_bf[...]) + bdn_ref[0], part * n)

    @pl.when(c >= nu_ref[0])
    def _():
        o_ref[...] = jnp.zeros(o_ref.shape, F32)


def _experts(hperm, chunk_e, n_used, w_gu, b_gu, w_dn, b_dn):
    E, D, F2 = w_gu.shape
    assert D == ROW_TILES * LANES
    P = hperm.shape[0] // ROW_TILES
    dff = F2 // 2
    n_chunks = P // MOE_CHUNK
    chunk = (MOE_CHUNK * ROW_TILES, LANES)
    grid_spec = pltpu.PrefetchScalarGridSpec(
        num_scalar_prefetch=2,
        grid=(n_chunks,),
        in_specs=[pl.BlockSpec(chunk, lambda c, ce, nu: (jnp.minimum(c, nu[0] - 1), 0)),
                  pl.BlockSpec((1, D, F2), lambda c, ce, nu: (ce[c], 0, 0)),
                  pl.BlockSpec((1, 1, F2), lambda c, ce, nu: (ce[c], 0, 0)),
                  pl.BlockSpec((1, dff, D), lambda c, ce, nu: (ce[c], 0, 0)),
                  pl.BlockSpec((1, 1, D), lambda c, ce, nu: (ce[c], 0, 0))],
        out_specs=pl.BlockSpec(chunk, lambda c, ce, nu: (c, 0)),
        scratch_shapes=[pltpu.VMEM((D, F2), BF16), pltpu.VMEM((dff, D), BF16)],
    )
    return pl.pallas_call(
        _expert_body,
        grid_spec=grid_spec,
        out_shape=jax.ShapeDtypeStruct(hperm.shape, F32),
        compiler_params=_cparams(1),
        name="experts",
    )(chunk_e, n_used, hperm, w_gu, b_gu.reshape(E, 1, F2), w_dn, b_dn.reshape(E, 1, D))


COMBINE_TILE = 256


def _combine_body(dest_ref, next_ref, x_ref, w_ref, y_hbm, o_ref, buf_ref, sems):
    i = pl.program_id(0)
    slot = i % 2

    def row_copy(s, r, k, d):
        return pltpu.make_async_copy(y_hbm.at[pl.ds(pl.multiple_of(d * ROW_TILES, ROW_TILES), ROW_TILES)],
                                     buf_ref.at[s, k, pl.ds(pl.multiple_of(r * ROW_TILES, ROW_TILES), ROW_TILES)],
                                     sems.at[s])

    def fetch(idx_ref, s):
        def start(r, c):
            for k in range(TOP_K):
                row_copy(s, r, k, idx_ref[0, 0, r * TOP_K + k]).start(priority=k % 2)
            return c

        lax.fori_loop(0, COMBINE_TILE, start, 0, unroll=DMA_UNROLL)

    @pl.when(i == 0)
    def _():
        fetch(dest_ref, slot)

    @pl.when(i + 1 < pl.num_programs(0))
    def _():
        fetch(next_ref, 1 - slot)

    def wait(r, c):
        for k in range(TOP_K):
            row_copy(slot, 0, 0, 0).wait()
        return c

    lax.fori_loop(0, COMBINE_TILE, wait, 0, unroll=DMA_UNROLL)
    gate = [jnp.broadcast_to(w_ref[:, k:k + 1], (COMBINE_TILE, LANES)) for k in range(TOP_K)]
    for s in range(ROW_TILES):
        sl = slice(s * LANES, (s + 1) * LANES)
        out = x_ref[:, sl]
        for k in range(TOP_K):
            out = out + gate[k] * buf_ref[slot, k, pl.ds(s, COMBINE_TILE, stride=ROW_TILES), :]
        o_ref[:, sl] = out


def _combine(x1, gate_w, dest, ys):
    T, D = x1.shape
    tc = COMBINE_TILE
    n = T // tc
    dest3 = dest.reshape(n, 1, tc * TOP_K)
    row = lambda w: pl.BlockSpec((tc, w), lambda i: (i, 0))
    idx = lambda f: pl.BlockSpec((1, 1, tc * TOP_K), f, memory_space=pltpu.SMEM)
    return pl.pallas_call(
        _combine_body,
        grid=(n,),
        in_specs=[idx(lambda i: (i, 0, 0)), idx(lambda i: (jnp.minimum(i + 1, n - 1), 0, 0)),
                  row(D), row(LANES), pl.BlockSpec(memory_space=pl.ANY)],
        out_specs=row(D),
        out_shape=jax.ShapeDtypeStruct((T, D), F32),
        scratch_shapes=[pltpu.VMEM((2, TOP_K, tc * ROW_TILES, LANES), F32), pltpu.SemaphoreType.DMA((2,))],
        compiler_params=_cparams(1),
        name="combine",
    )(dest3, dest3, x1, gate_w, ys)


def _mixer(x2, B, S, g_norm1, w_in, g_q, g_kc, g_ks, g_kw, pe_k, ck_w1, ck_b1, ck_w2, ck_b2,
           pe_v, cv_w1, cv_b1, cv_w2, cv_b2, conv_w, w_pa, w_pb, w_o):
    T, D = x2.shape
    G, H, hd = N_KV_HEADS, N_HEADS, HEAD_DIM
    qa, hk, hv, kas, kaw, vsw, gates, cbv, gab = _inproj(x2, g_norm1, w_in, g_q, g_ks, g_kw, S)
    nh = S // CMP_STRIDE
    kca = _compress(hk.reshape(B, nh, -1), pe_k, ck_w1, ck_b1, ck_w2, ck_b2, g_kc, True)
    vct = _compress(hv.reshape(B, nh, -1), pe_v, cv_w1, cv_b1, cv_w2, cv_b2, jnp.ones((hd,), F32), False)
    vsw5 = vsw.reshape(B, S, 2, G, hd)
    ones_rows = jnp.concatenate([jnp.ones((1,), BF16), jnp.zeros((V_ROWS - hd - 1,), BF16)])

    def key_major(v, tile):
        vt = v.reshape(B, S // tile, tile, G, hd).transpose(0, 3, 1, 4, 2)
        extra = jnp.broadcast_to(ones_rows[None, None, None, :, None], vt.shape[:3] + (V_ROWS - hd, tile))
        return jnp.concatenate([vt, extra], axis=3)

    vst = key_major(vsw5[:, :, 0], KEY_TILE)
    vwt = key_major(vsw5[:, :, 1], Q_TILE)
    gat = gates[:, :3 * H].reshape(B, S, G, 3 * HEADS_PER_GROUP).transpose(0, 2, 3, 1)
    gat = jnp.pad(gat, ((0, 0), (0, 0), (0, 16 - 3 * HEADS_PER_GROUP), (0, 0)))
    o = _attention(qa.reshape(B, S, -1), gat, kca, vct, kas.reshape(B, S, -1), vst, kaw.reshape(B, S, -1),
                   vwt, B, S)
    return _mixer_out(x2, o.reshape(T, H * hd), cbv, gab, conv_w, w_pa, w_pb, w_o, S)


def _moe(x1, g_norm2, w_r, b_r, w_gu, b_gu, w_dn, b_dn):
    T, D = x1.shape
    h2, mi, mf, cnt = _router(x1, g_norm2, w_r, b_r)
    top_e = mi[:, 0:TOP_K]
    rank = mi[:, TOP_K:2 * TOP_K]
    counts = cnt[0, :N_EXPERTS].astype(I32)
    padded = (counts + MOE_CHUNK - 1) // MOE_CHUNK * MOE_CHUNK
    pend = jnp.cumsum(padded)
    poffs = pend - padded
    dest = (poffs[top_e] + rank).reshape(-1)
    n_chunks = (T * TOP_K + MOE_CHUNK - 1) // MOE_CHUNK + N_EXPERTS
    chunk_start = jnp.arange(n_chunks, dtype=I32) * MOE_CHUNK
    chunk_e = jnp.minimum(jnp.sum((pend[None, :] <= chunk_start[:, None]).astype(I32), axis=1), N_EXPERTS - 1)
    n_used = (pend[-1:] // MOE_CHUNK).astype(I32)
    last_chunk = jnp.where(padded > 0, pend - MOE_CHUNK, -1).astype(I32)
    hperm = _dispatch(h2, dest, last_chunk, n_chunks * MOE_CHUNK)
    ys = _experts(hperm, chunk_e, n_used, w_gu, b_gu, w_dn, b_dn)
    return _combine(x1, mf, dest, ys)


def kernel(x, g_norm1, w_in, g_q, g_kc, g_ks, g_kw, pe_k, ck_w1, ck_b1, ck_w2, ck_b2, pe_v, cv_w1, cv_b1,
           cv_w2, cv_b2, conv_w, w_pa, w_pb, w_o, g_norm2, w_r, b_r, w_gu, b_gu, w_dn, b_dn):
    B, S, D = x.shape
    x2 = x.reshape(B * S, D)
    for l in range(g_norm1.shape[0]):
        x2 = _mixer(x2, B, S, g_norm1[l], w_in[l], g_q[l], g_kc[l], g_ks[l], g_kw[l], pe_k[l], ck_w1[l],
                    ck_b1[l], ck_w2[l], ck_b2[l], pe_v[l], cv_w1[l], cv_b1[l], cv_w2[l], cv_b2[l],
                    conv_w[l], w_pa[l], w_pb[l], w_o[l])
        x2 = _moe(x2, g_norm2[l], w_r[l], b_r[l], w_gu[l], b_gu[l], w_dn[l], b_dn[l])
    return x2.reshape(B, S, D)
```

```python
import functools

import numpy as np
import jax
import jax.numpy as jnp
from jax import lax
from jax.experimental import pallas as pl
from jax.experimental.pallas import tpu as pltpu

F32 = jnp.float32
BF16 = jnp.bfloat16
I32 = jnp.int32

N_HEADS = 8
HEAD_DIM = 64
N_KV_HEADS = 2
HEADS_PER_GROUP = N_HEADS // N_KV_HEADS
CMP_LEN = 32
CMP_STRIDE = 16
CMP_HID = 256
SEL_BLK = 64
SEL_TOPN = 16
WINDOW = 512
CONV_WIDTH = 512
CONV_K = 3
N_EXPERTS = 32
TOP_K = 4
SWIGLU_LIMIT = 7.0
SWIGLU_ALPHA = 1.702
MOE_CHUNK = 512
EPS = 1e-6
NEG = -1e30
POS = 1e30
MASK_BIG = 2.0 ** 100

LANES = 128
Q_TILE = 256
KEY_TILE = 512
N_AUG = 5
ROW_TILE = 512
VMEM_LIMIT = 56 * 1024 * 1024
GROUP_LANES = HEADS_PER_GROUP * Q_TILE
WIN_KEYS = WINDOW + Q_TILE
N_TILE_ROWS = 16
V_ROWS = HEAD_DIM + 8


def _cparams(n_axes):
    return pltpu.CompilerParams(dimension_semantics=("arbitrary",) * n_axes,
                                vmem_limit_bytes=VMEM_LIMIT)


def _dot(a, b):
    return jnp.dot(a, b, preferred_element_type=F32)


def _dot_nt(a, b):
    return lax.dot_general(a, b, (((1,), (1,)), ((), ())), preferred_element_type=F32)


ROW_TILES = 8


def _store_row_tiles(ref, val, first=0):
    n = val.shape[0]
    for s in range(ROW_TILES):
        ref[pl.ds(first * ROW_TILES + s, n, stride=ROW_TILES), :] = val[:, s * LANES:(s + 1) * LANES]


def _load_row_tiles(ref, n, first=0):
    return jnp.concatenate([ref[pl.ds(first * ROW_TILES + s, n, stride=ROW_TILES), :] for s in range(ROW_TILES)],
                           axis=1)


def _rms_pairs(v, bd):
    ss = _dot((v * v).astype(BF16), bd)
    return v * lax.rsqrt(ss + EPS)


def _inproj_body(x_ref, g1_ref, wq_ref, wkv_ref, wng_ref, wcv_ref, wmg_ref, gq_ref, gk_ref, bd_ref,
                 qtab_ref, kstab_ref, kwtab_ref,
                 qa_out, hk_out, hv_out, kas_out, kaw_out, vsw_out, gate_out, cbv_out, gab_out, raw_ref):
    x = x_ref[...]
    tm = x.shape[0]
    ms = jnp.mean(x * x, axis=-1, keepdims=True)
    h = (x * lax.rsqrt(ms + EPS) * g1_ref[...]).astype(BF16)
    bd = bd_ref[...]
    low = lax.broadcasted_iota(I32, (tm, LANES), 1) < HEAD_DIM

    def place(pair, tab_ref, out_ref, base, slot):
        for j, src in enumerate((pair, pltpu.roll(pair, HEAD_DIM, 1))):
            o = base + j * slot
            out_ref[:, o:o + LANES] = jnp.where(low, src, tab_ref[:, o:o + LANES].astype(F32)).astype(BF16)

    q = _dot(h, wq_ref[...])
    for c in range(N_HEADS * HEAD_DIM // LANES):
        sl = slice(c * LANES, (c + 1) * LANES)
        place(_rms_pairs(q[:, sl], bd) * gq_ref[:, sl], qtab_ref, qa_out, 2 * c * LANES, LANES)
    kv = _dot(h, wkv_ref[...])

    def emit_half_blocks(c, out_ref):
        nb = tm // CMP_STRIDE
        hw = CMP_STRIDE * HEAD_DIM
        raw_ref[...] = kv[:, c * LANES:(c + 1) * LANES]
        lo = lax.broadcasted_iota(I32, (nb, LANES), 1) < HEAD_DIM
        for u in range(CMP_STRIDE // 2):
            t0 = raw_ref[pl.ds(2 * u, nb, stride=CMP_STRIDE), :]
            t1 = raw_ref[pl.ds(2 * u + 1, nb, stride=CMP_STRIDE), :]
            out_ref[:, u * LANES:(u + 1) * LANES] = jnp.where(lo, t0, pltpu.roll(t1, HEAD_DIM, 1)).astype(BF16)
            out_ref[:, hw + u * LANES:hw + (u + 1) * LANES] = (
                jnp.where(lo, pltpu.roll(t0, HEAD_DIM, 1), t1).astype(BF16))

    emit_half_blocks(0, hk_out)
    emit_half_blocks(1, hv_out)
    place(_rms_pairs(kv[:, 256:384], bd) * gk_ref[:, 0:128], kstab_ref, kas_out, 0, 2 * LANES)
    for g in range(N_KV_HEADS):
        o = (2 * g + 1) * LANES
        kas_out[:, o:o + LANES] = kstab_ref[:, o:o + LANES]
    place(_rms_pairs(kv[:, 512:640], bd) * gk_ref[:, 128:256], kwtab_ref, kaw_out, 0, LANES)
    vsw_out[:, 0:128] = kv[:, 384:512].astype(BF16)
    vsw_out[:, 128:256] = kv[:, 640:768].astype(BF16)
    gate_out[...] = jax.nn.sigmoid(_dot(h, wng_ref[...]))
    cv = _dot(h, wcv_ref[...])
    cw = CONV_WIDTH
    cbv_out[:, 0:cw] = cv[:, 0:cw].astype(BF16)
    cbv_out[:, cw:2 * cw] = (cv[:, cw:2 * cw] * cv[:, 2 * cw:3 * cw]).astype(BF16)
    gab_out[...] = jax.nn.sigmoid(_dot(h, wmg_ref[...])).astype(BF16)


def _key_aug(pos):
    one = np.ones_like(pos, np.float32)
    return np.stack([one, one, (pos // 64 * 64).astype(np.float32), (pos % 64).astype(np.float32), 0 * one],
                    axis=-1)


def _slot_table(aug, slot, extra=None):
    S, n, _ = aug.shape
    tab = np.zeros((S, n, slot), np.float32)
    tab[:, :, HEAD_DIM:HEAD_DIM + N_AUG] = aug
    if extra is not None:
        tab[:, :, LANES:] = extra[:, None, :]
    return jnp.asarray(tab.reshape(S, n * slot), BF16)


def _inproj(x2, g1, w_in, g_q, g_ks, g_kw, S):
    T, D = x2.shape
    H, G = N_HEADS, N_KV_HEADS
    aw = H * HEAD_DIM
    kvw = G * HEAD_DIM
    o = 0
    wq = w_in[:, o:o + aw]; o += aw
    wkv = w_in[:, o:o + 6 * kvw]; o += 6 * kvw
    wng = w_in[:, o:o + 3 * H]; o += 3 * H
    wcv = w_in[:, o:o + 3 * CONV_WIDTH]; o += 3 * CONV_WIDTH
    wmg = w_in[:, o:o + 2 * D]
    wng = jnp.pad(wng, ((0, 0), (0, LANES - 3 * H)))
    wq, wkv, wng, wcv, wmg = (w.astype(BF16) for w in (wq, wkv, wng, wcv, wmg))
    gq = (jnp.tile(g_q, H) * (HEAD_DIM ** -0.5)).reshape(1, aw)
    gk = jnp.concatenate([jnp.tile(g_ks, G), jnp.tile(g_kw, G)]).reshape(1, 2 * kvw)
    idx = np.arange(LANES) // HEAD_DIM
    bd = jnp.asarray((idx[:, None] == idx[None, :]).astype(np.float32) / HEAD_DIM, BF16)
    pos = np.arange(S)
    hi = (pos // 64 * 64).astype(np.float32)[:, None]
    lo = (pos % 64).astype(np.float32)[:, None]
    sl = (2.0 ** (-8.0 * np.arange(1, H + 1) / H)).astype(np.float32)[None, :]
    aq = np.stack([-sl * hi, -sl * lo, np.broadcast_to(sl, (S, H)), np.broadcast_to(sl, (S, H)),
                   np.ones((S, H), np.float32)], axis=-1)
    ak = np.broadcast_to(_key_aug(pos)[:, None, :], (S, G, N_AUG))
    onehot = (pos[:, None] // SEL_BLK == np.arange(LANES)[None, :]).astype(np.float32)
    qtab = _slot_table(aq, LANES)
    kstab = _slot_table(ak, 2 * LANES, onehot)
    kwtab = _slot_table(ak, LANES)
    tm = ROW_TILE
    nst = S // tm
    row = lambda w: pl.BlockSpec((tm, w), lambda i: (i, 0))
    full = lambda a: pl.BlockSpec(a.shape, lambda i: (0,) * a.ndim)
    tab = lambda a: pl.BlockSpec((tm, a.shape[1]), lambda i: (i % nst, 0))
    ins = (x2, g1.reshape(1, D), wq, wkv, wng, wcv, wmg, gq, gk, bd, qtab, kstab, kwtab)
    widths = (H * LANES, G * 2 * LANES, G * LANES, 2 * kvw, LANES, 2 * CONV_WIDTH, 2 * D)
    dtypes = (BF16, BF16, BF16, BF16, F32, BF16, BF16)
    hw = G * CMP_STRIDE * HEAD_DIM
    nb = tm // CMP_STRIDE
    half = pl.BlockSpec((nb, hw), lambda i: (i, 0))
    half_shape = jax.ShapeDtypeStruct((T // CMP_STRIDE, hw), BF16)
    rows = [(row(w), jax.ShapeDtypeStruct((T, w), dt)) for w, dt in zip(widths, dtypes)]
    outs = [rows[0], (half, half_shape), (half, half_shape)] + rows[1:]
    return pl.pallas_call(
        _inproj_body,
        grid=(T // tm,),
        in_specs=[row(D)] + [full(a) for a in ins[1:10]] + [tab(a) for a in ins[10:]],
        out_specs=[o[0] for o in outs],
        out_shape=[o[1] for o in outs],
        scratch_shapes=[pltpu.VMEM((tm, LANES), F32)],
        compiler_params=_cparams(1),
        name="inproj",
    )(*ins)


def _compress_body(h_ref, w1_ref, pe_ref, b1_ref, w2_ref, b2_ref, g_ref, tab_ref, o_ref, *, for_keys):
    hb = h_ref[0]
    nc = hb.shape[0]
    a = _dot(hb, w1_ref[0])
    b = _dot(hb, w1_ref[1])
    c = _dot(pe_ref[0], w1_ref[0]) + _dot(pe_ref[1], w1_ref[1])
    pre = a + pltpu.roll(b, nc - 1, 0) + c[0:1, :] + b1_ref[...]
    hid = jax.nn.gelu(pre)
    out = _dot(hid.astype(BF16), w2_ref[...]) + b2_ref[...]
    if for_keys:
        ms = jnp.sum(out * out, axis=-1, keepdims=True) * (1.0 / HEAD_DIM)
        out = out * lax.rsqrt(ms + EPS) * g_ref[...]
        low = lax.broadcasted_iota(I32, out.shape, 1) < HEAD_DIM
        o_ref[0, 0] = jnp.where(low, out, tab_ref[...]).astype(BF16)
    else:
        o_ref[0, 0] = out.T[0:HEAD_DIM, :].astype(BF16)


def _compress(hh, pe, w1, b1, w2, b2, gain, for_keys):
    B, NC, _ = hh.shape
    G, HW = N_KV_HEADS, CMP_STRIDE * HEAD_DIM
    w1s = w1.reshape(2, HW, CMP_HID).astype(BF16)
    pes = jnp.broadcast_to(pe.reshape(2, 1, HW), (2, 8, HW)).astype(BF16)
    padl = lambda a: jnp.pad(a, ((0, 0), (0, LANES - HEAD_DIM)))
    tabn = np.zeros((NC, LANES), np.float32)
    tabn[:, HEAD_DIM:HEAD_DIM + N_AUG] = _key_aug(np.arange(NC) * CMP_STRIDE + (CMP_LEN - 1))
    tab = jnp.asarray(tabn)
    full = lambda a: pl.BlockSpec(a.shape, lambda b, g: (0,) * a.ndim)
    ins = (hh, w1s, pes, b1.reshape(1, CMP_HID), padl(w2).astype(BF16), padl(b2.reshape(1, HEAD_DIM)),
           padl(gain.reshape(1, HEAD_DIM)), tab)
    oshape = (B, G, NC, LANES) if for_keys else (B, G, HEAD_DIM, NC)
    return pl.pallas_call(
        functools.partial(_compress_body, for_keys=for_keys),
        grid=(B, G),
        in_specs=[pl.BlockSpec((1, NC, HW), lambda b, g: (b, 0, g))] + [full(a) for a in ins[1:]],
        out_specs=pl.BlockSpec((1, 1) + oshape[2:], lambda b, g: (b, g, 0, 0)),
        out_shape=jax.ShapeDtypeStruct(oshape, BF16),
        compiler_params=_cparams(2),
        name="compress_keys" if for_keys else "compress_values",
    )(*ins)


def _attn_body(qa_ref, g_ref, kca_ref, vct_ref, kas_ref, vst_ref, kaw_ref, vwt_ref, selmapt_ref, wbias_ref,
               tilemap_ref, cbias_ref, o_ref, qaug_ref, m_ref, acc_ref, s0_ref, s1_ref, flagv_ref, flags_ref,
               list_ref, sem,
               *, n_sel):
    i = pl.program_id(2)
    q0 = i * Q_TILE
    gl = GROUP_LANES
    for h in range(HEADS_PER_GROUP):
        qaug_ref[h * Q_TILE:(h + 1) * Q_TILE, 0:LANES] = qa_ref[0, :, h * LANES:(h + 1) * LANES]
    qa = qaug_ref[:, 0:LANES]

    nc = kca_ref.shape[2]
    s = _dot_nt(kca_ref[0, 0], qa)
    cb = cbias_ref[pl.ds(pl.multiple_of(nc - i * (Q_TILE // CMP_STRIDE), 8), nc), :]
    s = s + jnp.concatenate([cb] * HEADS_PER_GROUP, axis=1)
    m = jnp.max(s, axis=0, keepdims=True)
    p = jnp.exp(s - m)
    l = jnp.sum(p, axis=0, keepdims=True)
    has_entry = (q0 + (lax.broadcasted_iota(I32, (1, gl), 1) & (Q_TILE - 1))) >= CMP_LEN - 1
    pc = p * jnp.where(has_entry, 1.0 / l, 0.0)
    o_c = _dot(vct_ref[0, 0], pc.astype(BF16))

    ps = pc[:, 0:Q_TILE]
    for h in range(1, HEADS_PER_GROUP):
        ps = ps + pc[:, h * Q_TILE:(h + 1) * Q_TILE]
    ps_hi = ps.astype(BF16)
    ps_lo = (ps - ps_hi.astype(F32)).astype(BF16)
    imp = _dot(selmapt_ref[...], ps_hi) + _dot(selmapt_ref[...], ps_lo)
    jb = lax.broadcasted_iota(I32, (LANES, Q_TILE), 0)
    cur = (q0 + lax.broadcasted_iota(I32, (LANES, Q_TILE), 1)) // SEL_BLK
    forced = (jb == 0) | (jb == cur) | (jb == cur - 1)
    score = jnp.where(jb > cur, NEG, jnp.where(forced, POS, imp))
    jbf = jb.astype(F32)
    for _ in range(n_sel):
        mx = jnp.max(score, axis=0, keepdims=True)
        first = jnp.min(jnp.where(score == mx, jbf, float(LANES)), axis=0, keepdims=True)
        score = jnp.where(jbf == first, -jnp.inf, score)
    picked = score == -jnp.inf
    sel = jnp.where(picked, 1.0, 0.0)
    selbias = jnp.where(picked, 0.0, -MASK_BIG).T.astype(BF16)
    for h in range(HEADS_PER_GROUP):
        qaug_ref[h * Q_TILE:(h + 1) * Q_TILE, LANES:2 * LANES] = selbias
    tile_hits = jnp.max(_dot(tilemap_ref[...], sel.astype(BF16)), axis=1, keepdims=True)
    flagv_ref[...] = jnp.broadcast_to(tile_hits, flagv_ref.shape).astype(I32)
    flag_copy = pltpu.make_async_copy(flagv_ref, flags_ref, sem)
    flag_copy.start()

    w0 = pl.multiple_of(jnp.maximum(q0 - WINDOW, 0), Q_TILE)
    sw = _dot_nt(kaw_ref[0, pl.ds(w0, WIN_KEYS), :], qa)
    wb = wbias_ref[jnp.minimum(i, WINDOW // Q_TILE)]
    sw = sw + jnp.concatenate([wb] * HEADS_PER_GROUP, axis=1)
    mw = jnp.max(sw, axis=0, keepdims=True)
    pw = jnp.exp(sw - mw)
    c0 = w0 // Q_TILE
    vw = jnp.concatenate([vwt_ref[0, 0, c0 + j] for j in range(WIN_KEYS // Q_TILE)], axis=1)
    aw = _dot(vw, pw.astype(BF16))
    o_w = aw[0:HEAD_DIM] * (1.0 / aw[HEAD_DIM:HEAD_DIM + 1])

    qaug = qaug_ref[...]
    n_full = q0 // KEY_TILE
    kd = pl.multiple_of(n_full * KEY_TILE, KEY_TILE)
    sd = _dot_nt(kas_ref[0, pl.ds(kd, KEY_TILE), :], qaug)
    key = kd + lax.broadcasted_iota(I32, (KEY_TILE, gl), 0)
    qry = q0 + (lax.broadcasted_iota(I32, (KEY_TILE, gl), 1) & (Q_TILE - 1))
    s0_ref[...] = jnp.where(key <= qry, sd, NEG)
    m_ref[...] = jnp.full(m_ref.shape, -3.0e38, F32)
    acc_ref[...] = jnp.zeros(acc_ref.shape, F32)
    list_ref[0] = n_full

    def scores(kt):
        return _dot_nt(kas_ref[0, pl.ds(pl.multiple_of(kt * KEY_TILE, KEY_TILE), KEY_TILE), :], qaug)

    def absorb(s_ref, kt, live):
        sc = s_ref[...]
        m_old = m_ref[...]
        m_new = jnp.where(live, jnp.maximum(m_old, jnp.max(sc, axis=0, keepdims=True)), m_old)
        pv = _dot(vst_ref[0, 0, kt], jnp.exp(sc - m_new).astype(BF16))
        acc_ref[...] = jnp.exp(m_old - m_new) * acc_ref[...] + jnp.where(live, pv, 0.0)
        m_ref[...] = m_new

    flag_copy.wait()

    def compact(kt, n):
        active = flags_ref[kt, 0] > 0

        @pl.when(active)
        def _():
            list_ref[n] = kt

        return n + active.astype(I32)

    n_items = lax.fori_loop(0, n_full, compact, 1)

    last = n_items - 1

    def pair_body(j, carry):
        a = list_ref[2 * j]
        b = list_ref[jnp.minimum(2 * j + 1, last)]
        nxt = list_ref[jnp.minimum(2 * j + 2, last)]
        s1_ref[...] = scores(b)
        absorb(s0_ref, a, True)
        s0_ref[...] = scores(nxt)
        absorb(s1_ref, b, 2 * j + 1 <= last)
        return carry

    lax.fori_loop(0, (n_items + 1) // 2, pair_body, 0)

    o_s = acc_ref[0:HEAD_DIM, :] * (1.0 / acc_ref[HEAD_DIM:HEAD_DIM + 1, :])

    g = g_ref[0, 0]
    outs = []
    for h in range(HEADS_PER_GROUP):
        sl = slice(h * Q_TILE, (h + 1) * Q_TILE)
        outs.append(g[3 * h:3 * h + 1, :] * o_c[:, sl] + g[3 * h + 1:3 * h + 2, :] * o_s[:, sl]
                    + g[3 * h + 2:3 * h + 3, :] * o_w[:, sl])
    o_ref[0] = jnp.concatenate(outs, axis=0).T.astype(BF16)


def _attention(qa, gates_t, kca, vct, kas, vst, kaw, vwt, B, S):
    G, hd = N_KV_HEADS, HEAD_DIM
    NC = kca.shape[2]
    n_blk = S // SEL_BLK
    assert n_blk <= LANES and S % KEY_TILE == 0 and S >= WIN_KEYS
    n_sel = min(SEL_TOPN, n_blk)
    ratio, span = SEL_BLK // CMP_STRIDE, CMP_LEN // CMP_STRIDE
    sm = np.zeros((LANES, NC), np.float32)
    for j in range(n_blk):
        for a in range(ratio):
            for b in range(span):
                n = ratio * j + a - b
                if 0 <= n < NC - 1:
                    sm[j, n] += 1.0
    selmapt = jnp.asarray(sm, BF16)
    c = np.arange(WIN_KEYS)[:, None]
    r = np.arange(Q_TILE)[None, :]
    offs = np.arange(WINDOW // Q_TILE + 1)[:, None, None] * Q_TILE
    wbias = jnp.asarray(np.where((c - r <= offs) & (c - r > offs - WINDOW), 0.0, NEG), F32)
    tilemap = jnp.asarray(np.arange(LANES)[None, :] // (KEY_TILE // SEL_BLK) == np.arange(N_TILE_ROWS)[:, None],
                          BF16)
    assert S // KEY_TILE <= N_TILE_ROWS
    u = np.arange(2 * NC)[:, None] - NC
    cbias = jnp.asarray(np.where(CMP_STRIDE * u + (CMP_LEN - 1) <= np.arange(Q_TILE)[None, :], 0.0, NEG), F32)

    hpg = HEADS_PER_GROUP
    grp = lambda *blk: pl.BlockSpec((1, 1) + blk, lambda b, g, i: (b, g) + (0,) * len(blk))
    seq = lambda w: pl.BlockSpec((1, S, w), lambda b, g, i: (b, 0, g))
    const = lambda a: pl.BlockSpec(a.shape, lambda b, g, i: (0,) * a.ndim)
    return pl.pallas_call(
        functools.partial(_attn_body, n_sel=n_sel),
        grid=(B, G, S // Q_TILE),
        in_specs=[pl.BlockSpec((1, Q_TILE, hpg * LANES), lambda b, g, i: (b, i, g)),
                  pl.BlockSpec((1, 1, 16, Q_TILE), lambda b, g, i: (b, g, 0, i)),
                  grp(NC, LANES), grp(hd, NC), seq(2 * LANES), grp(S // KEY_TILE, V_ROWS, KEY_TILE),
                  seq(LANES), grp(S // Q_TILE, V_ROWS, Q_TILE), const(selmapt), const(wbias), const(tilemap),
                  const(cbias)],
        out_specs=pl.BlockSpec((1, Q_TILE, hpg * hd), lambda b, g, i: (b, i, g)),
        out_shape=jax.ShapeDtypeStruct((B, S, N_HEADS * hd), BF16),
        scratch_shapes=[pltpu.VMEM((GROUP_LANES, 2 * LANES), BF16), pltpu.VMEM((1, GROUP_LANES), F32),
                        pltpu.VMEM((V_ROWS, GROUP_LANES), F32), pltpu.VMEM((KEY_TILE, GROUP_LANES), F32),
                        pltpu.VMEM((KEY_TILE, GROUP_LANES), F32),
                        pltpu.VMEM((N_TILE_ROWS, LANES), I32), pltpu.SMEM((N_TILE_ROWS, LANES), I32),
                        pltpu.SMEM((N_TILE_ROWS,), I32), pltpu.SemaphoreType.DMA(())],
        compiler_params=_cparams(3),
        name="nsa_attention",
    )(qa, gates_t, kca, vct, kas, vst, kaw, vwt, selmapt, wbias, tilemap, cbias)


def _mixer_out_body(x_ref, oa_ref, cbv_ref, halo_ref, gab_ref, cw_ref, wpa_ref, wpb_ref, wo_ref, o_ref,
                    *, seq_len):
    i = pl.program_id(0)
    tm = x_ref.shape[0]
    cwd = CONV_WIDTH
    d = x_ref.shape[1]
    v = cbv_ref[:, cwd:2 * cwd].astype(F32)
    prev = halo_ref[:, cwd:2 * cwd].astype(F32)
    keep = ((i * tm) % seq_len != 0).astype(F32)
    p1 = prev[7:8, :] * keep
    p2 = prev[6:7, :] * keep
    ridx = lax.broadcasted_iota(I32, (tm, cwd), 0)
    v1 = jnp.where(ridx == 0, p1, pltpu.roll(v, 1, 0))
    v2 = jnp.where(ridx == 0, p2, jnp.where(ridx == 1, p1, pltpu.roll(v, 2, 0)))
    y = cw_ref[0:1, :] * v2 + cw_ref[1:2, :] * v1 + cw_ref[2:3, :] * v
    yb_in = (cbv_ref[:, 0:cwd].astype(F32) * y).astype(BF16)
    y_a = _dot(oa_ref[...], wpa_ref[...])
    y_b = _dot(yb_in, wpb_ref[...])
    merged = gab_ref[:, 0:d].astype(F32) * y_a + gab_ref[:, d:2 * d].astype(F32) * y_b
    o_ref[...] = x_ref[...] + _dot(merged.astype(BF16), wo_ref[...])


def _mixer_out(x2, oa, cbv, gab, conv_w, w_pa, w_pb, w_o, seq_len):
    T, D = x2.shape
    tm = ROW_TILE
    cw8 = jnp.pad(conv_w, ((0, 8 - CONV_K), (0, 0)))
    row = lambda w: pl.BlockSpec((tm, w), lambda i: (i, 0))
    full = lambda a: pl.BlockSpec(a.shape, lambda i: (0,) * a.ndim)
    halo = pl.BlockSpec((8, cbv.shape[1]), lambda i: (jnp.maximum(i * (tm // 8) - 1, 0), 0))
    wts = (cw8, w_pa.astype(BF16), w_pb.astype(BF16), w_o.astype(BF16))
    return pl.pallas_call(
        functools.partial(_mixer_out_body, seq_len=seq_len),
        grid=(T // tm,),
        in_specs=[row(D), row(oa.shape[1]), row(cbv.shape[1]), halo, row(gab.shape[1])] + [full(a) for a in wts],
        out_specs=row(D),
        out_shape=jax.ShapeDtypeStruct((T, D), F32),
        compiler_params=_cparams(1),
        name="mixer_out",
    )(x2, oa, cbv, cbv, gab, *wts)


def _router_body(x_ref, g2_ref, whi_ref, wlo_ref, br_ref, tri_ref, h_out, mi_out, mf_out, cnt_out):
    i = pl.program_id(0)

    @pl.when(i == 0)
    def _():
        cnt_out[...] = jnp.zeros(cnt_out.shape, F32)

    x = x_ref[...]
    tm = x.shape[0]
    ms = jnp.mean(x * x, axis=-1, keepdims=True)
    h = x * lax.rsqrt(ms + EPS) * g2_ref[...]
    _store_row_tiles(h_out, h)
    h_hi = h.astype(BF16)
    h_lo = (h - h_hi.astype(F32)).astype(BF16)
    logits = (_dot(h_hi, whi_ref[...]) + _dot(h_lo, whi_ref[...]) + _dot(h_hi, wlo_ref[...])) + br_ref[...]
    lane = lax.broadcasted_iota(I32, (tm, LANES), 1)
    lanef = lane.astype(F32)
    work = jnp.where(lane < N_EXPERTS, logits, -jnp.inf)
    vals, hits = [], []
    for _ in range(TOP_K):
        mx = jnp.max(work, axis=-1, keepdims=True)
        first = jnp.min(jnp.where(work == mx, lanef, float(LANES)), axis=-1, keepdims=True)
        hit = lanef == first
        vals.append(mx)
        hits.append(hit)
        work = jnp.where(hit, -jnp.inf, work)
    ex = [jnp.exp(v - vals[0]) for v in vals]
    den = ex[0]
    for e in ex[1:]:
        den = den + e
    cnt = jnp.zeros((tm, LANES), F32)
    for hit in hits:
        cnt = cnt + hit.astype(F32)
    before = _dot(tri_ref[...], cnt.astype(BF16)) + cnt_out[0:1, :]
    mi = jnp.zeros((tm, LANES), F32)
    mf = jnp.zeros((tm, LANES), F32)
    for k, hit in enumerate(hits):
        e_k = jnp.sum(jnp.where(hit, lanef, 0.0), axis=-1, keepdims=True)
        r_k = jnp.sum(jnp.where(hit, before, 0.0), axis=-1, keepdims=True)
        mi = jnp.where(lane == k, e_k, jnp.where(lane == TOP_K + k, r_k, mi))
        mf = jnp.where(lane == k, ex[k] / den, mf)
    mi_out[...] = mi.astype(I32)
    mf_out[...] = mf
    cnt_out[...] = cnt_out[...] + jnp.sum(cnt, axis=0, keepdims=True)


def _router(x1, g2, w_r, b_r):
    T, D = x1.shape
    assert D == ROW_TILES * LANES
    tm = ROW_TILE
    wpad = jnp.pad(w_r, ((0, 0), (0, LANES - N_EXPERTS)))
    whi = wpad.astype(BF16)
    wlo = (wpad - whi.astype(F32)).astype(BF16)
    br = jnp.pad(b_r, (0, LANES - N_EXPERTS)).reshape(1, LANES)
    tri = jnp.asarray(np.tril(np.ones((tm, tm), np.float32), -1), BF16)
    row = lambda w: pl.BlockSpec((tm, w), lambda i: (i, 0))
    full = lambda a: pl.BlockSpec(a.shape, lambda i: (0,) * a.ndim)
    ins = (x1, g2.reshape(1, D), whi, wlo, br, tri)
    return pl.pallas_call(
        _router_body,
        grid=(T // tm,),
        in_specs=[row(D)] + [full(a) for a in ins[1:]],
        out_specs=[pl.BlockSpec((tm * ROW_TILES, LANES), lambda i: (i, 0)), row(LANES), row(LANES),
                   pl.BlockSpec((8, LANES), lambda i: (0, 0))],
        out_shape=[jax.ShapeDtypeStruct((T * ROW_TILES, LANES), F32), jax.ShapeDtypeStruct((T, LANES), I32),
                   jax.ShapeDtypeStruct((T, LANES), F32), jax.ShapeDtypeStruct((8, LANES), F32)],
        compiler_params=_cparams(1),
        name="router",
    )(*ins)


DISPATCH_TILE = 512
DMA_UNROLL = 8


def _dispatch_body(dest_ref, last_ref, h_ref, o_hbm, zero_ref, sem, zsem):
    @pl.when(pl.program_id(0) == 0)
    def _():
        zero_ref[...] = jnp.zeros(zero_ref.shape, zero_ref.dtype)

        def clear(e):
            start = pl.multiple_of(last_ref[e] * ROW_TILES, MOE_CHUNK * ROW_TILES)
            return pltpu.make_async_copy(zero_ref, o_hbm.at[pl.ds(start, MOE_CHUNK * ROW_TILES)], zsem)

        for e in range(N_EXPERTS):
            @pl.when(last_ref[e] >= 0)
            def _():
                clear(e).start()
        for e in range(N_EXPERTS):
            @pl.when(last_ref[e] >= 0)
            def _():
                clear(e).wait()

    def row_copy(r, d):
        return pltpu.make_async_copy(h_ref.at[pl.ds(pl.multiple_of(r * ROW_TILES, ROW_TILES), ROW_TILES)],
                                     o_hbm.at[pl.ds(pl.multiple_of(d * ROW_TILES, ROW_TILES), ROW_TILES)], sem)

    def start(r, c):
        for k in range(TOP_K):
            row_copy(r, dest_ref[0, 0, r * TOP_K + k]).start(priority=k % 2)
        return c

    def wait(r, c):
        for k in range(TOP_K):
            row_copy(0, 0).wait()
        return c

    lax.fori_loop(0, DISPATCH_TILE, start, 0, unroll=DMA_UNROLL)
    lax.fori_loop(0, DISPATCH_TILE, wait, 0, unroll=DMA_UNROLL)


def _dispatch(h2, dest, last_chunk, n_rows):
    T = h2.shape[0] // ROW_TILES
    td = DISPATCH_TILE
    dest3 = dest.reshape(T // td, 1, td * TOP_K)
    return pl.pallas_call(
        _dispatch_body,
        grid=(T // td,),
        in_specs=[pl.BlockSpec((1, 1, td * TOP_K), lambda i: (i, 0, 0), memory_space=pltpu.SMEM),
                  pl.BlockSpec(memory_space=pltpu.SMEM),
                  pl.BlockSpec((td * ROW_TILES, LANES), lambda i: (i, 0))],
        out_specs=pl.BlockSpec(memory_space=pl.ANY),
        out_shape=jax.ShapeDtypeStruct((n_rows * ROW_TILES, LANES), h2.dtype),
        scratch_shapes=[pltpu.VMEM((MOE_CHUNK * ROW_TILES, LANES), h2.dtype), pltpu.SemaphoreType.DMA(()),
                        pltpu.SemaphoreType.DMA(())],
        compiler_params=_cparams(1),
        name="dispatch",
    )(dest3, last_chunk, h2)


EXPERT_SPLIT = 1


def _expert_body(ce_ref, nu_ref, x_ref, wgu_ref, bgu_ref, wdn_ref, bdn_ref, o_ref, wgu_bf, wdn_bf):
    c = pl.program_id(0)
    dff = wdn_ref.shape[1]

    @pl.when((c == 0) | (ce_ref[c] != ce_ref[jnp.maximum(c - 1, 0)]))
    def _():
        wgu_bf[...] = wgu_ref[0].astype(BF16)
        wdn_bf[...] = wdn_ref[0].astype(BF16)

    @pl.when(c < nu_ref[0])
    def _():
        for part in range(EXPERT_SPLIT):
            n = MOE_CHUNK // EXPERT_SPLIT
            x = _load_row_tiles(x_ref, n, part * n)
            gu = _dot(x.astype(BF16), wgu_bf[...]) + bgu_ref[0]
            g = jnp.minimum(gu[:, 0:dff], SWIGLU_LIMIT)
            u = jnp.clip(gu[:, dff:2 * dff], -SWIGLU_LIMIT, SWIGLU_LIMIT)
            act = (u + 1.0) * (g * jax.nn.sigmoid(SWIGLU_ALPHA * g))
            _store_row_tiles(o_ref, _dot(act.astype(BF16), wdn_bf[...]) + bdn_ref[0], part * n)

    @pl.when(c >= nu_ref[0])
    def _():
        o_ref[...] = jnp.zeros(o_ref.shape, F32)


def _experts(hperm, chunk_e, n_used, w_gu, b_gu, w_dn, b_dn):
    E, D, F2 = w_gu.shape
    assert D == ROW_TILES * LANES
    P = hperm.shape[0] // ROW_TILES
    dff = F2 // 2
    n_chunks = P // MOE_CHUNK
    chunk = (MOE_CHUNK * ROW_TILES, LANES)
    grid_spec = pltpu.PrefetchScalarGridSpec(
        num_scalar_prefetch=2,
        grid=(n_chunks,),
        in_specs=[pl.BlockSpec(chunk, lambda c, ce, nu: (jnp.minimum(c, nu[0] - 1), 0)),
                  pl.BlockSpec((1, D, F2), lambda c, ce, nu: (ce[c], 0, 0)),
                  pl.BlockSpec((1, 1, F2), lambda c, ce, nu: (ce[c], 0, 0)),
                  pl.BlockSpec((1, dff, D), lambda c, ce, nu: (ce[c], 0, 0)),
                  pl.BlockSpec((1, 1, D), lambda c, ce, nu: (ce[c], 0, 0))],
        out_specs=pl.BlockSpec(chunk, lambda c, ce, nu: (c, 0)),
        scratch_shapes=[pltpu.VMEM((D, F2), BF16), pltpu.VMEM((dff, D), BF16)],
    )
    return pl.pallas_call(
        _expert_body,
        grid_spec=grid_spec,
        out_shape=jax.ShapeDtypeStruct(hperm.shape, F32),
        compiler_params=_cparams(1),
        name="experts",
    )(chunk_e, n_used, hperm, w_gu, b_gu.reshape(E, 1, F2), w_dn, b_dn.reshape(E, 1, D))


COMBINE_TILE = 256


def _combine_body(dest_ref, next_ref, x_ref, w_ref, y_hbm, o_ref, buf_ref, sems):
    i = pl.program_id(0)
    slot = i % 2

    def row_copy(s, r, k, d):
        return pltpu.make_async_copy(y_hbm.at[pl.ds(pl.multiple_of(d * ROW_TILES, ROW_TILES), ROW_TILES)],
                                     buf_ref.at[s, k, pl.ds(pl.multiple_of(r * ROW_TILES, ROW_TILES), ROW_TILES)],
                                     sems.at[s])

    def fetch(idx_ref, s):
        def start(r, c):
            for k in range(TOP_K):
                row_copy(s, r, k, idx_ref[0, 0, r * TOP_K + k]).start(priority=k % 2)
            return c

        lax.fori_loop(0, COMBINE_TILE, start, 0, unroll=DMA_UNROLL)

    @pl.when(i == 0)
    def _():
        fetch(dest_ref, slot)

    @pl.when(i + 1 < pl.num_programs(0))
    def _():
        fetch(next_ref, 1 - slot)

    def wait(r, c):
        for k in range(TOP_K):
            row_copy(slot, 0, 0, 0).wait()
        return c

    lax.fori_loop(0, COMBINE_TILE, wait, 0, unroll=DMA_UNROLL)
    gate = [jnp.broadcast_to(w_ref[:, k:k + 1], (COMBINE_TILE, LANES)) for k in range(TOP_K)]
    for s in range(ROW_TILES):
        sl = slice(s * LANES, (s + 1) * LANES)
        out = x_ref[:, sl]
        for k in range(TOP_K):
            out = out + gate[k] * buf_ref[slot, k, pl.ds(s, COMBINE_TILE, stride=ROW_TILES), :]
        o_ref[:, sl] = out


def _combine(x1, gate_w, dest, ys):
    T, D = x1.shape
    tc = COMBINE_TILE
    n = T // tc
    dest3 = dest.reshape(n, 1, tc * TOP_K)
    row = lambda w: pl.BlockSpec((tc, w), lambda i: (i, 0))
    idx = lambda f: pl.BlockSpec((1, 1, tc * TOP_K), f, memory_space=pltpu.SMEM)
    return pl.pallas_call(
        _combine_body,
        grid=(n,),
        in_specs=[idx(lambda i: (i, 0, 0)), idx(lambda i: (jnp.minimum(i + 1, n - 1), 0, 0)),
                  row(D), row(LANES), pl.BlockSpec(memory_space=pl.ANY)],
        out_specs=row(D),
        out_shape=jax.ShapeDtypeStruct((T, D), F32),
        scratch_shapes=[pltpu.VMEM((2, TOP_K, tc * ROW_TILES, LANES), F32), pltpu.SemaphoreType.DMA((2,))],
        compiler_params=_cparams(1),
        name="combine",
    )(dest3, dest3, x1, gate_w, ys)


def _mixer(x2, B, S, g_norm1, w_in, g_q, g_kc, g_ks, g_kw, pe_k, ck_w1, ck_b1, ck_w2, ck_b2,
           pe_v, cv_w1, cv_b1, cv_w2, cv_b2, conv_w, w_pa, w_pb, w_o):
    T, D = x2.shape
    G, H, hd = N_KV_HEADS, N_HEADS, HEAD_DIM
    qa, hk, hv, kas, kaw, vsw, gates, cbv, gab = _inproj(x2, g_norm1, w_in, g_q, g_ks, g_kw, S)
    nh = S // CMP_STRIDE
    kca = _compress(hk.reshape(B, nh, -1), pe_k, ck_w1, ck_b1, ck_w2, ck_b2, g_kc, True)
    vct = _compress(hv.reshape(B, nh, -1), pe_v, cv_w1, cv_b1, cv_w2, cv_b2, jnp.ones((hd,), F32), False)
    vsw5 = vsw.reshape(B, S, 2, G, hd)
    ones_rows = jnp.concatenate([jnp.ones((1,), BF16), jnp.zeros((V_ROWS - hd - 1,), BF16)])

    def key_major(v, tile):
        vt = v.reshape(B, S // tile, tile, G, hd).transpose(0, 3, 1, 4, 2)
        extra = jnp.broadcast_to(ones_rows[None, None, None, :, None], vt.shape[:3] + (V_ROWS - hd, tile))
        return jnp.concatenate([vt, extra], axis=3)

    vst = key_major(vsw5[:, :, 0], KEY_TILE)
    vwt = key_major(vsw5[:, :, 1], Q_TILE)
    gat = gates[:, :3 * H].reshape(B, S, G, 3 * HEADS_PER_GROUP).transpose(0, 2, 3, 1)
    gat = jnp.pad(gat, ((0, 0), (0, 0), (0, 16 - 3 * HEADS_PER_GROUP), (0, 0)))
    o = _attention(qa.reshape(B, S, -1), gat, kca, vct, kas.reshape(B, S, -1), vst, kaw.reshape(B, S, -1),
                   vwt, B, S)
    return _mixer_out(x2, o.reshape(T, H * hd), cbv, gab, conv_w, w_pa, w_pb, w_o, S)


def _moe(x1, g_norm2, w_r, b_r, w_gu, b_gu, w_dn, b_dn):
    T, D = x1.shape
    h2, mi, mf, cnt = _router(x1, g_norm2, w_r, b_r)
    top_e = mi[:, 0:TOP_K]
    rank = mi[:, TOP_K:2 * TOP_K]
    counts = cnt[0, :N_EXPERTS].astype(I32)
    padded = (counts + MOE_CHUNK - 1) // MOE_CHUNK * MOE_CHUNK
    pend = jnp.cumsum(padded)
    poffs = pend - padded
    dest = (poffs[top_e] + rank).reshape(-1)
    n_chunks = (T * TOP_K + MOE_CHUNK - 1) // MOE_CHUNK + N_EXPERTS
    chunk_start = jnp.arange(n_chunks, dtype=I32) * MOE_CHUNK
    chunk_e = jnp.minimum(jnp.sum((pend[None, :] <= chunk_start[:, None]).astype(I32), axis=1), N_EXPERTS - 1)
    n_used = (pend[-1:] // MOE_CHUNK).astype(I32)
    last_chunk = jnp.where(padded > 0, pend - MOE_CHUNK, -1).astype(I32)
    hperm = _dispatch(h2, dest, last_chunk, n_chunks * MOE_CHUNK)
    ys = _experts(hperm, chunk_e, n_used, w_gu, b_gu, w_dn, b_dn)
    return _combine(x1, mf, dest, ys)


def kernel(x, g_norm1, w_in, g_q, g_kc, g_ks, g_kw, pe_k, ck_w1, ck_b1, ck_w2, ck_b2, pe_v, cv_w1, cv_b1,
           cv_w2, cv_b2, conv_w, w_pa, w_pb, w_o, g_norm2, w_r, b_r, w_gu, b_gu, w_dn, b_dn):
    B, S, D = x.shape
    x2 = x.reshape(B * S, D)
    for l in range(g_norm1.shape[0]):
        x2 = _mixer(x2, B, S, g_norm1[l], w_in[l], g_q[l], g_kc[l], g_ks[l], g_kw[l], pe_k[l], ck_w1[l],
                    ck_b1[l], ck_w2[l], ck_b2[l], pe_v[l], cv_w1[l], cv_b1[l], cv_w2[l], cv_b2[l],
                    conv_w[l], w_pa[l], w_pb[l], w_o[l])
        x2 = _moe(x2, g_norm2[l], w_r[l], b_r[l], w_gu[l], b_gu[l], w_dn[l], b_dn[l])
    return x2.reshape(B, S, D)
```

```python
import functools

import numpy as np
import jax
import jax.numpy as jnp
from jax import lax
from jax.experimental import pallas as pl
from jax.experimental.pallas import tpu as pltpu

F32 = jnp.float32
BF16 = jnp.bfloat16
I32 = jnp.int32

N_HEADS = 8
HEAD_DIM = 64
N_KV_HEADS = 2
HEADS_PER_GROUP = N_HEADS // N_KV_HEADS
CMP_LEN = 32
CMP_STRIDE = 16
CMP_HID = 256
SEL_BLK = 64
SEL_TOPN = 16
WINDOW = 512
CONV_WIDTH = 512
CONV_K = 3
N_EXPERTS = 32
TOP_K = 4
SWIGLU_LIMIT = 7.0
SWIGLU_ALPHA = 1.702
MOE_CHUNK = 512
EPS = 1e-6
NEG = -1e30
POS = 1e30
MASK_BIG = 2.0 ** 100

LANES = 128
Q_TILE = 256
KEY_TILE = 512
N_AUG = 5
ROW_TILE = 512
VMEM_LIMIT = 56 * 1024 * 1024
GROUP_LANES = HEADS_PER_GROUP * Q_TILE
WIN_KEYS = WINDOW + Q_TILE
N_TILE_ROWS = 16
CAUSAL_PARTS = 4
V_ROWS = HEAD_DIM + 8


def _cparams(n_axes):
    return pltpu.CompilerParams(dimension_semantics=("arbitrary",) * n_axes,
                                vmem_limit_bytes=VMEM_LIMIT)


def _dot(a, b):
    return jnp.dot(a, b, preferred_element_type=F32)


def _dot_nt(a, b):
    return lax.dot_general(a, b, (((1,), (1,)), ((), ())), preferred_element_type=F32)


ROW_TILES = 8


def _store_row_tiles(ref, val, first=0):
    n = val.shape[0]
    for s in range(ROW_TILES):
        ref[pl.ds(first * ROW_TILES + s, n, stride=ROW_TILES), :] = val[:, s * LANES:(s + 1) * LANES]


def _load_row_tiles(ref, n, first=0):
    return jnp.concatenate([ref[pl.ds(first * ROW_TILES + s, n, stride=ROW_TILES), :] for s in range(ROW_TILES)],
                           axis=1)


def _rms_pairs(v, bd):
    ss = _dot((v * v).astype(BF16), bd)
    return v * lax.rsqrt(ss + EPS)


def _inproj_body(x_ref, g1_ref, wq_ref, wkv_ref, wng_ref, wcv_ref, wmg_ref, gq_ref, gk_ref, bd_ref,
                 qtab_ref, kstab_ref, kwtab_ref,
                 qa_out, hk_out, hv_out, kas_out, kaw_out, vsw_out, gate_out, cbv_out, gab_out, raw_ref):
    x = x_ref[...]
    tm = x.shape[0]
    ms = jnp.mean(x * x, axis=-1, keepdims=True)
    h = (x * lax.rsqrt(ms + EPS) * g1_ref[...]).astype(BF16)
    bd = bd_ref[...]
    low = lax.broadcasted_iota(I32, (tm, LANES), 1) < HEAD_DIM

    def place(pair, tab_ref, out_ref, base, slot):
        for j, src in enumerate((pair, pltpu.roll(pair, HEAD_DIM, 1))):
            o = base + j * slot
            out_ref[:, o:o + LANES] = jnp.where(low, src, tab_ref[:, o:o + LANES].astype(F32)).astype(BF16)

    q = _dot(h, wq_ref[...])
    for c in range(N_HEADS * HEAD_DIM // LANES):
        sl = slice(c * LANES, (c + 1) * LANES)
        place(_rms_pairs(q[:, sl], bd) * gq_ref[:, sl], qtab_ref, qa_out, 2 * c * LANES, LANES)
    kv = _dot(h, wkv_ref[...])

    def emit_half_blocks(c, out_ref):
        nb = tm // CMP_STRIDE
        hw = CMP_STRIDE * HEAD_DIM
        raw_ref[...] = kv[:, c * LANES:(c + 1) * LANES]
        lo = lax.broadcasted_iota(I32, (nb, LANES), 1) < HEAD_DIM
        for u in range(CMP_STRIDE // 2):
            t0 = raw_ref[pl.ds(2 * u, nb, stride=CMP_STRIDE), :]
            t1 = raw_ref[pl.ds(2 * u + 1, nb, stride=CMP_STRIDE), :]
            out_ref[:, u * LANES:(u + 1) * LANES] = jnp.where(lo, t0, pltpu.roll(t1, HEAD_DIM, 1)).astype(BF16)
            out_ref[:, hw + u * LANES:hw + (u + 1) * LANES] = (
                jnp.where(lo, pltpu.roll(t0, HEAD_DIM, 1), t1).astype(BF16))

    emit_half_blocks(0, hk_out)
    emit_half_blocks(1, hv_out)
    place(_rms_pairs(kv[:, 256:384], bd) * gk_ref[:, 0:128], kstab_ref, kas_out, 0, 2 * LANES)
    for g in range(N_KV_HEADS):
        o = (2 * g + 1) * LANES
        kas_out[:, o:o + LANES] = kstab_ref[:, o:o + LANES]
    place(_rms_pairs(kv[:, 512:640], bd) * gk_ref[:, 128:256], kwtab_ref, kaw_out, 0, LANES)
    vsw_out[:, 0:128] = kv[:, 384:512].astype(BF16)
    vsw_out[:, 128:256] = kv[:, 640:768].astype(BF16)
    gate_out[...] = jax.nn.sigmoid(_dot(h, wng_ref[...]))
    cv = _dot(h, wcv_ref[...])
    cw = CONV_WIDTH
    cbv_out[:, 0:cw] = cv[:, 0:cw].astype(BF16)
    cbv_out[:, cw:2 * cw] = (cv[:, cw:2 * cw] * cv[:, 2 * cw:3 * cw]).astype(BF16)
    gab_out[...] = jax.nn.sigmoid(_dot(h, wmg_ref[...])).astype(BF16)


def _key_aug(pos):
    one = np.ones_like(pos, np.float32)
    return np.stack([one, one, (pos // 64 * 64).astype(np.float32), (pos % 64).astype(np.float32), 0 * one],
                    axis=-1)


def _slot_table(aug, slot, extra=None):
    S, n, _ = aug.shape
    tab = np.zeros((S, n, slot), np.float32)
    tab[:, :, HEAD_DIM:HEAD_DIM + N_AUG] = aug
    if extra is not None:
        tab[:, :, LANES:] = extra[:, None, :]
    return jnp.asarray(tab.reshape(S, n * slot), BF16)


def _inproj(x2, g1, w_in, g_q, g_ks, g_kw, S):
    T, D = x2.shape
    H, G = N_HEADS, N_KV_HEADS
    aw = H * HEAD_DIM
    kvw = G * HEAD_DIM
    o = 0
    wq = w_in[:, o:o + aw]; o += aw
    wkv = w_in[:, o:o + 6 * kvw]; o += 6 * kvw
    wng = w_in[:, o:o + 3 * H]; o += 3 * H
    wcv = w_in[:, o:o + 3 * CONV_WIDTH]; o += 3 * CONV_WIDTH
    wmg = w_in[:, o:o + 2 * D]
    wng = jnp.pad(wng, ((0, 0), (0, LANES - 3 * H)))
    wq, wkv, wng, wcv, wmg = (w.astype(BF16) for w in (wq, wkv, wng, wcv, wmg))
    gq = (jnp.tile(g_q, H) * (HEAD_DIM ** -0.5)).reshape(1, aw)
    gk = jnp.concatenate([jnp.tile(g_ks, G), jnp.tile(g_kw, G)]).reshape(1, 2 * kvw)
    idx = np.arange(LANES) // HEAD_DIM
    bd = jnp.asarray((idx[:, None] == idx[None, :]).astype(np.float32) / HEAD_DIM, BF16)
    pos = np.arange(S)
    hi = (pos // 64 * 64).astype(np.float32)[:, None]
    lo = (pos % 64).astype(np.float32)[:, None]
    sl = (2.0 ** (-8.0 * np.arange(1, H + 1) / H)).astype(np.float32)[None, :]
    aq = np.stack([-sl * hi, -sl * lo, np.broadcast_to(sl, (S, H)), np.broadcast_to(sl, (S, H)),
                   np.ones((S, H), np.float32)], axis=-1)
    ak = np.broadcast_to(_key_aug(pos)[:, None, :], (S, G, N_AUG))
    onehot = (pos[:, None] // SEL_BLK == np.arange(LANES)[None, :]).astype(np.float32)
    qtab = _slot_table(aq, LANES)
    kstab = _slot_table(ak, 2 * LANES, onehot)
    kwtab = _slot_table(ak, LANES)
    tm = ROW_TILE
    nst = S // tm
    row = lambda w: pl.BlockSpec((tm, w), lambda i: (i, 0))
    full = lambda a: pl.BlockSpec(a.shape, lambda i: (0,) * a.ndim)
    tab = lambda a: pl.BlockSpec((tm, a.shape[1]), lambda i: (i % nst, 0))
    ins = (x2, g1.reshape(1, D), wq, wkv, wng, wcv, wmg, gq, gk, bd, qtab, kstab, kwtab)
    widths = (H * LANES, G * 2 * LANES, G * LANES, 2 * kvw, LANES, 2 * CONV_WIDTH, 2 * D)
    dtypes = (BF16, BF16, BF16, BF16, F32, BF16, BF16)
    hw = G * CMP_STRIDE * HEAD_DIM
    nb = tm // CMP_STRIDE
    half = pl.BlockSpec((nb, hw), lambda i: (i, 0))
    half_shape = jax.ShapeDtypeStruct((T // CMP_STRIDE, hw), BF16)
    rows = [(row(w), jax.ShapeDtypeStruct((T, w), dt)) for w, dt in zip(widths, dtypes)]
    outs = [rows[0], (half, half_shape), (half, half_shape)] + rows[1:]
    return pl.pallas_call(
        _inproj_body,
        grid=(T // tm,),
        in_specs=[row(D)] + [full(a) for a in ins[1:10]] + [tab(a) for a in ins[10:]],
        out_specs=[o[0] for o in outs],
        out_shape=[o[1] for o in outs],
        scratch_shapes=[pltpu.VMEM((tm, LANES), F32)],
        compiler_params=_cparams(1),
        name="inproj",
    )(*ins)


def _compress_body(h_ref, w1_ref, pe_ref, b1_ref, w2_ref, b2_ref, g_ref, tab_ref, o_ref, *, for_keys):
    hb = h_ref[0]
    nc = hb.shape[0]
    a = _dot(hb, w1_ref[0])
    b = _dot(hb, w1_ref[1])
    c = _dot(pe_ref[0], w1_ref[0]) + _dot(pe_ref[1], w1_ref[1])
    pre = a + pltpu.roll(b, nc - 1, 0) + c[0:1, :] + b1_ref[...]
    hid = jax.nn.gelu(pre)
    out = _dot(hid.astype(BF16), w2_ref[...]) + b2_ref[...]
    if for_keys:
        ms = jnp.sum(out * out, axis=-1, keepdims=True) * (1.0 / HEAD_DIM)
        out = out * lax.rsqrt(ms + EPS) * g_ref[...]
        low = lax.broadcasted_iota(I32, out.shape, 1) < HEAD_DIM
        o_ref[0, 0] = jnp.where(low, out, tab_ref[...]).astype(BF16)
    else:
        o_ref[0, 0] = out.T[0:HEAD_DIM, :].astype(BF16)


def _compress(hh, pe, w1, b1, w2, b2, gain, for_keys):
    B, NC, _ = hh.shape
    G, HW = N_KV_HEADS, CMP_STRIDE * HEAD_DIM
    w1s = w1.reshape(2, HW, CMP_HID).astype(BF16)
    pes = jnp.broadcast_to(pe.reshape(2, 1, HW), (2, 8, HW)).astype(BF16)
    padl = lambda a: jnp.pad(a, ((0, 0), (0, LANES - HEAD_DIM)))
    tabn = np.zeros((NC, LANES), np.float32)
    tabn[:, HEAD_DIM:HEAD_DIM + N_AUG] = _key_aug(np.arange(NC) * CMP_STRIDE + (CMP_LEN - 1))
    tab = jnp.asarray(tabn)
    full = lambda a: pl.BlockSpec(a.shape, lambda b, g: (0,) * a.ndim)
    ins = (hh, w1s, pes, b1.reshape(1, CMP_HID), padl(w2).astype(BF16), padl(b2.reshape(1, HEAD_DIM)),
           padl(gain.reshape(1, HEAD_DIM)), tab)
    oshape = (B, G, NC, LANES) if for_keys else (B, G, HEAD_DIM, NC)
    return pl.pallas_call(
        functools.partial(_compress_body, for_keys=for_keys),
        grid=(B, G),
        in_specs=[pl.BlockSpec((1, NC, HW), lambda b, g: (b, 0, g))] + [full(a) for a in ins[1:]],
        out_specs=pl.BlockSpec((1, 1) + oshape[2:], lambda b, g: (b, g, 0, 0)),
        out_shape=jax.ShapeDtypeStruct(oshape, BF16),
        compiler_params=_cparams(2),
        name="compress_keys" if for_keys else "compress_values",
    )(*ins)


def _attn_body(qa_ref, g_ref, kca_ref, vct_ref, kas_ref, vst_ref, kaw_ref, vwt_ref, selmapt_ref, wbias_ref,
               tilemap_ref, cbias_ref, o_ref, qaug_ref, m_ref, acc_ref, s0_ref, s1_ref, oc_ref, flagv_ref,
               flags_ref, list_ref, sem,
               *, n_sel):
    i = pl.program_id(2)
    q0 = i * Q_TILE
    gl = GROUP_LANES
    for h in range(HEADS_PER_GROUP):
        qaug_ref[h * Q_TILE:(h + 1) * Q_TILE, 0:LANES] = qa_ref[0, :, h * LANES:(h + 1) * LANES]
    qa = qaug_ref[:, 0:LANES]

    nc = kca_ref.shape[2]

    def compressed_and_select(n_ent, n_blk):
        s = _dot_nt(kca_ref[0, 0, 0:n_ent, :], qa)
        cb = cbias_ref[pl.ds(pl.multiple_of(nc - i * (Q_TILE // CMP_STRIDE), 8), n_ent), :]
        s = s + jnp.concatenate([cb] * HEADS_PER_GROUP, axis=1)
        m = jnp.max(s, axis=0, keepdims=True)
        p = jnp.exp(s - m)
        l = jnp.sum(p, axis=0, keepdims=True)
        has_entry = (q0 + (lax.broadcasted_iota(I32, (1, gl), 1) & (Q_TILE - 1))) >= CMP_LEN - 1
        pc = p * jnp.where(has_entry, 1.0 / l, 0.0)
        oc_ref[...] = _dot(vct_ref[0, 0, :, 0:n_ent], pc.astype(BF16))

        ps = pc[:, 0:Q_TILE]
        for h in range(1, HEADS_PER_GROUP):
            ps = ps + pc[:, h * Q_TILE:(h + 1) * Q_TILE]
        ps_hi = ps.astype(BF16)
        ps_lo = (ps - ps_hi.astype(F32)).astype(BF16)
        smap = selmapt_ref[0:n_blk, 0:n_ent]
        imp = _dot(smap, ps_hi) + _dot(smap, ps_lo)
        jb = lax.broadcasted_iota(I32, (n_blk, Q_TILE), 0)
        cur = (q0 + lax.broadcasted_iota(I32, (n_blk, Q_TILE), 1)) // SEL_BLK
        forced = (jb == 0) | (jb == cur) | (jb == cur - 1)
        score = jnp.where(jb > cur, NEG, jnp.where(forced, POS, imp))
        jbf = jb.astype(F32)
        for _ in range(n_sel):
            mx = jnp.max(score, axis=0, keepdims=True)
            first = jnp.min(jnp.where(score == mx, jbf, float(LANES)), axis=0, keepdims=True)
            score = jnp.where(jbf == first, -jnp.inf, score)
        picked = jnp.where(score == -jnp.inf, 1.0, 0.0)
        if n_blk < LANES:
            picked = jnp.concatenate([picked, jnp.zeros((LANES - n_blk, Q_TILE), F32)], axis=0)
        selbias = jnp.where(picked > 0.0, 0.0, -MASK_BIG).T.astype(BF16)
        for h in range(HEADS_PER_GROUP):
            qaug_ref[h * Q_TILE:(h + 1) * Q_TILE, LANES:2 * LANES] = selbias
        tile_hits = jnp.max(_dot(tilemap_ref[...], picked.astype(BF16)), axis=1, keepdims=True)
        flagv_ref[...] = jnp.broadcast_to(tile_hits, flagv_ref.shape).astype(I32)

    if nc % (CAUSAL_PARTS * LANES) == 0 and LANES % CAUSAL_PARTS == 0:
        ent_step, blk_step = nc // CAUSAL_PARTS, LANES // CAUSAL_PARTS
        need_ent = (q0 + Q_TILE) // CMP_STRIDE
        for part in range(1, CAUSAL_PARTS + 1):
            @pl.when((need_ent + ent_step - 1) // ent_step == part)
            def _():
                compressed_and_select(part * ent_step, part * blk_step)
    else:
        compressed_and_select(nc, LANES)
    o_c = oc_ref[...]
    flag_copy = pltpu.make_async_copy(flagv_ref, flags_ref, sem)
    flag_copy.start()

    w0 = pl.multiple_of(jnp.maximum(q0 - WINDOW, 0), Q_TILE)
    sw = _dot_nt(kaw_ref[0, pl.ds(w0, WIN_KEYS), :], qa)
    wb = wbias_ref[jnp.minimum(i, WINDOW // Q_TILE)]
    sw = sw + jnp.concatenate([wb] * HEADS_PER_GROUP, axis=1)
    mw = jnp.max(sw, axis=0, keepdims=True)
    pw = jnp.exp(sw - mw)
    c0 = w0 // Q_TILE
    vw = jnp.concatenate([vwt_ref[0, 0, c0 + j] for j in range(WIN_KEYS // Q_TILE)], axis=1)
    aw = _dot(vw, pw.astype(BF16))
    o_w = aw[0:HEAD_DIM] * (1.0 / aw[HEAD_DIM:HEAD_DIM + 1])

    qaug = qaug_ref[...]
    n_full = q0 // KEY_TILE
    kd = pl.multiple_of(n_full * KEY_TILE, KEY_TILE)
    sd = _dot_nt(kas_ref[0, pl.ds(kd, KEY_TILE), :], qaug)
    key = kd + lax.broadcasted_iota(I32, (KEY_TILE, gl), 0)
    qry = q0 + (lax.broadcasted_iota(I32, (KEY_TILE, gl), 1) & (Q_TILE - 1))
    s0_ref[...] = jnp.where(key <= qry, sd, NEG)
    m_ref[...] = jnp.full(m_ref.shape, -3.0e38, F32)
    acc_ref[...] = jnp.zeros(acc_ref.shape, F32)
    list_ref[0] = n_full

    def scores(kt):
        return _dot_nt(kas_ref[0, pl.ds(pl.multiple_of(kt * KEY_TILE, KEY_TILE), KEY_TILE), :], qaug)

    def absorb(s_ref, kt, live):
        sc = s_ref[...]
        m_old = m_ref[...]
        m_new = jnp.where(live, jnp.maximum(m_old, jnp.max(sc, axis=0, keepdims=True)), m_old)
        pv = _dot(vst_ref[0, 0, kt], jnp.exp(sc - m_new).astype(BF16))
        acc_ref[...] = jnp.exp(m_old - m_new) * acc_ref[...] + jnp.where(live, pv, 0.0)
        m_ref[...] = m_new

    flag_copy.wait()

    def compact(kt, n):
        active = flags_ref[kt, 0] > 0

        @pl.when(active)
        def _():
            list_ref[n] = kt

        return n + active.astype(I32)

    n_items = lax.fori_loop(0, n_full, compact, 1)

    last = n_items - 1

    def pair_body(j, carry):
        a = list_ref[2 * j]
        b = list_ref[jnp.minimum(2 * j + 1, last)]
        nxt = list_ref[jnp.minimum(2 * j + 2, last)]
        s1_ref[...] = scores(b)
        absorb(s0_ref, a, True)
        s0_ref[...] = scores(nxt)
        absorb(s1_ref, b, 2 * j + 1 <= last)
        return carry

    lax.fori_loop(0, (n_items + 1) // 2, pair_body, 0)

    o_s = acc_ref[0:HEAD_DIM, :] * (1.0 / acc_ref[HEAD_DIM:HEAD_DIM + 1, :])

    g = g_ref[0, 0]
    outs = []
    for h in range(HEADS_PER_GROUP):
        sl = slice(h * Q_TILE, (h + 1) * Q_TILE)
        outs.append(g[3 * h:3 * h + 1, :] * o_c[:, sl] + g[3 * h + 1:3 * h + 2, :] * o_s[:, sl]
                    + g[3 * h + 2:3 * h + 3, :] * o_w[:, sl])
    o_ref[0] = jnp.concatenate(outs, axis=0).T.astype(BF16)


def _attention(qa, gates_t, kca, vct, kas, vst, kaw, vwt, B, S):
    G, hd = N_KV_HEADS, HEAD_DIM
    NC = kca.shape[2]
    n_blk = S // SEL_BLK
    assert n_blk <= LANES and S % KEY_TILE == 0 and S >= WIN_KEYS
    n_sel = min(SEL_TOPN, n_blk)
    ratio, span = SEL_BLK // CMP_STRIDE, CMP_LEN // CMP_STRIDE
    sm = np.zeros((LANES, NC), np.float32)
    for j in range(n_blk):
        for a in range(ratio):
            for b in range(span):
                n = ratio * j + a - b
                if 0 <= n < NC - 1:
                    sm[j, n] += 1.0
    selmapt = jnp.asarray(sm, BF16)
    c = np.arange(WIN_KEYS)[:, None]
    r = np.arange(Q_TILE)[None, :]
    offs = np.arange(WINDOW // Q_TILE + 1)[:, None, None] * Q_TILE
    wbias = jnp.asarray(np.where((c - r <= offs) & (c - r > offs - WINDOW), 0.0, NEG), F32)
    tilemap = jnp.asarray(np.arange(LANES)[None, :] // (KEY_TILE // SEL_BLK) == np.arange(N_TILE_ROWS)[:, None],
                          BF16)
    assert S // KEY_TILE <= N_TILE_ROWS
    u = np.arange(2 * NC)[:, None] - NC
    cbias = jnp.asarray(np.where(CMP_STRIDE * u + (CMP_LEN - 1) <= np.arange(Q_TILE)[None, :], 0.0, NEG), F32)

    hpg = HEADS_PER_GROUP
    grp = lambda *blk: pl.BlockSpec((1, 1) + blk, lambda b, g, i: (b, g) + (0,) * len(blk))
    seq = lambda w: pl.BlockSpec((1, S, w), lambda b, g, i: (b, 0, g))
    const = lambda a: pl.BlockSpec(a.shape, lambda b, g, i: (0,) * a.ndim)
    return pl.pallas_call(
        functools.partial(_attn_body, n_sel=n_sel),
        grid=(B, G, S // Q_TILE),
        in_specs=[pl.BlockSpec((1, Q_TILE, hpg * LANES), lambda b, g, i: (b, i, g)),
                  pl.BlockSpec((1, 1, 16, Q_TILE), lambda b, g, i: (b, g, 0, i)),
                  grp(NC, LANES), grp(hd, NC), seq(2 * LANES), grp(S // KEY_TILE, V_ROWS, KEY_TILE),
                  seq(LANES), grp(S // Q_TILE, V_ROWS, Q_TILE), const(selmapt), const(wbias), const(tilemap),
                  const(cbias)],
        out_specs=pl.BlockSpec((1, Q_TILE, hpg * hd), lambda b, g, i: (b, i, g)),
        out_shape=jax.ShapeDtypeStruct((B, S, N_HEADS * hd), BF16),
        scratch_shapes=[pltpu.VMEM((GROUP_LANES, 2 * LANES), BF16), pltpu.VMEM((1, GROUP_LANES), F32),
                        pltpu.VMEM((V_ROWS, GROUP_LANES), F32), pltpu.VMEM((KEY_TILE, GROUP_LANES), F32),
                        pltpu.VMEM((KEY_TILE, GROUP_LANES), F32), pltpu.VMEM((hd, GROUP_LANES), F32),
                        pltpu.VMEM((N_TILE_ROWS, LANES), I32), pltpu.SMEM((N_TILE_ROWS, LANES), I32),
                        pltpu.SMEM((N_TILE_ROWS,), I32), pltpu.SemaphoreType.DMA(())],
        compiler_params=_cparams(3),
        name="nsa_attention",
    )(qa, gates_t, kca, vct, kas, vst, kaw, vwt, selmapt, wbias, tilemap, cbias)


def _mixer_out_body(x_ref, oa_ref, cbv_ref, halo_ref, gab_ref, cw_ref, wpa_ref, wpb_ref, wo_ref, o_ref,
                    *, seq_len):
    i = pl.program_id(0)
    tm = x_ref.shape[0]
    cwd = CONV_WIDTH
    d = x_ref.shape[1]
    v = cbv_ref[:, cwd:2 * cwd].astype(F32)
    prev = halo_ref[:, cwd:2 * cwd].astype(F32)
    keep = ((i * tm) % seq_len != 0).astype(F32)
    p1 = prev[7:8, :] * keep
    p2 = prev[6:7, :] * keep
    ridx = lax.broadcasted_iota(I32, (tm, cwd), 0)
    v1 = jnp.where(ridx == 0, p1, pltpu.roll(v, 1, 0))
    v2 = jnp.where(ridx == 0, p2, jnp.where(ridx == 1, p1, pltpu.roll(v, 2, 0)))
    y = cw_ref[0:1, :] * v2 + cw_ref[1:2, :] * v1 + cw_ref[2:3, :] * v
    yb_in = (cbv_ref[:, 0:cwd].astype(F32) * y).astype(BF16)
    y_a = _dot(oa_ref[...], wpa_ref[...])
    y_b = _dot(yb_in, wpb_ref[...])
    merged = gab_ref[:, 0:d].astype(F32) * y_a + gab_ref[:, d:2 * d].astype(F32) * y_b
    o_ref[...] = x_ref[...] + _dot(merged.astype(BF16), wo_ref[...])


def _mixer_out(x2, oa, cbv, gab, conv_w, w_pa, w_pb, w_o, seq_len):
    T, D = x2.shape
    tm = ROW_TILE
    cw8 = jnp.pad(conv_w, ((0, 8 - CONV_K), (0, 0)))
    row = lambda w: pl.BlockSpec((tm, w), lambda i: (i, 0))
    full = lambda a: pl.BlockSpec(a.shape, lambda i: (0,) * a.ndim)
    halo = pl.BlockSpec((8, cbv.shape[1]), lambda i: (jnp.maximum(i * (tm // 8) - 1, 0), 0))
    wts = (cw8, w_pa.astype(BF16), w_pb.astype(BF16), w_o.astype(BF16))
    return pl.pallas_call(
        functools.partial(_mixer_out_body, seq_len=seq_len),
        grid=(T // tm,),
        in_specs=[row(D), row(oa.shape[1]), row(cbv.shape[1]), halo, row(gab.shape[1])] + [full(a) for a in wts],
        out_specs=row(D),
        out_shape=jax.ShapeDtypeStruct((T, D), F32),
        compiler_params=_cparams(1),
        name="mixer_out",
    )(x2, oa, cbv, cbv, gab, *wts)


def _router_body(x_ref, g2_ref, whi_ref, wlo_ref, br_ref, tri_ref, h_out, mi_out, mf_out, cnt_out):
    i = pl.program_id(0)

    @pl.when(i == 0)
    def _():
        cnt_out[...] = jnp.zeros(cnt_out.shape, F32)

    x = x_ref[...]
    tm = x.shape[0]
    ms = jnp.mean(x * x, axis=-1, keepdims=True)
    h = x * lax.rsqrt(ms + EPS) * g2_ref[...]
    _store_row_tiles(h_out, h)
    h_hi = h.astype(BF16)
    h_lo = (h - h_hi.astype(F32)).astype(BF16)
    logits = (_dot(h_hi, whi_ref[...]) + _dot(h_lo, whi_ref[...]) + _dot(h_hi, wlo_ref[...])) + br_ref[...]
    lane = lax.broadcasted_iota(I32, (tm, LANES), 1)
    lanef = lane.astype(F32)
    work = jnp.where(lane < N_EXPERTS, logits, -jnp.inf)
    vals, hits = [], []
    for _ in range(TOP_K):
        mx = jnp.max(work, axis=-1, keepdims=True)
        first = jnp.min(jnp.where(work == mx, lanef, float(LANES)), axis=-1, keepdims=True)
        hit = lanef == first
        vals.append(mx)
        hits.append(hit)
        work = jnp.where(hit, -jnp.inf, work)
    ex = [jnp.exp(v - vals[0]) for v in vals]
    den = ex[0]
    for e in ex[1:]:
        den = den + e
    cnt = jnp.zeros((tm, LANES), F32)
    for hit in hits:
        cnt = cnt + hit.astype(F32)
    before = _dot(tri_ref[...], cnt.astype(BF16)) + cnt_out[0:1, :]
    mi = jnp.zeros((tm, LANES), F32)
    mf = jnp.zeros((tm, LANES), F32)
    for k, hit in enumerate(hits):
        e_k = jnp.sum(jnp.where(hit, lanef, 0.0), axis=-1, keepdims=True)
        r_k = jnp.sum(jnp.where(hit, before, 0.0), axis=-1, keepdims=True)
        mi = jnp.where(lane == k, e_k, jnp.where(lane == TOP_K + k, r_k, mi))
        mf = jnp.where(lane == k, ex[k] / den, mf)
    mi_out[...] = mi[:, 0:2 * TOP_K].astype(I32)
    mf_out[...] = mf
    cnt_out[...] = cnt_out[...] + jnp.sum(cnt, axis=0, keepdims=True)


def _router(x1, g2, w_r, b_r):
    T, D = x1.shape
    assert D == ROW_TILES * LANES
    tm = ROW_TILE
    wpad = jnp.pad(w_r, ((0, 0), (0, LANES - N_EXPERTS)))
    whi = wpad.astype(BF16)
    wlo = (wpad - whi.astype(F32)).astype(BF16)
    br = jnp.pad(b_r, (0, LANES - N_EXPERTS)).reshape(1, LANES)
    tri = jnp.asarray(np.tril(np.ones((tm, tm), np.float32), -1), BF16)
    row = lambda w: pl.BlockSpec((tm, w), lambda i: (i, 0))
    full = lambda a: pl.BlockSpec(a.shape, lambda i: (0,) * a.ndim)
    ins = (x1, g2.reshape(1, D), whi, wlo, br, tri)
    return pl.pallas_call(
        _router_body,
        grid=(T // tm,),
        in_specs=[row(D)] + [full(a) for a in ins[1:]],
        out_specs=[pl.BlockSpec((tm * ROW_TILES, LANES), lambda i: (i, 0)), row(2 * TOP_K), row(LANES),
                   pl.BlockSpec((8, LANES), lambda i: (0, 0))],
        out_shape=[jax.ShapeDtypeStruct((T * ROW_TILES, LANES), F32), jax.ShapeDtypeStruct((T, 2 * TOP_K), I32),
                   jax.ShapeDtypeStruct((T, LANES), F32), jax.ShapeDtypeStruct((8, LANES), F32)],
        compiler_params=_cparams(1),
        name="router",
    )(*ins)


DISPATCH_TILE = 512
DMA_UNROLL = 8


def _dispatch_body(dest_ref, last_ref, h_ref, o_hbm, zero_ref, sem, zsem):
    @pl.when(pl.program_id(0) == 0)
    def _():
        zero_ref[...] = jnp.zeros(zero_ref.shape, zero_ref.dtype)

        def clear(e):
            start = pl.multiple_of(last_ref[e] * ROW_TILES, MOE_CHUNK * ROW_TILES)
            return pltpu.make_async_copy(zero_ref, o_hbm.at[pl.ds(start, MOE_CHUNK * ROW_TILES)], zsem)

        for e in range(N_EXPERTS):
            @pl.when(last_ref[e] >= 0)
            def _():
                clear(e).start()
        for e in range(N_EXPERTS):
            @pl.when(last_ref[e] >= 0)
            def _():
                clear(e).wait()

    def row_copy(r, d):
        return pltpu.make_async_copy(h_ref.at[pl.ds(pl.multiple_of(r * ROW_TILES, ROW_TILES), ROW_TILES)],
                                     o_hbm.at[pl.ds(pl.multiple_of(d * ROW_TILES, ROW_TILES), ROW_TILES)], sem)

    def start(r, c):
        for k in range(TOP_K):
            row_copy(r, dest_ref[0, 0, r * TOP_K + k]).start(priority=k % 2)
        return c

    def wait(r, c):
        for k in range(TOP_K):
            row_copy(0, 0).wait()
        return c

    lax.fori_loop(0, DISPATCH_TILE, start, 0, unroll=DMA_UNROLL)
    lax.fori_loop(0, DISPATCH_TILE, wait, 0, unroll=DMA_UNROLL)


def _dispatch(h2, dest, last_chunk, n_rows):
    T = h2.shape[0] // ROW_TILES
    td = DISPATCH_TILE
    dest3 = dest.reshape(T // td, 1, td * TOP_K)
    return pl.pallas_call(
        _dispatch_body,
        grid=(T // td,),
        in_specs=[pl.BlockSpec((1, 1, td * TOP_K), lambda i: (i, 0, 0), memory_space=pltpu.SMEM),
                  pl.BlockSpec(memory_space=pltpu.SMEM),
                  pl.BlockSpec((td * ROW_TILES, LANES), lambda i: (i, 0))],
        out_specs=pl.BlockSpec(memory_space=pl.ANY),
        out_shape=jax.ShapeDtypeStruct((n_rows * ROW_TILES, LANES), h2.dtype),
        scratch_shapes=[pltpu.VMEM((MOE_CHUNK * ROW_TILES, LANES), h2.dtype), pltpu.SemaphoreType.DMA(()),
                        pltpu.SemaphoreType.DMA(())],
        compiler_params=_cparams(1),
        name="dispatch",
    )(dest3, last_chunk, h2)


def _expert_body(ce_ref, nu_ref, x_ref, wgu_ref, bgu_ref, wdn_ref, bdn_ref, o_ref, wgu_bf, wdn_bf):
    c = pl.program_id(0)
    dff = wdn_ref.shape[1]

    @pl.when((c == 0) | (ce_ref[c] != ce_ref[jnp.maximum(c - 1, 0)]))
    def _():
        wgu_bf[...] = wgu_ref[0].astype(BF16)
        wdn_bf[...] = wdn_ref[0].astype(BF16)

    @pl.when(c < nu_ref[0])
    def _():
        x = _load_row_tiles(x_ref, MOE_CHUNK)
        gu = _dot(x.astype(BF16), wgu_bf[...]) + bgu_ref[0]
        g = jnp.minimum(gu[:, 0:dff], SWIGLU_LIMIT)
        u = jnp.clip(gu[:, dff:2 * dff], -SWIGLU_LIMIT, SWIGLU_LIMIT)
        act = (u + 1.0) * (g * jax.nn.sigmoid(SWIGLU_ALPHA * g))
        _store_row_tiles(o_ref, _dot(act.astype(BF16), wdn_bf[...]) + bdn_ref[0])

    @pl.when(c >= nu_ref[0])
    def _():
        o_ref[...] = jnp.zeros(o_ref.shape, F32)


def _experts(hperm, chunk_e, n_used, w_gu, b_gu, w_dn, b_dn):
    E, D, F2 = w_gu.shape
    assert D == ROW_TILES * LANES
    P = hperm.shape[0] // ROW_TILES
    dff = F2 // 2
    n_chunks = P // MOE_CHUNK
    chunk = (MOE_CHUNK * ROW_TILES, LANES)
    grid_spec = pltpu.PrefetchScalarGridSpec(
        num_scalar_prefetch=2,
        grid=(n_chunks,),
        in_specs=[pl.BlockSpec(chunk, lambda c, ce, nu: (jnp.minimum(c, nu[0] - 1), 0)),
                  pl.BlockSpec((1, D, F2), lambda c, ce, nu: (ce[c], 0, 0)),
                  pl.BlockSpec((1, 1, F2), lambda c, ce, nu: (ce[c], 0, 0)),
                  pl.BlockSpec((1, dff, D), lambda c, ce, nu: (ce[c], 0, 0)),
                  pl.BlockSpec((1, 1, D), lambda c, ce, nu: (ce[c], 0, 0))],
        out_specs=pl.BlockSpec(chunk, lambda c, ce, nu: (c, 0)),
        scratch_shapes=[pltpu.VMEM((D, F2), BF16), pltpu.VMEM((dff, D), BF16)],
    )
    return pl.pallas_call(
        _expert_body,
        grid_spec=grid_spec,
        out_shape=jax.ShapeDtypeStruct(hperm.shape, F32),
        compiler_params=_cparams(1),
        name="experts",
    )(chunk_e, n_used, hperm, w_gu, b_gu.reshape(E, 1, F2), w_dn, b_dn.reshape(E, 1, D))


COMBINE_TILE = 256


def _combine_body(dest_ref, next_ref, x_ref, w_ref, y_hbm, o_ref, buf_ref, sems):
    i = pl.program_id(0)
    slot = i % 2

    def row_copy(s, r, k, d):
        return pltpu.make_async_copy(y_hbm.at[pl.ds(pl.multiple_of(d * ROW_TILES, ROW_TILES), ROW_TILES)],
                                     buf_ref.at[s, k, pl.ds(pl.multiple_of(r * ROW_TILES, ROW_TILES), ROW_TILES)],
                                     sems.at[s])

    def fetch(idx_ref, s):
        def start(r, c):
            for k in range(TOP_K):
                row_copy(s, r, k, idx_ref[0, 0, r * TOP_K + k]).start(priority=k % 2)
            return c

        lax.fori_loop(0, COMBINE_TILE, start, 0, unroll=DMA_UNROLL)

    @pl.when(i == 0)
    def _():
        fetch(dest_ref, slot)

    @pl.when(i + 1 < pl.num_programs(0))
    def _():
        fetch(next_ref, 1 - slot)

    def wait(r, c):
        for k in range(TOP_K):
            row_copy(slot, 0, 0, 0).wait()
        return c

    lax.fori_loop(0, COMBINE_TILE, wait, 0, unroll=DMA_UNROLL)
    gate = [jnp.broadcast_to(w_ref[:, k:k + 1], (COMBINE_TILE, LANES)) for k in range(TOP_K)]
    for s in range(ROW_TILES):
        sl = slice(s * LANES, (s + 1) * LANES)
        out = x_ref[:, sl]
        for k in range(TOP_K):
            out = out + gate[k] * buf_ref[slot, k, pl.ds(s, COMBINE_TILE, stride=ROW_TILES), :]
        o_ref[:, sl] = out


def _combine(x1, gate_w, dest, ys):
    T, D = x1.shape
    tc = COMBINE_TILE
    n = T // tc
    dest3 = dest.reshape(n, 1, tc * TOP_K)
    row = lambda w: pl.BlockSpec((tc, w), lambda i: (i, 0))
    idx = lambda f: pl.BlockSpec((1, 1, tc * TOP_K), f, memory_space=pltpu.SMEM)
    return pl.pallas_call(
        _combine_body,
        grid=(n,),
        in_specs=[idx(lambda i: (i, 0, 0)), idx(lambda i: (jnp.minimum(i + 1, n - 1), 0, 0)),
                  row(D), row(LANES), pl.BlockSpec(memory_space=pl.ANY)],
        out_specs=row(D),
        out_shape=jax.ShapeDtypeStruct((T, D), F32),
        scratch_shapes=[pltpu.VMEM((2, TOP_K, tc * ROW_TILES, LANES), F32), pltpu.SemaphoreType.DMA((2,))],
        compiler_params=_cparams(1),
        name="combine",
    )(dest3, dest3, x1, gate_w, ys)


def _mixer(x2, B, S, g_norm1, w_in, g_q, g_kc, g_ks, g_kw, pe_k, ck_w1, ck_b1, ck_w2, ck_b2,
           pe_v, cv_w1, cv_b1, cv_w2, cv_b2, conv_w, w_pa, w_pb, w_o):
    T, D = x2.shape
    G, H, hd = N_KV_HEADS, N_HEADS, HEAD_DIM
    qa, hk, hv, kas, kaw, vsw, gates, cbv, gab = _inproj(x2, g_norm1, w_in, g_q, g_ks, g_kw, S)
    nh = S // CMP_STRIDE
    kca = _compress(hk.reshape(B, nh, -1), pe_k, ck_w1, ck_b1, ck_w2, ck_b2, g_kc, True)
    vct = _compress(hv.reshape(B, nh, -1), pe_v, cv_w1, cv_b1, cv_w2, cv_b2, jnp.ones((hd,), F32), False)
    vsw5 = vsw.reshape(B, S, 2, G, hd)
    ones_rows = jnp.concatenate([jnp.ones((1,), BF16), jnp.zeros((V_ROWS - hd - 1,), BF16)])

    def key_major(v, tile):
        vt = v.reshape(B, S // tile, tile, G, hd).transpose(0, 3, 1, 4, 2)
        extra = jnp.broadcast_to(ones_rows[None, None, None, :, None], vt.shape[:3] + (V_ROWS - hd, tile))
        return jnp.concatenate([vt, extra], axis=3)

    vst = key_major(vsw5[:, :, 0], KEY_TILE)
    vwt = key_major(vsw5[:, :, 1], Q_TILE)
    gat = gates[:, :3 * H].reshape(B, S, G, 3 * HEADS_PER_GROUP).transpose(0, 2, 3, 1)
    gat = jnp.pad(gat, ((0, 0), (0, 0), (0, 16 - 3 * HEADS_PER_GROUP), (0, 0)))
    o = _attention(qa.reshape(B, S, -1), gat, kca, vct, kas.reshape(B, S, -1), vst, kaw.reshape(B, S, -1),
                   vwt, B, S)
    return _mixer_out(x2, o.reshape(T, H * hd), cbv, gab, conv_w, w_pa, w_pb, w_o, S)


def _moe(x1, g_norm2, w_r, b_r, w_gu, b_gu, w_dn, b_dn):
    T, D = x1.shape
    h2, mi, mf, cnt = _router(x1, g_norm2, w_r, b_r)
    top_e = mi[:, 0:TOP_K]
    rank = mi[:, TOP_K:2 * TOP_K]
    counts = cnt[0, :N_EXPERTS].astype(I32)
    padded = (counts + MOE_CHUNK - 1) // MOE_CHUNK * MOE_CHUNK
    pend = jnp.cumsum(padded)
    poffs = pend - padded
    dest = (poffs[top_e] + rank).reshape(-1)
    n_chunks = (T * TOP_K + MOE_CHUNK - 1) // MOE_CHUNK + N_EXPERTS
    chunk_start = jnp.arange(n_chunks, dtype=I32) * MOE_CHUNK
    chunk_e = jnp.minimum(jnp.sum((pend[None, :] <= chunk_start[:, None]).astype(I32), axis=1), N_EXPERTS - 1)
    n_used = (pend[-1:] // MOE_CHUNK).astype(I32)
    last_chunk = jnp.where(padded > 0, pend - MOE_CHUNK, -1).astype(I32)
    hperm = _dispatch(h2, dest, last_chunk, n_chunks * MOE_CHUNK)
    ys = _experts(hperm, chunk_e, n_used, w_gu, b_gu, w_dn, b_dn)
    return _combine(x1, mf, dest, ys)


def kernel(x, g_norm1, w_in, g_q, g_kc, g_ks, g_kw, pe_k, ck_w1, ck_b1, ck_w2, ck_b2, pe_v, cv_w1, cv_b1,
           cv_w2, cv_b2, conv_w, w_pa, w_pb, w_o, g_norm2, w_r, b_r, w_gu, b_gu, w_dn, b_dn):
    B, S, D = x.shape
    x2 = x.reshape(B * S, D)
    for l in range(g_norm1.shape[0]):
        x2 = _mixer(x2, B, S, g_norm1[l], w_in[l], g_q[l], g_kc[l], g_ks[l], g_kw[l], pe_k[l], ck_w1[l],
                    ck_b1[l], ck_w2[l], ck_b2[l], pe_v[l], cv_w1[l], cv_b1[l], cv_w2[l], cv_b2[l],
                    conv_w[l], w_pa[l], w_pb[l], w_o[l])
        x2 = _moe(x2, g_norm2[l], w_r[l], b_r[l], w_gu[l], b_gu[l], w_dn[l], b_dn[l])
    return x2.reshape(B, S, D)
```

```python
import functools

import numpy as np
import jax
import jax.numpy as jnp
from jax import lax
from jax.experimental import pallas as pl
from jax.experimental.pallas import tpu as pltpu

F32 = jnp.float32
BF16 = jnp.bfloat16
I32 = jnp.int32

N_HEADS = 8
HEAD_DIM = 64
N_KV_HEADS = 2
HEADS_PER_GROUP = N_HEADS // N_KV_HEADS
CMP_LEN = 32
CMP_STRIDE = 16
CMP_HID = 256
SEL_BLK = 64
SEL_TOPN = 16
WINDOW = 512
CONV_WIDTH = 512
CONV_K = 3
N_EXPERTS = 32
TOP_K = 4
SWIGLU_LIMIT = 7.0
SWIGLU_ALPHA = 1.702
MOE_CHUNK = 512
EPS = 1e-6
NEG = -1e30
N_FORCED = 3

LANES = 128
Q_TILE = 256
KEY_TILE = 512
N_AUG = 5
ROW_TILE = 512
VMEM_LIMIT = 56 * 1024 * 1024
GROUP_LANES = HEADS_PER_GROUP * Q_TILE
WIN_KEYS = WINDOW + Q_TILE
N_TILE_ROWS = 16
V_ROWS = HEAD_DIM + 8


def _cparams(n_axes):
    return pltpu.CompilerParams(dimension_semantics=("arbitrary",) * n_axes,
                                vmem_limit_bytes=VMEM_LIMIT)


def _dot(a, b):
    return jnp.dot(a, b, preferred_element_type=F32)


def _dot_nt(a, b):
    return lax.dot_general(a, b, (((1,), (1,)), ((), ())), preferred_element_type=F32)


ROW_TILES = 8


def _store_row_tiles(ref, val, first=0):
    n = val.shape[0]
    for s in range(ROW_TILES):
        ref[pl.ds(first * ROW_TILES + s, n, stride=ROW_TILES), :] = val[:, s * LANES:(s + 1) * LANES]


def _load_row_tiles(ref, n, first=0):
    return jnp.concatenate([ref[pl.ds(first * ROW_TILES + s, n, stride=ROW_TILES), :] for s in range(ROW_TILES)],
                           axis=1)


def _rms_pairs(v, bd):
    ss = _dot((v * v).astype(BF16), bd)
    return v * lax.rsqrt(ss + EPS)


def _inproj_body(x_ref, g1_ref, wq_ref, wkv_ref, wng_ref, wcv_ref, wmg_ref, gq_ref, gk_ref, bd_ref,
                 qtab_ref, kwtab_ref,
                 qa_out, hk_out, hv_out, kas_out, kaw_out, vsw_out, gate_out, cbv_out, gab_out, raw_ref):
    x = x_ref[...]
    tm = x.shape[0]
    ms = jnp.mean(x * x, axis=-1, keepdims=True)
    h = (x * lax.rsqrt(ms + EPS) * g1_ref[...]).astype(BF16)
    bd = bd_ref[...]
    low = lax.broadcasted_iota(I32, (tm, LANES), 1) < HEAD_DIM

    def place(pair, tab_ref, out_ref, base, slot):
        for j, src in enumerate((pair, pltpu.roll(pair, HEAD_DIM, 1))):
            o = base + j * slot
            out_ref[:, o:o + LANES] = jnp.where(low, src, tab_ref[:, o:o + LANES].astype(F32)).astype(BF16)

    q = _dot(h, wq_ref[...])
    for c in range(N_HEADS * HEAD_DIM // LANES):
        sl = slice(c * LANES, (c + 1) * LANES)
        place(_rms_pairs(q[:, sl], bd) * gq_ref[:, sl], qtab_ref, qa_out, 2 * c * LANES, LANES)
    kv = _dot(h, wkv_ref[...])

    def emit_half_blocks(c, out_ref):
        nb = tm // CMP_STRIDE
        hw = CMP_STRIDE * HEAD_DIM
        raw_ref[...] = kv[:, c * LANES:(c + 1) * LANES]
        lo = lax.broadcasted_iota(I32, (nb, LANES), 1) < HEAD_DIM
        for u in range(CMP_STRIDE // 2):
            t0 = raw_ref[pl.ds(2 * u, nb, stride=CMP_STRIDE), :]
            t1 = raw_ref[pl.ds(2 * u + 1, nb, stride=CMP_STRIDE), :]
            out_ref[:, u * LANES:(u + 1) * LANES] = jnp.where(lo, t0, pltpu.roll(t1, HEAD_DIM, 1)).astype(BF16)
            out_ref[:, hw + u * LANES:hw + (u + 1) * LANES] = (
                jnp.where(lo, pltpu.roll(t0, HEAD_DIM, 1), t1).astype(BF16))

    emit_half_blocks(0, hk_out)
    emit_half_blocks(1, hv_out)
    place(_rms_pairs(kv[:, 256:384], bd) * gk_ref[:, 0:128], kwtab_ref, kas_out, 0, LANES)
    place(_rms_pairs(kv[:, 512:640], bd) * gk_ref[:, 128:256], kwtab_ref, kaw_out, 0, LANES)
    vsw_out[:, 0:128] = kv[:, 384:512].astype(BF16)
    vsw_out[:, 128:256] = kv[:, 640:768].astype(BF16)
    gate_out[...] = jax.nn.sigmoid(_dot(h, wng_ref[...]))
    cv = _dot(h, wcv_ref[...])
    cw = CONV_WIDTH
    cbv_out[:, 0:cw] = cv[:, 0:cw].astype(BF16)
    cbv_out[:, cw:2 * cw] = (cv[:, cw:2 * cw] * cv[:, 2 * cw:3 * cw]).astype(BF16)
    gab_out[...] = jax.nn.sigmoid(_dot(h, wmg_ref[...])).astype(BF16)


def _key_aug(pos):
    one = np.ones_like(pos, np.float32)
    return np.stack([one, one, (pos // 64 * 64).astype(np.float32), (pos % 64).astype(np.float32), 0 * one],
                    axis=-1)


def _slot_table(aug, slot):
    S, n, _ = aug.shape
    tab = np.zeros((S, n, slot), np.float32)
    tab[:, :, HEAD_DIM:HEAD_DIM + N_AUG] = aug
    return jnp.asarray(tab.reshape(S, n * slot), BF16)


def _inproj(x2, g1, w_in, g_q, g_ks, g_kw, S):
    T, D = x2.shape
    H, G = N_HEADS, N_KV_HEADS
    aw = H * HEAD_DIM
    kvw = G * HEAD_DIM
    o = 0
    wq = w_in[:, o:o + aw]; o += aw
    wkv = w_in[:, o:o + 6 * kvw]; o += 6 * kvw
    wng = w_in[:, o:o + 3 * H]; o += 3 * H
    wcv = w_in[:, o:o + 3 * CONV_WIDTH]; o += 3 * CONV_WIDTH
    wmg = w_in[:, o:o + 2 * D]
    wng = jnp.pad(wng, ((0, 0), (0, LANES - 3 * H)))
    wq, wkv, wng, wcv, wmg = (w.astype(BF16) for w in (wq, wkv, wng, wcv, wmg))
    gq = (jnp.tile(g_q, H) * (HEAD_DIM ** -0.5)).reshape(1, aw)
    gk = jnp.concatenate([jnp.tile(g_ks, G), jnp.tile(g_kw, G)]).reshape(1, 2 * kvw)
    idx = np.arange(LANES) // HEAD_DIM
    bd = jnp.asarray((idx[:, None] == idx[None, :]).astype(np.float32) / HEAD_DIM, BF16)
    pos = np.arange(S)
    hi = (pos // 64 * 64).astype(np.float32)[:, None]
    lo = (pos % 64).astype(np.float32)[:, None]
    sl = (2.0 ** (-8.0 * np.arange(1, H + 1) / H)).astype(np.float32)[None, :]
    aq = np.stack([-sl * hi, -sl * lo, np.broadcast_to(sl, (S, H)), np.broadcast_to(sl, (S, H)),
                   np.ones((S, H), np.float32)], axis=-1)
    ak = np.broadcast_to(_key_aug(pos)[:, None, :], (S, G, N_AUG))
    qtab = _slot_table(aq, LANES)
    kwtab = _slot_table(ak, LANES)
    tm = ROW_TILE
    nst = S // tm
    row = lambda w: pl.BlockSpec((tm, w), lambda i: (i, 0))
    full = lambda a: pl.BlockSpec(a.shape, lambda i: (0,) * a.ndim)
    tab = lambda a: pl.BlockSpec((tm, a.shape[1]), lambda i: (i % nst, 0))
    ins = (x2, g1.reshape(1, D), wq, wkv, wng, wcv, wmg, gq, gk, bd, qtab, kwtab)
    widths = (H * LANES, G * LANES, G * LANES, 2 * kvw, LANES, 2 * CONV_WIDTH, 2 * D)
    dtypes = (BF16, BF16, BF16, BF16, F32, BF16, BF16)
    hw = G * CMP_STRIDE * HEAD_DIM
    nb = tm // CMP_STRIDE
    half = pl.BlockSpec((nb, hw), lambda i: (i, 0))
    half_shape = jax.ShapeDtypeStruct((T // CMP_STRIDE, hw), BF16)
    rows = [(row(w), jax.ShapeDtypeStruct((T, w), dt)) for w, dt in zip(widths, dtypes)]
    outs = [rows[0], (half, half_shape), (half, half_shape)] + rows[1:]
    return pl.pallas_call(
        _inproj_body,
        grid=(T // tm,),
        in_specs=[row(D)] + [full(a) for a in ins[1:10]] + [tab(a) for a in ins[10:]],
        out_specs=[o[0] for o in outs],
        out_shape=[o[1] for o in outs],
        scratch_shapes=[pltpu.VMEM((tm, LANES), F32)],
        compiler_params=_cparams(1),
        name="inproj",
    )(*ins)


def _compress_body(h_ref, w1_ref, pe_ref, b1_ref, w2_ref, b2_ref, g_ref, tab_ref, o_ref, *, for_keys):
    hb = h_ref[0]
    nc = hb.shape[0]
    a = _dot(hb, w1_ref[0])
    b = _dot(hb, w1_ref[1])
    c = _dot(pe_ref[0], w1_ref[0]) + _dot(pe_ref[1], w1_ref[1])
    pre = a + pltpu.roll(b, nc - 1, 0) + c[0:1, :] + b1_ref[...]
    hid = jax.nn.gelu(pre)
    out = _dot(hid.astype(BF16), w2_ref[...]) + b2_ref[...]
    if for_keys:
        ms = jnp.sum(out * out, axis=-1, keepdims=True) * (1.0 / HEAD_DIM)
        out = out * lax.rsqrt(ms + EPS) * g_ref[...]
        low = lax.broadcasted_iota(I32, out.shape, 1) < HEAD_DIM
        o_ref[0, 0] = jnp.where(low, out, tab_ref[...]).astype(BF16)
    else:
        o_ref[0, 0] = out.T[0:HEAD_DIM, :].astype(BF16)


def _compress(hh, pe, w1, b1, w2, b2, gain, for_keys):
    B, NC, _ = hh.shape
    G, HW = N_KV_HEADS, CMP_STRIDE * HEAD_DIM
    w1s = w1.reshape(2, HW, CMP_HID).astype(BF16)
    pes = jnp.broadcast_to(pe.reshape(2, 1, HW), (2, 8, HW)).astype(BF16)
    padl = lambda a: jnp.pad(a, ((0, 0), (0, LANES - HEAD_DIM)))
    tabn = np.zeros((NC, LANES), np.float32)
    tabn[:, HEAD_DIM:HEAD_DIM + N_AUG] = _key_aug(np.arange(NC) * CMP_STRIDE + (CMP_LEN - 1))
    tab = jnp.asarray(tabn)
    full = lambda a: pl.BlockSpec(a.shape, lambda b, g: (0,) * a.ndim)
    ins = (hh, w1s, pes, b1.reshape(1, CMP_HID), padl(w2).astype(BF16), padl(b2.reshape(1, HEAD_DIM)),
           padl(gain.reshape(1, HEAD_DIM)), tab)
    oshape = (B, G, NC, LANES) if for_keys else (B, G, HEAD_DIM, NC)
    return pl.pallas_call(
        functools.partial(_compress_body, for_keys=for_keys),
        grid=(B, G),
        in_specs=[pl.BlockSpec((1, NC, HW), lambda b, g: (b, 0, g))] + [full(a) for a in ins[1:]],
        out_specs=pl.BlockSpec((1, 1) + oshape[2:], lambda b, g: (b, g, 0, 0)),
        out_shape=jax.ShapeDtypeStruct(oshape, BF16),
        compiler_params=_cparams(2),
        name="compress_keys" if for_keys else "compress_values",
    )(*ins)


def _attn_body(qa_ref, g_ref, kca_ref, vct_ref, kas_ref, vst_ref, kaw_ref, vwt_ref, selmapt_ref, wbias_ref,
               tilemap_ref, cbias_ref, o_ref, selb_ref, m_ref, acc_ref, s0_ref, s1_ref, flagv_ref,
               flags_ref, list_ref, sem,
               *, n_sel):
    i = pl.program_id(2)
    q0 = i * Q_TILE
    gl = GROUP_LANES
    qa = jnp.concatenate([qa_ref[0, :, h * LANES:(h + 1) * LANES] for h in range(HEADS_PER_GROUP)], axis=0)

    nc = kca_ref.shape[2]
    s = _dot_nt(kca_ref[0, 0], qa)
    cb = cbias_ref[pl.ds(pl.multiple_of(nc - i * (Q_TILE // CMP_STRIDE), 8), nc), :]
    s = s + jnp.concatenate([cb] * HEADS_PER_GROUP, axis=1)
    m = jnp.max(s, axis=0, keepdims=True)
    p = jnp.exp(s - m)
    l = jnp.sum(p, axis=0, keepdims=True)
    has_entry = (q0 + (lax.broadcasted_iota(I32, (1, gl), 1) & (Q_TILE - 1))) >= CMP_LEN - 1
    pc = p * jnp.where(has_entry, 1.0 / l, 0.0)
    o_c = _dot(vct_ref[0, 0], pc.astype(BF16))

    ps = pc[:, 0:Q_TILE]
    for h in range(1, HEADS_PER_GROUP):
        ps = ps + pc[:, h * Q_TILE:(h + 1) * Q_TILE]
    ps_hi = ps.astype(BF16)
    ps_lo = (ps - ps_hi.astype(F32)).astype(BF16)
    imp = _dot(selmapt_ref[...], ps_hi) + _dot(selmapt_ref[...], ps_lo)
    jb = lax.broadcasted_iota(I32, (LANES, Q_TILE), 0)
    cur = (q0 + lax.broadcasted_iota(I32, (LANES, Q_TILE), 1)) // SEL_BLK
    forced = (jb == 0) | (jb == cur) | (jb == cur - 1)
    score = jnp.where(forced, -jnp.inf, jnp.where(jb > cur, NEG, imp))
    jbf = jb.astype(F32)
    for _ in range(n_sel - N_FORCED):
        mx = jnp.max(score, axis=0, keepdims=True)
        first = jnp.min(jnp.where(score == mx, jbf, float(LANES)), axis=0, keepdims=True)
        score = jnp.where(jbf == first, -jnp.inf, score)
    picked = score == -jnp.inf
    bias_t = jnp.where(picked, 0.0, NEG)
    selb_ref[...] = jnp.concatenate([bias_t] * HEADS_PER_GROUP, axis=1)
    tile_hits = jnp.max(_dot(tilemap_ref[...], jnp.where(picked, 1.0, 0.0).astype(BF16)), axis=1, keepdims=True)
    flagv_ref[...] = jnp.broadcast_to(tile_hits, flagv_ref.shape).astype(I32)
    flag_copy = pltpu.make_async_copy(flagv_ref, flags_ref, sem)
    flag_copy.start()

    w0 = pl.multiple_of(jnp.maximum(q0 - WINDOW, 0), Q_TILE)
    sw = _dot_nt(kaw_ref[0, pl.ds(w0, WIN_KEYS), :], qa)
    wb = wbias_ref[jnp.minimum(i, WINDOW // Q_TILE)]
    sw = sw + jnp.concatenate([wb] * HEADS_PER_GROUP, axis=1)
    mw = jnp.max(sw, axis=0, keepdims=True)
    pw = jnp.exp(sw - mw)
    c0 = w0 // Q_TILE
    vw = jnp.concatenate([vwt_ref[0, 0, c0 + j] for j in range(WIN_KEYS // Q_TILE)], axis=1)
    aw = _dot(vw, pw.astype(BF16))
    o_w = aw[0:HEAD_DIM] * (1.0 / aw[HEAD_DIM:HEAD_DIM + 1])

    def scores(kt):
        sc = _dot_nt(kas_ref[0, pl.ds(pl.multiple_of(kt * KEY_TILE, KEY_TILE), KEY_TILE), :], qa)
        blocks = KEY_TILE // SEL_BLK
        mask = [jnp.broadcast_to(selb_ref[pl.ds(kt * blocks + j, 1), :], (SEL_BLK, gl)) for j in range(blocks)]
        return sc + jnp.concatenate(mask, axis=0)

    n_full = q0 // KEY_TILE
    key = n_full * KEY_TILE + lax.broadcasted_iota(I32, (KEY_TILE, gl), 0)
    qry = q0 + (lax.broadcasted_iota(I32, (KEY_TILE, gl), 1) & (Q_TILE - 1))
    s0_ref[...] = jnp.where(key <= qry, scores(n_full), NEG)
    m_ref[...] = jnp.full(m_ref.shape, -3.0e38, F32)
    acc_ref[...] = jnp.zeros(acc_ref.shape, F32)
    list_ref[0] = n_full

    def absorb(s_ref, kt, live):
        sc = s_ref[...]
        m_old = m_ref[...]
        m_new = jnp.where(live, jnp.maximum(m_old, jnp.max(sc, axis=0, keepdims=True)), m_old)
        pv = _dot(vst_ref[0, 0, kt], jnp.exp(sc - m_new).astype(BF16))
        acc_ref[...] = jnp.exp(m_old - m_new) * acc_ref[...] + jnp.where(live, pv, 0.0)
        m_ref[...] = m_new

    flag_copy.wait()

    def compact(kt, n):
        active = flags_ref[kt, 0] > 0

        @pl.when(active)
        def _():
            list_ref[n] = kt

        return n + active.astype(I32)

    n_items = lax.fori_loop(0, n_full, compact, 1)

    last = n_items - 1

    def pair_body(j, carry):
        a = list_ref[2 * j]
        b = list_ref[jnp.minimum(2 * j + 1, last)]
        nxt = list_ref[jnp.minimum(2 * j + 2, last)]
        s1_ref[...] = scores(b)
        absorb(s0_ref, a, True)
        s0_ref[...] = scores(nxt)
        absorb(s1_ref, b, 2 * j + 1 <= last)
        return carry

    lax.fori_loop(0, (n_items + 1) // 2, pair_body, 0)

    o_s = acc_ref[0:HEAD_DIM, :] * (1.0 / acc_ref[HEAD_DIM:HEAD_DIM + 1, :])

    g = g_ref[0, 0]
    outs = []
    for h in range(HEADS_PER_GROUP):
        sl = slice(h * Q_TILE, (h + 1) * Q_TILE)
        outs.append(g[3 * h:3 * h + 1, :] * o_c[:, sl] + g[3 * h + 1:3 * h + 2, :] * o_s[:, sl]
                    + g[3 * h + 2:3 * h + 3, :] * o_w[:, sl])
    o_ref[0] = jnp.concatenate(outs, axis=0).T.astype(BF16)


def _attention(qa, gates_t, kca, vct, kas, vst, kaw, vwt, B, S):
    G, hd = N_KV_HEADS, HEAD_DIM
    NC = kca.shape[2]
    n_blk = S // SEL_BLK
    assert n_blk <= LANES and S % KEY_TILE == 0 and S >= WIN_KEYS
    n_sel = min(SEL_TOPN, n_blk)
    assert n_sel > N_FORCED
    ratio, span = SEL_BLK // CMP_STRIDE, CMP_LEN // CMP_STRIDE
    sm = np.zeros((LANES, NC), np.float32)
    for j in range(n_blk):
        for a in range(ratio):
            for b in range(span):
                n = ratio * j + a - b
                if 0 <= n < NC - 1:
                    sm[j, n] += 1.0
    selmapt = jnp.asarray(sm, BF16)
    c = np.arange(WIN_KEYS)[:, None]
    r = np.arange(Q_TILE)[None, :]
    offs = np.arange(WINDOW // Q_TILE + 1)[:, None, None] * Q_TILE
    wbias = jnp.asarray(np.where((c - r <= offs) & (c - r > offs - WINDOW), 0.0, NEG), F32)
    tilemap = jnp.asarray(np.arange(LANES)[None, :] // (KEY_TILE // SEL_BLK) == np.arange(N_TILE_ROWS)[:, None],
                          BF16)
    assert S // KEY_TILE <= N_TILE_ROWS
    u = np.arange(2 * NC)[:, None] - NC
    cbias = jnp.asarray(np.where(CMP_STRIDE * u + (CMP_LEN - 1) <= np.arange(Q_TILE)[None, :], 0.0, NEG), F32)

    hpg = HEADS_PER_GROUP
    grp = lambda *blk: pl.BlockSpec((1, 1) + blk, lambda b, g, i: (b, g) + (0,) * len(blk))
    seq = lambda w: pl.BlockSpec((1, S, w), lambda b, g, i: (b, 0, g))
    const = lambda a: pl.BlockSpec(a.shape, lambda b, g, i: (0,) * a.ndim)
    return pl.pallas_call(
        functools.partial(_attn_body, n_sel=n_sel),
        grid=(B, G, S // Q_TILE),
        in_specs=[pl.BlockSpec((1, Q_TILE, hpg * LANES), lambda b, g, i: (b, i, g)),
                  pl.BlockSpec((1, 1, 16, Q_TILE), lambda b, g, i: (b, g, 0, i)),
                  grp(NC, LANES), grp(hd, NC), seq(LANES), grp(S // KEY_TILE, V_ROWS, KEY_TILE),
                  seq(LANES), grp(S // Q_TILE, V_ROWS, Q_TILE), const(selmapt), const(wbias), const(tilemap),
                  const(cbias)],
        out_specs=pl.BlockSpec((1, Q_TILE, hpg * hd), lambda b, g, i: (b, i, g)),
        out_shape=jax.ShapeDtypeStruct((B, S, N_HEADS * hd), BF16),
        scratch_shapes=[pltpu.VMEM((LANES, GROUP_LANES), F32), pltpu.VMEM((1, GROUP_LANES), F32),
                        pltpu.VMEM((V_ROWS, GROUP_LANES), F32), pltpu.VMEM((KEY_TILE, GROUP_LANES), F32),
                        pltpu.VMEM((KEY_TILE, GROUP_LANES), F32),
                        pltpu.VMEM((N_TILE_ROWS, LANES), I32), pltpu.SMEM((N_TILE_ROWS, LANES), I32),
                        pltpu.SMEM((N_TILE_ROWS,), I32), pltpu.SemaphoreType.DMA(())],
        compiler_params=_cparams(3),
        name="nsa_attention",
    )(qa, gates_t, kca, vct, kas, vst, kaw, vwt, selmapt, wbias, tilemap, cbias)


def _mixer_out_body(x_ref, oa_ref, cbv_ref, halo_ref, gab_ref, cw_ref, wpa_ref, wpb_ref, wo_ref, o_ref,
                    *, seq_len):
    i = pl.program_id(0)
    tm = x_ref.shape[0]
    cwd = CONV_WIDTH
    d = x_ref.shape[1]
    v = cbv_ref[:, cwd:2 * cwd].astype(F32)
    prev = halo_ref[:, cwd:2 * cwd].astype(F32)
    keep = ((i * tm) % seq_len != 0).astype(F32)
    p1 = prev[7:8, :] * keep
    p2 = prev[6:7, :] * keep
    ridx = lax.broadcasted_iota(I32, (tm, cwd), 0)
    v1 = jnp.where(ridx == 0, p1, pltpu.roll(v, 1, 0))
    v2 = jnp.where(ridx == 0, p2, jnp.where(ridx == 1, p1, pltpu.roll(v, 2, 0)))
    y = cw_ref[0:1, :] * v2 + cw_ref[1:2, :] * v1 + cw_ref[2:3, :] * v
    yb_in = (cbv_ref[:, 0:cwd].astype(F32) * y).astype(BF16)
    y_a = _dot(oa_ref[...], wpa_ref[...])
    y_b = _dot(yb_in, wpb_ref[...])
    merged = gab_ref[:, 0:d].astype(F32) * y_a + gab_ref[:, d:2 * d].astype(F32) * y_b
    o_ref[...] = x_ref[...] + _dot(merged.astype(BF16), wo_ref[...])


def _mixer_out(x2, oa, cbv, gab, conv_w, w_pa, w_pb, w_o, seq_len):
    T, D = x2.shape
    tm = ROW_TILE
    cw8 = jnp.pad(conv_w, ((0, 8 - CONV_K), (0, 0)))
    row = lambda w: pl.BlockSpec((tm, w), lambda i: (i, 0))
    full = lambda a: pl.BlockSpec(a.shape, lambda i: (0,) * a.ndim)
    halo = pl.BlockSpec((8, cbv.shape[1]), lambda i: (jnp.maximum(i * (tm // 8) - 1, 0), 0))
    wts = (cw8, w_pa.astype(BF16), w_pb.astype(BF16), w_o.astype(BF16))
    return pl.pallas_call(
        functools.partial(_mixer_out_body, seq_len=seq_len),
        grid=(T // tm,),
        in_specs=[row(D), row(oa.shape[1]), row(cbv.shape[1]), halo, row(gab.shape[1])] + [full(a) for a in wts],
        out_specs=row(D),
        out_shape=jax.ShapeDtypeStruct((T, D), F32),
        compiler_params=_cparams(1),
        name="mixer_out",
    )(x2, oa, cbv, cbv, gab, *wts)


def _router_body(x_ref, g2_ref, whi_ref, wlo_ref, br_ref, tri_ref, h_out, mi_out, mf_out, cnt_out):
    i = pl.program_id(0)

    @pl.when(i == 0)
    def _():
        cnt_out[...] = jnp.zeros(cnt_out.shape, F32)

    x = x_ref[...]
    tm = x.shape[0]
    ms = jnp.mean(x * x, axis=-1, keepdims=True)
    h = x * lax.rsqrt(ms + EPS) * g2_ref[...]
    _store_row_tiles(h_out, h)
    h_hi = h.astype(BF16)
    h_lo = (h - h_hi.astype(F32)).astype(BF16)
    logits = (_dot(h_hi, whi_ref[...]) + _dot(h_lo, whi_ref[...]) + _dot(h_hi, wlo_ref[...])) + br_ref[...]
    lane = lax.broadcasted_iota(I32, (tm, LANES), 1)
    lanef = lane.astype(F32)
    work = jnp.where(lane < N_EXPERTS, logits, -jnp.inf)
    vals, hits = [], []
    for _ in range(TOP_K):
        mx = jnp.max(work, axis=-1, keepdims=True)
        first = jnp.min(jnp.where(work == mx, lanef, float(LANES)), axis=-1, keepdims=True)
        hit = lanef == first
        vals.append(mx)
        hits.append(hit)
        work = jnp.where(hit, -jnp.inf, work)
    ex = [jnp.exp(v - vals[0]) for v in vals]
    den = ex[0]
    for e in ex[1:]:
        den = den + e
    cnt = jnp.zeros((tm, LANES), F32)
    for hit in hits:
        cnt = cnt + hit.astype(F32)
    before = _dot(tri_ref[...], cnt.astype(BF16)) + cnt_out[0:1, :]
    mi = jnp.zeros((tm, LANES), F32)
    mf = jnp.zeros((tm, LANES), F32)
    for k, hit in enumerate(hits):
        e_k = jnp.sum(jnp.where(hit, lanef, 0.0), axis=-1, keepdims=True)
        r_k = jnp.sum(jnp.where(hit, before, 0.0), axis=-1, keepdims=True)
        mi = jnp.where(lane == k, e_k, jnp.where(lane == TOP_K + k, r_k, mi))
        mf = jnp.where(lane == k, ex[k] / den, mf)
    mi_out[...] = mi[:, 0:2 * TOP_K].astype(I32)
    mf_out[...] = mf
    cnt_out[...] = cnt_out[...] + jnp.sum(cnt, axis=0, keepdims=True)


def _router(x1, g2, w_r, b_r):
    T, D = x1.shape
    assert D == ROW_TILES * LANES
    tm = ROW_TILE
    wpad = jnp.pad(w_r, ((0, 0), (0, LANES - N_EXPERTS)))
    whi = wpad.astype(BF16)
    wlo = (wpad - whi.astype(F32)).astype(BF16)
    br = jnp.pad(b_r, (0, LANES - N_EXPERTS)).reshape(1, LANES)
    tri = jnp.asarray(np.tril(np.ones((tm, tm), np.float32), -1), BF16)
    row = lambda w: pl.BlockSpec((tm, w), lambda i: (i, 0))
    full = lambda a: pl.BlockSpec(a.shape, lambda i: (0,) * a.ndim)
    ins = (x1, g2.reshape(1, D), whi, wlo, br, tri)
    return pl.pallas_call(
        _router_body,
        grid=(T // tm,),
        in_specs=[row(D)] + [full(a) for a in ins[1:]],
        out_specs=[pl.BlockSpec((tm * ROW_TILES, LANES), lambda i: (i, 0)), row(2 * TOP_K), row(LANES),
                   pl.BlockSpec((8, LANES), lambda i: (0, 0))],
        out_shape=[jax.ShapeDtypeStruct((T * ROW_TILES, LANES), F32), jax.ShapeDtypeStruct((T, 2 * TOP_K), I32),
                   jax.ShapeDtypeStruct((T, LANES), F32), jax.ShapeDtypeStruct((8, LANES), F32)],
        compiler_params=_cparams(1),
        name="router",
    )(*ins)


DISPATCH_TILE = 512
DMA_UNROLL = 8


def _dispatch_body(dest_ref, last_ref, h_ref, o_hbm, zero_ref, sem, zsem):
    @pl.when(pl.program_id(0) == 0)
    def _():
        zero_ref[...] = jnp.zeros(zero_ref.shape, zero_ref.dtype)

        def clear(e):
            start = pl.multiple_of(last_ref[e] * ROW_TILES, MOE_CHUNK * ROW_TILES)
            return pltpu.make_async_copy(zero_ref, o_hbm.at[pl.ds(start, MOE_CHUNK * ROW_TILES)], zsem)

        for e in range(N_EXPERTS):
            @pl.when(last_ref[e] >= 0)
            def _():
                clear(e).start()
        for e in range(N_EXPERTS):
            @pl.when(last_ref[e] >= 0)
            def _():
                clear(e).wait()

    def row_copy(r, d):
        return pltpu.make_async_copy(h_ref.at[pl.ds(pl.multiple_of(r * ROW_TILES, ROW_TILES), ROW_TILES)],
                                     o_hbm.at[pl.ds(pl.multiple_of(d * ROW_TILES, ROW_TILES), ROW_TILES)], sem)

    def start(r, c):
        for k in range(TOP_K):
            row_copy(r, dest_ref[0, 0, r * TOP_K + k]).start(priority=k % 2)
        return c

    def wait(r, c):
        for k in range(TOP_K):
            row_copy(0, 0).wait()
        return c

    lax.fori_loop(0, DISPATCH_TILE, start, 0, unroll=DMA_UNROLL)
    lax.fori_loop(0, DISPATCH_TILE, wait, 0, unroll=DMA_UNROLL)


def _dispatch(h2, dest, last_chunk, n_rows):
    T = h2.shape[0] // ROW_TILES
    td = DISPATCH_TILE
    dest3 = dest.reshape(T // td, 1, td * TOP_K)
    return pl.pallas_call(
        _dispatch_body,
        grid=(T // td,),
        in_specs=[pl.BlockSpec((1, 1, td * TOP_K), lambda i: (i, 0, 0), memory_space=pltpu.SMEM),
                  pl.BlockSpec(memory_space=pltpu.SMEM),
                  pl.BlockSpec((td * ROW_TILES, LANES), lambda i: (i, 0))],
        out_specs=pl.BlockSpec(memory_space=pl.ANY),
        out_shape=jax.ShapeDtypeStruct((n_rows * ROW_TILES, LANES), h2.dtype),
        scratch_shapes=[pltpu.VMEM((MOE_CHUNK * ROW_TILES, LANES), h2.dtype), pltpu.SemaphoreType.DMA(()),
                        pltpu.SemaphoreType.DMA(())],
        compiler_params=_cparams(1),
        name="dispatch",
    )(dest3, last_chunk, h2)


def _expert_body(ce_ref, nu_ref, x_ref, wgu_ref, bgu_ref, wdn_ref, bdn_ref, o_ref, wgu_bf, wdn_bf):
    c = pl.program_id(0)
    dff = wdn_ref.shape[1]

    @pl.when((c == 0) | (ce_ref[c] != ce_ref[jnp.maximum(c - 1, 0)]))
    def _():
        wgu_bf[...] = wgu_ref[0].astype(BF16)
        wdn_bf[...] = wdn_ref[0].astype(BF16)

    @pl.when(c < nu_ref[0])
    def _():
        x = _load_row_tiles(x_ref, MOE_CHUNK)
        gu = _dot(x.astype(BF16), wgu_bf[...]) + bgu_ref[0]
        g = jnp.minimum(gu[:, 0:dff], SWIGLU_LIMIT)
        u = jnp.clip(gu[:, dff:2 * dff], -SWIGLU_LIMIT, SWIGLU_LIMIT)
        act = (u + 1.0) * (g * jax.nn.sigmoid(SWIGLU_ALPHA * g))
        _store_row_tiles(o_ref, _dot(act.astype(BF16), wdn_bf[...]) + bdn_ref[0])

    @pl.when(c >= nu_ref[0])
    def _():
        o_ref[...] = jnp.zeros(o_ref.shape, F32)


def _experts(hperm, chunk_e, n_used, w_gu, b_gu, w_dn, b_dn):
    E, D, F2 = w_gu.shape
    assert D == ROW_TILES * LANES
    P = hperm.shape[0] // ROW_TILES
    dff = F2 // 2
    n_chunks = P // MOE_CHUNK
    chunk = (MOE_CHUNK * ROW_TILES, LANES)
    grid_spec = pltpu.PrefetchScalarGridSpec(
        num_scalar_prefetch=2,
        grid=(n_chunks,),
        in_specs=[pl.BlockSpec(chunk, lambda c, ce, nu: (jnp.minimum(c, nu[0] - 1), 0)),
                  pl.BlockSpec((1, D, F2), lambda c, ce, nu: (ce[c], 0, 0)),
                  pl.BlockSpec((1, 1, F2), lambda c, ce, nu: (ce[c], 0, 0)),
                  pl.BlockSpec((1, dff, D), lambda c, ce, nu: (ce[c], 0, 0)),
                  pl.BlockSpec((1, 1, D), lambda c, ce, nu: (ce[c], 0, 0))],
        out_specs=pl.BlockSpec(chunk, lambda c, ce, nu: (c, 0)),
        scratch_shapes=[pltpu.VMEM((D, F2), BF16), pltpu.VMEM((dff, D), BF16)],
    )
    return pl.pallas_call(
        _expert_body,
        grid_spec=grid_spec,
        out_shape=jax.ShapeDtypeStruct(hperm.shape, F32),
        compiler_params=_cparams(1),
        name="experts",
    )(chunk_e, n_used, hperm, w_gu, b_gu.reshape(E, 1, F2), w_dn, b_dn.reshape(E, 1, D))


COMBINE_TILE = 256


def _combine_body(dest_ref, next_ref, x_ref, w_ref, y_hbm, o_ref, buf_ref, sems):
    i = pl.program_id(0)
    slot = i % 2

    def row_copy(s, r, k, d):
        return pltpu.make_async_copy(y_hbm.at[pl.ds(pl.multiple_of(d * ROW_TILES, ROW_TILES), ROW_TILES)],
                                     buf_ref.at[s, k, pl.ds(pl.multiple_of(r * ROW_TILES, ROW_TILES), ROW_TILES)],
                                     sems.at[s])

    def fetch(idx_ref, s):
        def start(r, c):
            for k in range(TOP_K):
                row_copy(s, r, k, idx_ref[0, 0, r * TOP_K + k]).start(priority=k % 2)
            return c

        lax.fori_loop(0, COMBINE_TILE, start, 0, unroll=DMA_UNROLL)

    @pl.when(i == 0)
    def _():
        fetch(dest_ref, slot)

    @pl.when(i + 1 < pl.num_programs(0))
    def _():
        fetch(next_ref, 1 - slot)

    def wait(r, c):
        for k in range(TOP_K):
            row_copy(slot, 0, 0, 0).wait()
        return c

    lax.fori_loop(0, COMBINE_TILE, wait, 0, unroll=DMA_UNROLL)
    gate = [jnp.broadcast_to(w_ref[:, k:k + 1], (COMBINE_TILE, LANES)) for k in range(TOP_K)]
    for s in range(ROW_TILES):
        sl = slice(s * LANES, (s + 1) * LANES)
        out = x_ref[:, sl]
        for k in range(TOP_K):
            out = out + gate[k] * buf_ref[slot, k, pl.ds(s, COMBINE_TILE, stride=ROW_TILES), :]
        o_ref[:, sl] = out


def _combine(x1, gate_w, dest, ys):
    T, D = x1.shape
    tc = COMBINE_TILE
    n = T // tc
    dest3 = dest.reshape(n, 1, tc * TOP_K)
    row = lambda w: pl.BlockSpec((tc, w), lambda i: (i, 0))
    idx = lambda f: pl.BlockSpec((1, 1, tc * TOP_K), f, memory_space=pltpu.SMEM)
    return pl.pallas_call(
        _combine_body,
        grid=(n,),
        in_specs=[idx(lambda i: (i, 0, 0)), idx(lambda i: (jnp.minimum(i + 1, n - 1), 0, 0)),
                  row(D), row(LANES), pl.BlockSpec(memory_space=pl.ANY)],
        out_specs=row(D),
        out_shape=jax.ShapeDtypeStruct((T, D), F32),
        scratch_shapes=[pltpu.VMEM((2, TOP_K, tc * ROW_TILES, LANES), F32), pltpu.SemaphoreType.DMA((2,))],
        compiler_params=_cparams(1),
        name="combine",
    )(dest3, dest3, x1, gate_w, ys)


def _mixer(x2, B, S, g_norm1, w_in, g_q, g_kc, g_ks, g_kw, pe_k, ck_w1, ck_b1, ck_w2, ck_b2,
           pe_v, cv_w1, cv_b1, cv_w2, cv_b2, conv_w, w_pa, w_pb, w_o):
    T, D = x2.shape
    G, H, hd = N_KV_HEADS, N_HEADS, HEAD_DIM
    qa, hk, hv, kas, kaw, vsw, gates, cbv, gab = _inproj(x2, g_norm1, w_in, g_q, g_ks, g_kw, S)
    nh = S // CMP_STRIDE
    kca = _compress(hk.reshape(B, nh, -1), pe_k, ck_w1, ck_b1, ck_w2, ck_b2, g_kc, True)
    vct = _compress(hv.reshape(B, nh, -1), pe_v, cv_w1, cv_b1, cv_w2, cv_b2, jnp.ones((hd,), F32), False)
    vsw5 = vsw.reshape(B, S, 2, G, hd)
    ones_rows = jnp.concatenate([jnp.ones((1,), BF16), jnp.zeros((V_ROWS - hd - 1,), BF16)])

    def key_major(v, tile):
        vt = v.reshape(B, S // tile, tile, G, hd).transpose(0, 3, 1, 4, 2)
        extra = jnp.broadcast_to(ones_rows[None, None, None, :, None], vt.shape[:3] + (V_ROWS - hd, tile))
        return jnp.concatenate([vt, extra], axis=3)

    vst = key_major(vsw5[:, :, 0], KEY_TILE)
    vwt = key_major(vsw5[:, :, 1], Q_TILE)
    gat = gates[:, :3 * H].reshape(B, S, G, 3 * HEADS_PER_GROUP).transpose(0, 2, 3, 1)
    gat = jnp.pad(gat, ((0, 0), (0, 0), (0, 16 - 3 * HEADS_PER_GROUP), (0, 0)))
    o = _attention(qa.reshape(B, S, -1), gat, kca, vct, kas.reshape(B, S, -1), vst, kaw.reshape(B, S, -1),
                   vwt, B, S)
    return _mixer_out(x2, o.reshape(T, H * hd), cbv, gab, conv_w, w_pa, w_pb, w_o, S)


def _moe(x1, g_norm2, w_r, b_r, w_gu, b_gu, w_dn, b_dn):
    T, D = x1.shape
    h2, mi, mf, cnt = _router(x1, g_norm2, w_r, b_r)
    top_e = mi[:, 0:TOP_K]
    rank = mi[:, TOP_K:2 * TOP_K]
    counts = cnt[0, :N_EXPERTS].astype(I32)
    padded = (counts + MOE_CHUNK - 1) // MOE_CHUNK * MOE_CHUNK
    pend = jnp.cumsum(padded)
    poffs = pend - padded
    dest = (poffs[top_e] + rank).reshape(-1)
    n_chunks = (T * TOP_K + MOE_CHUNK - 1) // MOE_CHUNK + N_EXPERTS
    chunk_start = jnp.arange(n_chunks, dtype=I32) * MOE_CHUNK
    chunk_e = jnp.minimum(jnp.sum((pend[None, :] <= chunk_start[:, None]).astype(I32), axis=1), N_EXPERTS - 1)
    n_used = (pend[-1:] // MOE_CHUNK).astype(I32)
    last_chunk = jnp.where(padded > 0, pend - MOE_CHUNK, -1).astype(I32)
    hperm = _dispatch(h2, dest, last_chunk, n_chunks * MOE_CHUNK)
    ys = _experts(hperm, chunk_e, n_used, w_gu, b_gu, w_dn, b_dn)
    return _combine(x1, mf, dest, ys)


def kernel(x, g_norm1, w_in, g_q, g_kc, g_ks, g_kw, pe_k, ck_w1, ck_b1, ck_w2, ck_b2, pe_v, cv_w1, cv_b1,
           cv_w2, cv_b2, conv_w, w_pa, w_pb, w_o, g_norm2, w_r, b_r, w_gu, b_gu, w_dn, b_dn):
    B, S, D = x.shape
    x2 = x.reshape(B * S, D)
    for l in range(g_norm1.shape[0]):
        x2 = _mixer(x2, B, S, g_norm1[l], w_in[l], g_q[l], g_kc[l], g_ks[l], g_kw[l], pe_k[l], ck_w1[l],
                    ck_b1[l], ck_w2[l], ck_b2[l], pe_v[l], cv_w1[l], cv_b1[l], cv_w2[l], cv_b2[l],
                    conv_w[l], w_pa[l], w_pb[l], w_o[l])
        x2 = _moe(x2, g_norm2[l], w_r[l], b_r[l], w_gu[l], b_gu[l], w_dn[l], b_dn[l])
    return x2.reshape(B, S, D)
```

```python
import functools

import numpy as np
import jax
import jax.numpy as jnp
from jax import lax
from jax.experimental import pallas as pl
from jax.experimental.pallas import tpu as pltpu

F32 = jnp.float32
BF16 = jnp.bfloat16
I32 = jnp.int32

N_HEADS = 8
HEAD_DIM = 64
N_KV_HEADS = 2
HEADS_PER_GROUP = N_HEADS // N_KV_HEADS
CMP_LEN = 32
CMP_STRIDE = 16
CMP_HID = 256
SEL_BLK = 64
SEL_TOPN = 16
WINDOW = 512
CONV_WIDTH = 512
CONV_K = 3
N_EXPERTS = 32
TOP_K = 4
SWIGLU_LIMIT = 7.0
SWIGLU_ALPHA = 1.702
MOE_CHUNK = 512
EPS = 1e-6
NEG = -1e30
N_FORCED = 3

LANES = 128
Q_TILE = 256
KEY_TILE = 512
LOG2E = float(np.log2(np.e))
N_SPLIT = 3
N_AUG = 4 * N_SPLIT + 1
ROW_TILE = 512
VMEM_LIMIT = 56 * 1024 * 1024
GROUP_LANES = HEADS_PER_GROUP * Q_TILE
WIN_KEYS = WINDOW + Q_TILE
N_TILE_ROWS = 16
V_ROWS = HEAD_DIM + 8


def _cparams(n_axes):
    return pltpu.CompilerParams(dimension_semantics=("arbitrary",) * n_axes,
                                vmem_limit_bytes=VMEM_LIMIT)


def _dot(a, b):
    return jnp.dot(a, b, preferred_element_type=F32)


def _dot_nt(a, b):
    return lax.dot_general(a, b, (((1,), (1,)), ((), ())), preferred_element_type=F32)


ROW_TILES = 8


def _store_row_tiles(ref, val, first=0):
    n = val.shape[0]
    for s in range(ROW_TILES):
        ref[pl.ds(first * ROW_TILES + s, n, stride=ROW_TILES), :] = val[:, s * LANES:(s + 1) * LANES]


def _load_row_tiles(ref, n, first=0):
    return jnp.concatenate([ref[pl.ds(first * ROW_TILES + s, n, stride=ROW_TILES), :] for s in range(ROW_TILES)],
                           axis=1)


def _rms_pairs(v, bd):
    ss = _dot((v * v).astype(BF16), bd)
    return v * lax.rsqrt(ss + EPS)


def _inproj_body(x_ref, g1_ref, wq_ref, wkv_ref, wng_ref, wcv_ref, wmg_ref, gq_ref, gk_ref, bd_ref,
                 qtab_ref, kwtab_ref,
                 qa_out, hk_out, hv_out, kas_out, kaw_out, vsw_out, gate_out, cbv_out, gab_out, raw_ref):
    x = x_ref[...]
    tm = x.shape[0]
    ms = jnp.mean(x * x, axis=-1, keepdims=True)
    h = (x * lax.rsqrt(ms + EPS) * g1_ref[...]).astype(BF16)
    bd = bd_ref[...]
    low = lax.broadcasted_iota(I32, (tm, LANES), 1) < HEAD_DIM

    def place(pair, tab_ref, out_ref, base, slot):
        for j, src in enumerate((pair, pltpu.roll(pair, HEAD_DIM, 1))):
            o = base + j * slot
            out_ref[:, o:o + LANES] = jnp.where(low, src, tab_ref[:, o:o + LANES].astype(F32)).astype(BF16)

    q = _dot(h, wq_ref[...])
    for c in range(N_HEADS * HEAD_DIM // LANES):
        sl = slice(c * LANES, (c + 1) * LANES)
        place(_rms_pairs(q[:, sl], bd) * gq_ref[:, sl], qtab_ref, qa_out, 2 * c * LANES, LANES)
    kv = _dot(h, wkv_ref[...])

    def emit_half_blocks(c, out_ref):
        nb = tm // CMP_STRIDE
        hw = CMP_STRIDE * HEAD_DIM
        raw_ref[...] = kv[:, c * LANES:(c + 1) * LANES]
        lo = lax.broadcasted_iota(I32, (nb, LANES), 1) < HEAD_DIM
        for u in range(CMP_STRIDE // 2):
            t0 = raw_ref[pl.ds(2 * u, nb, stride=CMP_STRIDE), :]
            t1 = raw_ref[pl.ds(2 * u + 1, nb, stride=CMP_STRIDE), :]
            out_ref[:, u * LANES:(u + 1) * LANES] = jnp.where(lo, t0, pltpu.roll(t1, HEAD_DIM, 1)).astype(BF16)
            out_ref[:, hw + u * LANES:hw + (u + 1) * LANES] = (
                jnp.where(lo, pltpu.roll(t0, HEAD_DIM, 1), t1).astype(BF16))

    emit_half_blocks(0, hk_out)
    emit_half_blocks(1, hv_out)
    place(_rms_pairs(kv[:, 256:384], bd) * gk_ref[:, 0:128], kwtab_ref, kas_out, 0, LANES)
    place(_rms_pairs(kv[:, 512:640], bd) * gk_ref[:, 128:256], kwtab_ref, kaw_out, 0, LANES)
    vsw_out[:, 0:128] = kv[:, 384:512].astype(BF16)
    vsw_out[:, 128:256] = kv[:, 640:768].astype(BF16)
    gate_out[...] = jax.nn.sigmoid(_dot(h, wng_ref[...]))
    cv = _dot(h, wcv_ref[...])
    cw = CONV_WIDTH
    cbv_out[:, 0:cw] = cv[:, 0:cw].astype(BF16)
    cbv_out[:, cw:2 * cw] = (cv[:, cw:2 * cw] * cv[:, 2 * cw:3 * cw]).astype(BF16)
    gab_out[...] = jax.nn.sigmoid(_dot(h, wmg_ref[...])).astype(BF16)


def _bf16_pieces(v):
    def round_bf16(a):
        u = np.ascontiguousarray(a, np.float32).view(np.uint32).astype(np.uint64)
        return ((u + 0x7FFF + ((u >> 16) & 1)) & 0xFFFF0000).astype(np.uint32).view(np.float32)

    pieces, rest = [], np.asarray(v, np.float32)
    for _ in range(N_SPLIT):
        pieces.append(round_bf16(rest))
        rest = rest - pieces[-1]
    return pieces


def _key_aug(pos):
    one = np.ones_like(pos, np.float32)
    hi = (pos // 64 * 64).astype(np.float32)
    lo = (pos % 64).astype(np.float32)
    return np.stack([one] * (2 * N_SPLIT) + [hi] * N_SPLIT + [lo] * N_SPLIT + [0 * one], axis=-1)


def _slot_table(aug, slot):
    S, n, _ = aug.shape
    tab = np.zeros((S, n, slot), np.float32)
    tab[:, :, HEAD_DIM:HEAD_DIM + N_AUG] = aug
    return jnp.asarray(tab.reshape(S, n * slot), BF16)


def _inproj(x2, g1, w_in, g_q, g_ks, g_kw, S):
    T, D = x2.shape
    H, G = N_HEADS, N_KV_HEADS
    aw = H * HEAD_DIM
    kvw = G * HEAD_DIM
    o = 0
    wq = w_in[:, o:o + aw]; o += aw
    wkv = w_in[:, o:o + 6 * kvw]; o += 6 * kvw
    wng = w_in[:, o:o + 3 * H]; o += 3 * H
    wcv = w_in[:, o:o + 3 * CONV_WIDTH]; o += 3 * CONV_WIDTH
    wmg = w_in[:, o:o + 2 * D]
    wng = jnp.pad(wng, ((0, 0), (0, LANES - 3 * H)))
    wq, wkv, wng, wcv, wmg = (w.astype(BF16) for w in (wq, wkv, wng, wcv, wmg))
    gq = (jnp.tile(g_q, H) * (HEAD_DIM ** -0.5 * LOG2E)).reshape(1, aw)
    gk = jnp.concatenate([jnp.tile(g_ks, G), jnp.tile(g_kw, G)]).reshape(1, 2 * kvw)
    idx = np.arange(LANES) // HEAD_DIM
    bd = jnp.asarray((idx[:, None] == idx[None, :]).astype(np.float32) / HEAD_DIM, BF16)
    pos = np.arange(S)
    hi = (pos // 64 * 64).astype(np.float64)[:, None]
    lo = (pos % 64).astype(np.float64)[:, None]
    c = LOG2E * 2.0 ** (-8.0 * np.arange(1, H + 1) / H)[None, :]
    cs = np.broadcast_to(c, (S, H))
    aq = np.stack(_bf16_pieces(-c * hi) + _bf16_pieces(-c * lo) + _bf16_pieces(cs) + _bf16_pieces(cs)
                  + [np.ones((S, H), np.float32)], axis=-1)
    ak = np.broadcast_to(_key_aug(pos)[:, None, :], (S, G, N_AUG))
    qtab = _slot_table(aq, LANES)
    kwtab = _slot_table(ak, LANES)
    tm = ROW_TILE
    nst = S // tm
    row = lambda w: pl.BlockSpec((tm, w), lambda i: (i, 0))
    full = lambda a: pl.BlockSpec(a.shape, lambda i: (0,) * a.ndim)
    tab = lambda a: pl.BlockSpec((tm, a.shape[1]), lambda i: (i % nst, 0))
    ins = (x2, g1.reshape(1, D), wq, wkv, wng, wcv, wmg, gq, gk, bd, qtab, kwtab)
    widths = (H * LANES, G * LANES, G * LANES, 2 * kvw, LANES, 2 * CONV_WIDTH, 2 * D)
    dtypes = (BF16, BF16, BF16, BF16, F32, BF16, BF16)
    hw = G * CMP_STRIDE * HEAD_DIM
    nb = tm // CMP_STRIDE
    half = pl.BlockSpec((nb, hw), lambda i: (i, 0))
    half_shape = jax.ShapeDtypeStruct((T // CMP_STRIDE, hw), BF16)
    rows = [(row(w), jax.ShapeDtypeStruct((T, w), dt)) for w, dt in zip(widths, dtypes)]
    outs = [rows[0], (half, half_shape), (half, half_shape)] + rows[1:]
    return pl.pallas_call(
        _inproj_body,
        grid=(T // tm,),
        in_specs=[row(D)] + [full(a) for a in ins[1:10]] + [tab(a) for a in ins[10:]],
        out_specs=[o[0] for o in outs],
        out_shape=[o[1] for o in outs],
        scratch_shapes=[pltpu.VMEM((tm, LANES), F32)],
        compiler_params=_cparams(1),
        name="inproj",
    )(*ins)


def _compress_body(h_ref, w1_ref, pe_ref, b1_ref, w2_ref, b2_ref, g_ref, tab_ref, o_ref, *, for_keys):
    hb = h_ref[0]
    nc = hb.shape[0]
    a = _dot(hb, w1_ref[0])
    b = _dot(hb, w1_ref[1])
    c = _dot(pe_ref[0], w1_ref[0]) + _dot(pe_ref[1], w1_ref[1])
    pre = a + pltpu.roll(b, nc - 1, 0) + c[0:1, :] + b1_ref[...]
    hid = jax.nn.gelu(pre)
    out = _dot(hid.astype(BF16), w2_ref[...]) + b2_ref[...]
    if for_keys:
        ms = jnp.sum(out * out, axis=-1, keepdims=True) * (1.0 / HEAD_DIM)
        out = out * lax.rsqrt(ms + EPS) * g_ref[...]
        low = lax.broadcasted_iota(I32, out.shape, 1) < HEAD_DIM
        o_ref[0, 0] = jnp.where(low, out, tab_ref[...]).astype(BF16)
    else:
        o_ref[0, 0] = out.T[0:HEAD_DIM, :].astype(BF16)


def _compress(hh, pe, w1, b1, w2, b2, gain, for_keys):
    B, NC, _ = hh.shape
    G, HW = N_KV_HEADS, CMP_STRIDE * HEAD_DIM
    w1s = w1.reshape(2, HW, CMP_HID).astype(BF16)
    pes = jnp.broadcast_to(pe.reshape(2, 1, HW), (2, 8, HW)).astype(BF16)
    padl = lambda a: jnp.pad(a, ((0, 0), (0, LANES - HEAD_DIM)))
    tabn = np.zeros((NC, LANES), np.float32)
    tabn[:, HEAD_DIM:HEAD_DIM + N_AUG] = _key_aug(np.arange(NC) * CMP_STRIDE + (CMP_LEN - 1))
    tab = jnp.asarray(tabn)
    full = lambda a: pl.BlockSpec(a.shape, lambda b, g: (0,) * a.ndim)
    ins = (hh, w1s, pes, b1.reshape(1, CMP_HID), padl(w2).astype(BF16), padl(b2.reshape(1, HEAD_DIM)),
           padl(gain.reshape(1, HEAD_DIM)), tab)
    oshape = (B, G, NC, LANES) if for_keys else (B, G, HEAD_DIM, NC)
    return pl.pallas_call(
        functools.partial(_compress_body, for_keys=for_keys),
        grid=(B, G),
        in_specs=[pl.BlockSpec((1, NC, HW), lambda b, g: (b, 0, g))] + [full(a) for a in ins[1:]],
        out_specs=pl.BlockSpec((1, 1) + oshape[2:], lambda b, g: (b, g, 0, 0)),
        out_shape=jax.ShapeDtypeStruct(oshape, BF16),
        compiler_params=_cparams(2),
        name="compress_keys" if for_keys else "compress_values",
    )(*ins)


def _attn_body(qa_ref, g_ref, kca_ref, vct_ref, kas_ref, vst_ref, kaw_ref, vwt_ref, selmapt_ref, wbias_ref,
               tilemap_ref, cbias_ref, o_ref, selb_ref, m_ref, acc_ref, s0_ref, s1_ref, flagv_ref,
               flags_ref, list_ref, sem,
               *, n_sel):
    i = pl.program_id(2)
    q0 = i * Q_TILE
    gl = GROUP_LANES
    qa = jnp.concatenate([qa_ref[0, :, h * LANES:(h + 1) * LANES] for h in range(HEADS_PER_GROUP)], axis=0)

    nc = kca_ref.shape[2]
    s = _dot_nt(kca_ref[0, 0], qa)
    cb = cbias_ref[pl.ds(pl.multiple_of(nc - i * (Q_TILE // CMP_STRIDE), 8), nc), :]
    s = s + jnp.concatenate([cb] * HEADS_PER_GROUP, axis=1)
    m = jnp.max(s, axis=0, keepdims=True)
    p = jnp.exp2(s - m)
    l = jnp.sum(p, axis=0, keepdims=True)
    has_entry = (q0 + (lax.broadcasted_iota(I32, (1, gl), 1) & (Q_TILE - 1))) >= CMP_LEN - 1
    pc = p * jnp.where(has_entry, 1.0 / l, 0.0)
    o_c = _dot(vct_ref[0, 0], pc.astype(BF16))

    ps = pc[:, 0:Q_TILE]
    for h in range(1, HEADS_PER_GROUP):
        ps = ps + pc[:, h * Q_TILE:(h + 1) * Q_TILE]
    ps_hi = ps.astype(BF16)
    ps_lo = (ps - ps_hi.astype(F32)).astype(BF16)
    imp = _dot(selmapt_ref[...], ps_hi) + _dot(selmapt_ref[...], ps_lo)
    jb = lax.broadcasted_iota(I32, (LANES, Q_TILE), 0)
    cur = (q0 + lax.broadcasted_iota(I32, (LANES, Q_TILE), 1)) // SEL_BLK
    forced = (jb == 0) | (jb == cur) | (jb == cur - 1)
    score = jnp.where(forced, -jnp.inf, jnp.where(jb > cur, NEG, imp))
    jbf = jb.astype(F32)
    for _ in range(n_sel - N_FORCED):
        mx = jnp.max(score, axis=0, keepdims=True)
        first = jnp.min(jnp.where(score == mx, jbf, float(LANES)), axis=0, keepdims=True)
        score = jnp.where(jbf == first, -jnp.inf, score)
    picked = score == -jnp.inf
    bias_t = jnp.where(picked, 0.0, NEG)
    selb_ref[...] = jnp.concatenate([bias_t] * HEADS_PER_GROUP, axis=1)
    tile_hits = jnp.max(_dot(tilemap_ref[...], jnp.where(picked, 1.0, 0.0).astype(BF16)), axis=1, keepdims=True)
    flagv_ref[...] = jnp.broadcast_to(tile_hits, flagv_ref.shape).astype(I32)
    flag_copy = pltpu.make_async_copy(flagv_ref, flags_ref, sem)
    flag_copy.start()

    w0 = pl.multiple_of(jnp.maximum(q0 - WINDOW, 0), Q_TILE)
    sw = _dot_nt(kaw_ref[0, pl.ds(w0, WIN_KEYS), :], qa)
    wb = wbias_ref[jnp.minimum(i, WINDOW // Q_TILE)]
    sw = sw + jnp.concatenate([wb] * HEADS_PER_GROUP, axis=1)
    mw = jnp.max(sw, axis=0, keepdims=True)
    pw = jnp.exp2(sw - mw)
    c0 = w0 // Q_TILE
    vw = jnp.concatenate([vwt_ref[0, 0, c0 + j] for j in range(WIN_KEYS // Q_TILE)], axis=1)
    aw = _dot(vw, pw.astype(BF16))
    o_w = aw[0:HEAD_DIM] * (1.0 / aw[HEAD_DIM:HEAD_DIM + 1])

    def scores(kt):
        sc = _dot_nt(kas_ref[0, pl.ds(pl.multiple_of(kt * KEY_TILE, KEY_TILE), KEY_TILE), :], qa)
        blocks = KEY_TILE // SEL_BLK
        mask = [jnp.broadcast_to(selb_ref[pl.ds(kt * blocks + j, 1), :], (SEL_BLK, gl)) for j in range(blocks)]
        return sc + jnp.concatenate(mask, axis=0)

    n_full = q0 // KEY_TILE
    key = n_full * KEY_TILE + lax.broadcasted_iota(I32, (KEY_TILE, gl), 0)
    qry = q0 + (lax.broadcasted_iota(I32, (KEY_TILE, gl), 1) & (Q_TILE - 1))
    s0_ref[...] = jnp.where(key <= qry, scores(n_full), NEG)
    m_ref[...] = jnp.full(m_ref.shape, -3.0e38, F32)
    acc_ref[...] = jnp.zeros(acc_ref.shape, F32)
    list_ref[0] = n_full

    def absorb(s_ref, kt, live):
        sc = s_ref[...]
        m_old = m_ref[...]
        m_new = jnp.where(live, jnp.maximum(m_old, jnp.max(sc, axis=0, keepdims=True)), m_old)
        pv = _dot(vst_ref[0, 0, kt], jnp.exp2(sc - m_new).astype(BF16))
        acc_ref[...] = jnp.exp2(m_old - m_new) * acc_ref[...] + jnp.where(live, pv, 0.0)
        m_ref[...] = m_new

    flag_copy.wait()

    def compact(kt, n):
        active = flags_ref[kt, 0] > 0

        @pl.when(active)
        def _():
            list_ref[n] = kt

        return n + active.astype(I32)

    n_items = lax.fori_loop(0, n_full, compact, 1)

    last = n_items - 1

    def pair_body(j, carry):
        a = list_ref[2 * j]
        b = list_ref[jnp.minimum(2 * j + 1, last)]
        nxt = list_ref[jnp.minimum(2 * j + 2, last)]
        s1_ref[...] = scores(b)
        absorb(s0_ref, a, True)
        s0_ref[...] = scores(nxt)
        absorb(s1_ref, b, 2 * j + 1 <= last)
        return carry

    lax.fori_loop(0, (n_items + 1) // 2, pair_body, 0)

    o_s = acc_ref[0:HEAD_DIM, :] * (1.0 / acc_ref[HEAD_DIM:HEAD_DIM + 1, :])

    g = g_ref[0, 0]
    outs = []
    for h in range(HEADS_PER_GROUP):
        sl = slice(h * Q_TILE, (h + 1) * Q_TILE)
        outs.append(g[3 * h:3 * h + 1, :] * o_c[:, sl] + g[3 * h + 1:3 * h + 2, :] * o_s[:, sl]
                    + g[3 * h + 2:3 * h + 3, :] * o_w[:, sl])
    o_ref[0] = jnp.concatenate(outs, axis=0).T.astype(BF16)


def _attention(qa, gates_t, kca, vct, kas, vst, kaw, vwt, B, S):
    G, hd = N_KV_HEADS, HEAD_DIM
    NC = kca.shape[2]
    n_blk = S // SEL_BLK
    assert n_blk <= LANES and S % KEY_TILE == 0 and S >= WIN_KEYS
    n_sel = min(SEL_TOPN, n_blk)
    assert n_sel > N_FORCED
    ratio, span = SEL_BLK // CMP_STRIDE, CMP_LEN // CMP_STRIDE
    sm = np.zeros((LANES, NC), np.float32)
    for j in range(n_blk):
        for a in range(ratio):
            for b in range(span):
                n = ratio * j + a - b
                if 0 <= n < NC - 1:
                    sm[j, n] += 1.0
    selmapt = jnp.asarray(sm, BF16)
    c = np.arange(WIN_KEYS)[:, None]
    r = np.arange(Q_TILE)[None, :]
    offs = np.arange(WINDOW // Q_TILE + 1)[:, None, None] * Q_TILE
    wbias = jnp.asarray(np.where((c - r <= offs) & (c - r > offs - WINDOW), 0.0, NEG), F32)
    tilemap = jnp.asarray(np.arange(LANES)[None, :] // (KEY_TILE // SEL_BLK) == np.arange(N_TILE_ROWS)[:, None],
                          BF16)
    assert S // KEY_TILE <= N_TILE_ROWS
    u = np.arange(2 * NC)[:, None] - NC
    cbias = jnp.asarray(np.where(CMP_STRIDE * u + (CMP_LEN - 1) <= np.arange(Q_TILE)[None, :], 0.0, NEG), F32)

    hpg = HEADS_PER_GROUP
    grp = lambda *blk: pl.BlockSpec((1, 1) + blk, lambda b, g, i: (b, g) + (0,) * len(blk))
    seq = lambda w: pl.BlockSpec((1, S, w), lambda b, g, i: (b, 0, g))
    const = lambda a: pl.BlockSpec(a.shape, lambda b, g, i: (0,) * a.ndim)
    return pl.pallas_call(
        functools.partial(_attn_body, n_sel=n_sel),
        grid=(B, G, S // Q_TILE),
        in_specs=[pl.BlockSpec((1, Q_TILE, hpg * LANES), lambda b, g, i: (b, i, g)),
                  pl.BlockSpec((1, 1, 16, Q_TILE), lambda b, g, i: (b, g, 0, i)),
                  grp(NC, LANES), grp(hd, NC), seq(LANES), grp(S // KEY_TILE, V_ROWS, KEY_TILE),
                  seq(LANES), grp(S // Q_TILE, V_ROWS, Q_TILE), const(selmapt), const(wbias), const(tilemap),
                  const(cbias)],
        out_specs=pl.BlockSpec((1, Q_TILE, hpg * hd), lambda b, g, i: (b, i, g)),
        out_shape=jax.ShapeDtypeStruct((B, S, N_HEADS * hd), BF16),
        scratch_shapes=[pltpu.VMEM((LANES, GROUP_LANES), F32), pltpu.VMEM((1, GROUP_LANES), F32),
                        pltpu.VMEM((V_ROWS, GROUP_LANES), F32), pltpu.VMEM((KEY_TILE, GROUP_LANES), F32),
                        pltpu.VMEM((KEY_TILE, GROUP_LANES), F32),
                        pltpu.VMEM((N_TILE_ROWS, LANES), I32), pltpu.SMEM((N_TILE_ROWS, LANES), I32),
                        pltpu.SMEM((N_TILE_ROWS,), I32), pltpu.SemaphoreType.DMA(())],
        compiler_params=_cparams(3),
        name="nsa_attention",
    )(qa, gates_t, kca, vct, kas, vst, kaw, vwt, selmapt, wbias, tilemap, cbias)


def _mixer_out_body(x_ref, oa_ref, cbv_ref, halo_ref, gab_ref, cw_ref, wpa_ref, wpb_ref, wo_ref, o_ref,
                    *, seq_len):
    i = pl.program_id(0)
    tm = x_ref.shape[0]
    cwd = CONV_WIDTH
    d = x_ref.shape[1]
    v = cbv_ref[:, cwd:2 * cwd].astype(F32)
    prev = halo_ref[:, cwd:2 * cwd].astype(F32)
    keep = ((i * tm) % seq_len != 0).astype(F32)
    p1 = prev[7:8, :] * keep
    p2 = prev[6:7, :] * keep
    ridx = lax.broadcasted_iota(I32, (tm, cwd), 0)
    v1 = jnp.where(ridx == 0, p1, pltpu.roll(v, 1, 0))
    v2 = jnp.where(ridx == 0, p2, jnp.where(ridx == 1, p1, pltpu.roll(v, 2, 0)))
    y = cw_ref[0:1, :] * v2 + cw_ref[1:2, :] * v1 + cw_ref[2:3, :] * v
    yb_in = (cbv_ref[:, 0:cwd].astype(F32) * y).astype(BF16)
    y_a = _dot(oa_ref[...], wpa_ref[...])
    y_b = _dot(yb_in, wpb_ref[...])
    merged = gab_ref[:, 0:d].astype(F32) * y_a + gab_ref[:, d:2 * d].astype(F32) * y_b
    o_ref[...] = x_ref[...] + _dot(merged.astype(BF16), wo_ref[...])


def _mixer_out(x2, oa, cbv, gab, conv_w, w_pa, w_pb, w_o, seq_len):
    T, D = x2.shape
    tm = ROW_TILE
    cw8 = jnp.pad(conv_w, ((0, 8 - CONV_K), (0, 0)))
    row = lambda w: pl.BlockSpec((tm, w), lambda i: (i, 0))
    full = lambda a: pl.BlockSpec(a.shape, lambda i: (0,) * a.ndim)
    halo = pl.BlockSpec((8, cbv.shape[1]), lambda i: (jnp.maximum(i * (tm // 8) - 1, 0), 0))
    wts = (cw8, w_pa.astype(BF16), w_pb.astype(BF16), w_o.astype(BF16))
    return pl.pallas_call(
        functools.partial(_mixer_out_body, seq_len=seq_len),
        grid=(T // tm,),
        in_specs=[row(D), row(oa.shape[1]), row(cbv.shape[1]), halo, row(gab.shape[1])] + [full(a) for a in wts],
        out_specs=row(D),
        out_shape=jax.ShapeDtypeStruct((T, D), F32),
        compiler_params=_cparams(1),
        name="mixer_out",
    )(x2, oa, cbv, cbv, gab, *wts)


def _router_body(x_ref, g2_ref, whi_ref, wlo_ref, br_ref, tri_ref, h_out, mi_out, mf_out, cnt_out):
    i = pl.program_id(0)

    @pl.when(i == 0)
    def _():
        cnt_out[...] = jnp.zeros(cnt_out.shape, F32)

    x = x_ref[...]
    tm = x.shape[0]
    ms = jnp.mean(x * x, axis=-1, keepdims=True)
    h = x * lax.rsqrt(ms + EPS) * g2_ref[...]
    _store_row_tiles(h_out, h)
    h_hi = h.astype(BF16)
    h_lo = (h - h_hi.astype(F32)).astype(BF16)
    logits = (_dot(h_hi, whi_ref[...]) + _dot(h_lo, whi_ref[...]) + _dot(h_hi, wlo_ref[...])) + br_ref[...]
    lane = lax.broadcasted_iota(I32, (tm, LANES), 1)
    lanef = lane.astype(F32)
    work = jnp.where(lane < N_EXPERTS, logits, -jnp.inf)
    vals, hits = [], []
    for _ in range(TOP_K):
        mx = jnp.max(work, axis=-1, keepdims=True)
        first = jnp.min(jnp.where(work == mx, lanef, float(LANES)), axis=-1, keepdims=True)
        hit = lanef == first
        vals.append(mx)
        hits.append(hit)
        work = jnp.where(hit, -jnp.inf, work)
    ex = [jnp.exp(v - vals[0]) for v in vals]
    den = ex[0]
    for e in ex[1:]:
        den = den + e
    cnt = jnp.zeros((tm, LANES), F32)
    for hit in hits:
        cnt = cnt + hit.astype(F32)
    before = _dot(tri_ref[...], cnt.astype(BF16)) + cnt_out[0:1, :]
    mi = jnp.zeros((tm, LANES), F32)
    mf = jnp.zeros((tm, LANES), F32)
    for k, hit in enumerate(hits):
        e_k = jnp.sum(jnp.where(hit, lanef, 0.0), axis=-1, keepdims=True)
        r_k = jnp.sum(jnp.where(hit, before, 0.0), axis=-1, keepdims=True)
        mi = jnp.where(lane == k, e_k, jnp.where(lane == TOP_K + k, r_k, mi))
        mf = jnp.where(lane == k, ex[k] / den, mf)
    mi_out[...] = mi[:, 0:2 * TOP_K].astype(I32)
    mf_out[...] = mf
    cnt_out[...] = cnt_out[...] + jnp.sum(cnt, axis=0, keepdims=True)


def _router(x1, g2, w_r, b_r):
    T, D = x1.shape
    assert D == ROW_TILES * LANES
    tm = ROW_TILE
    wpad = jnp.pad(w_r, ((0, 0), (0, LANES - N_EXPERTS)))
    whi = wpad.astype(BF16)
    wlo = (wpad - whi.astype(F32)).astype(BF16)
    br = jnp.pad(b_r, (0, LANES - N_EXPERTS)).reshape(1, LANES)
    tri = jnp.asarray(np.tril(np.ones((tm, tm), np.float32), -1), BF16)
    row = lambda w: pl.BlockSpec((tm, w), lambda i: (i, 0))
    full = lambda a: pl.BlockSpec(a.shape, lambda i: (0,) * a.ndim)
    ins = (x1, g2.reshape(1, D), whi, wlo, br, tri)
    return pl.pallas_call(
        _router_body,
        grid=(T // tm,),
        in_specs=[row(D)] + [full(a) for a in ins[1:]],
        out_specs=[pl.BlockSpec((tm * ROW_TILES, LANES), lambda i: (i, 0)), row(2 * TOP_K), row(LANES),
                   pl.BlockSpec((8, LANES), lambda i: (0, 0))],
        out_shape=[jax.ShapeDtypeStruct((T * ROW_TILES, LANES), F32), jax.ShapeDtypeStruct((T, 2 * TOP_K), I32),
                   jax.ShapeDtypeStruct((T, LANES), F32), jax.ShapeDtypeStruct((8, LANES), F32)],
        compiler_params=_cparams(1),
        name="router",
    )(*ins)


DISPATCH_TILE = 512
DMA_UNROLL = 8


def _dispatch_body(dest_ref, last_ref, h_ref, o_hbm, zero_ref, sem, zsem):
    @pl.when(pl.program_id(0) == 0)
    def _():
        zero_ref[...] = jnp.zeros(zero_ref.shape, zero_ref.dtype)

        def clear(e):
            start = pl.multiple_of(last_ref[e] * ROW_TILES, MOE_CHUNK * ROW_TILES)
            return pltpu.make_async_copy(zero_ref, o_hbm.at[pl.ds(start, MOE_CHUNK * ROW_TILES)], zsem)

        for e in range(N_EXPERTS):
            @pl.when(last_ref[e] >= 0)
            def _():
                clear(e).start()
        for e in range(N_EXPERTS):
            @pl.when(last_ref[e] >= 0)
            def _():
                clear(e).wait()

    def row_copy(r, d):
        return pltpu.make_async_copy(h_ref.at[pl.ds(pl.multiple_of(r * ROW_TILES, ROW_TILES), ROW_TILES)],
                                     o_hbm.at[pl.ds(pl.multiple_of(d * ROW_TILES, ROW_TILES), ROW_TILES)], sem)

    def start(r, c):
        for k in range(TOP_K):
            row_copy(r, dest_ref[0, 0, r * TOP_K + k]).start(priority=k % 2)
        return c

    def wait(r, c):
        for k in range(TOP_K):
            row_copy(0, 0).wait()
        return c

    lax.fori_loop(0, DISPATCH_TILE, start, 0, unroll=DMA_UNROLL)
    lax.fori_loop(0, DISPATCH_TILE, wait, 0, unroll=DMA_UNROLL)


def _dispatch(h2, dest, last_chunk, n_rows):
    T = h2.shape[0] // ROW_TILES
    td = DISPATCH_TILE
    dest3 = dest.reshape(T // td, 1, td * TOP_K)
    return pl.pallas_call(
        _dispatch_body,
        grid=(T // td,),
        in_specs=[pl.BlockSpec((1, 1, td * TOP_K), lambda i: (i, 0, 0), memory_space=pltpu.SMEM),
                  pl.BlockSpec(memory_space=pltpu.SMEM),
                  pl.BlockSpec((td * ROW_TILES, LANES), lambda i: (i, 0))],
        out_specs=pl.BlockSpec(memory_space=pl.ANY),
        out_shape=jax.ShapeDtypeStruct((n_rows * ROW_TILES, LANES), h2.dtype),
        scratch_shapes=[pltpu.VMEM((MOE_CHUNK * ROW_TILES, LANES), h2.dtype), pltpu.SemaphoreType.DMA(()),
                        pltpu.SemaphoreType.DMA(())],
        compiler_params=_cparams(1),
        name="dispatch",
    )(dest3, last_chunk, h2)


def _expert_body(ce_ref, nu_ref, x_ref, wgu_ref, bgu_ref, wdn_ref, bdn_ref, o_ref, wgu_bf, wdn_bf):
    c = pl.program_id(0)
    dff = wdn_ref.shape[1]

    @pl.when((c == 0) | (ce_ref[c] != ce_ref[jnp.maximum(c - 1, 0)]))
    def _():
        wgu_bf[...] = wgu_ref[0].astype(BF16)
        wdn_bf[...] = wdn_ref[0].astype(BF16)

    @pl.when(c < nu_ref[0])
    def _():
        x = _load_row_tiles(x_ref, MOE_CHUNK)
        gu = _dot(x.astype(BF16), wgu_bf[...]) + bgu_ref[0]
        g = jnp.minimum(gu[:, 0:dff], SWIGLU_LIMIT)
        u = jnp.clip(gu[:, dff:2 * dff], -SWIGLU_LIMIT, SWIGLU_LIMIT)
        act = (u + 1.0) * (g * jax.nn.sigmoid(SWIGLU_ALPHA * g))
        _store_row_tiles(o_ref, _dot(act.astype(BF16), wdn_bf[...]) + bdn_ref[0])

    @pl.when(c >= nu_ref[0])
    def _():
        o_ref[...] = jnp.zeros(o_ref.shape, F32)


def _experts(hperm, chunk_e, n_used, w_gu, b_gu, w_dn, b_dn):
    E, D, F2 = w_gu.shape
    assert D == ROW_TILES * LANES
    P = hperm.shape[0] // ROW_TILES
    dff = F2 // 2
    n_chunks = P // MOE_CHUNK
    chunk = (MOE_CHUNK * ROW_TILES, LANES)
    grid_spec = pltpu.PrefetchScalarGridSpec(
        num_scalar_prefetch=2,
        grid=(n_chunks,),
        in_specs=[pl.BlockSpec(chunk, lambda c, ce, nu: (jnp.minimum(c, nu[0] - 1), 0)),
                  pl.BlockSpec((1, D, F2), lambda c, ce, nu: (ce[c], 0, 0)),
                  pl.BlockSpec((1, 1, F2), lambda c, ce, nu: (ce[c], 0, 0)),
                  pl.BlockSpec((1, dff, D), lambda c, ce, nu: (ce[c], 0, 0)),
                  pl.BlockSpec((1, 1, D), lambda c, ce, nu: (ce[c], 0, 0))],
        out_specs=pl.BlockSpec(chunk, lambda c, ce, nu: (c, 0)),
        scratch_shapes=[pltpu.VMEM((D, F2), BF16), pltpu.VMEM((dff, D), BF16)],
    )
    return pl.pallas_call(
        _expert_body,
        grid_spec=grid_spec,
        out_shape=jax.ShapeDtypeStruct(hperm.shape, F32),
        compiler_params=_cparams(1),
        name="experts",
    )(chunk_e, n_used, hperm, w_gu, b_gu.reshape(E, 1, F2), w_dn, b_dn.reshape(E, 1, D))


COMBINE_TILE = 256


def _combine_body(dest_ref, next_ref, x_ref, w_ref, y_hbm, o_ref, buf_ref, sems):
    i = pl.program_id(0)
    slot = i % 2

    def row_copy(s, r, k, d):
        return pltpu.make_async_copy(y_hbm.at[pl.ds(pl.multiple_of(d * ROW_TILES, ROW_TILES), ROW_TILES)],
                                     buf_ref.at[s, k, pl.ds(pl.multiple_of(r * ROW_TILES, ROW_TILES), ROW_TILES)],
                                     sems.at[s])

    def fetch(idx_ref, s):
        def start(r, c):
            for k in range(TOP_K):
                row_copy(s, r, k, idx_ref[0, 0, r * TOP_K + k]).start(priority=k % 2)
            return c

        lax.fori_loop(0, COMBINE_TILE, start, 0, unroll=DMA_UNROLL)

    @pl.when(i == 0)
    def _():
        fetch(dest_ref, slot)

    @pl.when(i + 1 < pl.num_programs(0))
    def _():
        fetch(next_ref, 1 - slot)

    def wait(r, c):
        for k in range(TOP_K):
            row_copy(slot, 0, 0, 0).wait()
        return c

    lax.fori_loop(0, COMBINE_TILE, wait, 0, unroll=DMA_UNROLL)
    gate = [jnp.broadcast_to(w_ref[:, k:k + 1], (COMBINE_TILE, LANES)) for k in range(TOP_K)]
    for s in range(ROW_TILES):
        sl = slice(s * LANES, (s + 1) * LANES)
        out = x_ref[:, sl]
        for k in range(TOP_K):
            out = out + gate[k] * buf_ref[slot, k, pl.ds(s, COMBINE_TILE, stride=ROW_TILES), :]
        o_ref[:, sl] = out


def _combine(x1, gate_w, dest, ys):
    T, D = x1.shape
    tc = COMBINE_TILE
    n = T // tc
    dest3 = dest.reshape(n, 1, tc * TOP_K)
    row = lambda w: pl.BlockSpec((tc, w), lambda i: (i, 0))
    idx = lambda f: pl.BlockSpec((1, 1, tc * TOP_K), f, memory_space=pltpu.SMEM)
    return pl.pallas_call(
        _combine_body,
        grid=(n,),
        in_specs=[idx(lambda i: (i, 0, 0)), idx(lambda i: (jnp.minimum(i + 1, n - 1), 0, 0)),
                  row(D), row(LANES), pl.BlockSpec(memory_space=pl.ANY)],
        out_specs=row(D),
        out_shape=jax.ShapeDtypeStruct((T, D), F32),
        scratch_shapes=[pltpu.VMEM((2, TOP_K, tc * ROW_TILES, LANES), F32), pltpu.SemaphoreType.DMA((2,))],
        compiler_params=_cparams(1),
        name="combine",
    )(dest3, dest3, x1, gate_w, ys)


def _mixer(x2, B, S, g_norm1, w_in, g_q, g_kc, g_ks, g_kw, pe_k, ck_w1, ck_b1, ck_w2, ck_b2,
           pe_v, cv_w1, cv_b1, cv_w2, cv_b2, conv_w, w_pa, w_pb, w_o):
    T, D = x2.shape
    G, H, hd = N_KV_HEADS, N_HEADS, HEAD_DIM
    qa, hk, hv, kas, kaw, vsw, gates, cbv, gab = _inproj(x2, g_norm1, w_in, g_q, g_ks, g_kw, S)
    nh = S // CMP_STRIDE
    kca = _compress(hk.reshape(B, nh, -1), pe_k, ck_w1, ck_b1, ck_w2, ck_b2, g_kc, True)
    vct = _compress(hv.reshape(B, nh, -1), pe_v, cv_w1, cv_b1, cv_w2, cv_b2, jnp.ones((hd,), F32), False)
    vsw5 = vsw.reshape(B, S, 2, G, hd)
    ones_rows = jnp.concatenate([jnp.ones((1,), BF16), jnp.zeros((V_ROWS - hd - 1,), BF16)])

    def key_major(v, tile):
        vt = v.reshape(B, S // tile, tile, G, hd).transpose(0, 3, 1, 4, 2)
        extra = jnp.broadcast_to(ones_rows[None, None, None, :, None], vt.shape[:3] + (V_ROWS - hd, tile))
        return jnp.concatenate([vt, extra], axis=3)

    vst = key_major(vsw5[:, :, 0], KEY_TILE)
    vwt = key_major(vsw5[:, :, 1], Q_TILE)
    gat = gates[:, :3 * H].reshape(B, S, G, 3 * HEADS_PER_GROUP).transpose(0, 2, 3, 1)
    gat = jnp.pad(gat, ((0, 0), (0, 0), (0, 16 - 3 * HEADS_PER_GROUP), (0, 0)))
    o = _attention(qa.reshape(B, S, -1), gat, kca, vct, kas.reshape(B, S, -1), vst, kaw.reshape(B, S, -1),
                   vwt, B, S)
    return _mixer_out(x2, o.reshape(T, H * hd), cbv, gab, conv_w, w_pa, w_pb, w_o, S)


def _moe(x1, g_norm2, w_r, b_r, w_gu, b_gu, w_dn, b_dn):
    T, D = x1.shape
    h2, mi, mf, cnt = _router(x1, g_norm2, w_r, b_r)
    top_e = mi[:, 0:TOP_K]
    rank = mi[:, TOP_K:2 * TOP_K]
    counts = cnt[0, :N_EXPERTS].astype(I32)
    padded = (counts + MOE_CHUNK - 1) // MOE_CHUNK * MOE_CHUNK
    pend = jnp.cumsum(padded)
    poffs = pend - padded
    dest = (poffs[top_e] + rank).reshape(-1)
    n_chunks = (T * TOP_K + MOE_CHUNK - 1) // MOE_CHUNK + N_EXPERTS
    chunk_start = jnp.arange(n_chunks, dtype=I32) * MOE_CHUNK
    chunk_e = jnp.minimum(jnp.sum((pend[None, :] <= chunk_start[:, None]).astype(I32), axis=1), N_EXPERTS - 1)
    n_used = (pend[-1:] // MOE_CHUNK).astype(I32)
    last_chunk = jnp.where(padded > 0, pend - MOE_CHUNK, -1).astype(I32)
    hperm = _dispatch(h2, dest, last_chunk, n_chunks * MOE_CHUNK)
    ys = _experts(hperm, chunk_e, n_used, w_gu, b_gu, w_dn, b_dn)
    return _combine(x1, mf, dest, ys)


def kernel(x, g_norm1, w_in, g_q, g_kc, g_ks, g_kw, pe_k, ck_w1, ck_b1, ck_w2, ck_b2, pe_v, cv_w1, cv_b1,
           cv_w2, cv_b2, conv_w, w_pa, w_pb, w_o, g_norm2, w_r, b_r, w_gu, b_gu, w_dn, b_dn):
    B, S, D = x.shape
    x2 = x.reshape(B * S, D)
    for l in range(g_norm1.shape[0]):
        x2 = _mixer(x2, B, S, g_norm1[l], w_in[l], g_q[l], g_kc[l], g_ks[l], g_kw[l], pe_k[l], ck_w1[l],
                    ck_b1[l], ck_w2[l], ck_b2[l], pe_v[l], cv_w1[l], cv_b1[l], cv_w2[l], cv_b2[l],
                    conv_w[l], w_pa[l], w_pb[l], w_o[l])
        x2 = _moe(x2, g_norm2[l], w_r[l], b_r[l], w_gu[l], b_gu[l], w_dn[l], b_dn[l])
    return x2.reshape(B, S, D)
```

```python
import functools

import numpy as np
import jax
import jax.numpy as jnp
from jax import lax
from jax.experimental import pallas as pl
from jax.experimental.pallas import tpu as pltpu

F32 = jnp.float32
BF16 = jnp.bfloat16
I32 = jnp.int32

N_HEADS = 8
HEAD_DIM = 64
N_KV_HEADS = 2
HEADS_PER_GROUP = N_HEADS // N_KV_HEADS
CMP_LEN = 32
CMP_STRIDE = 16
CMP_HID = 256
SEL_BLK = 64
SEL_TOPN = 16
WINDOW = 512
CONV_WIDTH = 512
CONV_K = 3
N_EXPERTS = 32
TOP_K = 4
SWIGLU_LIMIT = 7.0
SWIGLU_ALPHA = 1.702
MOE_CHUNK = 512
EPS = 1e-6
NEG = -1e30
N_FORCED = 3

LANES = 128
Q_TILE = 256
KEY_TILE = 512
LOG2E = float(np.log2(np.e))
N_SPLIT = 3
N_AUG = 4 * N_SPLIT + 1
ROW_TILE = 512
VMEM_LIMIT = 56 * 1024 * 1024
GROUP_LANES = HEADS_PER_GROUP * Q_TILE
WIN_KEYS = WINDOW + Q_TILE
N_TILE_ROWS = 16
V_ROWS = HEAD_DIM + 8


def _cparams(n_axes):
    return pltpu.CompilerParams(dimension_semantics=("arbitrary",) * n_axes,
                                vmem_limit_bytes=VMEM_LIMIT)


def _dot(a, b):
    return jnp.dot(a, b, preferred_element_type=F32)


def _dot_nt(a, b):
    return lax.dot_general(a, b, (((1,), (1,)), ((), ())), preferred_element_type=F32)


ROW_TILES = 8


def _store_row_tiles(ref, val):
    n = val.shape[0]
    for s in range(ROW_TILES):
        ref[pl.ds(s, n, stride=ROW_TILES), :] = val[:, s * LANES:(s + 1) * LANES]


def _load_row_tiles(ref, n):
    return jnp.concatenate([ref[pl.ds(s, n, stride=ROW_TILES), :] for s in range(ROW_TILES)], axis=1)


def _rms_pairs(v, bd):
    ss = _dot((v * v).astype(BF16), bd)
    return v * lax.rsqrt(ss + EPS)


def _inproj_body(x_ref, g1_ref, wq_ref, wkv_ref, wng_ref, wcv_ref, wmg_ref, gq_ref, gk_ref, bd_ref,
                 qtab_ref, kwtab_ref,
                 qa_out, hk_out, hv_out, kas_out, kaw_out, vsw_out, gate_out, cbv_out, gab_out, raw_ref):
    x = x_ref[...]
    tm = x.shape[0]
    ms = jnp.mean(x * x, axis=-1, keepdims=True)
    h = (x * lax.rsqrt(ms + EPS) * g1_ref[...]).astype(BF16)
    bd = bd_ref[...]
    low = lax.broadcasted_iota(I32, (tm, LANES), 1) < HEAD_DIM

    def place(pair, tab_ref, out_ref, base, slot):
        for j, src in enumerate((pair, pltpu.roll(pair, HEAD_DIM, 1))):
            o = base + j * slot
            out_ref[:, o:o + LANES] = jnp.where(low, src, tab_ref[:, o:o + LANES].astype(F32)).astype(BF16)

    q = _dot(h, wq_ref[...])
    for c in range(N_HEADS * HEAD_DIM // LANES):
        sl = slice(c * LANES, (c + 1) * LANES)
        place(_rms_pairs(q[:, sl], bd) * gq_ref[:, sl], qtab_ref, qa_out, 2 * c * LANES, LANES)
    kv = _dot(h, wkv_ref[...])

    def emit_half_blocks(c, out_ref):
        nb = tm // CMP_STRIDE
        hw = CMP_STRIDE * HEAD_DIM
        raw_ref[...] = kv[:, c * LANES:(c + 1) * LANES]
        lo = lax.broadcasted_iota(I32, (nb, LANES), 1) < HEAD_DIM
        for u in range(CMP_STRIDE // 2):
            t0 = raw_ref[pl.ds(2 * u, nb, stride=CMP_STRIDE), :]
            t1 = raw_ref[pl.ds(2 * u + 1, nb, stride=CMP_STRIDE), :]
            out_ref[:, u * LANES:(u + 1) * LANES] = jnp.where(lo, t0, pltpu.roll(t1, HEAD_DIM, 1)).astype(BF16)
            out_ref[:, hw + u * LANES:hw + (u + 1) * LANES] = (
                jnp.where(lo, pltpu.roll(t0, HEAD_DIM, 1), t1).astype(BF16))

    emit_half_blocks(0, hk_out)
    emit_half_blocks(1, hv_out)
    place(_rms_pairs(kv[:, 256:384], bd) * gk_ref[:, 0:128], kwtab_ref, kas_out, 0, LANES)
    place(_rms_pairs(kv[:, 512:640], bd) * gk_ref[:, 128:256], kwtab_ref, kaw_out, 0, LANES)
    vsw_out[:, 0:128] = kv[:, 384:512].astype(BF16)
    vsw_out[:, 128:256] = kv[:, 640:768].astype(BF16)
    gate_out[...] = jax.nn.sigmoid(_dot(h, wng_ref[...]))
    cv = _dot(h, wcv_ref[...])
    cw = CONV_WIDTH
    cbv_out[:, 0:cw] = cv[:, 0:cw].astype(BF16)
    cbv_out[:, cw:2 * cw] = (cv[:, cw:2 * cw] * cv[:, 2 * cw:3 * cw]).astype(BF16)
    gab_out[...] = jax.nn.sigmoid(_dot(h, wmg_ref[...])).astype(BF16)


def _bf16_pieces(v):
    def round_bf16(a):
        u = np.ascontiguousarray(a, np.float32).view(np.uint32).astype(np.uint64)
        return ((u + 0x7FFF + ((u >> 16) & 1)) & 0xFFFF0000).astype(np.uint32).view(np.float32)

    pieces, rest = [], np.asarray(v, np.float32)
    for _ in range(N_SPLIT):
        pieces.append(round_bf16(rest))
        rest = rest - pieces[-1]
    return pieces


def _key_aug(pos):
    one = np.ones_like(pos, np.float32)
    hi = (pos // 64 * 64).astype(np.float32)
    lo = (pos % 64).astype(np.float32)
    return np.stack([one] * (2 * N_SPLIT) + [hi] * N_SPLIT + [lo] * N_SPLIT + [0 * one], axis=-1)


def _slot_table(aug, slot):
    S, n, _ = aug.shape
    tab = np.zeros((S, n, slot), np.float32)
    tab[:, :, HEAD_DIM:HEAD_DIM + N_AUG] = aug
    return jnp.asarray(tab.reshape(S, n * slot), BF16)


def _inproj(x2, g1, w_in, g_q, g_ks, g_kw, S):
    T, D = x2.shape
    H, G = N_HEADS, N_KV_HEADS
    aw = H * HEAD_DIM
    kvw = G * HEAD_DIM
    o = 0
    wq = w_in[:, o:o + aw]; o += aw
    wkv = w_in[:, o:o + 6 * kvw]; o += 6 * kvw
    wng = w_in[:, o:o + 3 * H]; o += 3 * H
    wcv = w_in[:, o:o + 3 * CONV_WIDTH]; o += 3 * CONV_WIDTH
    wmg = w_in[:, o:o + 2 * D]
    wng = jnp.pad(wng, ((0, 0), (0, LANES - 3 * H)))
    wq, wkv, wng, wcv, wmg = (w.astype(BF16) for w in (wq, wkv, wng, wcv, wmg))
    gq = (jnp.tile(g_q, H) * (HEAD_DIM ** -0.5 * LOG2E)).reshape(1, aw)
    gk = jnp.concatenate([jnp.tile(g_ks, G), jnp.tile(g_kw, G)]).reshape(1, 2 * kvw)
    idx = np.arange(LANES) // HEAD_DIM
    bd = jnp.asarray((idx[:, None] == idx[None, :]).astype(np.float32) / HEAD_DIM, BF16)
    pos = np.arange(S)
    hi = (pos // 64 * 64).astype(np.float64)[:, None]
    lo = (pos % 64).astype(np.float64)[:, None]
    c = LOG2E * 2.0 ** (-8.0 * np.arange(1, H + 1) / H)[None, :]
    cs = np.broadcast_to(c, (S, H))
    aq = np.stack(_bf16_pieces(-c * hi) + _bf16_pieces(-c * lo) + _bf16_pieces(cs) + _bf16_pieces(cs)
                  + [np.ones((S, H), np.float32)], axis=-1)
    ak = np.broadcast_to(_key_aug(pos)[:, None, :], (S, G, N_AUG))
    qtab = _slot_table(aq, LANES)
    kwtab = _slot_table(ak, LANES)
    tm = ROW_TILE
    nst = S // tm
    row = lambda w: pl.BlockSpec((tm, w), lambda i: (i, 0))
    full = lambda a: pl.BlockSpec(a.shape, lambda i: (0,) * a.ndim)
    tab = lambda a: pl.BlockSpec((tm, a.shape[1]), lambda i: (i % nst, 0))
    ins = (x2, g1.reshape(1, D), wq, wkv, wng, wcv, wmg, gq, gk, bd, qtab, kwtab)
    widths = (H * LANES, G * LANES, G * LANES, 2 * kvw, LANES, 2 * CONV_WIDTH, 2 * D)
    dtypes = (BF16, BF16, BF16, BF16, F32, BF16, BF16)
    hw = G * CMP_STRIDE * HEAD_DIM
    nb = tm // CMP_STRIDE
    half = pl.BlockSpec((nb, hw), lambda i: (i, 0))
    half_shape = jax.ShapeDtypeStruct((T // CMP_STRIDE, hw), BF16)
    rows = [(row(w), jax.ShapeDtypeStruct((T, w), dt)) for w, dt in zip(widths, dtypes)]
    outs = [rows[0], (half, half_shape), (half, half_shape)] + rows[1:]
    return pl.pallas_call(
        _inproj_body,
        grid=(T // tm,),
        in_specs=[row(D)] + [full(a) for a in ins[1:10]] + [tab(a) for a in ins[10:]],
        out_specs=[o[0] for o in outs],
        out_shape=[o[1] for o in outs],
        scratch_shapes=[pltpu.VMEM((tm, LANES), F32)],
        compiler_params=_cparams(1),
        name="inproj",
    )(*ins)


def _compress_body(h_ref, w1_ref, pe_ref, b1_ref, w2_ref, b2_ref, g_ref, tab_ref, o_ref, *, for_keys):
    hb = h_ref[0]
    nc = hb.shape[0]
    a = _dot(hb, w1_ref[0])
    b = _dot(hb, w1_ref[1])
    c = _dot(pe_ref[0], w1_ref[0]) + _dot(pe_ref[1], w1_ref[1])
    pre = a + pltpu.roll(b, nc - 1, 0) + c[0:1, :] + b1_ref[...]
    hid = jax.nn.gelu(pre)
    out = _dot(hid.astype(BF16), w2_ref[...]) + b2_ref[...]
    if for_keys:
        ms = jnp.sum(out * out, axis=-1, keepdims=True) * (1.0 / HEAD_DIM)
        out = out * lax.rsqrt(ms + EPS) * g_ref[...]
        low = lax.broadcasted_iota(I32, out.shape, 1) < HEAD_DIM
        o_ref[0, 0] = jnp.where(low, out, tab_ref[...]).astype(BF16)
    else:
        o_ref[0, 0] = out.T[0:HEAD_DIM, :].astype(BF16)


def _compress(hh, pe, w1, b1, w2, b2, gain, for_keys):
    B, NC, _ = hh.shape
    G, HW = N_KV_HEADS, CMP_STRIDE * HEAD_DIM
    w1s = w1.reshape(2, HW, CMP_HID).astype(BF16)
    pes = jnp.broadcast_to(pe.reshape(2, 1, HW), (2, 8, HW)).astype(BF16)
    padl = lambda a: jnp.pad(a, ((0, 0), (0, LANES - HEAD_DIM)))
    tabn = np.zeros((NC, LANES), np.float32)
    tabn[:, HEAD_DIM:HEAD_DIM + N_AUG] = _key_aug(np.arange(NC) * CMP_STRIDE + (CMP_LEN - 1))
    tab = jnp.asarray(tabn)
    full = lambda a: pl.BlockSpec(a.shape, lambda b, g: (0,) * a.ndim)
    ins = (hh, w1s, pes, b1.reshape(1, CMP_HID), padl(w2).astype(BF16), padl(b2.reshape(1, HEAD_DIM)),
           padl(gain.reshape(1, HEAD_DIM)), tab)
    oshape = (B, G, NC, LANES) if for_keys else (B, G, HEAD_DIM, NC)
    return pl.pallas_call(
        functools.partial(_compress_body, for_keys=for_keys),
        grid=(B, G),
        in_specs=[pl.BlockSpec((1, NC, HW), lambda b, g: (b, 0, g))] + [full(a) for a in ins[1:]],
        out_specs=pl.BlockSpec((1, 1) + oshape[2:], lambda b, g: (b, g, 0, 0)),
        out_shape=jax.ShapeDtypeStruct(oshape, BF16),
        compiler_params=_cparams(2),
        name="compress_keys" if for_keys else "compress_values",
    )(*ins)


def _attn_body(qa_ref, g_ref, kca_ref, vct_ref, kas_ref, vst_ref, kaw_ref, vwt_ref, selmapt_ref, wbias_ref,
               tilemap_ref, cbias_ref, o_ref, selb_ref, m_ref, acc_ref, s0_ref, s1_ref, flagv_ref,
               flags_ref, list_ref, sem,
               *, n_sel):
    i = pl.program_id(2)
    q0 = i * Q_TILE
    gl = GROUP_LANES
    qa = jnp.concatenate([qa_ref[0, :, h * LANES:(h + 1) * LANES] for h in range(HEADS_PER_GROUP)], axis=0)

    nc = kca_ref.shape[2]
    s = _dot_nt(kca_ref[0, 0], qa)
    cb = cbias_ref[pl.ds(pl.multiple_of(nc - i * (Q_TILE // CMP_STRIDE), 8), nc), :]
    s = s + jnp.concatenate([cb] * HEADS_PER_GROUP, axis=1)
    m = jnp.max(s, axis=0, keepdims=True)
    p = jnp.exp2(s - m)
    l = jnp.sum(p, axis=0, keepdims=True)
    has_entry = (q0 + (lax.broadcasted_iota(I32, (1, gl), 1) & (Q_TILE - 1))) >= CMP_LEN - 1
    pc = p * jnp.where(has_entry, 1.0 / l, 0.0)
    o_c = _dot(vct_ref[0, 0], pc.astype(BF16))

    ps = pc[:, 0:Q_TILE]
    for h in range(1, HEADS_PER_GROUP):
        ps = ps + pc[:, h * Q_TILE:(h + 1) * Q_TILE]
    ps_hi = ps.astype(BF16)
    ps_lo = (ps - ps_hi.astype(F32)).astype(BF16)
    imp = _dot(selmapt_ref[...], ps_hi) + _dot(selmapt_ref[...], ps_lo)
    jb = lax.broadcasted_iota(I32, (LANES, Q_TILE), 0)
    cur = (q0 + lax.broadcasted_iota(I32, (LANES, Q_TILE), 1)) // SEL_BLK
    forced = (jb == 0) | (jb == cur) | (jb == cur - 1)
    score = jnp.where(forced, -jnp.inf, jnp.where(jb > cur, NEG, imp))
    jbf = jb.astype(F32)
    for _ in range(n_sel - N_FORCED):
        mx = jnp.max(score, axis=0, keepdims=True)
        first = jnp.min(jnp.where(score == mx, jbf, float(LANES)), axis=0, keepdims=True)
        score = jnp.where(jbf == first, -jnp.inf, score)
    picked = score == -jnp.inf
    bias_t = jnp.where(picked, 0.0, NEG)
    selb_ref[...] = jnp.concatenate([bias_t] * HEADS_PER_GROUP, axis=1)
    tile_hits = jnp.max(_dot(tilemap_ref[...], jnp.where(picked, 1.0, 0.0).astype(BF16)), axis=1, keepdims=True)
    flagv_ref[...] = jnp.broadcast_to(tile_hits, flagv_ref.shape).astype(I32)
    flag_copy = pltpu.make_async_copy(flagv_ref, flags_ref, sem)
    flag_copy.start()

    w0 = pl.multiple_of(jnp.maximum(q0 - WINDOW, 0), Q_TILE)
    sw = _dot_nt(kaw_ref[0, pl.ds(w0, WIN_KEYS), :], qa)
    wb = wbias_ref[jnp.minimum(i, WINDOW // Q_TILE)]
    sw = sw + jnp.concatenate([wb] * HEADS_PER_GROUP, axis=1)
    mw = jnp.max(sw, axis=0, keepdims=True)
    pw = jnp.exp2(sw - mw)
    c0 = w0 // Q_TILE
    vw = jnp.concatenate([vwt_ref[0, 0, c0 + j] for j in range(WIN_KEYS // Q_TILE)], axis=1)
    aw = _dot(vw, pw.astype(BF16))
    o_w = aw[0:HEAD_DIM] * (1.0 / aw[HEAD_DIM:HEAD_DIM + 1])

    def scores(kt):
        sc = _dot_nt(kas_ref[0, pl.ds(pl.multiple_of(kt * KEY_TILE, KEY_TILE), KEY_TILE), :], qa)
        blocks = KEY_TILE // SEL_BLK
        mask = [jnp.broadcast_to(selb_ref[pl.ds(kt * blocks + j, 1), :], (SEL_BLK, gl)) for j in range(blocks)]
        return sc + jnp.concatenate(mask, axis=0)

    n_full = q0 // KEY_TILE
    key = n_full * KEY_TILE + lax.broadcasted_iota(I32, (KEY_TILE, gl), 0)
    qry = q0 + (lax.broadcasted_iota(I32, (KEY_TILE, gl), 1) & (Q_TILE - 1))
    s0_ref[...] = jnp.where(key <= qry, scores(n_full), NEG)
    m_ref[...] = jnp.full(m_ref.shape, -3.0e38, F32)
    acc_ref[...] = jnp.zeros(acc_ref.shape, F32)
    list_ref[0] = n_full

    def absorb(s_ref, kt, live):
        sc = s_ref[...]
        m_old = m_ref[...]
        m_new = jnp.where(live, jnp.maximum(m_old, jnp.max(sc, axis=0, keepdims=True)), m_old)
        pv = _dot(vst_ref[0, 0, kt], jnp.exp2(sc - m_new).astype(BF16))
        acc_ref[...] = jnp.exp2(m_old - m_new) * acc_ref[...] + jnp.where(live, pv, 0.0)
        m_ref[...] = m_new

    flag_copy.wait()

    def compact(kt, n):
        active = flags_ref[kt, 0] > 0

        @pl.when(active)
        def _():
            list_ref[n] = kt

        return n + active.astype(I32)

    n_items = lax.fori_loop(0, n_full, compact, 1)

    last = n_items - 1

    def pair_body(j, carry):
        a = list_ref[2 * j]
        b = list_ref[jnp.minimum(2 * j + 1, last)]
        nxt = list_ref[jnp.minimum(2 * j + 2, last)]
        s1_ref[...] = scores(b)
        absorb(s0_ref, a, True)
        s0_ref[...] = scores(nxt)
        absorb(s1_ref, b, 2 * j + 1 <= last)
        return carry

    lax.fori_loop(0, (n_items + 1) // 2, pair_body, 0)

    o_s = acc_ref[0:HEAD_DIM, :] * (1.0 / acc_ref[HEAD_DIM:HEAD_DIM + 1, :])

    g = g_ref[0, 0]
    outs = []
    for h in range(HEADS_PER_GROUP):
        sl = slice(h * Q_TILE, (h + 1) * Q_TILE)
        outs.append(g[3 * h:3 * h + 1, :] * o_c[:, sl] + g[3 * h + 1:3 * h + 2, :] * o_s[:, sl]
                    + g[3 * h + 2:3 * h + 3, :] * o_w[:, sl])
    o_ref[0] = jnp.concatenate(outs, axis=0).T.astype(BF16)


def _attention(qa, gates_t, kca, vct, kas, vst, kaw, vwt, B, S):
    G, hd = N_KV_HEADS, HEAD_DIM
    NC = kca.shape[2]
    n_blk = S // SEL_BLK
    assert n_blk <= LANES and S % KEY_TILE == 0 and S >= WIN_KEYS
    n_sel = min(SEL_TOPN, n_blk)
    assert n_sel > N_FORCED
    ratio, span = SEL_BLK // CMP_STRIDE, CMP_LEN // CMP_STRIDE
    sm = np.zeros((LANES, NC), np.float32)
    for j in range(n_blk):
        for a in range(ratio):
            for b in range(span):
                n = ratio * j + a - b
                if 0 <= n < NC - 1:
                    sm[j, n] += 1.0
    selmapt = jnp.asarray(sm, BF16)
    c = np.arange(WIN_KEYS)[:, None]
    r = np.arange(Q_TILE)[None, :]
    offs = np.arange(WINDOW // Q_TILE + 1)[:, None, None] * Q_TILE
    wbias = jnp.asarray(np.where((c - r <= offs) & (c - r > offs - WINDOW), 0.0, NEG), F32)
    tilemap = jnp.asarray(np.arange(LANES)[None, :] // (KEY_TILE // SEL_BLK) == np.arange(N_TILE_ROWS)[:, None],
                          BF16)
    assert S // KEY_TILE <= N_TILE_ROWS
    u = np.arange(2 * NC)[:, None] - NC
    cbias = jnp.asarray(np.where(CMP_STRIDE * u + (CMP_LEN - 1) <= np.arange(Q_TILE)[None, :], 0.0, NEG), F32)

    hpg = HEADS_PER_GROUP
    grp = lambda *blk: pl.BlockSpec((1, 1) + blk, lambda b, g, i: (b, g) + (0,) * len(blk))
    seq = lambda w: pl.BlockSpec((1, S, w), lambda b, g, i: (b, 0, g))
    const = lambda a: pl.BlockSpec(a.shape, lambda b, g, i: (0,) * a.ndim)
    return pl.pallas_call(
        functools.partial(_attn_body, n_sel=n_sel),
        grid=(B, G, S // Q_TILE),
        in_specs=[pl.BlockSpec((1, Q_TILE, hpg * LANES), lambda b, g, i: (b, i, g)),
                  pl.BlockSpec((1, 1, 16, Q_TILE), lambda b, g, i: (b, g, 0, i)),
                  grp(NC, LANES), grp(hd, NC), seq(LANES), grp(S // KEY_TILE, V_ROWS, KEY_TILE),
                  seq(LANES), grp(S // Q_TILE, V_ROWS, Q_TILE), const(selmapt), const(wbias), const(tilemap),
                  const(cbias)],
        out_specs=pl.BlockSpec((1, Q_TILE, hpg * hd), lambda b, g, i: (b, i, g)),
        out_shape=jax.ShapeDtypeStruct((B, S, N_HEADS * hd), BF16),
        scratch_shapes=[pltpu.VMEM((LANES, GROUP_LANES), F32), pltpu.VMEM((1, GROUP_LANES), F32),
                        pltpu.VMEM((V_ROWS, GROUP_LANES), F32), pltpu.VMEM((KEY_TILE, GROUP_LANES), F32),
                        pltpu.VMEM((KEY_TILE, GROUP_LANES), F32),
                        pltpu.VMEM((N_TILE_ROWS, LANES), I32), pltpu.SMEM((N_TILE_ROWS, LANES), I32),
                        pltpu.SMEM((N_TILE_ROWS,), I32), pltpu.SemaphoreType.DMA(())],
        compiler_params=_cparams(3),
        name="nsa_attention",
    )(qa, gates_t, kca, vct, kas, vst, kaw, vwt, selmapt, wbias, tilemap, cbias)


def _mixer_out_body(x_ref, oa_ref, cbv_ref, halo_ref, gab_ref, cw_ref, wpa_ref, wpb_ref, wo_ref,
                    g2_ref, whi_ref, wlo_ref, br_ref, tri_ref, o_ref, h_out, mi_out, mf_out, cnt_out,
                    *, seq_len):
    i = pl.program_id(0)
    tm = x_ref.shape[0]
    cwd = CONV_WIDTH
    d = x_ref.shape[1]
    v = cbv_ref[:, cwd:2 * cwd].astype(F32)
    prev = halo_ref[:, cwd:2 * cwd].astype(F32)
    keep = ((i * tm) % seq_len != 0).astype(F32)
    p1 = prev[7:8, :] * keep
    p2 = prev[6:7, :] * keep
    ridx = lax.broadcasted_iota(I32, (tm, cwd), 0)
    v1 = jnp.where(ridx == 0, p1, pltpu.roll(v, 1, 0))
    v2 = jnp.where(ridx == 0, p2, jnp.where(ridx == 1, p1, pltpu.roll(v, 2, 0)))
    y = cw_ref[0:1, :] * v2 + cw_ref[1:2, :] * v1 + cw_ref[2:3, :] * v
    yb_in = (cbv_ref[:, 0:cwd].astype(F32) * y).astype(BF16)
    y_a = _dot(oa_ref[...], wpa_ref[...])
    y_b = _dot(yb_in, wpb_ref[...])
    merged = gab_ref[:, 0:d].astype(F32) * y_a + gab_ref[:, d:2 * d].astype(F32) * y_b
    x1 = x_ref[...] + _dot(merged.astype(BF16), wo_ref[...])
    o_ref[...] = x1
    _route(x1, g2_ref, whi_ref, wlo_ref, br_ref, tri_ref, h_out, mi_out, mf_out, cnt_out)


def _mixer_out(x2, oa, cbv, gab, conv_w, w_pa, w_pb, w_o, seq_len, g2, w_r, b_r):
    T, D = x2.shape
    assert D == ROW_TILES * LANES
    tm = ROW_TILE
    cw8 = jnp.pad(conv_w, ((0, 8 - CONV_K), (0, 0)))
    wpad = jnp.pad(w_r, ((0, 0), (0, LANES - N_EXPERTS)))
    whi = wpad.astype(BF16)
    wlo = (wpad - whi.astype(F32)).astype(BF16)
    br = jnp.pad(b_r, (0, LANES - N_EXPERTS)).reshape(1, LANES)
    tri = jnp.asarray(np.tril(np.ones((tm, tm), np.float32), -1), BF16)
    row = lambda w: pl.BlockSpec((tm, w), lambda i: (i, 0))
    full = lambda a: pl.BlockSpec(a.shape, lambda i: (0,) * a.ndim)
    halo = pl.BlockSpec((8, cbv.shape[1]), lambda i: (jnp.maximum(i * (tm // 8) - 1, 0), 0))
    wts = (cw8, w_pa.astype(BF16), w_pb.astype(BF16), w_o.astype(BF16), g2.reshape(1, D), whi, wlo, br, tri)
    return pl.pallas_call(
        functools.partial(_mixer_out_body, seq_len=seq_len),
        grid=(T // tm,),
        in_specs=[row(D), row(oa.shape[1]), row(cbv.shape[1]), halo, row(gab.shape[1])] + [full(a) for a in wts],
        out_specs=[row(D), pl.BlockSpec((tm * ROW_TILES, LANES), lambda i: (i, 0)), row(2 * TOP_K), row(LANES),
                   pl.BlockSpec((8, LANES), lambda i: (0, 0))],
        out_shape=[jax.ShapeDtypeStruct((T, D), F32), jax.ShapeDtypeStruct((T * ROW_TILES, LANES), F32),
                   jax.ShapeDtypeStruct((T, 2 * TOP_K), I32), jax.ShapeDtypeStruct((T, LANES), F32),
                   jax.ShapeDtypeStruct((8, LANES), F32)],
        compiler_params=_cparams(1),
        name="mixer_out_router",
    )(x2, oa, cbv, cbv, gab, *wts)


def _route(x, g2_ref, whi_ref, wlo_ref, br_ref, tri_ref, h_out, mi_out, mf_out, cnt_out):
    i = pl.program_id(0)

    @pl.when(i == 0)
    def _():
        cnt_out[...] = jnp.zeros(cnt_out.shape, F32)

    tm = x.shape[0]
    ms = jnp.mean(x * x, axis=-1, keepdims=True)
    h = x * lax.rsqrt(ms + EPS) * g2_ref[...]
    _store_row_tiles(h_out, h)
    h_hi = h.astype(BF16)
    h_lo = (h - h_hi.astype(F32)).astype(BF16)
    logits = (_dot(h_hi, whi_ref[...]) + _dot(h_lo, whi_ref[...]) + _dot(h_hi, wlo_ref[...])) + br_ref[...]
    lane = lax.broadcasted_iota(I32, (tm, LANES), 1)
    lanef = lane.astype(F32)
    work = jnp.where(lane < N_EXPERTS, logits, -jnp.inf)
    vals, hits = [], []
    for _ in range(TOP_K):
        mx = jnp.max(work, axis=-1, keepdims=True)
        first = jnp.min(jnp.where(work == mx, lanef, float(LANES)), axis=-1, keepdims=True)
        hit = lanef == first
        vals.append(mx)
        hits.append(hit)
        work = jnp.where(hit, -jnp.inf, work)
    ex = [jnp.exp(v - vals[0]) for v in vals]
    den = ex[0]
    for e in ex[1:]:
        den = den + e
    cnt = jnp.zeros((tm, LANES), F32)
    for hit in hits:
        cnt = cnt + hit.astype(F32)
    before = _dot(tri_ref[...], cnt.astype(BF16)) + cnt_out[0:1, :]
    mi = jnp.zeros((tm, LANES), F32)
    mf = jnp.zeros((tm, LANES), F32)
    for k, hit in enumerate(hits):
        e_k = jnp.sum(jnp.where(hit, lanef, 0.0), axis=-1, keepdims=True)
        r_k = jnp.sum(jnp.where(hit, before, 0.0), axis=-1, keepdims=True)
        mi = jnp.where(lane == k, e_k, jnp.where(lane == TOP_K + k, r_k, mi))
        mf = jnp.where(lane == k, ex[k] / den, mf)
    mi_out[...] = mi[:, 0:2 * TOP_K].astype(I32)
    mf_out[...] = mf
    cnt_out[...] = cnt_out[...] + jnp.sum(cnt, axis=0, keepdims=True)


DISPATCH_TILE = 512
DMA_UNROLL = 8


def _dispatch_body(dest_ref, last_ref, h_ref, o_hbm, zero_ref, sem, zsem):
    @pl.when(pl.program_id(0) == 0)
    def _():
        zero_ref[...] = jnp.zeros(zero_ref.shape, zero_ref.dtype)

        def clear(e):
            start = pl.multiple_of(last_ref[e] * ROW_TILES, MOE_CHUNK * ROW_TILES)
            return pltpu.make_async_copy(zero_ref, o_hbm.at[pl.ds(start, MOE_CHUNK * ROW_TILES)], zsem)

        for e in range(N_EXPERTS):
            @pl.when(last_ref[e] >= 0)
            def _():
                clear(e).start()
        for e in range(N_EXPERTS):
            @pl.when(last_ref[e] >= 0)
            def _():
                clear(e).wait()

    def row_copy(r, d):
        return pltpu.make_async_copy(h_ref.at[pl.ds(pl.multiple_of(r * ROW_TILES, ROW_TILES), ROW_TILES)],
                                     o_hbm.at[pl.ds(pl.multiple_of(d * ROW_TILES, ROW_TILES), ROW_TILES)], sem)

    def start(r, c):
        for k in range(TOP_K):
            row_copy(r, dest_ref[0, 0, r * TOP_K + k]).start(priority=k % 2)
        return c

    def wait(r, c):
        for k in range(TOP_K):
            row_copy(0, 0).wait()
        return c

    lax.fori_loop(0, DISPATCH_TILE, start, 0, unroll=DMA_UNROLL)
    lax.fori_loop(0, DISPATCH_TILE, wait, 0, unroll=DMA_UNROLL)


def _dispatch(h2, dest, last_chunk, n_rows):
    T = h2.shape[0] // ROW_TILES
    td = DISPATCH_TILE
    dest3 = dest.reshape(T // td, 1, td * TOP_K)
    return pl.pallas_call(
        _dispatch_body,
        grid=(T // td,),
        in_specs=[pl.BlockSpec((1, 1, td * TOP_K), lambda i: (i, 0, 0), memory_space=pltpu.SMEM),
                  pl.BlockSpec(memory_space=pltpu.SMEM),
                  pl.BlockSpec((td * ROW_TILES, LANES), lambda i: (i, 0))],
        out_specs=pl.BlockSpec(memory_space=pl.ANY),
        out_shape=jax.ShapeDtypeStruct((n_rows * ROW_TILES, LANES), h2.dtype),
        scratch_shapes=[pltpu.VMEM((MOE_CHUNK * ROW_TILES, LANES), h2.dtype), pltpu.SemaphoreType.DMA(()),
                        pltpu.SemaphoreType.DMA(())],
        compiler_params=_cparams(1),
        name="dispatch",
    )(dest3, last_chunk, h2)


def _expert_body(ce_ref, nu_ref, x_ref, wgu_ref, bgu_ref, wdn_ref, bdn_ref, o_ref, wgu_bf, wdn_bf):
    c = pl.program_id(0)
    dff = wdn_ref.shape[1]

    @pl.when((c == 0) | (ce_ref[c] != ce_ref[jnp.maximum(c - 1, 0)]))
    def _():
        wgu_bf[...] = wgu_ref[0].astype(BF16)
        wdn_bf[...] = wdn_ref[0].astype(BF16)

    @pl.when(c < nu_ref[0])
    def _():
        x = _load_row_tiles(x_ref, MOE_CHUNK)
        gu = _dot(x.astype(BF16), wgu_bf[...]) + bgu_ref[0]
        g = jnp.minimum(gu[:, 0:dff], SWIGLU_LIMIT)
        u = jnp.clip(gu[:, dff:2 * dff], -SWIGLU_LIMIT, SWIGLU_LIMIT)
        act = (u + 1.0) * (g * jax.nn.sigmoid(SWIGLU_ALPHA * g))
        _store_row_tiles(o_ref, _dot(act.astype(BF16), wdn_bf[...]) + bdn_ref[0])

    @pl.when(c >= nu_ref[0])
    def _():
        o_ref[...] = jnp.zeros(o_ref.shape, F32)


def _experts(hperm, chunk_e, n_used, w_gu, b_gu, w_dn, b_dn):
    E, D, F2 = w_gu.shape
    assert D == ROW_TILES * LANES
    P = hperm.shape[0] // ROW_TILES
    dff = F2 // 2
    n_chunks = P // MOE_CHUNK
    chunk = (MOE_CHUNK * ROW_TILES, LANES)
    grid_spec = pltpu.PrefetchScalarGridSpec(
        num_scalar_prefetch=2,
        grid=(n_chunks,),
        in_specs=[pl.BlockSpec(chunk, lambda c, ce, nu: (jnp.minimum(c, nu[0] - 1), 0)),
                  pl.BlockSpec((1, D, F2), lambda c, ce, nu: (ce[c], 0, 0)),
                  pl.BlockSpec((1, 1, F2), lambda c, ce, nu: (ce[c], 0, 0)),
                  pl.BlockSpec((1, dff, D), lambda c, ce, nu: (ce[c], 0, 0)),
                  pl.BlockSpec((1, 1, D), lambda c, ce, nu: (ce[c], 0, 0))],
        out_specs=pl.BlockSpec(chunk, lambda c, ce, nu: (c, 0)),
        scratch_shapes=[pltpu.VMEM((D, F2), BF16), pltpu.VMEM((dff, D), BF16)],
    )
    return pl.pallas_call(
        _expert_body,
        grid_spec=grid_spec,
        out_shape=jax.ShapeDtypeStruct(hperm.shape, F32),
        compiler_params=_cparams(1),
        name="experts",
    )(chunk_e, n_used, hperm, w_gu, b_gu.reshape(E, 1, F2), w_dn, b_dn.reshape(E, 1, D))


COMBINE_TILE = 256


def _combine_body(dest_ref, next_ref, x_ref, w_ref, y_hbm, o_ref, buf_ref, sems):
    i = pl.program_id(0)
    slot = i % 2

    def row_copy(s, r, k, d):
        return pltpu.make_async_copy(y_hbm.at[pl.ds(pl.multiple_of(d * ROW_TILES, ROW_TILES), ROW_TILES)],
                                     buf_ref.at[s, k, pl.ds(pl.multiple_of(r * ROW_TILES, ROW_TILES), ROW_TILES)],
                                     sems.at[s])

    def fetch(idx_ref, s):
        def start(r, c):
            for k in range(TOP_K):
                row_copy(s, r, k, idx_ref[0, 0, r * TOP_K + k]).start(priority=k % 2)
            return c

        lax.fori_loop(0, COMBINE_TILE, start, 0, unroll=DMA_UNROLL)

    @pl.when(i == 0)
    def _():
        fetch(dest_ref, slot)

    @pl.when(i + 1 < pl.num_programs(0))
    def _():
        fetch(next_ref, 1 - slot)

    def wait(r, c):
        for k in range(TOP_K):
            row_copy(slot, 0, 0, 0).wait()
        return c

    lax.fori_loop(0, COMBINE_TILE, wait, 0, unroll=DMA_UNROLL)
    gate = [jnp.broadcast_to(w_ref[:, k:k + 1], (COMBINE_TILE, LANES)) for k in range(TOP_K)]
    for s in range(ROW_TILES):
        sl = slice(s * LANES, (s + 1) * LANES)
        out = x_ref[:, sl]
        for k in range(TOP_K):
            out = out + gate[k] * buf_ref[slot, k, pl.ds(s, COMBINE_TILE, stride=ROW_TILES), :]
        o_ref[:, sl] = out


def _combine(x1, gate_w, dest, ys):
    T, D = x1.shape
    tc = COMBINE_TILE
    n = T // tc
    dest3 = dest.reshape(n, 1, tc * TOP_K)
    row = lambda w: pl.BlockSpec((tc, w), lambda i: (i, 0))
    idx = lambda f: pl.BlockSpec((1, 1, tc * TOP_K), f, memory_space=pltpu.SMEM)
    return pl.pallas_call(
        _combine_body,
        grid=(n,),
        in_specs=[idx(lambda i: (i, 0, 0)), idx(lambda i: (jnp.minimum(i + 1, n - 1), 0, 0)),
                  row(D), row(LANES), pl.BlockSpec(memory_space=pl.ANY)],
        out_specs=row(D),
        out_shape=jax.ShapeDtypeStruct((T, D), F32),
        scratch_shapes=[pltpu.VMEM((2, TOP_K, tc * ROW_TILES, LANES), F32), pltpu.SemaphoreType.DMA((2,))],
        compiler_params=_cparams(1),
        name="combine",
    )(dest3, dest3, x1, gate_w, ys)


def _mixer(x2, B, S, g_norm1, w_in, g_q, g_kc, g_ks, g_kw, pe_k, ck_w1, ck_b1, ck_w2, ck_b2,
           pe_v, cv_w1, cv_b1, cv_w2, cv_b2, conv_w, w_pa, w_pb, w_o, g_norm2, w_r, b_r):
    T, D = x2.shape
    G, H, hd = N_KV_HEADS, N_HEADS, HEAD_DIM
    qa, hk, hv, kas, kaw, vsw, gates, cbv, gab = _inproj(x2, g_norm1, w_in, g_q, g_ks, g_kw, S)
    nh = S // CMP_STRIDE
    kca = _compress(hk.reshape(B, nh, -1), pe_k, ck_w1, ck_b1, ck_w2, ck_b2, g_kc, True)
    vct = _compress(hv.reshape(B, nh, -1), pe_v, cv_w1, cv_b1, cv_w2, cv_b2, jnp.ones((hd,), F32), False)
    vsw5 = vsw.reshape(B, S, 2, G, hd)
    ones_rows = jnp.concatenate([jnp.ones((1,), BF16), jnp.zeros((V_ROWS - hd - 1,), BF16)])

    def key_major(v, tile):
        vt = v.reshape(B, S // tile, tile, G, hd).transpose(0, 3, 1, 4, 2)
        extra = jnp.broadcast_to(ones_rows[None, None, None, :, None], vt.shape[:3] + (V_ROWS - hd, tile))
        return jnp.concatenate([vt, extra], axis=3)

    vst = key_major(vsw5[:, :, 0], KEY_TILE)
    vwt = key_major(vsw5[:, :, 1], Q_TILE)
    gat = gates[:, :3 * H].reshape(B, S, G, 3 * HEADS_PER_GROUP).transpose(0, 2, 3, 1)
    gat = jnp.pad(gat, ((0, 0), (0, 0), (0, 16 - 3 * HEADS_PER_GROUP), (0, 0)))
    o = _attention(qa.reshape(B, S, -1), gat, kca, vct, kas.reshape(B, S, -1), vst, kaw.reshape(B, S, -1),
                   vwt, B, S)
    return _mixer_out(x2, o.reshape(T, H * hd), cbv, gab, conv_w, w_pa, w_pb, w_o, S, g_norm2, w_r, b_r)


def _moe(x1, h2, mi, mf, cnt, w_gu, b_gu, w_dn, b_dn):
    T, D = x1.shape
    top_e = mi[:, 0:TOP_K].reshape(-1)
    rank = mi[:, TOP_K:2 * TOP_K].reshape(-1)
    counts = cnt[0, :N_EXPERTS].astype(I32)
    padded = (counts + MOE_CHUNK - 1) // MOE_CHUNK * MOE_CHUNK
    pend = jnp.cumsum(padded)
    poffs = pend - padded
    dest = poffs[top_e] + rank
    n_chunks = (T * TOP_K + MOE_CHUNK - 1) // MOE_CHUNK + N_EXPERTS
    chunk_start = jnp.arange(n_chunks, dtype=I32) * MOE_CHUNK
    chunk_e = jnp.minimum(jnp.sum((pend[None, :] <= chunk_start[:, None]).astype(I32), axis=1), N_EXPERTS - 1)
    n_used = (pend[-1:] // MOE_CHUNK).astype(I32)
    last_chunk = jnp.where(padded > 0, pend - MOE_CHUNK, -1).astype(I32)
    hperm = _dispatch(h2, dest, last_chunk, n_chunks * MOE_CHUNK)
    ys = _experts(hperm, chunk_e, n_used, w_gu, b_gu, w_dn, b_dn)
    return _combine(x1, mf, dest, ys)


def kernel(x, g_norm1, w_in, g_q, g_kc, g_ks, g_kw, pe_k, ck_w1, ck_b1, ck_w2, ck_b2, pe_v, cv_w1, cv_b1,
           cv_w2, cv_b2, conv_w, w_pa, w_pb, w_o, g_norm2, w_r, b_r, w_gu, b_gu, w_dn, b_dn):
    B, S, D = x.shape
    x2 = x.reshape(B * S, D)
    for l in range(g_norm1.shape[0]):
        routed = _mixer(x2, B, S, g_norm1[l], w_in[l], g_q[l], g_kc[l], g_ks[l], g_kw[l], pe_k[l], ck_w1[l],
                        ck_b1[l], ck_w2[l], ck_b2[l], pe_v[l], cv_w1[l], cv_b1[l], cv_w2[l], cv_b2[l],
                        conv_w[l], w_pa[l], w_pb[l], w_o[l], g_norm2[l], w_r[l], b_r[l])
        x2 = _moe(*routed, w_gu[l], b_gu[l], w_dn[l], b_dn[l])
    return x2.reshape(B, S, D)
```

```python
import functools

import numpy as np
import jax
import jax.numpy as jnp
from jax import lax
from jax.experimental import pallas as pl
from jax.experimental.pallas import tpu as pltpu

F32 = jnp.float32
BF16 = jnp.bfloat16
I32 = jnp.int32

N_HEADS = 8
HEAD_DIM = 64
N_KV_HEADS = 2
HEADS_PER_GROUP = N_HEADS // N_KV_HEADS
CMP_LEN = 32
CMP_STRIDE = 16
CMP_HID = 256
SEL_BLK = 64
SEL_TOPN = 16
WINDOW = 512
CONV_WIDTH = 512
CONV_K = 3
N_EXPERTS = 32
TOP_K = 4
SWIGLU_LIMIT = 7.0
SWIGLU_ALPHA = 1.702
MOE_CHUNK = 512
EPS = 1e-6
NEG = -1e30
N_FORCED = 3

LANES = 128
Q_TILE = 256
KEY_TILE = 512
LOG2E = float(np.log2(np.e))
N_SPLIT = 3
N_AUG = 4 * N_SPLIT + 1
ROW_TILE = 512
VMEM_LIMIT = 56 * 1024 * 1024
GROUP_LANES = HEADS_PER_GROUP * Q_TILE
WIN_KEYS = WINDOW + Q_TILE
N_TILE_ROWS = 16
V_ROWS = HEAD_DIM + 8


def _cparams(n_axes):
    return pltpu.CompilerParams(dimension_semantics=("arbitrary",) * n_axes,
                                vmem_limit_bytes=VMEM_LIMIT)


def _dot(a, b):
    return jnp.dot(a, b, preferred_element_type=F32)


def _dot_nt(a, b):
    return lax.dot_general(a, b, (((1,), (1,)), ((), ())), preferred_element_type=F32)


ROW_TILES = 8


def _store_row_tiles(ref, val):
    n = val.shape[0]
    for s in range(ROW_TILES):
        ref[pl.ds(s, n, stride=ROW_TILES), :] = val[:, s * LANES:(s + 1) * LANES]


def _load_row_tiles(ref, n):
    return jnp.concatenate([ref[pl.ds(s, n, stride=ROW_TILES), :] for s in range(ROW_TILES)], axis=1)


def _rms_pairs(v, bd):
    ss = _dot((v * v).astype(BF16), bd)
    return v * lax.rsqrt(ss + EPS)


def _inproj_body(x_ref, g1_ref, wq_ref, wkv_ref, wng_ref, wcv_ref, wmg_ref, gq_ref, gk_ref, bd_ref,
                 qtab_ref, kwtab_ref,
                 qa_out, hk_out, hv_out, kas_out, kaw_out, vsw_out, gate_out, cbv_out, gab_out, raw_ref):
    x = x_ref[...]
    tm = x.shape[0]
    ms = jnp.mean(x * x, axis=-1, keepdims=True)
    h = (x * lax.rsqrt(ms + EPS) * g1_ref[...]).astype(BF16)
    bd = bd_ref[...]
    low = lax.broadcasted_iota(I32, (tm, LANES), 1) < HEAD_DIM

    def place(pair, tab_ref, out_ref, base, slot):
        for j, src in enumerate((pair, pltpu.roll(pair, HEAD_DIM, 1))):
            o = base + j * slot
            out_ref[:, o:o + LANES] = jnp.where(low, src, tab_ref[:, o:o + LANES].astype(F32)).astype(BF16)

    q = _dot(h, wq_ref[...])
    for c in range(N_HEADS * HEAD_DIM // LANES):
        sl = slice(c * LANES, (c + 1) * LANES)
        place(_rms_pairs(q[:, sl], bd) * gq_ref[:, sl], qtab_ref, qa_out, 2 * c * LANES, LANES)
    kv = _dot(h, wkv_ref[...])

    def emit_half_blocks(c, out_ref):
        nb = tm // CMP_STRIDE
        hw = CMP_STRIDE * HEAD_DIM
        raw_ref[...] = kv[:, c * LANES:(c + 1) * LANES]
        lo = lax.broadcasted_iota(I32, (nb, LANES), 1) < HEAD_DIM
        for u in range(CMP_STRIDE // 2):
            t0 = raw_ref[pl.ds(2 * u, nb, stride=CMP_STRIDE), :]
            t1 = raw_ref[pl.ds(2 * u + 1, nb, stride=CMP_STRIDE), :]
            out_ref[:, u * LANES:(u + 1) * LANES] = jnp.where(lo, t0, pltpu.roll(t1, HEAD_DIM, 1)).astype(BF16)
            out_ref[:, hw + u * LANES:hw + (u + 1) * LANES] = (
                jnp.where(lo, pltpu.roll(t0, HEAD_DIM, 1), t1).astype(BF16))

    emit_half_blocks(0, hk_out)
    emit_half_blocks(1, hv_out)
    place(_rms_pairs(kv[:, 256:384], bd) * gk_ref[:, 0:128], kwtab_ref, kas_out, 0, LANES)
    place(_rms_pairs(kv[:, 512:640], bd) * gk_ref[:, 128:256], kwtab_ref, kaw_out, 0, LANES)
    vsw_out[:, 0:128] = kv[:, 384:512].astype(BF16)
    vsw_out[:, 128:256] = kv[:, 640:768].astype(BF16)
    gate_out[...] = jax.nn.sigmoid(_dot(h, wng_ref[...]))
    cv = _dot(h, wcv_ref[...])
    cw = CONV_WIDTH
    cbv_out[:, 0:cw] = cv[:, 0:cw].astype(BF16)
    cbv_out[:, cw:2 * cw] = (cv[:, cw:2 * cw] * cv[:, 2 * cw:3 * cw]).astype(BF16)
    gab_out[...] = jax.nn.sigmoid(_dot(h, wmg_ref[...])).astype(BF16)


def _bf16_pieces(v):
    def round_bf16(a):
        u = np.ascontiguousarray(a, np.float32).view(np.uint32).astype(np.uint64)
        return ((u + 0x7FFF + ((u >> 16) & 1)) & 0xFFFF0000).astype(np.uint32).view(np.float32)

    pieces, rest = [], np.asarray(v, np.float32)
    for _ in range(N_SPLIT):
        pieces.append(round_bf16(rest))
        rest = rest - pieces[-1]
    return pieces


def _key_aug(pos):
    one = np.ones_like(pos, np.float32)
    hi = (pos // 64 * 64).astype(np.float32)
    lo = (pos % 64).astype(np.float32)
    return np.stack([one] * (2 * N_SPLIT) + [hi] * N_SPLIT + [lo] * N_SPLIT + [0 * one], axis=-1)


def _slot_table(aug, slot):
    S, n, _ = aug.shape
    tab = np.zeros((S, n, slot), np.float32)
    tab[:, :, HEAD_DIM:HEAD_DIM + N_AUG] = aug
    return jnp.asarray(tab.reshape(S, n * slot), BF16)


def _inproj(x2, g1, w_in, g_q, g_ks, g_kw, S):
    T, D = x2.shape
    H, G = N_HEADS, N_KV_HEADS
    aw = H * HEAD_DIM
    kvw = G * HEAD_DIM
    o = 0
    wq = w_in[:, o:o + aw]; o += aw
    wkv = w_in[:, o:o + 6 * kvw]; o += 6 * kvw
    wng = w_in[:, o:o + 3 * H]; o += 3 * H
    wcv = w_in[:, o:o + 3 * CONV_WIDTH]; o += 3 * CONV_WIDTH
    wmg = w_in[:, o:o + 2 * D]
    wng = jnp.pad(wng, ((0, 0), (0, LANES - 3 * H)))
    wq, wkv, wng, wcv, wmg = (w.astype(BF16) for w in (wq, wkv, wng, wcv, wmg))
    gq = (jnp.tile(g_q, H) * (HEAD_DIM ** -0.5 * LOG2E)).reshape(1, aw)
    gk = jnp.concatenate([jnp.tile(g_ks, G), jnp.tile(g_kw, G)]).reshape(1, 2 * kvw)
    idx = np.arange(LANES) // HEAD_DIM
    bd = jnp.asarray((idx[:, None] == idx[None, :]).astype(np.float32) / HEAD_DIM, BF16)
    pos = np.arange(S)
    hi = (pos // 64 * 64).astype(np.float64)[:, None]
    lo = (pos % 64).astype(np.float64)[:, None]
    c = LOG2E * 2.0 ** (-8.0 * np.arange(1, H + 1) / H)[None, :]
    cs = np.broadcast_to(c, (S, H))
    aq = np.stack(_bf16_pieces(-c * hi) + _bf16_pieces(-c * lo) + _bf16_pieces(cs) + _bf16_pieces(cs)
                  + [np.ones((S, H), np.float32)], axis=-1)
    ak = np.broadcast_to(_key_aug(pos)[:, None, :], (S, G, N_AUG))
    qtab = _slot_table(aq, LANES)
    kwtab = _slot_table(ak, LANES)
    tm = ROW_TILE
    nst = S // tm
    row = lambda w: pl.BlockSpec((tm, w), lambda i: (i, 0))
    full = lambda a: pl.BlockSpec(a.shape, lambda i: (0,) * a.ndim)
    tab = lambda a: pl.BlockSpec((tm, a.shape[1]), lambda i: (i % nst, 0))
    ins = (x2, g1.reshape(1, D), wq, wkv, wng, wcv, wmg, gq, gk, bd, qtab, kwtab)
    widths = (H * LANES, G * LANES, G * LANES, 2 * kvw, LANES, 2 * CONV_WIDTH, 2 * D)
    dtypes = (BF16, BF16, BF16, BF16, F32, BF16, BF16)
    hw = G * CMP_STRIDE * HEAD_DIM
    nb = tm // CMP_STRIDE
    half = pl.BlockSpec((nb, hw), lambda i: (i, 0))
    half_shape = jax.ShapeDtypeStruct((T // CMP_STRIDE, hw), BF16)
    rows = [(row(w), jax.ShapeDtypeStruct((T, w), dt)) for w, dt in zip(widths, dtypes)]
    outs = [rows[0], (half, half_shape), (half, half_shape)] + rows[1:]
    return pl.pallas_call(
        _inproj_body,
        grid=(T // tm,),
        in_specs=[row(D)] + [full(a) for a in ins[1:10]] + [tab(a) for a in ins[10:]],
        out_specs=[o[0] for o in outs],
        out_shape=[o[1] for o in outs],
        scratch_shapes=[pltpu.VMEM((tm, LANES), F32)],
        compiler_params=_cparams(1),
        name="inproj",
    )(*ins)


def _compress_body(h_ref, w1_ref, pe_ref, b1_ref, w2_ref, b2_ref, g_ref, tab_ref, o_ref, *, for_keys):
    hb = h_ref[0]
    nc = hb.shape[0]
    a = _dot(hb, w1_ref[0])
    b = _dot(hb, w1_ref[1])
    c = _dot(pe_ref[0], w1_ref[0]) + _dot(pe_ref[1], w1_ref[1])
    pre = a + pltpu.roll(b, nc - 1, 0) + c[0:1, :] + b1_ref[...]
    hid = jax.nn.gelu(pre)
    out = _dot(hid.astype(BF16), w2_ref[...]) + b2_ref[...]
    if for_keys:
        ms = jnp.sum(out * out, axis=-1, keepdims=True) * (1.0 / HEAD_DIM)
        out = out * lax.rsqrt(ms + EPS) * g_ref[...]
        low = lax.broadcasted_iota(I32, out.shape, 1) < HEAD_DIM
        o_ref[0, 0] = jnp.where(low, out, tab_ref[...]).astype(BF16)
    else:
        o_ref[0, 0] = out.T[0:HEAD_DIM, :].astype(BF16)


def _compress(hh, pe, w1, b1, w2, b2, gain, for_keys):
    B, NC, _ = hh.shape
    G, HW = N_KV_HEADS, CMP_STRIDE * HEAD_DIM
    w1s = w1.reshape(2, HW, CMP_HID).astype(BF16)
    pes = jnp.broadcast_to(pe.reshape(2, 1, HW), (2, 8, HW)).astype(BF16)
    padl = lambda a: jnp.pad(a, ((0, 0), (0, LANES - HEAD_DIM)))
    tabn = np.zeros((NC, LANES), np.float32)
    tabn[:, HEAD_DIM:HEAD_DIM + N_AUG] = _key_aug(np.arange(NC) * CMP_STRIDE + (CMP_LEN - 1))
    tab = jnp.asarray(tabn)
    full = lambda a: pl.BlockSpec(a.shape, lambda b, g: (0,) * a.ndim)
    ins = (hh, w1s, pes, b1.reshape(1, CMP_HID), padl(w2).astype(BF16), padl(b2.reshape(1, HEAD_DIM)),
           padl(gain.reshape(1, HEAD_DIM)), tab)
    oshape = (B, G, NC, LANES) if for_keys else (B, G, HEAD_DIM, NC)
    return pl.pallas_call(
        functools.partial(_compress_body, for_keys=for_keys),
        grid=(B, G),
        in_specs=[pl.BlockSpec((1, NC, HW), lambda b, g: (b, 0, g))] + [full(a) for a in ins[1:]],
        out_specs=pl.BlockSpec((1, 1) + oshape[2:], lambda b, g: (b, g, 0, 0)),
        out_shape=jax.ShapeDtypeStruct(oshape, BF16),
        compiler_params=_cparams(2),
        name="compress_keys" if for_keys else "compress_values",
    )(*ins)


def _attn_body(qa_ref, g_ref, kca_ref, vct_ref, kas_ref, vst_ref, kaw_ref, vwt_ref, selmapt_ref, wbias_ref,
               tilemap_ref, cbias_ref, o_ref, selb_ref, m_ref, acc_ref, s0_ref, s1_ref, flagv_ref,
               flags_ref, list_ref, sem,
               *, n_sel):
    i = pl.program_id(2)
    q0 = i * Q_TILE
    gl = GROUP_LANES
    qa = jnp.concatenate([qa_ref[0, :, h * LANES:(h + 1) * LANES] for h in range(HEADS_PER_GROUP)], axis=0)

    nc = kca_ref.shape[2]
    s = _dot_nt(kca_ref[0, 0], qa)
    cb = cbias_ref[pl.ds(pl.multiple_of(nc - i * (Q_TILE // CMP_STRIDE), 8), nc), :]
    s = s + jnp.concatenate([cb] * HEADS_PER_GROUP, axis=1)
    m = jnp.max(s, axis=0, keepdims=True)
    p = jnp.exp2(s - m)
    l = jnp.sum(p, axis=0, keepdims=True)
    has_entry = (q0 + (lax.broadcasted_iota(I32, (1, gl), 1) & (Q_TILE - 1))) >= CMP_LEN - 1
    pc = p * jnp.where(has_entry, 1.0 / l, 0.0)
    o_c = _dot(vct_ref[0, 0], pc.astype(BF16))

    ps = pc[:, 0:Q_TILE]
    for h in range(1, HEADS_PER_GROUP):
        ps = ps + pc[:, h * Q_TILE:(h + 1) * Q_TILE]
    ps_hi = ps.astype(BF16)
    ps_lo = (ps - ps_hi.astype(F32)).astype(BF16)
    imp = _dot(selmapt_ref[...], ps_hi) + _dot(selmapt_ref[...], ps_lo)
    jb = lax.broadcasted_iota(I32, (LANES, Q_TILE), 0)
    cur = (q0 + lax.broadcasted_iota(I32, (LANES, Q_TILE), 1)) // SEL_BLK
    forced = (jb == 0) | (jb == cur) | (jb == cur - 1)
    score = jnp.where(forced, -jnp.inf, jnp.where(jb > cur, NEG, imp))
    jbf = jb.astype(F32)
    for _ in range(n_sel - N_FORCED):
        mx = jnp.max(score, axis=0, keepdims=True)
        first = jnp.min(jnp.where(score == mx, jbf, float(LANES)), axis=0, keepdims=True)
        score = jnp.where(jbf == first, -jnp.inf, score)
    picked = score == -jnp.inf
    bias_t = jnp.where(picked, 0.0, NEG)
    selb_ref[...] = jnp.concatenate([bias_t] * HEADS_PER_GROUP, axis=1)
    tile_hits = jnp.max(_dot(tilemap_ref[...], jnp.where(picked, 1.0, 0.0).astype(BF16)), axis=1, keepdims=True)
    flagv_ref[...] = jnp.broadcast_to(tile_hits, flagv_ref.shape).astype(I32)
    flag_copy = pltpu.make_async_copy(flagv_ref, flags_ref, sem)
    flag_copy.start()

    w0 = pl.multiple_of(jnp.maximum(q0 - WINDOW, 0), Q_TILE)
    sw = _dot_nt(kaw_ref[0, pl.ds(w0, WIN_KEYS), :], qa)
    wb = wbias_ref[jnp.minimum(i, WINDOW // Q_TILE)]
    sw = sw + jnp.concatenate([wb] * HEADS_PER_GROUP, axis=1)
    mw = jnp.max(sw, axis=0, keepdims=True)
    pw = jnp.exp2(sw - mw)
    c0 = w0 // Q_TILE
    vw = jnp.concatenate([vwt_ref[0, 0, c0 + j] for j in range(WIN_KEYS // Q_TILE)], axis=1)
    aw = _dot(vw, pw.astype(BF16))
    o_w = aw[0:HEAD_DIM] * (1.0 / aw[HEAD_DIM:HEAD_DIM + 1])

    def scores(kt):
        sc = _dot_nt(kas_ref[0, pl.ds(pl.multiple_of(kt * KEY_TILE, KEY_TILE), KEY_TILE), :], qa)
        blocks = KEY_TILE // SEL_BLK
        mask = [jnp.broadcast_to(selb_ref[pl.ds(kt * blocks + j, 1), :], (SEL_BLK, gl)) for j in range(blocks)]
        return sc + jnp.concatenate(mask, axis=0)

    n_full = q0 // KEY_TILE
    key = n_full * KEY_TILE + lax.broadcasted_iota(I32, (KEY_TILE, gl), 0)
    qry = q0 + (lax.broadcasted_iota(I32, (KEY_TILE, gl), 1) & (Q_TILE - 1))
    s0_ref[...] = jnp.where(key <= qry, scores(n_full), NEG)
    m_ref[...] = jnp.full(m_ref.shape, -3.0e38, F32)
    acc_ref[...] = jnp.zeros(acc_ref.shape, F32)
    list_ref[0] = n_full

    def absorb(s_ref, kt, live):
        sc = s_ref[...]
        m_old = m_ref[...]
        m_new = jnp.where(live, jnp.maximum(m_old, jnp.max(sc, axis=0, keepdims=True)), m_old)
        pv = _dot(vst_ref[0, 0, kt], jnp.exp2(sc - m_new).astype(BF16))
        acc_ref[...] = jnp.exp2(m_old - m_new) * acc_ref[...] + jnp.where(live, pv, 0.0)
        m_ref[...] = m_new

    flag_copy.wait()

    def compact(kt, n):
        active = flags_ref[kt, 0] > 0

        @pl.when(active)
        def _():
            list_ref[n] = kt

        return n + active.astype(I32)

    n_items = lax.fori_loop(0, n_full, compact, 1)

    last = n_items - 1

    def pair_body(j, carry):
        a = list_ref[2 * j]
        b = list_ref[jnp.minimum(2 * j + 1, last)]
        nxt = list_ref[jnp.minimum(2 * j + 2, last)]
        s1_ref[...] = scores(b)
        absorb(s0_ref, a, True)
        s0_ref[...] = scores(nxt)
        absorb(s1_ref, b, 2 * j + 1 <= last)
        return carry

    lax.fori_loop(0, (n_items + 1) // 2, pair_body, 0)

    o_s = acc_ref[0:HEAD_DIM, :] * (1.0 / acc_ref[HEAD_DIM:HEAD_DIM + 1, :])

    g = g_ref[0, 0]
    outs = []
    for h in range(HEADS_PER_GROUP):
        sl = slice(h * Q_TILE, (h + 1) * Q_TILE)
        outs.append(g[3 * h:3 * h + 1, :] * o_c[:, sl] + g[3 * h + 1:3 * h + 2, :] * o_s[:, sl]
                    + g[3 * h + 2:3 * h + 3, :] * o_w[:, sl])
    o_ref[0] = jnp.concatenate(outs, axis=0).T.astype(BF16)


def _attention(qa, gates_t, kca, vct, kas, vst, kaw, vwt, B, S):
    G, hd = N_KV_HEADS, HEAD_DIM
    NC = kca.shape[2]
    n_blk = S // SEL_BLK
    assert n_blk <= LANES and S % KEY_TILE == 0 and S >= WIN_KEYS
    n_sel = min(SEL_TOPN, n_blk)
    assert n_sel > N_FORCED
    ratio, span = SEL_BLK // CMP_STRIDE, CMP_LEN // CMP_STRIDE
    sm = np.zeros((LANES, NC), np.float32)
    for j in range(n_blk):
        for a in range(ratio):
            for b in range(span):
                n = ratio * j + a - b
                if 0 <= n < NC - 1:
                    sm[j, n] += 1.0
    selmapt = jnp.asarray(sm, BF16)
    c = np.arange(WIN_KEYS)[:, None]
    r = np.arange(Q_TILE)[None, :]
    offs = np.arange(WINDOW // Q_TILE + 1)[:, None, None] * Q_TILE
    wbias = jnp.asarray(np.where((c - r <= offs) & (c - r > offs - WINDOW), 0.0, NEG), F32)
    tilemap = jnp.asarray(np.arange(LANES)[None, :] // (KEY_TILE // SEL_BLK) == np.arange(N_TILE_ROWS)[:, None],
                          BF16)
    assert S // KEY_TILE <= N_TILE_ROWS
    u = np.arange(2 * NC)[:, None] - NC
    cbias = jnp.asarray(np.where(CMP_STRIDE * u + (CMP_LEN - 1) <= np.arange(Q_TILE)[None, :], 0.0, NEG), F32)

    hpg = HEADS_PER_GROUP
    grp = lambda *blk: pl.BlockSpec((1, 1) + blk, lambda b, g, i: (b, g) + (0,) * len(blk))
    seq = lambda w: pl.BlockSpec((1, S, w), lambda b, g, i: (b, 0, g))
    const = lambda a: pl.BlockSpec(a.shape, lambda b, g, i: (0,) * a.ndim)
    return pl.pallas_call(
        functools.partial(_attn_body, n_sel=n_sel),
        grid=(B, G, S // Q_TILE),
        in_specs=[pl.BlockSpec((1, Q_TILE, hpg * LANES), lambda b, g, i: (b, i, g)),
                  pl.BlockSpec((1, 1, 16, Q_TILE), lambda b, g, i: (b, g, 0, i)),
                  grp(NC, LANES), grp(hd, NC), seq(LANES), grp(S // KEY_TILE, V_ROWS, KEY_TILE),
                  seq(LANES), grp(S // Q_TILE, V_ROWS, Q_TILE), const(selmapt), const(wbias), const(tilemap),
                  const(cbias)],
        out_specs=pl.BlockSpec((1, Q_TILE, hpg * hd), lambda b, g, i: (b, i, g)),
        out_shape=jax.ShapeDtypeStruct((B, S, N_HEADS * hd), BF16),
        scratch_shapes=[pltpu.VMEM((LANES, GROUP_LANES), F32), pltpu.VMEM((1, GROUP_LANES), F32),
                        pltpu.VMEM((V_ROWS, GROUP_LANES), F32), pltpu.VMEM((KEY_TILE, GROUP_LANES), F32),
                        pltpu.VMEM((KEY_TILE, GROUP_LANES), F32),
                        pltpu.VMEM((N_TILE_ROWS, LANES), I32), pltpu.SMEM((N_TILE_ROWS, LANES), I32),
                        pltpu.SMEM((N_TILE_ROWS,), I32), pltpu.SemaphoreType.DMA(())],
        compiler_params=_cparams(3),
        name="nsa_attention",
    )(qa, gates_t, kca, vct, kas, vst, kaw, vwt, selmapt, wbias, tilemap, cbias)


def _mixer_out_body(x_ref, oa_ref, cbv_ref, halo_ref, gab_ref, cw_ref, wpa_ref, wpb_ref, wo_ref,
                    g2_ref, whi_ref, wlo_ref, br_ref, tri_ref, o_ref, h_out, mi_out, mf_out, cnt_out,
                    *, seq_len):
    i = pl.program_id(0)
    tm = x_ref.shape[0]
    cwd = CONV_WIDTH
    d = x_ref.shape[1]
    v = cbv_ref[:, cwd:2 * cwd].astype(F32)
    prev = halo_ref[:, cwd:2 * cwd].astype(F32)
    keep = ((i * tm) % seq_len != 0).astype(F32)
    p1 = prev[7:8, :] * keep
    p2 = prev[6:7, :] * keep
    ridx = lax.broadcasted_iota(I32, (tm, cwd), 0)
    v1 = jnp.where(ridx == 0, p1, pltpu.roll(v, 1, 0))
    v2 = jnp.where(ridx == 0, p2, jnp.where(ridx == 1, p1, pltpu.roll(v, 2, 0)))
    y = cw_ref[0:1, :] * v2 + cw_ref[1:2, :] * v1 + cw_ref[2:3, :] * v
    yb_in = (cbv_ref[:, 0:cwd].astype(F32) * y).astype(BF16)
    y_a = _dot(oa_ref[...], wpa_ref[...])
    y_b = _dot(yb_in, wpb_ref[...])
    merged = gab_ref[:, 0:d].astype(F32) * y_a + gab_ref[:, d:2 * d].astype(F32) * y_b
    x1 = x_ref[...] + _dot(merged.astype(BF16), wo_ref[...])
    o_ref[...] = x1
    _route(x1, g2_ref, whi_ref, wlo_ref, br_ref, tri_ref, h_out, mi_out, mf_out, cnt_out)


def _mixer_out(x2, oa, cbv, gab, conv_w, w_pa, w_pb, w_o, seq_len, g2, w_r, b_r):
    T, D = x2.shape
    assert D == ROW_TILES * LANES
    tm = ROW_TILE
    cw8 = jnp.pad(conv_w, ((0, 8 - CONV_K), (0, 0)))
    wpad = jnp.pad(w_r, ((0, 0), (0, LANES - N_EXPERTS)))
    whi = wpad.astype(BF16)
    wlo = (wpad - whi.astype(F32)).astype(BF16)
    br = jnp.pad(b_r, (0, LANES - N_EXPERTS)).reshape(1, LANES)
    tri = jnp.asarray(np.tril(np.ones((tm, tm), np.float32), -1), BF16)
    row = lambda w: pl.BlockSpec((tm, w), lambda i: (i, 0))
    full = lambda a: pl.BlockSpec(a.shape, lambda i: (0,) * a.ndim)
    halo = pl.BlockSpec((8, cbv.shape[1]), lambda i: (jnp.maximum(i * (tm // 8) - 1, 0), 0))
    wts = (cw8, w_pa.astype(BF16), w_pb.astype(BF16), w_o.astype(BF16), g2.reshape(1, D), whi, wlo, br, tri)
    return pl.pallas_call(
        functools.partial(_mixer_out_body, seq_len=seq_len),
        grid=(T // tm,),
        in_specs=[row(D), row(oa.shape[1]), row(cbv.shape[1]), halo, row(gab.shape[1])] + [full(a) for a in wts],
        out_specs=[row(D), pl.BlockSpec((tm * ROW_TILES, LANES), lambda i: (i, 0)),
                   pl.BlockSpec((2 * TOP_K, tm), lambda i: (0, i)), row(LANES),
                   pl.BlockSpec((8, LANES), lambda i: (0, 0))],
        out_shape=[jax.ShapeDtypeStruct((T, D), F32), jax.ShapeDtypeStruct((T * ROW_TILES, LANES), F32),
                   jax.ShapeDtypeStruct((2 * TOP_K, T), I32), jax.ShapeDtypeStruct((T, LANES), F32),
                   jax.ShapeDtypeStruct((8, LANES), F32)],
        compiler_params=_cparams(1),
        name="mixer_out_router",
    )(x2, oa, cbv, cbv, gab, *wts)


def _route(x, g2_ref, whi_ref, wlo_ref, br_ref, tri_ref, h_out, mi_out, mf_out, cnt_out):
    i = pl.program_id(0)

    @pl.when(i == 0)
    def _():
        cnt_out[...] = jnp.zeros(cnt_out.shape, F32)

    tm = x.shape[0]
    ms = jnp.mean(x * x, axis=-1, keepdims=True)
    h = x * lax.rsqrt(ms + EPS) * g2_ref[...]
    _store_row_tiles(h_out, h)
    h_hi = h.astype(BF16)
    h_lo = (h - h_hi.astype(F32)).astype(BF16)
    logits = (_dot(h_hi, whi_ref[...]) + _dot(h_lo, whi_ref[...]) + _dot(h_hi, wlo_ref[...])) + br_ref[...]
    lane = lax.broadcasted_iota(I32, (tm, LANES), 1)
    lanef = lane.astype(F32)
    work = jnp.where(lane < N_EXPERTS, logits, -jnp.inf)
    vals, hits = [], []
    for _ in range(TOP_K):
        mx = jnp.max(work, axis=-1, keepdims=True)
        first = jnp.min(jnp.where(work == mx, lanef, float(LANES)), axis=-1, keepdims=True)
        hit = lanef == first
        vals.append(mx)
        hits.append(hit)
        work = jnp.where(hit, -jnp.inf, work)
    ex = [jnp.exp(v - vals[0]) for v in vals]
    den = ex[0]
    for e in ex[1:]:
        den = den + e
    cnt = jnp.zeros((tm, LANES), F32)
    for hit in hits:
        cnt = cnt + hit.astype(F32)
    before = _dot(tri_ref[...], cnt.astype(BF16)) + cnt_out[0:1, :]
    mi = jnp.zeros((tm, LANES), F32)
    mf = jnp.zeros((tm, LANES), F32)
    for k, hit in enumerate(hits):
        e_k = jnp.sum(jnp.where(hit, lanef, 0.0), axis=-1, keepdims=True)
        r_k = jnp.sum(jnp.where(hit, before, 0.0), axis=-1, keepdims=True)
        mi = jnp.where(lane == k, e_k, jnp.where(lane == TOP_K + k, r_k, mi))
        mf = jnp.where(lane == k, ex[k] / den, mf)
    mi_out[...] = mi.T[0:2 * TOP_K, :].astype(I32)
    mf_out[...] = mf
    cnt_out[...] = cnt_out[...] + jnp.sum(cnt, axis=0, keepdims=True)


DISPATCH_TILE = 512
DMA_UNROLL = 8


def _dispatch_body(dest_ref, last_ref, h_ref, o_hbm, zero_ref, sem, zsem):
    @pl.when(pl.program_id(0) == 0)
    def _():
        zero_ref[...] = jnp.zeros(zero_ref.shape, zero_ref.dtype)

        def clear(e):
            start = pl.multiple_of(last_ref[e] * ROW_TILES, MOE_CHUNK * ROW_TILES)
            return pltpu.make_async_copy(zero_ref, o_hbm.at[pl.ds(start, MOE_CHUNK * ROW_TILES)], zsem)

        for e in range(N_EXPERTS):
            @pl.when(last_ref[e] >= 0)
            def _():
                clear(e).start()
        for e in range(N_EXPERTS):
            @pl.when(last_ref[e] >= 0)
            def _():
                clear(e).wait()

    def row_copy(r, d):
        return pltpu.make_async_copy(h_ref.at[pl.ds(pl.multiple_of(r * ROW_TILES, ROW_TILES), ROW_TILES)],
                                     o_hbm.at[pl.ds(pl.multiple_of(d * ROW_TILES, ROW_TILES), ROW_TILES)], sem)

    def start(r, c):
        for k in range(TOP_K):
            row_copy(r, dest_ref[0, k, r]).start(priority=k % 2)
        return c

    def wait(r, c):
        for k in range(TOP_K):
            row_copy(0, 0).wait()
        return c

    lax.fori_loop(0, DISPATCH_TILE, start, 0, unroll=DMA_UNROLL)
    lax.fori_loop(0, DISPATCH_TILE, wait, 0, unroll=DMA_UNROLL)


def _dispatch(h2, dest, last_chunk, n_rows):
    T = h2.shape[0] // ROW_TILES
    td = DISPATCH_TILE
    dest3 = dest.reshape(TOP_K, T // td, td).transpose(1, 0, 2)
    return pl.pallas_call(
        _dispatch_body,
        grid=(T // td,),
        in_specs=[pl.BlockSpec((1, TOP_K, td), lambda i: (i, 0, 0), memory_space=pltpu.SMEM),
                  pl.BlockSpec(memory_space=pltpu.SMEM),
                  pl.BlockSpec((td * ROW_TILES, LANES), lambda i: (i, 0))],
        out_specs=pl.BlockSpec(memory_space=pl.ANY),
        out_shape=jax.ShapeDtypeStruct((n_rows * ROW_TILES, LANES), h2.dtype),
        scratch_shapes=[pltpu.VMEM((MOE_CHUNK * ROW_TILES, LANES), h2.dtype), pltpu.SemaphoreType.DMA(()),
                        pltpu.SemaphoreType.DMA(())],
        compiler_params=_cparams(1),
        name="dispatch",
    )(dest3, last_chunk, h2)


def _expert_body(ce_ref, nu_ref, x_ref, wgu_ref, bgu_ref, wdn_ref, bdn_ref, o_ref, wgu_bf, wdn_bf):
    c = pl.program_id(0)
    dff = wdn_ref.shape[1]

    @pl.when((c == 0) | (ce_ref[c] != ce_ref[jnp.maximum(c - 1, 0)]))
    def _():
        wgu_bf[...] = wgu_ref[0].astype(BF16)
        wdn_bf[...] = wdn_ref[0].astype(BF16)

    @pl.when(c < nu_ref[0])
    def _():
        x = _load_row_tiles(x_ref, MOE_CHUNK)
        gu = _dot(x.astype(BF16), wgu_bf[...]) + bgu_ref[0]
        g = jnp.minimum(gu[:, 0:dff], SWIGLU_LIMIT)
        u = jnp.clip(gu[:, dff:2 * dff], -SWIGLU_LIMIT, SWIGLU_LIMIT)
        act = (u + 1.0) * (g * jax.nn.sigmoid(SWIGLU_ALPHA * g))
        _store_row_tiles(o_ref, _dot(act.astype(BF16), wdn_bf[...]) + bdn_ref[0])

    @pl.when(c >= nu_ref[0])
    def _():
        o_ref[...] = jnp.zeros(o_ref.shape, F32)


def _experts(hperm, chunk_e, n_used, w_gu, b_gu, w_dn, b_dn):
    E, D, F2 = w_gu.shape
    assert D == ROW_TILES * LANES
    P = hperm.shape[0] // ROW_TILES
    dff = F2 // 2
    n_chunks = P // MOE_CHUNK
    chunk = (MOE_CHUNK * ROW_TILES, LANES)
    grid_spec = pltpu.PrefetchScalarGridSpec(
        num_scalar_prefetch=2,
        grid=(n_chunks,),
        in_specs=[pl.BlockSpec(chunk, lambda c, ce, nu: (jnp.minimum(c, nu[0] - 1), 0)),
                  pl.BlockSpec((1, D, F2), lambda c, ce, nu: (ce[c], 0, 0)),
                  pl.BlockSpec((1, 1, F2), lambda c, ce, nu: (ce[c], 0, 0)),
                  pl.BlockSpec((1, dff, D), lambda c, ce, nu: (ce[c], 0, 0)),
                  pl.BlockSpec((1, 1, D), lambda c, ce, nu: (ce[c], 0, 0))],
        out_specs=pl.BlockSpec(chunk, lambda c, ce, nu: (c, 0)),
        scratch_shapes=[pltpu.VMEM((D, F2), BF16), pltpu.VMEM((dff, D), BF16)],
    )
    return pl.pallas_call(
        _expert_body,
        grid_spec=grid_spec,
        out_shape=jax.ShapeDtypeStruct(hperm.shape, F32),
        compiler_params=_cparams(1),
        name="experts",
    )(chunk_e, n_used, hperm, w_gu, b_gu.reshape(E, 1, F2), w_dn, b_dn.reshape(E, 1, D))


COMBINE_TILE = 256


def _combine_body(dest_ref, next_ref, x_ref, w_ref, y_hbm, o_ref, buf_ref, sems):
    i = pl.program_id(0)
    slot = i % 2

    def row_copy(s, r, k, d):
        return pltpu.make_async_copy(y_hbm.at[pl.ds(pl.multiple_of(d * ROW_TILES, ROW_TILES), ROW_TILES)],
                                     buf_ref.at[s, k, pl.ds(pl.multiple_of(r * ROW_TILES, ROW_TILES), ROW_TILES)],
                                     sems.at[s])

    def fetch(idx_ref, s):
        def start(r, c):
            for k in range(TOP_K):
                row_copy(s, r, k, idx_ref[0, k, r]).start(priority=k % 2)
            return c

        lax.fori_loop(0, COMBINE_TILE, start, 0, unroll=DMA_UNROLL)

    @pl.when(i == 0)
    def _():
        fetch(dest_ref, slot)

    @pl.when(i + 1 < pl.num_programs(0))
    def _():
        fetch(next_ref, 1 - slot)

    def wait(r, c):
        for k in range(TOP_K):
            row_copy(slot, 0, 0, 0).wait()
        return c

    lax.fori_loop(0, COMBINE_TILE, wait, 0, unroll=DMA_UNROLL)
    gate = [jnp.broadcast_to(w_ref[:, k:k + 1], (COMBINE_TILE, LANES)) for k in range(TOP_K)]
    for s in range(ROW_TILES):
        sl = slice(s * LANES, (s + 1) * LANES)
        out = x_ref[:, sl]
        for k in range(TOP_K):
            out = out + gate[k] * buf_ref[slot, k, pl.ds(s, COMBINE_TILE, stride=ROW_TILES), :]
        o_ref[:, sl] = out


def _combine(x1, gate_w, dest, ys):
    T, D = x1.shape
    tc = COMBINE_TILE
    n = T // tc
    dest3 = dest.reshape(TOP_K, n, tc).transpose(1, 0, 2)
    row = lambda w: pl.BlockSpec((tc, w), lambda i: (i, 0))
    idx = lambda f: pl.BlockSpec((1, TOP_K, tc), f, memory_space=pltpu.SMEM)
    return pl.pallas_call(
        _combine_body,
        grid=(n,),
        in_specs=[idx(lambda i: (i, 0, 0)), idx(lambda i: (jnp.minimum(i + 1, n - 1), 0, 0)),
                  row(D), row(LANES), pl.BlockSpec(memory_space=pl.ANY)],
        out_specs=row(D),
        out_shape=jax.ShapeDtypeStruct((T, D), F32),
        scratch_shapes=[pltpu.VMEM((2, TOP_K, tc * ROW_TILES, LANES), F32), pltpu.SemaphoreType.DMA((2,))],
        compiler_params=_cparams(1),
        name="combine",
    )(dest3, dest3, x1, gate_w, ys)


def _mixer(x2, B, S, g_norm1, w_in, g_q, g_kc, g_ks, g_kw, pe_k, ck_w1, ck_b1, ck_w2, ck_b2,
           pe_v, cv_w1, cv_b1, cv_w2, cv_b2, conv_w, w_pa, w_pb, w_o, g_norm2, w_r, b_r):
    T, D = x2.shape
    G, H, hd = N_KV_HEADS, N_HEADS, HEAD_DIM
    qa, hk, hv, kas, kaw, vsw, gates, cbv, gab = _inproj(x2, g_norm1, w_in, g_q, g_ks, g_kw, S)
    nh = S // CMP_STRIDE
    kca = _compress(hk.reshape(B, nh, -1), pe_k, ck_w1, ck_b1, ck_w2, ck_b2, g_kc, True)
    vct = _compress(hv.reshape(B, nh, -1), pe_v, cv_w1, cv_b1, cv_w2, cv_b2, jnp.ones((hd,), F32), False)
    vsw5 = vsw.reshape(B, S, 2, G, hd)
    ones_rows = jnp.concatenate([jnp.ones((1,), BF16), jnp.zeros((V_ROWS - hd - 1,), BF16)])

    def key_major(v, tile):
        vt = v.reshape(B, S // tile, tile, G, hd).transpose(0, 3, 1, 4, 2)
        extra = jnp.broadcast_to(ones_rows[None, None, None, :, None], vt.shape[:3] + (V_ROWS - hd, tile))
        return jnp.concatenate([vt, extra], axis=3)

    vst = key_major(vsw5[:, :, 0], KEY_TILE)
    vwt = key_major(vsw5[:, :, 1], Q_TILE)
    gat = gates[:, :3 * H].reshape(B, S, G, 3 * HEADS_PER_GROUP).transpose(0, 2, 3, 1)
    gat = jnp.pad(gat, ((0, 0), (0, 0), (0, 16 - 3 * HEADS_PER_GROUP), (0, 0)))
    o = _attention(qa.reshape(B, S, -1), gat, kca, vct, kas.reshape(B, S, -1), vst, kaw.reshape(B, S, -1),
                   vwt, B, S)
    return _mixer_out(x2, o.reshape(T, H * hd), cbv, gab, conv_w, w_pa, w_pb, w_o, S, g_norm2, w_r, b_r)


def _moe(x1, h2, mi, mf, cnt, w_gu, b_gu, w_dn, b_dn):
    T, D = x1.shape
    top_e = mi[0:TOP_K]
    rank = mi[TOP_K:2 * TOP_K]
    counts = cnt[0, :N_EXPERTS].astype(I32)
    padded = (counts + MOE_CHUNK - 1) // MOE_CHUNK * MOE_CHUNK
    pend = jnp.cumsum(padded)
    poffs = pend - padded
    dest = poffs[top_e] + rank
    n_chunks = (T * TOP_K + MOE_CHUNK - 1) // MOE_CHUNK + N_EXPERTS
    chunk_start = jnp.arange(n_chunks, dtype=I32) * MOE_CHUNK
    chunk_e = jnp.minimum(jnp.sum((pend[None, :] <= chunk_start[:, None]).astype(I32), axis=1), N_EXPERTS - 1)
    n_used = (pend[-1:] // MOE_CHUNK).astype(I32)
    last_chunk = jnp.where(padded > 0, pend - MOE_CHUNK, -1).astype(I32)
    hperm = _dispatch(h2, dest, last_chunk, n_chunks * MOE_CHUNK)
    ys = _experts(hperm, chunk_e, n_used, w_gu, b_gu, w_dn, b_dn)
    return _combine(x1, mf, dest, ys)


def kernel(x, g_norm1, w_in, g_q, g_kc, g_ks, g_kw, pe_k, ck_w1, ck_b1, ck_w2, ck_b2, pe_v, cv_w1, cv_b1,
           cv_w2, cv_b2, conv_w, w_pa, w_pb, w_o, g_norm2, w_r, b_r, w_gu, b_gu, w_dn, b_dn):
    B, S, D = x.shape
    x2 = x.reshape(B * S, D)
    for l in range(g_norm1.shape[0]):
        routed = _mixer(x2, B, S, g_norm1[l], w_in[l], g_q[l], g_kc[l], g_ks[l], g_kw[l], pe_k[l], ck_w1[l],
                        ck_b1[l], ck_w2[l], ck_b2[l], pe_v[l], cv_w1[l], cv_b1[l], cv_w2[l], cv_b2[l],
                        conv_w[l], w_pa[l], w_pb[l], w_o[l], g_norm2[l], w_r[l], b_r[l])
        x2 = _moe(*routed, w_gu[l], b_gu[l], w_dn[l], b_dn[l])
    return x2.reshape(B, S, D)
```

```python
import functools

import numpy as np
import jax
import jax.numpy as jnp
from jax import lax
from jax.experimental import pallas as pl
from jax.experimental.pallas import tpu as pltpu

F32 = jnp.float32
BF16 = jnp.bfloat16
I32 = jnp.int32

N_HEADS = 8
HEAD_DIM = 64
N_KV_HEADS = 2
HEADS_PER_GROUP = N_HEADS // N_KV_HEADS
CMP_LEN = 32
CMP_STRIDE = 16
CMP_HID = 256
SEL_BLK = 64
SEL_TOPN = 16
WINDOW = 512
CONV_WIDTH = 512
CONV_K = 3
N_EXPERTS = 32
TOP_K = 4
SWIGLU_LIMIT = 7.0
SWIGLU_ALPHA = 1.702
MOE_CHUNK = 512
EPS = 1e-6
NEG = -1e30
N_FORCED = 3

LANES = 128
Q_TILE = 256
KEY_TILE = 512
LOG2E = float(np.log2(np.e))
N_SPLIT = 3
N_AUG = 4 * N_SPLIT + 1
ROW_TILE = 512
VMEM_LIMIT = 56 * 1024 * 1024
GROUP_LANES = HEADS_PER_GROUP * Q_TILE
WIN_KEYS = WINDOW + Q_TILE
N_TILE_ROWS = 16
V_ROWS = HEAD_DIM + 8


def _cparams(n_axes):
    return pltpu.CompilerParams(dimension_semantics=("arbitrary",) * n_axes,
                                vmem_limit_bytes=VMEM_LIMIT)


def _dot(a, b):
    return jnp.dot(a, b, preferred_element_type=F32)


def _dot_nt(a, b):
    return lax.dot_general(a, b, (((1,), (1,)), ((), ())), preferred_element_type=F32)


ROW_TILES = 8


def _store_row_tiles(ref, val):
    n = val.shape[0]
    for s in range(ROW_TILES):
        ref[pl.ds(s, n, stride=ROW_TILES), :] = val[:, s * LANES:(s + 1) * LANES]


def _load_row_tiles(ref, n):
    return jnp.concatenate([ref[pl.ds(s, n, stride=ROW_TILES), :] for s in range(ROW_TILES)], axis=1)


def _rms_pairs(v, bd):
    ss = _dot((v * v).astype(BF16), bd)
    return v * lax.rsqrt(ss + EPS)


def _inproj_body(x_ref, g1_ref, wq_ref, wkv_ref, wng_ref, wcv_ref, wmg_ref, gq_ref, gk_ref, bd_ref,
                 qtab_ref, kwtab_ref,
                 qa_out, hk_out, hv_out, kas_out, kaw_out, vsw_out, gate_out, cbv_out, gab_out, raw_ref):
    x = x_ref[...]
    tm = x.shape[0]
    ms = jnp.mean(x * x, axis=-1, keepdims=True)
    h = (x * lax.rsqrt(ms + EPS) * g1_ref[...]).astype(BF16)
    bd = bd_ref[...]
    low = lax.broadcasted_iota(I32, (tm, LANES), 1) < HEAD_DIM

    def place(pair, tab_ref, out_ref, base, slot):
        for j, src in enumerate((pair, pltpu.roll(pair, HEAD_DIM, 1))):
            o = base + j * slot
            out_ref[:, o:o + LANES] = jnp.where(low, src, tab_ref[:, o:o + LANES].astype(F32)).astype(BF16)

    q = _dot(h, wq_ref[...])
    for c in range(N_HEADS * HEAD_DIM // LANES):
        sl = slice(c * LANES, (c + 1) * LANES)
        place(_rms_pairs(q[:, sl], bd) * gq_ref[:, sl], qtab_ref, qa_out, 2 * c * LANES, LANES)
    kv = _dot(h, wkv_ref[...])

    def emit_half_blocks(c, out_ref):
        nb = tm // CMP_STRIDE
        hw = CMP_STRIDE * HEAD_DIM
        raw_ref[...] = kv[:, c * LANES:(c + 1) * LANES]
        lo = lax.broadcasted_iota(I32, (nb, LANES), 1) < HEAD_DIM
        for u in range(CMP_STRIDE // 2):
            t0 = raw_ref[pl.ds(2 * u, nb, stride=CMP_STRIDE), :]
            t1 = raw_ref[pl.ds(2 * u + 1, nb, stride=CMP_STRIDE), :]
            out_ref[:, u * LANES:(u + 1) * LANES] = jnp.where(lo, t0, pltpu.roll(t1, HEAD_DIM, 1)).astype(BF16)
            out_ref[:, hw + u * LANES:hw + (u + 1) * LANES] = (
                jnp.where(lo, pltpu.roll(t0, HEAD_DIM, 1), t1).astype(BF16))

    emit_half_blocks(0, hk_out)
    emit_half_blocks(1, hv_out)
    place(_rms_pairs(kv[:, 256:384], bd) * gk_ref[:, 0:128], kwtab_ref, kas_out, 0, LANES)
    place(_rms_pairs(kv[:, 512:640], bd) * gk_ref[:, 128:256], kwtab_ref, kaw_out, 0, LANES)
    vsw_out[:, 0:128] = kv[:, 384:512].astype(BF16)
    vsw_out[:, 128:256] = kv[:, 640:768].astype(BF16)
    gate_out[...] = jax.nn.sigmoid(_dot(h, wng_ref[...]))
    cv = _dot(h, wcv_ref[...])
    cw = CONV_WIDTH
    cbv_out[:, 0:cw] = cv[:, 0:cw].astype(BF16)
    cbv_out[:, cw:2 * cw] = (cv[:, cw:2 * cw] * cv[:, 2 * cw:3 * cw]).astype(BF16)
    gab_out[...] = jax.nn.sigmoid(_dot(h, wmg_ref[...])).astype(BF16)


def _bf16_pieces(v):
    def round_bf16(a):
        u = np.ascontiguousarray(a, np.float32).view(np.uint32).astype(np.uint64)
        return ((u + 0x7FFF + ((u >> 16) & 1)) & 0xFFFF0000).astype(np.uint32).view(np.float32)

    pieces, rest = [], np.asarray(v, np.float32)
    for _ in range(N_SPLIT):
        pieces.append(round_bf16(rest))
        rest = rest - pieces[-1]
    return pieces


def _key_aug(pos):
    one = np.ones_like(pos, np.float32)
    hi = (pos // 64 * 64).astype(np.float32)
    lo = (pos % 64).astype(np.float32)
    return np.stack([one] * (2 * N_SPLIT) + [hi] * N_SPLIT + [lo] * N_SPLIT + [0 * one], axis=-1)


def _slot_table(aug, slot):
    S, n, _ = aug.shape
    tab = np.zeros((S, n, slot), np.float32)
    tab[:, :, HEAD_DIM:HEAD_DIM + N_AUG] = aug
    return jnp.asarray(tab.reshape(S, n * slot), BF16)


def _inproj(x2, g1, w_in, g_q, g_ks, g_kw, S):
    T, D = x2.shape
    H, G = N_HEADS, N_KV_HEADS
    aw = H * HEAD_DIM
    kvw = G * HEAD_DIM
    o = 0
    wq = w_in[:, o:o + aw]; o += aw
    wkv = w_in[:, o:o + 6 * kvw]; o += 6 * kvw
    wng = w_in[:, o:o + 3 * H]; o += 3 * H
    wcv = w_in[:, o:o + 3 * CONV_WIDTH]; o += 3 * CONV_WIDTH
    wmg = w_in[:, o:o + 2 * D]
    wng = jnp.pad(wng, ((0, 0), (0, LANES - 3 * H)))
    wq, wkv, wng, wcv, wmg = (w.astype(BF16) for w in (wq, wkv, wng, wcv, wmg))
    gq = (jnp.tile(g_q, H) * (HEAD_DIM ** -0.5 * LOG2E)).reshape(1, aw)
    gk = jnp.concatenate([jnp.tile(g_ks, G), jnp.tile(g_kw, G)]).reshape(1, 2 * kvw)
    idx = np.arange(LANES) // HEAD_DIM
    bd = jnp.asarray((idx[:, None] == idx[None, :]).astype(np.float32) / HEAD_DIM, BF16)
    pos = np.arange(S)
    hi = (pos // 64 * 64).astype(np.float64)[:, None]
    lo = (pos % 64).astype(np.float64)[:, None]
    c = LOG2E * 2.0 ** (-8.0 * np.arange(1, H + 1) / H)[None, :]
    cs = np.broadcast_to(c, (S, H))
    aq = np.stack(_bf16_pieces(-c * hi) + _bf16_pieces(-c * lo) + _bf16_pieces(cs) + _bf16_pieces(cs)
                  + [np.ones((S, H), np.float32)], axis=-1)
    ak = np.broadcast_to(_key_aug(pos)[:, None, :], (S, G, N_AUG))
    qtab = _slot_table(aq, LANES)
    kwtab = _slot_table(ak, LANES)
    tm = ROW_TILE
    nst = S // tm
    row = lambda w: pl.BlockSpec((tm, w), lambda i: (i, 0))
    full = lambda a: pl.BlockSpec(a.shape, lambda i: (0,) * a.ndim)
    tab = lambda a: pl.BlockSpec((tm, a.shape[1]), lambda i: (i % nst, 0))
    ins = (x2, g1.reshape(1, D), wq, wkv, wng, wcv, wmg, gq, gk, bd, qtab, kwtab)
    widths = (H * LANES, G * LANES, G * LANES, 2 * kvw, LANES, 2 * CONV_WIDTH, 2 * D)
    dtypes = (BF16, BF16, BF16, BF16, F32, BF16, BF16)
    hw = G * CMP_STRIDE * HEAD_DIM
    nb = tm // CMP_STRIDE
    half = pl.BlockSpec((nb, hw), lambda i: (i, 0))
    half_shape = jax.ShapeDtypeStruct((T // CMP_STRIDE, hw), BF16)
    rows = [(row(w), jax.ShapeDtypeStruct((T, w), dt)) for w, dt in zip(widths, dtypes)]
    outs = [rows[0], (half, half_shape), (half, half_shape)] + rows[1:]
    return pl.pallas_call(
        _inproj_body,
        grid=(T // tm,),
        in_specs=[row(D)] + [full(a) for a in ins[1:10]] + [tab(a) for a in ins[10:]],
        out_specs=[o[0] for o in outs],
        out_shape=[o[1] for o in outs],
        scratch_shapes=[pltpu.VMEM((tm, LANES), F32)],
        compiler_params=_cparams(1),
        name="inproj",
    )(*ins)


def _compress_body(h_ref, w1_ref, pe_ref, b1_ref, w2_ref, b2_ref, g_ref, tab_ref, o_ref, *, for_keys):
    hb = h_ref[0]
    nc = hb.shape[0]
    a = _dot(hb, w1_ref[0])
    b = _dot(hb, w1_ref[1])
    c = _dot(pe_ref[0], w1_ref[0]) + _dot(pe_ref[1], w1_ref[1])
    pre = a + pltpu.roll(b, nc - 1, 0) + c[0:1, :] + b1_ref[...]
    hid = jax.nn.gelu(pre)
    out = _dot(hid.astype(BF16), w2_ref[...]) + b2_ref[...]
    if for_keys:
        ms = jnp.sum(out * out, axis=-1, keepdims=True) * (1.0 / HEAD_DIM)
        out = out * lax.rsqrt(ms + EPS) * g_ref[...]
        low = lax.broadcasted_iota(I32, out.shape, 1) < HEAD_DIM
        o_ref[0, 0] = jnp.where(low, out, tab_ref[...]).astype(BF16)
    else:
        o_ref[0, 0] = out.T[0:HEAD_DIM, :].astype(BF16)


def _compress(hh, pe, w1, b1, w2, b2, gain, for_keys):
    B, NC, _ = hh.shape
    G, HW = N_KV_HEADS, CMP_STRIDE * HEAD_DIM
    w1s = w1.reshape(2, HW, CMP_HID).astype(BF16)
    pes = jnp.broadcast_to(pe.reshape(2, 1, HW), (2, 8, HW)).astype(BF16)
    padl = lambda a: jnp.pad(a, ((0, 0), (0, LANES - HEAD_DIM)))
    tabn = np.zeros((NC, LANES), np.float32)
    tabn[:, HEAD_DIM:HEAD_DIM + N_AUG] = _key_aug(np.arange(NC) * CMP_STRIDE + (CMP_LEN - 1))
    tab = jnp.asarray(tabn)
    full = lambda a: pl.BlockSpec(a.shape, lambda b, g: (0,) * a.ndim)
    ins = (hh, w1s, pes, b1.reshape(1, CMP_HID), padl(w2).astype(BF16), padl(b2.reshape(1, HEAD_DIM)),
           padl(gain.reshape(1, HEAD_DIM)), tab)
    oshape = (B, G, NC, LANES) if for_keys else (B, G, HEAD_DIM, NC)
    return pl.pallas_call(
        functools.partial(_compress_body, for_keys=for_keys),
        grid=(B, G),
        in_specs=[pl.BlockSpec((1, NC, HW), lambda b, g: (b, 0, g))] + [full(a) for a in ins[1:]],
        out_specs=pl.BlockSpec((1, 1) + oshape[2:], lambda b, g: (b, g, 0, 0)),
        out_shape=jax.ShapeDtypeStruct(oshape, BF16),
        compiler_params=_cparams(2),
        name="compress_keys" if for_keys else "compress_values",
    )(*ins)


def _attn_body(qa_ref, g_ref, kca_ref, vct_ref, kas_ref, vst_ref, kaw_ref, vwt_ref, selmapt_ref, wbias_ref,
               tilemap_ref, cbias_ref, o_ref, selb_ref, m_ref, acc_ref, s0_ref, s1_ref, flagv_ref,
               flags_ref, list_ref, sem,
               *, n_sel):
    i = pl.program_id(2)
    q0 = i * Q_TILE
    gl = GROUP_LANES
    qa = jnp.concatenate([qa_ref[0, :, h * LANES:(h + 1) * LANES] for h in range(HEADS_PER_GROUP)], axis=0)

    nc = kca_ref.shape[2]
    s = _dot_nt(kca_ref[0, 0], qa)
    cb = cbias_ref[pl.ds(pl.multiple_of(nc - i * (Q_TILE // CMP_STRIDE), 8), nc), :]
    s = s + jnp.concatenate([cb] * HEADS_PER_GROUP, axis=1)
    m = jnp.max(s, axis=0, keepdims=True)
    p = jnp.exp2(s - m)
    l = jnp.sum(p, axis=0, keepdims=True)
    has_entry = (q0 + (lax.broadcasted_iota(I32, (1, gl), 1) & (Q_TILE - 1))) >= CMP_LEN - 1
    pc = p * jnp.where(has_entry, 1.0 / l, 0.0)
    o_c = _dot(vct_ref[0, 0], pc.astype(BF16))

    ps = pc[:, 0:Q_TILE]
    for h in range(1, HEADS_PER_GROUP):
        ps = ps + pc[:, h * Q_TILE:(h + 1) * Q_TILE]
    ps_hi = ps.astype(BF16)
    ps_lo = (ps - ps_hi.astype(F32)).astype(BF16)
    imp = _dot(selmapt_ref[...], ps_hi) + _dot(selmapt_ref[...], ps_lo)
    jb = lax.broadcasted_iota(I32, (LANES, Q_TILE), 0)
    cur = (q0 + lax.broadcasted_iota(I32, (LANES, Q_TILE), 1)) // SEL_BLK
    forced = (jb == 0) | (jb == cur) | (jb == cur - 1)
    score = jnp.where(forced, -jnp.inf, jnp.where(jb > cur, NEG, imp))
    jbf = jb.astype(F32)
    for _ in range(n_sel - N_FORCED):
        mx = jnp.max(score, axis=0, keepdims=True)
        first = jnp.min(jnp.where(score == mx, jbf, float(LANES)), axis=0, keepdims=True)
        score = jnp.where(jbf == first, -jnp.inf, score)
    picked = score == -jnp.inf
    bias_t = jnp.where(picked, 0.0, NEG)
    selb_ref[...] = jnp.concatenate([bias_t] * HEADS_PER_GROUP, axis=1)
    tile_hits = jnp.max(_dot(tilemap_ref[...], jnp.where(picked, 1.0, 0.0).astype(BF16)), axis=1, keepdims=True)
    flagv_ref[...] = jnp.broadcast_to(tile_hits, flagv_ref.shape).astype(I32)
    flag_copy = pltpu.make_async_copy(flagv_ref, flags_ref, sem)
    flag_copy.start()

    w0 = pl.multiple_of(jnp.maximum(q0 - WINDOW, 0), Q_TILE)
    sw = _dot_nt(kaw_ref[0, pl.ds(w0, WIN_KEYS), :], qa)
    wb = wbias_ref[jnp.minimum(i, WINDOW // Q_TILE)]
    sw = sw + jnp.concatenate([wb] * HEADS_PER_GROUP, axis=1)
    mw = jnp.max(sw, axis=0, keepdims=True)
    pw = jnp.exp2(sw - mw)
    c0 = w0 // Q_TILE
    vw = jnp.concatenate([vwt_ref[0, 0, c0 + j] for j in range(WIN_KEYS // Q_TILE)], axis=1)
    aw = _dot(vw, pw.astype(BF16))
    o_w = aw[0:HEAD_DIM] * (1.0 / aw[HEAD_DIM:HEAD_DIM + 1])

    def scores(kt):
        sc = _dot_nt(kas_ref[0, pl.ds(pl.multiple_of(kt * KEY_TILE, KEY_TILE), KEY_TILE), :], qa)
        blocks = KEY_TILE // SEL_BLK
        mask = [jnp.broadcast_to(selb_ref[pl.ds(kt * blocks + j, 1), :], (SEL_BLK, gl)) for j in range(blocks)]
        return sc + jnp.concatenate(mask, axis=0)

    n_full = q0 // KEY_TILE
    key = n_full * KEY_TILE + lax.broadcasted_iota(I32, (KEY_TILE, gl), 0)
    qry = q0 + (lax.broadcasted_iota(I32, (KEY_TILE, gl), 1) & (Q_TILE - 1))
    s0_ref[...] = jnp.where(key <= qry, scores(n_full), NEG)
    m_ref[...] = jnp.full(m_ref.shape, -3.0e38, F32)
    acc_ref[...] = jnp.zeros(acc_ref.shape, F32)
    list_ref[0] = n_full

    def absorb(s_ref, kt, live):
        sc = s_ref[...]
        m_old = m_ref[...]
        m_new = jnp.where(live, jnp.maximum(m_old, jnp.max(sc, axis=0, keepdims=True)), m_old)
        pv = _dot(vst_ref[0, 0, kt], jnp.exp2(sc - m_new).astype(BF16))
        acc_ref[...] = jnp.exp2(m_old - m_new) * acc_ref[...] + jnp.where(live, pv, 0.0)
        m_ref[...] = m_new

    flag_copy.wait()

    def compact(kt, n):
        active = flags_ref[kt, 0] > 0

        @pl.when(active)
        def _():
            list_ref[n] = kt

        return n + active.astype(I32)

    n_items = lax.fori_loop(0, n_full, compact, 1)

    last = n_items - 1

    def pair_body(j, carry):
        a = list_ref[2 * j]
        b = list_ref[jnp.minimum(2 * j + 1, last)]
        nxt = list_ref[jnp.minimum(2 * j + 2, last)]
        s1_ref[...] = scores(b)
        absorb(s0_ref, a, True)
        s0_ref[...] = scores(nxt)
        absorb(s1_ref, b, 2 * j + 1 <= last)
        return carry

    lax.fori_loop(0, (n_items + 1) // 2, pair_body, 0)

    o_s = acc_ref[0:HEAD_DIM, :] * (1.0 / acc_ref[HEAD_DIM:HEAD_DIM + 1, :])

    g = g_ref[0, 0]
    outs = []
    for h in range(HEADS_PER_GROUP):
        sl = slice(h * Q_TILE, (h + 1) * Q_TILE)
        outs.append(g[3 * h:3 * h + 1, :] * o_c[:, sl] + g[3 * h + 1:3 * h + 2, :] * o_s[:, sl]
                    + g[3 * h + 2:3 * h + 3, :] * o_w[:, sl])
    o_ref[0] = jnp.concatenate(outs, axis=0).T.astype(BF16)


def _attention(qa, gates_t, kca, vct, kas, vst, kaw, vwt, B, S):
    G, hd = N_KV_HEADS, HEAD_DIM
    NC = kca.shape[2]
    n_blk = S // SEL_BLK
    assert n_blk <= LANES and S % KEY_TILE == 0 and S >= WIN_KEYS
    n_sel = min(SEL_TOPN, n_blk)
    assert n_sel > N_FORCED
    ratio, span = SEL_BLK // CMP_STRIDE, CMP_LEN // CMP_STRIDE
    sm = np.zeros((LANES, NC), np.float32)
    for j in range(n_blk):
        for a in range(ratio):
            for b in range(span):
                n = ratio * j + a - b
                if 0 <= n < NC - 1:
                    sm[j, n] += 1.0
    selmapt = jnp.asarray(sm, BF16)
    c = np.arange(WIN_KEYS)[:, None]
    r = np.arange(Q_TILE)[None, :]
    offs = np.arange(WINDOW // Q_TILE + 1)[:, None, None] * Q_TILE
    wbias = jnp.asarray(np.where((c - r <= offs) & (c - r > offs - WINDOW), 0.0, NEG), F32)
    tilemap = jnp.asarray(np.arange(LANES)[None, :] // (KEY_TILE // SEL_BLK) == np.arange(N_TILE_ROWS)[:, None],
                          BF16)
    assert S // KEY_TILE <= N_TILE_ROWS
    u = np.arange(2 * NC)[:, None] - NC
    cbias = jnp.asarray(np.where(CMP_STRIDE * u + (CMP_LEN - 1) <= np.arange(Q_TILE)[None, :], 0.0, NEG), F32)

    hpg = HEADS_PER_GROUP
    grp = lambda *blk: pl.BlockSpec((1, 1) + blk, lambda b, g, i: (b, g) + (0,) * len(blk))
    seq = lambda w: pl.BlockSpec((1, S, w), lambda b, g, i: (b, 0, g))
    const = lambda a: pl.BlockSpec(a.shape, lambda b, g, i: (0,) * a.ndim)
    return pl.pallas_call(
        functools.partial(_attn_body, n_sel=n_sel),
        grid=(B, G, S // Q_TILE),
        in_specs=[pl.BlockSpec((1, Q_TILE, hpg * LANES), lambda b, g, i: (b, i, g)),
                  pl.BlockSpec((1, 1, 16, Q_TILE), lambda b, g, i: (b, g, 0, i)),
                  grp(NC, LANES), grp(hd, NC), seq(LANES), grp(S // KEY_TILE, V_ROWS, KEY_TILE),
                  seq(LANES), grp(S // Q_TILE, V_ROWS, Q_TILE), const(selmapt), const(wbias), const(tilemap),
                  const(cbias)],
        out_specs=pl.BlockSpec((1, Q_TILE, hpg * hd), lambda b, g, i: (b, i, g)),
        out_shape=jax.ShapeDtypeStruct((B, S, N_HEADS * hd), BF16),
        scratch_shapes=[pltpu.VMEM((LANES, GROUP_LANES), F32), pltpu.VMEM((1, GROUP_LANES), F32),
                        pltpu.VMEM((V_ROWS, GROUP_LANES), F32), pltpu.VMEM((KEY_TILE, GROUP_LANES), F32),
                        pltpu.VMEM((KEY_TILE, GROUP_LANES), F32),
                        pltpu.VMEM((N_TILE_ROWS, LANES), I32), pltpu.SMEM((N_TILE_ROWS, LANES), I32),
                        pltpu.SMEM((N_TILE_ROWS,), I32), pltpu.SemaphoreType.DMA(())],
        compiler_params=_cparams(3),
        name="nsa_attention",
    )(qa, gates_t, kca, vct, kas, vst, kaw, vwt, selmapt, wbias, tilemap, cbias)


def _mixer_out_body(x_ref, oa_ref, cbv_ref, halo_ref, gab_ref, cw_ref, wpa_ref, wpb_ref, wo_ref,
                    g2_ref, whi_ref, wlo_ref, br_ref, tri_ref, o_ref, h_out, mi_out, mf_out, cnt_out,
                    *, seq_len):
    i = pl.program_id(0)
    tm = x_ref.shape[0]
    cwd = CONV_WIDTH
    d = x_ref.shape[1]
    v = cbv_ref[:, cwd:2 * cwd].astype(F32)
    prev = halo_ref[:, cwd:2 * cwd].astype(F32)
    keep = ((i * tm) % seq_len != 0).astype(F32)
    p1 = prev[7:8, :] * keep
    p2 = prev[6:7, :] * keep
    ridx = lax.broadcasted_iota(I32, (tm, cwd), 0)
    v1 = jnp.where(ridx == 0, p1, pltpu.roll(v, 1, 0))
    v2 = jnp.where(ridx == 0, p2, jnp.where(ridx == 1, p1, pltpu.roll(v, 2, 0)))
    y = cw_ref[0:1, :] * v2 + cw_ref[1:2, :] * v1 + cw_ref[2:3, :] * v
    yb_in = (cbv_ref[:, 0:cwd].astype(F32) * y).astype(BF16)
    y_a = _dot(oa_ref[...], wpa_ref[...])
    y_b = _dot(yb_in, wpb_ref[...])
    merged = gab_ref[:, 0:d].astype(F32) * y_a + gab_ref[:, d:2 * d].astype(F32) * y_b
    x1 = x_ref[...] + _dot(merged.astype(BF16), wo_ref[...])
    o_ref[...] = x1
    _route(x1, g2_ref, whi_ref, wlo_ref, br_ref, tri_ref, h_out, mi_out, mf_out, cnt_out)


def _mixer_out(x2, oa, cbv, gab, conv_w, w_pa, w_pb, w_o, seq_len, g2, w_r, b_r):
    T, D = x2.shape
    assert D == ROW_TILES * LANES
    tm = ROW_TILE
    cw8 = jnp.pad(conv_w, ((0, 8 - CONV_K), (0, 0)))
    wpad = jnp.pad(w_r, ((0, 0), (0, LANES - N_EXPERTS)))
    whi = wpad.astype(BF16)
    wlo = (wpad - whi.astype(F32)).astype(BF16)
    br = jnp.pad(b_r, (0, LANES - N_EXPERTS)).reshape(1, LANES)
    tri = jnp.asarray(np.tril(np.ones((tm, tm), np.float32), -1), BF16)
    row = lambda w: pl.BlockSpec((tm, w), lambda i: (i, 0))
    full = lambda a: pl.BlockSpec(a.shape, lambda i: (0,) * a.ndim)
    halo = pl.BlockSpec((8, cbv.shape[1]), lambda i: (jnp.maximum(i * (tm // 8) - 1, 0), 0))
    wts = (cw8, w_pa.astype(BF16), w_pb.astype(BF16), w_o.astype(BF16), g2.reshape(1, D), whi, wlo, br, tri)
    return pl.pallas_call(
        functools.partial(_mixer_out_body, seq_len=seq_len),
        grid=(T // tm,),
        in_specs=[row(D), row(oa.shape[1]), row(cbv.shape[1]), halo, row(gab.shape[1])] + [full(a) for a in wts],
        out_specs=[row(D), pl.BlockSpec((tm * ROW_TILES, LANES), lambda i: (i, 0)), row(2 * TOP_K), row(LANES),
                   pl.BlockSpec((8, LANES), lambda i: (0, 0))],
        out_shape=[jax.ShapeDtypeStruct((T, D), F32), jax.ShapeDtypeStruct((T * ROW_TILES, LANES), F32),
                   jax.ShapeDtypeStruct((T, 2 * TOP_K), I32), jax.ShapeDtypeStruct((T, LANES), F32),
                   jax.ShapeDtypeStruct((8, LANES), F32)],
        compiler_params=_cparams(1),
        name="mixer_out_router",
    )(x2, oa, cbv, cbv, gab, *wts)


def _route(x, g2_ref, whi_ref, wlo_ref, br_ref, tri_ref, h_out, mi_out, mf_out, cnt_out):
    i = pl.program_id(0)

    @pl.when(i == 0)
    def _():
        cnt_out[...] = jnp.zeros(cnt_out.shape, F32)

    tm = x.shape[0]
    ms = jnp.mean(x * x, axis=-1, keepdims=True)
    h = x * lax.rsqrt(ms + EPS) * g2_ref[...]
    _store_row_tiles(h_out, h)
    h_hi = h.astype(BF16)
    h_lo = (h - h_hi.astype(F32)).astype(BF16)
    logits = (_dot(h_hi, whi_ref[...]) + _dot(h_lo, whi_ref[...]) + _dot(h_hi, wlo_ref[...])) + br_ref[...]
    lane = lax.broadcasted_iota(I32, (tm, LANES), 1)
    lanef = lane.astype(F32)
    work = jnp.where(lane < N_EXPERTS, logits, -jnp.inf)
    vals, hits = [], []
    for _ in range(TOP_K):
        mx = jnp.max(work, axis=-1, keepdims=True)
        first = jnp.min(jnp.where(work == mx, lanef, float(LANES)), axis=-1, keepdims=True)
        hit = lanef == first
        vals.append(mx)
        hits.append(hit)
        work = jnp.where(hit, -jnp.inf, work)
    ex = [jnp.exp(v - vals[0]) for v in vals]
    den = ex[0]
    for e in ex[1:]:
        den = den + e
    cnt = jnp.zeros((tm, LANES), F32)
    for hit in hits:
        cnt = cnt + hit.astype(F32)
    before = _dot(tri_ref[...], cnt.astype(BF16)) + cnt_out[0:1, :]
    mi = jnp.zeros((tm, LANES), F32)
    mf = jnp.zeros((tm, LANES), F32)
    for k, hit in enumerate(hits):
        e_k = jnp.sum(jnp.where(hit, lanef, 0.0), axis=-1, keepdims=True)
        r_k = jnp.sum(jnp.where(hit, before, 0.0), axis=-1, keepdims=True)
        mi = jnp.where(lane == k, e_k, jnp.where(lane == TOP_K + k, r_k, mi))
        mf = jnp.where(lane == k, ex[k] / den, mf)
    mi_out[...] = mi[:, 0:2 * TOP_K].astype(I32)
    mf_out[...] = mf
    cnt_out[...] = cnt_out[...] + jnp.sum(cnt, axis=0, keepdims=True)


DISPATCH_TILE = 512
DMA_UNROLL = 8


def _dispatch_body(dest_ref, last_ref, h_ref, o_hbm, zero_ref, sem, zsem):
    @pl.when(pl.program_id(0) == 0)
    def _():
        zero_ref[...] = jnp.zeros(zero_ref.shape, zero_ref.dtype)

        def clear(e):
            start = pl.multiple_of(last_ref[e] * ROW_TILES, MOE_CHUNK * ROW_TILES)
            return pltpu.make_async_copy(zero_ref, o_hbm.at[pl.ds(start, MOE_CHUNK * ROW_TILES)], zsem)

        for e in range(N_EXPERTS):
            @pl.when(last_ref[e] >= 0)
            def _():
                clear(e).start()
        for e in range(N_EXPERTS):
            @pl.when(last_ref[e] >= 0)
            def _():
                clear(e).wait()

    def row_copy(r, d):
        return pltpu.make_async_copy(h_ref.at[pl.ds(pl.multiple_of(r * ROW_TILES, ROW_TILES), ROW_TILES)],
                                     o_hbm.at[pl.ds(pl.multiple_of(d * ROW_TILES, ROW_TILES), ROW_TILES)], sem)

    def start(r, c):
        for k in range(TOP_K):
            row_copy(r, dest_ref[0, 0, r * TOP_K + k]).start(priority=k % 2)
        return c

    def wait(r, c):
        for k in range(TOP_K):
            row_copy(0, 0).wait()
        return c

    lax.fori_loop(0, DISPATCH_TILE, start, 0, unroll=DMA_UNROLL)
    lax.fori_loop(0, DISPATCH_TILE, wait, 0, unroll=DMA_UNROLL)


def _dispatch(h2, dest, last_chunk, n_rows):
    T = h2.shape[0] // ROW_TILES
    td = DISPATCH_TILE
    dest3 = dest.reshape(T // td, 1, td * TOP_K)
    return pl.pallas_call(
        _dispatch_body,
        grid=(T // td,),
        in_specs=[pl.BlockSpec((1, 1, td * TOP_K), lambda i: (i, 0, 0), memory_space=pltpu.SMEM),
                  pl.BlockSpec(memory_space=pltpu.SMEM),
                  pl.BlockSpec((td * ROW_TILES, LANES), lambda i: (i, 0))],
        out_specs=pl.BlockSpec(memory_space=pl.ANY),
        out_shape=jax.ShapeDtypeStruct((n_rows * ROW_TILES, LANES), h2.dtype),
        scratch_shapes=[pltpu.VMEM((MOE_CHUNK * ROW_TILES, LANES), h2.dtype), pltpu.SemaphoreType.DMA(()),
                        pltpu.SemaphoreType.DMA(())],
        compiler_params=_cparams(1),
        name="dispatch",
    )(dest3, last_chunk, h2)


def _expert_body(ce_ref, nu_ref, x_ref, wgu_ref, bgu_ref, wdn_ref, bdn_ref, o_ref, wgu_bf, wdn_bf):
    c = pl.program_id(0)
    dff = wdn_ref.shape[1]

    @pl.when((c == 0) | (ce_ref[c] != ce_ref[jnp.maximum(c - 1, 0)]))
    def _():
        wgu_bf[...] = wgu_ref[0].astype(BF16)
        wdn_bf[...] = wdn_ref[0].astype(BF16)

    @pl.when(c < nu_ref[0])
    def _():
        x = _load_row_tiles(x_ref, MOE_CHUNK)
        gu = _dot(x.astype(BF16), wgu_bf[...]) + bgu_ref[0]
        g = jnp.minimum(gu[:, 0:dff], SWIGLU_LIMIT)
        u = jnp.clip(gu[:, dff:2 * dff], -SWIGLU_LIMIT, SWIGLU_LIMIT)
        act = (u + 1.0) * (g * jax.nn.sigmoid(SWIGLU_ALPHA * g))
        _store_row_tiles(o_ref, _dot(act.astype(BF16), wdn_bf[...]) + bdn_ref[0])

    @pl.when(c >= nu_ref[0])
    def _():
        o_ref[...] = jnp.zeros(o_ref.shape, F32)


def _experts(hperm, chunk_e, n_used, w_gu, b_gu, w_dn, b_dn):
    E, D, F2 = w_gu.shape
    assert D == ROW_TILES * LANES
    P = hperm.shape[0] // ROW_TILES
    dff = F2 // 2
    n_chunks = P // MOE_CHUNK
    chunk = (MOE_CHUNK * ROW_TILES, LANES)
    grid_spec = pltpu.PrefetchScalarGridSpec(
        num_scalar_prefetch=2,
        grid=(n_chunks,),
        in_specs=[pl.BlockSpec(chunk, lambda c, ce, nu: (jnp.minimum(c, nu[0] - 1), 0)),
                  pl.BlockSpec((1, D, F2), lambda c, ce, nu: (ce[c], 0, 0)),
                  pl.BlockSpec((1, 1, F2), lambda c, ce, nu: (ce[c], 0, 0)),
                  pl.BlockSpec((1, dff, D), lambda c, ce, nu: (ce[c], 0, 0)),
                  pl.BlockSpec((1, 1, D), lambda c, ce, nu: (ce[c], 0, 0))],
        out_specs=pl.BlockSpec(chunk, lambda c, ce, nu: (c, 0)),
        scratch_shapes=[pltpu.VMEM((D, F2), BF16), pltpu.VMEM((dff, D), BF16)],
    )
    return pl.pallas_call(
        _expert_body,
        grid_spec=grid_spec,
        out_shape=jax.ShapeDtypeStruct(hperm.shape, F32),
        compiler_params=_cparams(1),
        name="experts",
    )(chunk_e, n_used, hperm, w_gu, b_gu.reshape(E, 1, F2), w_dn, b_dn.reshape(E, 1, D))


COMBINE_TILE = 256


def _combine_body(dest_ref, next_ref, x_ref, w_ref, y_hbm, o_ref, buf_ref, sems):
    i = pl.program_id(0)
    slot = i % 2

    def row_copy(s, r, k, d):
        return pltpu.make_async_copy(y_hbm.at[pl.ds(pl.multiple_of(d * ROW_TILES, ROW_TILES), ROW_TILES)],
                                     buf_ref.at[s, k, pl.ds(pl.multiple_of(r * ROW_TILES, ROW_TILES), ROW_TILES)],
                                     sems.at[s])

    def fetch(idx_ref, s):
        def start(r, c):
            for k in range(TOP_K):
                row_copy(s, r, k, idx_ref[0, 0, r * TOP_K + k]).start(priority=k % 2)
            return c

        lax.fori_loop(0, COMBINE_TILE, start, 0, unroll=DMA_UNROLL)

    @pl.when(i == 0)
    def _():
        fetch(dest_ref, slot)

    @pl.when(i + 1 < pl.num_programs(0))
    def _():
        fetch(next_ref, 1 - slot)

    def wait(r, c):
        for k in range(TOP_K):
            row_copy(slot, 0, 0, 0).wait()
        return c

    lax.fori_loop(0, COMBINE_TILE, wait, 0, unroll=DMA_UNROLL)
    gate = [jnp.broadcast_to(w_ref[:, k:k + 1], (COMBINE_TILE, LANES)) for k in range(TOP_K)]
    for s in range(ROW_TILES):
        sl = slice(s * LANES, (s + 1) * LANES)
        out = x_ref[:, sl]
        for k in range(TOP_K):
            out = out + gate[k] * buf_ref[slot, k, pl.ds(s, COMBINE_TILE, stride=ROW_TILES), :]
        o_ref[:, sl] = out


def _combine(x1, gate_w, dest, ys):
    T, D = x1.shape
    tc = COMBINE_TILE
    n = T // tc
    dest3 = dest.reshape(n, 1, tc * TOP_K)
    row = lambda w: pl.BlockSpec((tc, w), lambda i: (i, 0))
    idx = lambda f: pl.BlockSpec((1, 1, tc * TOP_K), f, memory_space=pltpu.SMEM)
    return pl.pallas_call(
        _combine_body,
        grid=(n,),
        in_specs=[idx(lambda i: (i, 0, 0)), idx(lambda i: (jnp.minimum(i + 1, n - 1), 0, 0)),
                  row(D), row(LANES), pl.BlockSpec(memory_space=pl.ANY)],
        out_specs=row(D),
        out_shape=jax.ShapeDtypeStruct((T, D), F32),
        scratch_shapes=[pltpu.VMEM((2, TOP_K, tc * ROW_TILES, LANES), F32), pltpu.SemaphoreType.DMA((2,))],
        compiler_params=_cparams(1),
        name="combine",
    )(dest3, dest3, x1, gate_w, ys)


def _mixer(x2, B, S, g_norm1, w_in, g_q, g_kc, g_ks, g_kw, pe_k, ck_w1, ck_b1, ck_w2, ck_b2,
           pe_v, cv_w1, cv_b1, cv_w2, cv_b2, conv_w, w_pa, w_pb, w_o, g_norm2, w_r, b_r):
    T, D = x2.shape
    G, H, hd = N_KV_HEADS, N_HEADS, HEAD_DIM
    qa, hk, hv, kas, kaw, vsw, gates, cbv, gab = _inproj(x2, g_norm1, w_in, g_q, g_ks, g_kw, S)
    nh = S // CMP_STRIDE
    kca = _compress(hk.reshape(B, nh, -1), pe_k, ck_w1, ck_b1, ck_w2, ck_b2, g_kc, True)
    vct = _compress(hv.reshape(B, nh, -1), pe_v, cv_w1, cv_b1, cv_w2, cv_b2, jnp.ones((hd,), F32), False)
    vsw5 = vsw.reshape(B, S, 2, G, hd)
    ones_rows = jnp.concatenate([jnp.ones((1,), BF16), jnp.zeros((V_ROWS - hd - 1,), BF16)])

    def key_major(v, tile):
        vt = v.reshape(B, S // tile, tile, G, hd).transpose(0, 3, 1, 4, 2)
        extra = jnp.broadcast_to(ones_rows[None, None, None, :, None], vt.shape[:3] + (V_ROWS - hd, tile))
        return jnp.concatenate([vt, extra], axis=3)

    vst = key_major(vsw5[:, :, 0], KEY_TILE)
    vwt = key_major(vsw5[:, :, 1], Q_TILE)
    gat = gates[:, :3 * H].reshape(B, S, G, 3 * HEADS_PER_GROUP).transpose(0, 2, 3, 1)
    gat = jnp.pad(gat, ((0, 0), (0, 0), (0, 16 - 3 * HEADS_PER_GROUP), (0, 0)))
    o = _attention(qa.reshape(B, S, -1), gat, kca, vct, kas.reshape(B, S, -1), vst, kaw.reshape(B, S, -1),
                   vwt, B, S)
    return _mixer_out(x2, o.reshape(T, H * hd), cbv, gab, conv_w, w_pa, w_pb, w_o, S, g_norm2, w_r, b_r)


def _moe(x1, h2, mi, mf, cnt, w_gu, b_gu, w_dn, b_dn):
    T, D = x1.shape
    top_e = mi[:, 0:TOP_K]
    rank = mi[:, TOP_K:2 * TOP_K]
    counts = cnt[0, :N_EXPERTS].astype(I32)
    padded = (counts + MOE_CHUNK - 1) // MOE_CHUNK * MOE_CHUNK
    pend = jnp.cumsum(padded)
    poffs = pend - padded
    dest = (poffs[top_e] + rank).reshape(-1)
    n_chunks = (T * TOP_K + MOE_CHUNK - 1) // MOE_CHUNK + N_EXPERTS
    chunk_start = jnp.arange(n_chunks, dtype=I32) * MOE_CHUNK
    chunk_e = jnp.minimum(jnp.sum((pend[None, :] <= chunk_start[:, None]).astype(I32), axis=1), N_EXPERTS - 1)
    n_used = (pend[-1:] // MOE_CHUNK).astype(I32)
    last_chunk = jnp.where(padded > 0, pend - MOE_CHUNK, -1).astype(I32)
    hperm = _dispatch(h2, dest, last_chunk, n_chunks * MOE_CHUNK)
    ys = _experts(hperm, chunk_e, n_used, w_gu, b_gu, w_dn, b_dn)
    return _combine(x1, mf, dest, ys)


def kernel(x, g_norm1, w_in, g_q, g_kc, g_ks, g_kw, pe_k, ck_w1, ck_b1, ck_w2, ck_b2, pe_v, cv_w1, cv_b1,
           cv_w2, cv_b2, conv_w, w_pa, w_pb, w_o, g_norm2, w_r, b_r, w_gu, b_gu, w_dn, b_dn):
    B, S, D = x.shape
    x2 = x.reshape(B * S, D)
    for l in range(g_norm1.shape[0]):
        routed = _mixer(x2, B, S, g_norm1[l], w_in[l], g_q[l], g_kc[l], g_ks[l], g_kw[l], pe_k[l], ck_w1[l],
                        ck_b1[l], ck_w2[l], ck_b2[l], pe_v[l], cv_w1[l], cv_b1[l], cv_w2[l], cv_b2[l],
                        conv_w[l], w_pa[l], w_pb[l], w_o[l], g_norm2[l], w_r[l], b_r[l])
        x2 = _moe(*routed, w_gu[l], b_gu[l], w_dn[l], b_dn[l])
    return x2.reshape(B, S, D)
```

```python
import functools

import numpy as np
import jax
import jax.numpy as jnp
from jax import lax
from jax.experimental import pallas as pl
from jax.experimental.pallas import tpu as pltpu

F32 = jnp.float32
BF16 = jnp.bfloat16
I32 = jnp.int32

N_HEADS = 8
HEAD_DIM = 64
N_KV_HEADS = 2
HEADS_PER_GROUP = N_HEADS // N_KV_HEADS
CMP_LEN = 32
CMP_STRIDE = 16
CMP_HID = 256
SEL_BLK = 64
SEL_TOPN = 16
WINDOW = 512
CONV_WIDTH = 512
CONV_K = 3
N_EXPERTS = 32
TOP_K = 4
SWIGLU_LIMIT = 7.0
SWIGLU_ALPHA = 1.702
MOE_CHUNK = 512
EPS = 1e-6
NEG = -1e30
N_FORCED = 3

LANES = 128
Q_TILE = 256
KEY_TILE = 512
LOG2E = float(np.log2(np.e))
N_SPLIT = 3
N_AUG = 4 * N_SPLIT + 1
ROW_TILE = 512
VMEM_LIMIT = 56 * 1024 * 1024
GROUP_LANES = HEADS_PER_GROUP * Q_TILE
WIN_KEYS = WINDOW + Q_TILE
N_TILE_ROWS = 16
V_ROWS = HEAD_DIM + 8


def _cparams(n_axes):
    return pltpu.CompilerParams(dimension_semantics=("arbitrary",) * n_axes,
                                vmem_limit_bytes=VMEM_LIMIT)


def _dot(a, b):
    return jnp.dot(a, b, preferred_element_type=F32)


def _dot_nt(a, b):
    return lax.dot_general(a, b, (((1,), (1,)), ((), ())), preferred_element_type=F32)


ROW_TILES = 8


def _store_row_tiles(ref, val):
    n = val.shape[0]
    for s in range(ROW_TILES):
        ref[pl.ds(s, n, stride=ROW_TILES), :] = val[:, s * LANES:(s + 1) * LANES]


def _load_row_tiles(ref, n):
    return jnp.concatenate([ref[pl.ds(s, n, stride=ROW_TILES), :] for s in range(ROW_TILES)], axis=1)


def _rms_pairs(v, bd):
    ss = _dot((v * v).astype(BF16), bd)
    return v * lax.rsqrt(ss + EPS)


def _inproj_body(x_ref, g1_ref, wq_ref, wkv_ref, wng_ref, wcv_ref, wmg_ref, gq_ref, gk_ref, bd_ref,
                 qtab_ref, kwtab_ref,
                 qa_out, hk_out, hv_out, kas_out, kaw_out, vst_out, vwt_out, gate_out, cbv_out, gab_out, raw_ref):
    x = x_ref[...]
    tm = x.shape[0]
    ms = jnp.mean(x * x, axis=-1, keepdims=True)
    h = (x * lax.rsqrt(ms + EPS) * g1_ref[...]).astype(BF16)
    bd = bd_ref[...]
    low = lax.broadcasted_iota(I32, (tm, LANES), 1) < HEAD_DIM

    def place(pair, tab_ref, out_ref, base, slot):
        for j, src in enumerate((pair, pltpu.roll(pair, HEAD_DIM, 1))):
            o = base + j * slot
            out_ref[:, o:o + LANES] = jnp.where(low, src, tab_ref[:, o:o + LANES].astype(F32)).astype(BF16)

    q = _dot(h, wq_ref[...])
    for c in range(N_HEADS * HEAD_DIM // LANES):
        sl = slice(c * LANES, (c + 1) * LANES)
        place(_rms_pairs(q[:, sl], bd) * gq_ref[:, sl], qtab_ref, qa_out, 2 * c * LANES, LANES)
    kv = _dot(h, wkv_ref[...])

    def emit_half_blocks(c, out_ref):
        nb = tm // CMP_STRIDE
        hw = CMP_STRIDE * HEAD_DIM
        raw_ref[...] = kv[:, c * LANES:(c + 1) * LANES]
        lo = lax.broadcasted_iota(I32, (nb, LANES), 1) < HEAD_DIM
        for u in range(CMP_STRIDE // 2):
            t0 = raw_ref[pl.ds(2 * u, nb, stride=CMP_STRIDE), :]
            t1 = raw_ref[pl.ds(2 * u + 1, nb, stride=CMP_STRIDE), :]
            out_ref[:, u * LANES:(u + 1) * LANES] = jnp.where(lo, t0, pltpu.roll(t1, HEAD_DIM, 1)).astype(BF16)
            out_ref[:, hw + u * LANES:hw + (u + 1) * LANES] = (
                jnp.where(lo, pltpu.roll(t0, HEAD_DIM, 1), t1).astype(BF16))

    emit_half_blocks(0, hk_out)
    emit_half_blocks(1, hv_out)
    place(_rms_pairs(kv[:, 256:384], bd) * gk_ref[:, 0:128], kwtab_ref, kas_out, 0, LANES)
    place(_rms_pairs(kv[:, 512:640], bd) * gk_ref[:, 128:256], kwtab_ref, kaw_out, 0, LANES)
    tail = jnp.where(lax.broadcasted_iota(I32, (V_ROWS - HEAD_DIM, tm), 0) == 0, 1.0, 0.0)

    def emit_values(c, out_ref, tile):
        vt = kv[:, c * LANES:(c + 1) * LANES].T
        for g in range(N_KV_HEADS):
            full = jnp.concatenate([vt[g * HEAD_DIM:(g + 1) * HEAD_DIM], tail], axis=0).astype(BF16)
            for j in range(tm // tile):
                out_ref[0, g, j] = full[:, j * tile:(j + 1) * tile]

    emit_values(3, vst_out, KEY_TILE)
    emit_values(5, vwt_out, Q_TILE)
    gate_out[...] = jax.nn.sigmoid(_dot(h, wng_ref[...]))
    cv = _dot(h, wcv_ref[...])
    cw = CONV_WIDTH
    cbv_out[:, 0:cw] = cv[:, 0:cw].astype(BF16)
    cbv_out[:, cw:2 * cw] = (cv[:, cw:2 * cw] * cv[:, 2 * cw:3 * cw]).astype(BF16)
    gab_out[...] = jax.nn.sigmoid(_dot(h, wmg_ref[...])).astype(BF16)


def _bf16_pieces(v):
    def round_bf16(a):
        u = np.ascontiguousarray(a, np.float32).view(np.uint32).astype(np.uint64)
        return ((u + 0x7FFF + ((u >> 16) & 1)) & 0xFFFF0000).astype(np.uint32).view(np.float32)

    pieces, rest = [], np.asarray(v, np.float32)
    for _ in range(N_SPLIT):
        pieces.append(round_bf16(rest))
        rest = rest - pieces[-1]
    return pieces


def _key_aug(pos):
    one = np.ones_like(pos, np.float32)
    hi = (pos // 64 * 64).astype(np.float32)
    lo = (pos % 64).astype(np.float32)
    return np.stack([one] * (2 * N_SPLIT) + [hi] * N_SPLIT + [lo] * N_SPLIT + [0 * one], axis=-1)


def _slot_table(aug, slot):
    S, n, _ = aug.shape
    tab = np.zeros((S, n, slot), np.float32)
    tab[:, :, HEAD_DIM:HEAD_DIM + N_AUG] = aug
    return jnp.asarray(tab.reshape(S, n * slot), BF16)


def _inproj(x2, g1, w_in, g_q, g_ks, g_kw, S):
    T, D = x2.shape
    H, G = N_HEADS, N_KV_HEADS
    aw = H * HEAD_DIM
    kvw = G * HEAD_DIM
    o = 0
    wq = w_in[:, o:o + aw]; o += aw
    wkv = w_in[:, o:o + 6 * kvw]; o += 6 * kvw
    wng = w_in[:, o:o + 3 * H]; o += 3 * H
    wcv = w_in[:, o:o + 3 * CONV_WIDTH]; o += 3 * CONV_WIDTH
    wmg = w_in[:, o:o + 2 * D]
    wng = jnp.pad(wng, ((0, 0), (0, LANES - 3 * H)))
    wq, wkv, wng, wcv, wmg = (w.astype(BF16) for w in (wq, wkv, wng, wcv, wmg))
    gq = (jnp.tile(g_q, H) * (HEAD_DIM ** -0.5 * LOG2E)).reshape(1, aw)
    gk = jnp.concatenate([jnp.tile(g_ks, G), jnp.tile(g_kw, G)]).reshape(1, 2 * kvw)
    idx = np.arange(LANES) // HEAD_DIM
    bd = jnp.asarray((idx[:, None] == idx[None, :]).astype(np.float32) / HEAD_DIM, BF16)
    pos = np.arange(S)
    hi = (pos // 64 * 64).astype(np.float64)[:, None]
    lo = (pos % 64).astype(np.float64)[:, None]
    c = LOG2E * 2.0 ** (-8.0 * np.arange(1, H + 1) / H)[None, :]
    cs = np.broadcast_to(c, (S, H))
    aq = np.stack(_bf16_pieces(-c * hi) + _bf16_pieces(-c * lo) + _bf16_pieces(cs) + _bf16_pieces(cs)
                  + [np.ones((S, H), np.float32)], axis=-1)
    ak = np.broadcast_to(_key_aug(pos)[:, None, :], (S, G, N_AUG))
    qtab = _slot_table(aq, LANES)
    kwtab = _slot_table(ak, LANES)
    tm = ROW_TILE
    nst = S // tm
    row = lambda w: pl.BlockSpec((tm, w), lambda i: (i, 0))
    full = lambda a: pl.BlockSpec(a.shape, lambda i: (0,) * a.ndim)
    tab = lambda a: pl.BlockSpec((tm, a.shape[1]), lambda i: (i % nst, 0))
    ins = (x2, g1.reshape(1, D), wq, wkv, wng, wcv, wmg, gq, gk, bd, qtab, kwtab)
    widths = (H * LANES, G * LANES, G * LANES, LANES, 2 * CONV_WIDTH, 2 * D)
    dtypes = (BF16, BF16, BF16, F32, BF16, BF16)
    hw = G * CMP_STRIDE * HEAD_DIM
    nb = tm // CMP_STRIDE
    half = pl.BlockSpec((nb, hw), lambda i: (i, 0))
    half_shape = jax.ShapeDtypeStruct((T // CMP_STRIDE, hw), BF16)
    assert tm % KEY_TILE == 0 and tm % Q_TILE == 0

    def values(tile):
        return (pl.BlockSpec((1, G, tm // tile, V_ROWS, tile), lambda i: (i // nst, 0, i % nst, 0, 0)),
                jax.ShapeDtypeStruct((T // S, G, S // tile, V_ROWS, tile), BF16))

    rows = [(row(w), jax.ShapeDtypeStruct((T, w), dt)) for w, dt in zip(widths, dtypes)]
    outs = [rows[0], (half, half_shape), (half, half_shape), rows[1], rows[2], values(KEY_TILE),
            values(Q_TILE)] + rows[3:]
    return pl.pallas_call(
        _inproj_body,
        grid=(T // tm,),
        in_specs=[row(D)] + [full(a) for a in ins[1:10]] + [tab(a) for a in ins[10:]],
        out_specs=[o[0] for o in outs],
        out_shape=[o[1] for o in outs],
        scratch_shapes=[pltpu.VMEM((tm, LANES), F32)],
        compiler_params=_cparams(1),
        name="inproj",
    )(*ins)


def _compress_body(h_ref, w1_ref, pe_ref, b1_ref, w2_ref, b2_ref, g_ref, tab_ref, o_ref, *, for_keys):
    hb = h_ref[0]
    nc = hb.shape[0]
    a = _dot(hb, w1_ref[0])
    b = _dot(hb, w1_ref[1])
    c = _dot(pe_ref[0], w1_ref[0]) + _dot(pe_ref[1], w1_ref[1])
    pre = a + pltpu.roll(b, nc - 1, 0) + c[0:1, :] + b1_ref[...]
    hid = jax.nn.gelu(pre)
    out = _dot(hid.astype(BF16), w2_ref[...]) + b2_ref[...]
    if for_keys:
        ms = jnp.sum(out * out, axis=-1, keepdims=True) * (1.0 / HEAD_DIM)
        out = out * lax.rsqrt(ms + EPS) * g_ref[...]
        low = lax.broadcasted_iota(I32, out.shape, 1) < HEAD_DIM
        o_ref[0, 0] = jnp.where(low, out, tab_ref[...]).astype(BF16)
    else:
        o_ref[0, 0] = out.T[0:HEAD_DIM, :].astype(BF16)


def _compress(hh, pe, w1, b1, w2, b2, gain, for_keys):
    B, NC, _ = hh.shape
    G, HW = N_KV_HEADS, CMP_STRIDE * HEAD_DIM
    w1s = w1.reshape(2, HW, CMP_HID).astype(BF16)
    pes = jnp.broadcast_to(pe.reshape(2, 1, HW), (2, 8, HW)).astype(BF16)
    padl = lambda a: jnp.pad(a, ((0, 0), (0, LANES - HEAD_DIM)))
    tabn = np.zeros((NC, LANES), np.float32)
    tabn[:, HEAD_DIM:HEAD_DIM + N_AUG] = _key_aug(np.arange(NC) * CMP_STRIDE + (CMP_LEN - 1))
    tab = jnp.asarray(tabn)
    full = lambda a: pl.BlockSpec(a.shape, lambda b, g: (0,) * a.ndim)
    ins = (hh, w1s, pes, b1.reshape(1, CMP_HID), padl(w2).astype(BF16), padl(b2.reshape(1, HEAD_DIM)),
           padl(gain.reshape(1, HEAD_DIM)), tab)
    oshape = (B, G, NC, LANES) if for_keys else (B, G, HEAD_DIM, NC)
    return pl.pallas_call(
        functools.partial(_compress_body, for_keys=for_keys),
        grid=(B, G),
        in_specs=[pl.BlockSpec((1, NC, HW), lambda b, g: (b, 0, g))] + [full(a) for a in ins[1:]],
        out_specs=pl.BlockSpec((1, 1) + oshape[2:], lambda b, g: (b, g, 0, 0)),
        out_shape=jax.ShapeDtypeStruct(oshape, BF16),
        compiler_params=_cparams(2),
        name="compress_keys" if for_keys else "compress_values",
    )(*ins)


def _attn_body(qa_ref, g_ref, kca_ref, vct_ref, kas_ref, vst_ref, kaw_ref, vwt_ref, selmapt_ref, wbias_ref,
               tilemap_ref, cbias_ref, o_ref, selb_ref, m_ref, acc_ref, s0_ref, s1_ref, flagv_ref,
               flags_ref, list_ref, sem,
               *, n_sel):
    i = pl.program_id(2)
    q0 = i * Q_TILE
    gl = GROUP_LANES
    qa = jnp.concatenate([qa_ref[0, :, h * LANES:(h + 1) * LANES] for h in range(HEADS_PER_GROUP)], axis=0)

    nc = kca_ref.shape[2]
    s = _dot_nt(kca_ref[0, 0], qa)
    cb = cbias_ref[pl.ds(pl.multiple_of(nc - i * (Q_TILE // CMP_STRIDE), 8), nc), :]
    s = s + jnp.concatenate([cb] * HEADS_PER_GROUP, axis=1)
    m = jnp.max(s, axis=0, keepdims=True)
    p = jnp.exp2(s - m)
    l = jnp.sum(p, axis=0, keepdims=True)
    has_entry = (q0 + (lax.broadcasted_iota(I32, (1, gl), 1) & (Q_TILE - 1))) >= CMP_LEN - 1
    pc = p * jnp.where(has_entry, 1.0 / l, 0.0)
    o_c = _dot(vct_ref[0, 0], pc.astype(BF16))

    ps = pc[:, 0:Q_TILE]
    for h in range(1, HEADS_PER_GROUP):
        ps = ps + pc[:, h * Q_TILE:(h + 1) * Q_TILE]
    ps_hi = ps.astype(BF16)
    ps_lo = (ps - ps_hi.astype(F32)).astype(BF16)
    imp = _dot(selmapt_ref[...], ps_hi) + _dot(selmapt_ref[...], ps_lo)
    jb = lax.broadcasted_iota(I32, (LANES, Q_TILE), 0)
    cur = (q0 + lax.broadcasted_iota(I32, (LANES, Q_TILE), 1)) // SEL_BLK
    forced = (jb == 0) | (jb == cur) | (jb == cur - 1)
    score = jnp.where(forced, -jnp.inf, jnp.where(jb > cur, NEG, imp))
    jbf = jb.astype(F32)
    for _ in range(n_sel - N_FORCED):
        mx = jnp.max(score, axis=0, keepdims=True)
        first = jnp.min(jnp.where(score == mx, jbf, float(LANES)), axis=0, keepdims=True)
        score = jnp.where(jbf == first, -jnp.inf, score)
    picked = score == -jnp.inf
    bias_t = jnp.where(picked, 0.0, NEG)
    selb_ref[...] = jnp.concatenate([bias_t] * HEADS_PER_GROUP, axis=1)
    tile_hits = jnp.max(_dot(tilemap_ref[...], jnp.where(picked, 1.0, 0.0).astype(BF16)), axis=1, keepdims=True)
    flagv_ref[...] = jnp.broadcast_to(tile_hits, flagv_ref.shape).astype(I32)
    flag_copy = pltpu.make_async_copy(flagv_ref, flags_ref, sem)
    flag_copy.start()

    w0 = pl.multiple_of(jnp.maximum(q0 - WINDOW, 0), Q_TILE)
    sw = _dot_nt(kaw_ref[0, pl.ds(w0, WIN_KEYS), :], qa)
    wb = wbias_ref[jnp.minimum(i, WINDOW // Q_TILE)]
    sw = sw + jnp.concatenate([wb] * HEADS_PER_GROUP, axis=1)
    mw = jnp.max(sw, axis=0, keepdims=True)
    pw = jnp.exp2(sw - mw)
    c0 = w0 // Q_TILE
    vw = jnp.concatenate([vwt_ref[0, 0, c0 + j] for j in range(WIN_KEYS // Q_TILE)], axis=1)
    aw = _dot(vw, pw.astype(BF16))
    o_w = aw[0:HEAD_DIM] * (1.0 / aw[HEAD_DIM:HEAD_DIM + 1])

    def scores(kt):
        sc = _dot_nt(kas_ref[0, pl.ds(pl.multiple_of(kt * KEY_TILE, KEY_TILE), KEY_TILE), :], qa)
        blocks = KEY_TILE // SEL_BLK
        mask = [jnp.broadcast_to(selb_ref[pl.ds(kt * blocks + j, 1), :], (SEL_BLK, gl)) for j in range(blocks)]
        return sc + jnp.concatenate(mask, axis=0)

    n_full = q0 // KEY_TILE
    key = n_full * KEY_TILE + lax.broadcasted_iota(I32, (KEY_TILE, gl), 0)
    qry = q0 + (lax.broadcasted_iota(I32, (KEY_TILE, gl), 1) & (Q_TILE - 1))
    s0_ref[...] = jnp.where(key <= qry, scores(n_full), NEG)
    m_ref[...] = jnp.full(m_ref.shape, -3.0e38, F32)
    acc_ref[...] = jnp.zeros(acc_ref.shape, F32)
    list_ref[0] = n_full

    def absorb(s_ref, kt, live):
        sc = s_ref[...]
        m_old = m_ref[...]
        m_new = jnp.where(live, jnp.maximum(m_old, jnp.max(sc, axis=0, keepdims=True)), m_old)
        pv = _dot(vst_ref[0, 0, kt], jnp.exp2(sc - m_new).astype(BF16))
        acc_ref[...] = jnp.exp2(m_old - m_new) * acc_ref[...] + jnp.where(live, pv, 0.0)
        m_ref[...] = m_new

    flag_copy.wait()

    def compact(kt, n):
        active = flags_ref[kt, 0] > 0

        @pl.when(active)
        def _():
            list_ref[n] = kt

        return n + active.astype(I32)

    n_items = lax.fori_loop(0, n_full, compact, 1)

    last = n_items - 1

    def pair_body(j, carry):
        a = list_ref[2 * j]
        b = list_ref[jnp.minimum(2 * j + 1, last)]
        nxt = list_ref[jnp.minimum(2 * j + 2, last)]
        s1_ref[...] = scores(b)
        absorb(s0_ref, a, True)
        s0_ref[...] = scores(nxt)
        absorb(s1_ref, b, 2 * j + 1 <= last)
        return carry

    lax.fori_loop(0, (n_items + 1) // 2, pair_body, 0)

    o_s = acc_ref[0:HEAD_DIM, :] * (1.0 / acc_ref[HEAD_DIM:HEAD_DIM + 1, :])

    g = g_ref[0, 0]
    outs = []
    for h in range(HEADS_PER_GROUP):
        sl = slice(h * Q_TILE, (h + 1) * Q_TILE)
        outs.append(g[3 * h:3 * h + 1, :] * o_c[:, sl] + g[3 * h + 1:3 * h + 2, :] * o_s[:, sl]
                    + g[3 * h + 2:3 * h + 3, :] * o_w[:, sl])
    o_ref[0] = jnp.concatenate(outs, axis=0).T.astype(BF16)


def _attention(qa, gates_t, kca, vct, kas, vst, kaw, vwt, B, S):
    G, hd = N_KV_HEADS, HEAD_DIM
    NC = kca.shape[2]
    n_blk = S // SEL_BLK
    assert n_blk <= LANES and S % KEY_TILE == 0 and S >= WIN_KEYS
    n_sel = min(SEL_TOPN, n_blk)
    assert n_sel > N_FORCED
    ratio, span = SEL_BLK // CMP_STRIDE, CMP_LEN // CMP_STRIDE
    sm = np.zeros((LANES, NC), np.float32)
    for j in range(n_blk):
        for a in range(ratio):
            for b in range(span):
                n = ratio * j + a - b
                if 0 <= n < NC - 1:
                    sm[j, n] += 1.0
    selmapt = jnp.asarray(sm, BF16)
    c = np.arange(WIN_KEYS)[:, None]
    r = np.arange(Q_TILE)[None, :]
    offs = np.arange(WINDOW // Q_TILE + 1)[:, None, None] * Q_TILE
    wbias = jnp.asarray(np.where((c - r <= offs) & (c - r > offs - WINDOW), 0.0, NEG), F32)
    tilemap = jnp.asarray(np.arange(LANES)[None, :] // (KEY_TILE // SEL_BLK) == np.arange(N_TILE_ROWS)[:, None],
                          BF16)
    assert S // KEY_TILE <= N_TILE_ROWS
    u = np.arange(2 * NC)[:, None] - NC
    cbias = jnp.asarray(np.where(CMP_STRIDE * u + (CMP_LEN - 1) <= np.arange(Q_TILE)[None, :], 0.0, NEG), F32)

    hpg = HEADS_PER_GROUP
    grp = lambda *blk: pl.BlockSpec((1, 1) + blk, lambda b, g, i: (b, g) + (0,) * len(blk))
    seq = lambda w: pl.BlockSpec((1, S, w), lambda b, g, i: (b, 0, g))
    const = lambda a: pl.BlockSpec(a.shape, lambda b, g, i: (0,) * a.ndim)
    return pl.pallas_call(
        functools.partial(_attn_body, n_sel=n_sel),
        grid=(B, G, S // Q_TILE),
        in_specs=[pl.BlockSpec((1, Q_TILE, hpg * LANES), lambda b, g, i: (b, i, g)),
                  pl.BlockSpec((1, 1, 16, Q_TILE), lambda b, g, i: (b, g, 0, i)),
                  grp(NC, LANES), grp(hd, NC), seq(LANES), grp(S // KEY_TILE, V_ROWS, KEY_TILE),
                  seq(LANES), grp(S // Q_TILE, V_ROWS, Q_TILE), const(selmapt), const(wbias), const(tilemap),
                  const(cbias)],
        out_specs=pl.BlockSpec((1, Q_TILE, hpg * hd), lambda b, g, i: (b, i, g)),
        out_shape=jax.ShapeDtypeStruct((B, S, N_HEADS * hd), BF16),
        scratch_shapes=[pltpu.VMEM((LANES, GROUP_LANES), F32), pltpu.VMEM((1, GROUP_LANES), F32),
                        pltpu.VMEM((V_ROWS, GROUP_LANES), F32), pltpu.VMEM((KEY_TILE, GROUP_LANES), F32),
                        pltpu.VMEM((KEY_TILE, GROUP_LANES), F32),
                        pltpu.VMEM((N_TILE_ROWS, LANES), I32), pltpu.SMEM((N_TILE_ROWS, LANES), I32),
                        pltpu.SMEM((N_TILE_ROWS,), I32), pltpu.SemaphoreType.DMA(())],
        compiler_params=_cparams(3),
        name="nsa_attention",
    )(qa, gates_t, kca, vct, kas, vst, kaw, vwt, selmapt, wbias, tilemap, cbias)


def _mixer_out_body(x_ref, oa_ref, cbv_ref, halo_ref, gab_ref, cw_ref, wpa_ref, wpb_ref, wo_ref,
                    g2_ref, whi_ref, wlo_ref, br_ref, tri_ref, o_ref, h_out, mi_out, mf_out, cnt_out,
                    *, seq_len):
    i = pl.program_id(0)
    tm = x_ref.shape[0]
    cwd = CONV_WIDTH
    d = x_ref.shape[1]
    v = cbv_ref[:, cwd:2 * cwd].astype(F32)
    prev = halo_ref[:, cwd:2 * cwd].astype(F32)
    keep = ((i * tm) % seq_len != 0).astype(F32)
    p1 = prev[7:8, :] * keep
    p2 = prev[6:7, :] * keep
    ridx = lax.broadcasted_iota(I32, (tm, cwd), 0)
    v1 = jnp.where(ridx == 0, p1, pltpu.roll(v, 1, 0))
    v2 = jnp.where(ridx == 0, p2, jnp.where(ridx == 1, p1, pltpu.roll(v, 2, 0)))
    y = cw_ref[0:1, :] * v2 + cw_ref[1:2, :] * v1 + cw_ref[2:3, :] * v
    yb_in = (cbv_ref[:, 0:cwd].astype(F32) * y).astype(BF16)
    y_a = _dot(oa_ref[...], wpa_ref[...])
    y_b = _dot(yb_in, wpb_ref[...])
    merged = gab_ref[:, 0:d].astype(F32) * y_a + gab_ref[:, d:2 * d].astype(F32) * y_b
    x1 = x_ref[...] + _dot(merged.astype(BF16), wo_ref[...])
    o_ref[...] = x1
    _route(x1, g2_ref, whi_ref, wlo_ref, br_ref, tri_ref, h_out, mi_out, mf_out, cnt_out)


def _mixer_out(x2, oa, cbv, gab, conv_w, w_pa, w_pb, w_o, seq_len, g2, w_r, b_r):
    T, D = x2.shape
    assert D == ROW_TILES * LANES
    tm = ROW_TILE
    cw8 = jnp.pad(conv_w, ((0, 8 - CONV_K), (0, 0)))
    wpad = jnp.pad(w_r, ((0, 0), (0, LANES - N_EXPERTS)))
    whi = wpad.astype(BF16)
    wlo = (wpad - whi.astype(F32)).astype(BF16)
    br = jnp.pad(b_r, (0, LANES - N_EXPERTS)).reshape(1, LANES)
    tri = jnp.asarray(np.tril(np.ones((tm, tm), np.float32), -1), BF16)
    row = lambda w: pl.BlockSpec((tm, w), lambda i: (i, 0))
    full = lambda a: pl.BlockSpec(a.shape, lambda i: (0,) * a.ndim)
    halo = pl.BlockSpec((8, cbv.shape[1]), lambda i: (jnp.maximum(i * (tm // 8) - 1, 0), 0))
    wts = (cw8, w_pa.astype(BF16), w_pb.astype(BF16), w_o.astype(BF16), g2.reshape(1, D), whi, wlo, br, tri)
    return pl.pallas_call(
        functools.partial(_mixer_out_body, seq_len=seq_len),
        grid=(T // tm,),
        in_specs=[row(D), row(oa.shape[1]), row(cbv.shape[1]), halo, row(gab.shape[1])] + [full(a) for a in wts],
        out_specs=[row(D), pl.BlockSpec((tm * ROW_TILES, LANES), lambda i: (i, 0)), row(2 * TOP_K), row(LANES),
                   pl.BlockSpec((8, LANES), lambda i: (0, 0))],
        out_shape=[jax.ShapeDtypeStruct((T, D), F32), jax.ShapeDtypeStruct((T * ROW_TILES, LANES), F32),
                   jax.ShapeDtypeStruct((T, 2 * TOP_K), I32), jax.ShapeDtypeStruct((T, LANES), F32),
                   jax.ShapeDtypeStruct((8, LANES), F32)],
        compiler_params=_cparams(1),
        name="mixer_out_router",
    )(x2, oa, cbv, cbv, gab, *wts)


def _route(x, g2_ref, whi_ref, wlo_ref, br_ref, tri_ref, h_out, mi_out, mf_out, cnt_out):
    i = pl.program_id(0)

    @pl.when(i == 0)
    def _():
        cnt_out[...] = jnp.zeros(cnt_out.shape, F32)

    tm = x.shape[0]
    ms = jnp.mean(x * x, axis=-1, keepdims=True)
    h = x * lax.rsqrt(ms + EPS) * g2_ref[...]
    _store_row_tiles(h_out, h)
    h_hi = h.astype(BF16)
    h_lo = (h - h_hi.astype(F32)).astype(BF16)
    logits = (_dot(h_hi, whi_ref[...]) + _dot(h_lo, whi_ref[...]) + _dot(h_hi, wlo_ref[...])) + br_ref[...]
    lane = lax.broadcasted_iota(I32, (tm, LANES), 1)
    lanef = lane.astype(F32)
    work = jnp.where(lane < N_EXPERTS, logits, -jnp.inf)
    vals, hits = [], []
    for _ in range(TOP_K):
        mx = jnp.max(work, axis=-1, keepdims=True)
        first = jnp.min(jnp.where(work == mx, lanef, float(LANES)), axis=-1, keepdims=True)
        hit = lanef == first
        vals.append(mx)
        hits.append(hit)
        work = jnp.where(hit, -jnp.inf, work)
    ex = [jnp.exp(v - vals[0]) for v in vals]
    den = ex[0]
    for e in ex[1:]:
        den = den + e
    cnt = jnp.zeros((tm, LANES), F32)
    for hit in hits:
        cnt = cnt + hit.astype(F32)
    before = _dot(tri_ref[...], cnt.astype(BF16)) + cnt_out[0:1, :]
    mi = jnp.zeros((tm, LANES), F32)
    mf = jnp.zeros((tm, LANES), F32)
    for k, hit in enumerate(hits):
        e_k = jnp.sum(jnp.where(hit, lanef, 0.0), axis=-1, keepdims=True)
        r_k = jnp.sum(jnp.where(hit, before, 0.0), axis=-1, keepdims=True)
        mi = jnp.where(lane == k, e_k, jnp.where(lane == TOP_K + k, r_k, mi))
        mf = jnp.where(lane == k, ex[k] / den, mf)
    mi_out[...] = mi[:, 0:2 * TOP_K].astype(I32)
    mf_out[...] = mf
    cnt_out[...] = cnt_out[...] + jnp.sum(cnt, axis=0, keepdims=True)


DISPATCH_TILE = 512
DMA_UNROLL = 8


def _dispatch_body(dest_ref, last_ref, h_ref, o_hbm, zero_ref, sem, zsem):
    @pl.when(pl.program_id(0) == 0)
    def _():
        zero_ref[...] = jnp.zeros(zero_ref.shape, zero_ref.dtype)

        def clear(e):
            start = pl.multiple_of(last_ref[e] * ROW_TILES, MOE_CHUNK * ROW_TILES)
            return pltpu.make_async_copy(zero_ref, o_hbm.at[pl.ds(start, MOE_CHUNK * ROW_TILES)], zsem)

        for e in range(N_EXPERTS):
            @pl.when(last_ref[e] >= 0)
            def _():
                clear(e).start()
        for e in range(N_EXPERTS):
            @pl.when(last_ref[e] >= 0)
            def _():
                clear(e).wait()

    def row_copy(r, d):
        return pltpu.make_async_copy(h_ref.at[pl.ds(pl.multiple_of(r * ROW_TILES, ROW_TILES), ROW_TILES)],
                                     o_hbm.at[pl.ds(pl.multiple_of(d * ROW_TILES, ROW_TILES), ROW_TILES)], sem)

    def start(r, c):
        for k in range(TOP_K):
            row_copy(r, dest_ref[0, 0, r * TOP_K + k]).start(priority=k % 2)
        return c

    def wait(r, c):
        for k in range(TOP_K):
            row_copy(0, 0).wait()
        return c

    lax.fori_loop(0, DISPATCH_TILE, start, 0, unroll=DMA_UNROLL)
    lax.fori_loop(0, DISPATCH_TILE, wait, 0, unroll=DMA_UNROLL)


def _dispatch(h2, dest, last_chunk, n_rows):
    T = h2.shape[0] // ROW_TILES
    td = DISPATCH_TILE
    dest3 = dest.reshape(T // td, 1, td * TOP_K)
    return pl.pallas_call(
        _dispatch_body,
        grid=(T // td,),
        in_specs=[pl.BlockSpec((1, 1, td * TOP_K), lambda i: (i, 0, 0), memory_space=pltpu.SMEM),
                  pl.BlockSpec(memory_space=pltpu.SMEM),
                  pl.BlockSpec((td * ROW_TILES, LANES), lambda i: (i, 0))],
        out_specs=pl.BlockSpec(memory_space=pl.ANY),
        out_shape=jax.ShapeDtypeStruct((n_rows * ROW_TILES, LANES), h2.dtype),
        scratch_shapes=[pltpu.VMEM((MOE_CHUNK * ROW_TILES, LANES), h2.dtype), pltpu.SemaphoreType.DMA(()),
                        pltpu.SemaphoreType.DMA(())],
        compiler_params=_cparams(1),
        name="dispatch",
    )(dest3, last_chunk, h2)


def _expert_body(ce_ref, nu_ref, x_ref, wgu_ref, bgu_ref, wdn_ref, bdn_ref, o_ref, wgu_bf, wdn_bf):
    c = pl.program_id(0)
    dff = wdn_ref.shape[1]

    @pl.when((c == 0) | (ce_ref[c] != ce_ref[jnp.maximum(c - 1, 0)]))
    def _():
        wgu_bf[...] = wgu_ref[0].astype(BF16)
        wdn_bf[...] = wdn_ref[0].astype(BF16)

    @pl.when(c < nu_ref[0])
    def _():
        x = _load_row_tiles(x_ref, MOE_CHUNK)
        gu = _dot(x.astype(BF16), wgu_bf[...]) + bgu_ref[0]
        g = jnp.minimum(gu[:, 0:dff], SWIGLU_LIMIT)
        u = jnp.clip(gu[:, dff:2 * dff], -SWIGLU_LIMIT, SWIGLU_LIMIT)
        act = (u + 1.0) * (g * jax.nn.sigmoid(SWIGLU_ALPHA * g))
        _store_row_tiles(o_ref, _dot(act.astype(BF16), wdn_bf[...]) + bdn_ref[0])

    @pl.when(c >= nu_ref[0])
    def _():
        o_ref[...] = jnp.zeros(o_ref.shape, F32)


def _experts(hperm, chunk_e, n_used, w_gu, b_gu, w_dn, b_dn):
    E, D, F2 = w_gu.shape
    assert D == ROW_TILES * LANES
    P = hperm.shape[0] // ROW_TILES
    dff = F2 // 2
    n_chunks = P // MOE_CHUNK
    chunk = (MOE_CHUNK * ROW_TILES, LANES)
    grid_spec = pltpu.PrefetchScalarGridSpec(
        num_scalar_prefetch=2,
        grid=(n_chunks,),
        in_specs=[pl.BlockSpec(chunk, lambda c, ce, nu: (jnp.minimum(c, nu[0] - 1), 0)),
                  pl.BlockSpec((1, D, F2), lambda c, ce, nu: (ce[c], 0, 0)),
                  pl.BlockSpec((1, 1, F2), lambda c, ce, nu: (ce[c], 0, 0)),
                  pl.BlockSpec((1, dff, D), lambda c, ce, nu: (ce[c], 0, 0)),
                  pl.BlockSpec((1, 1, D), lambda c, ce, nu: (ce[c], 0, 0))],
        out_specs=pl.BlockSpec(chunk, lambda c, ce, nu: (c, 0)),
        scratch_shapes=[pltpu.VMEM((D, F2), BF16), pltpu.VMEM((dff, D), BF16)],
    )
    return pl.pallas_call(
        _expert_body,
        grid_spec=grid_spec,
        out_shape=jax.ShapeDtypeStruct(hperm.shape, F32),
        compiler_params=_cparams(1),
        name="experts",
    )(chunk_e, n_used, hperm, w_gu, b_gu.reshape(E, 1, F2), w_dn, b_dn.reshape(E, 1, D))


COMBINE_TILE = 256


def _combine_body(dest_ref, next_ref, x_ref, w_ref, y_hbm, o_ref, buf_ref, sems):
    i = pl.program_id(0)
    slot = i % 2

    def row_copy(s, r, k, d):
        return pltpu.make_async_copy(y_hbm.at[pl.ds(pl.multiple_of(d * ROW_TILES, ROW_TILES), ROW_TILES)],
                                     buf_ref.at[s, k, pl.ds(pl.multiple_of(r * ROW_TILES, ROW_TILES), ROW_TILES)],
                                     sems.at[s])

    def fetch(idx_ref, s):
        def start(r, c):
            for k in range(TOP_K):
                row_copy(s, r, k, idx_ref[0, 0, r * TOP_K + k]).start(priority=k % 2)
            return c

        lax.fori_loop(0, COMBINE_TILE, start, 0, unroll=DMA_UNROLL)

    @pl.when(i == 0)
    def _():
        fetch(dest_ref, slot)

    @pl.when(i + 1 < pl.num_programs(0))
    def _():
        fetch(next_ref, 1 - slot)

    def wait(r, c):
        for k in range(TOP_K):
            row_copy(slot, 0, 0, 0).wait()
        return c

    lax.fori_loop(0, COMBINE_TILE, wait, 0, unroll=DMA_UNROLL)
    gate = [jnp.broadcast_to(w_ref[:, k:k + 1], (COMBINE_TILE, LANES)) for k in range(TOP_K)]
    for s in range(ROW_TILES):
        sl = slice(s * LANES, (s + 1) * LANES)
        out = x_ref[:, sl]
        for k in range(TOP_K):
            out = out + gate[k] * buf_ref[slot, k, pl.ds(s, COMBINE_TILE, stride=ROW_TILES), :]
        o_ref[:, sl] = out


def _combine(x1, gate_w, dest, ys):
    T, D = x1.shape
    tc = COMBINE_TILE
    n = T // tc
    dest3 = dest.reshape(n, 1, tc * TOP_K)
    row = lambda w: pl.BlockSpec((tc, w), lambda i: (i, 0))
    idx = lambda f: pl.BlockSpec((1, 1, tc * TOP_K), f, memory_space=pltpu.SMEM)
    return pl.pallas_call(
        _combine_body,
        grid=(n,),
        in_specs=[idx(lambda i: (i, 0, 0)), idx(lambda i: (jnp.minimum(i + 1, n - 1), 0, 0)),
                  row(D), row(LANES), pl.BlockSpec(memory_space=pl.ANY)],
        out_specs=row(D),
        out_shape=jax.ShapeDtypeStruct((T, D), F32),
        scratch_shapes=[pltpu.VMEM((2, TOP_K, tc * ROW_TILES, LANES), F32), pltpu.SemaphoreType.DMA((2,))],
        compiler_params=_cparams(1),
        name="combine",
    )(dest3, dest3, x1, gate_w, ys)


def _mixer(x2, B, S, g_norm1, w_in, g_q, g_kc, g_ks, g_kw, pe_k, ck_w1, ck_b1, ck_w2, ck_b2,
           pe_v, cv_w1, cv_b1, cv_w2, cv_b2, conv_w, w_pa, w_pb, w_o, g_norm2, w_r, b_r):
    T, D = x2.shape
    G, H, hd = N_KV_HEADS, N_HEADS, HEAD_DIM
    qa, hk, hv, kas, kaw, vst, vwt, gates, cbv, gab = _inproj(x2, g_norm1, w_in, g_q, g_ks, g_kw, S)
    nh = S // CMP_STRIDE
    kca = _compress(hk.reshape(B, nh, -1), pe_k, ck_w1, ck_b1, ck_w2, ck_b2, g_kc, True)
    vct = _compress(hv.reshape(B, nh, -1), pe_v, cv_w1, cv_b1, cv_w2, cv_b2, jnp.ones((hd,), F32), False)
    gat = gates[:, :3 * H].reshape(B, S, G, 3 * HEADS_PER_GROUP).transpose(0, 2, 3, 1)
    gat = jnp.pad(gat, ((0, 0), (0, 0), (0, 16 - 3 * HEADS_PER_GROUP), (0, 0)))
    o = _attention(qa.reshape(B, S, -1), gat, kca, vct, kas.reshape(B, S, -1), vst, kaw.reshape(B, S, -1),
                   vwt, B, S)
    return _mixer_out(x2, o.reshape(T, H * hd), cbv, gab, conv_w, w_pa, w_pb, w_o, S, g_norm2, w_r, b_r)


def _moe(x1, h2, mi, mf, cnt, w_gu, b_gu, w_dn, b_dn):
    T, D = x1.shape
    top_e = mi[:, 0:TOP_K]
    rank = mi[:, TOP_K:2 * TOP_K]
    counts = cnt[0, :N_EXPERTS].astype(I32)
    padded = (counts + MOE_CHUNK - 1) // MOE_CHUNK * MOE_CHUNK
    pend = jnp.cumsum(padded)
    poffs = pend - padded
    dest = (poffs[top_e] + rank).reshape(-1)
    n_chunks = (T * TOP_K + MOE_CHUNK - 1) // MOE_CHUNK + N_EXPERTS
    chunk_start = jnp.arange(n_chunks, dtype=I32) * MOE_CHUNK
    chunk_e = jnp.minimum(jnp.sum((pend[None, :] <= chunk_start[:, None]).astype(I32), axis=1), N_EXPERTS - 1)
    n_used = (pend[-1:] // MOE_CHUNK).astype(I32)
    last_chunk = jnp.where(padded > 0, pend - MOE_CHUNK, -1).astype(I32)
    hperm = _dispatch(h2, dest, last_chunk, n_chunks * MOE_CHUNK)
    ys = _experts(hperm, chunk_e, n_used, w_gu, b_gu, w_dn, b_dn)
    return _combine(x1, mf, dest, ys)


def kernel(x, g_norm1, w_in, g_q, g_kc, g_ks, g_kw, pe_k, ck_w1, ck_b1, ck_w2, ck_b2, pe_v, cv_w1, cv_b1,
           cv_w2, cv_b2, conv_w, w_pa, w_pb, w_o, g_norm2, w_r, b_r, w_gu, b_gu, w_dn, b_dn):
    B, S, D = x.shape
    x2 = x.reshape(B * S, D)
    for l in range(g_norm1.shape[0]):
        routed = _mixer(x2, B, S, g_norm1[l], w_in[l], g_q[l], g_kc[l], g_ks[l], g_kw[l], pe_k[l], ck_w1[l],
                        ck_b1[l], ck_w2[l], ck_b2[l], pe_v[l], cv_w1[l], cv_b1[l], cv_w2[l], cv_b2[l],
                        conv_w[l], w_pa[l], w_pb[l], w_o[l], g_norm2[l], w_r[l], b_r[l])
        x2 = _moe(*routed, w_gu[l], b_gu[l], w_dn[l], b_dn[l])
    return x2.reshape(B, S, D)
```

```python
import functools

import numpy as np
import jax
import jax.numpy as jnp
from jax import lax
from jax.experimental import pallas as pl
from jax.experimental.pallas import tpu as pltpu

F32 = jnp.float32
BF16 = jnp.bfloat16
I32 = jnp.int32

N_HEADS = 8
HEAD_DIM = 64
N_KV_HEADS = 2
HEADS_PER_GROUP = N_HEADS // N_KV_HEADS
CMP_LEN = 32
CMP_STRIDE = 16
CMP_HID = 256
SEL_BLK = 64
SEL_TOPN = 16
WINDOW = 512
CONV_WIDTH = 512
CONV_K = 3
N_EXPERTS = 32
TOP_K = 4
SWIGLU_LIMIT = 7.0
SWIGLU_ALPHA = 1.702
MOE_CHUNK = 512
EPS = 1e-6
NEG = -1e30
N_FORCED = 3

LANES = 128
Q_TILE = 256
KEY_TILE = 512
LOG2E = float(np.log2(np.e))
N_SPLIT = 3
N_AUG = 4 * N_SPLIT + 1
ROW_TILE = 512
VMEM_LIMIT = 56 * 1024 * 1024
GROUP_LANES = HEADS_PER_GROUP * Q_TILE
WIN_KEYS = WINDOW + Q_TILE
N_TILE_ROWS = 16
V_ROWS = HEAD_DIM + 8


def _cparams(n_axes):
    return pltpu.CompilerParams(dimension_semantics=("arbitrary",) * n_axes,
                                vmem_limit_bytes=VMEM_LIMIT)


def _dot(a, b):
    return jnp.dot(a, b, preferred_element_type=F32)


def _dot_nt(a, b):
    return lax.dot_general(a, b, (((1,), (1,)), ((), ())), preferred_element_type=F32)


ROW_TILES = 8


def _store_row_tiles(ref, val):
    n = val.shape[0]
    for s in range(ROW_TILES):
        ref[pl.ds(s, n, stride=ROW_TILES), :] = val[:, s * LANES:(s + 1) * LANES]


def _load_row_tiles(ref, n):
    return jnp.concatenate([ref[pl.ds(s, n, stride=ROW_TILES), :] for s in range(ROW_TILES)], axis=1)


def _rms_pairs(v, bd):
    ss = _dot((v * v).astype(BF16), bd)
    return v * lax.rsqrt(ss + EPS)


def _inproj_body(x_ref, g1_ref, wq_ref, wkv_ref, wng_ref, wcv_ref, wmg_ref, gq_ref, gk_ref, bd_ref,
                 qtab_ref, kwtab_ref,
                 qa_out, hk_out, hv_out, kas_out, kaw_out, vst_out, vwt_out, gate_out, cbv_out, gab_out, raw_ref):
    x = x_ref[...]
    tm = x.shape[0]
    ms = jnp.mean(x * x, axis=-1, keepdims=True)
    h = (x * lax.rsqrt(ms + EPS) * g1_ref[...]).astype(BF16)
    bd = bd_ref[...]
    low = lax.broadcasted_iota(I32, (tm, LANES), 1) < HEAD_DIM

    def place(pair, tab_ref, out_ref, base, slot):
        for j, src in enumerate((pair, pltpu.roll(pair, HEAD_DIM, 1))):
            o = base + j * slot
            out_ref[:, o:o + LANES] = jnp.where(low, src, tab_ref[:, o:o + LANES].astype(F32)).astype(BF16)

    q = _dot(h, wq_ref[...])
    for c in range(N_HEADS * HEAD_DIM // LANES):
        sl = slice(c * LANES, (c + 1) * LANES)
        place(_rms_pairs(q[:, sl], bd) * gq_ref[:, sl], qtab_ref, qa_out, 2 * c * LANES, LANES)
    kv = _dot(h, wkv_ref[...])

    def emit_half_blocks(c, out_ref):
        nb = tm // CMP_STRIDE
        hw = CMP_STRIDE * HEAD_DIM
        raw_ref[...] = kv[:, c * LANES:(c + 1) * LANES]
        lo = lax.broadcasted_iota(I32, (nb, LANES), 1) < HEAD_DIM
        for u in range(CMP_STRIDE // 2):
            t0 = raw_ref[pl.ds(2 * u, nb, stride=CMP_STRIDE), :]
            t1 = raw_ref[pl.ds(2 * u + 1, nb, stride=CMP_STRIDE), :]
            out_ref[:, u * LANES:(u + 1) * LANES] = jnp.where(lo, t0, pltpu.roll(t1, HEAD_DIM, 1)).astype(BF16)
            out_ref[:, hw + u * LANES:hw + (u + 1) * LANES] = (
                jnp.where(lo, pltpu.roll(t0, HEAD_DIM, 1), t1).astype(BF16))

    emit_half_blocks(0, hk_out)
    emit_half_blocks(1, hv_out)
    place(_rms_pairs(kv[:, 256:384], bd) * gk_ref[:, 0:128], kwtab_ref, kas_out, 0, LANES)
    place(_rms_pairs(kv[:, 512:640], bd) * gk_ref[:, 128:256], kwtab_ref, kaw_out, 0, LANES)
    tail = jnp.where(lax.broadcasted_iota(I32, (V_ROWS - HEAD_DIM, tm), 0) == 0, 1.0, 0.0)

    def emit_values(c, out_ref, tile):
        vt = kv[:, c * LANES:(c + 1) * LANES].T
        for g in range(N_KV_HEADS):
            full = jnp.concatenate([vt[g * HEAD_DIM:(g + 1) * HEAD_DIM], tail], axis=0).astype(BF16)
            for j in range(tm // tile):
                out_ref[0, g, j] = full[:, j * tile:(j + 1) * tile]

    emit_values(3, vst_out, KEY_TILE)
    emit_values(5, vwt_out, Q_TILE)
    gate_out[...] = jax.nn.sigmoid(_dot(h, wng_ref[...]))
    cv = _dot(h, wcv_ref[...])
    cw = CONV_WIDTH
    cbv_out[:, 0:cw] = cv[:, 0:cw].astype(BF16)
    cbv_out[:, cw:2 * cw] = (cv[:, cw:2 * cw] * cv[:, 2 * cw:3 * cw]).astype(BF16)
    gab_out[...] = jax.nn.sigmoid(_dot(h, wmg_ref[...])).astype(BF16)


def _bf16_pieces(v):
    def round_bf16(a):
        u = np.ascontiguousarray(a, np.float32).view(np.uint32).astype(np.uint64)
        return ((u + 0x7FFF + ((u >> 16) & 1)) & 0xFFFF0000).astype(np.uint32).view(np.float32)

    pieces, rest = [], np.asarray(v, np.float32)
    for _ in range(N_SPLIT):
        pieces.append(round_bf16(rest))
        rest = rest - pieces[-1]
    return pieces


def _key_aug(pos):
    one = np.ones_like(pos, np.float32)
    hi = (pos // 64 * 64).astype(np.float32)
    lo = (pos % 64).astype(np.float32)
    return np.stack([one] * (2 * N_SPLIT) + [hi] * N_SPLIT + [lo] * N_SPLIT + [0 * one], axis=-1)


def _slot_table(aug, slot):
    S, n, _ = aug.shape
    tab = np.zeros((S, n, slot), np.float32)
    tab[:, :, HEAD_DIM:HEAD_DIM + N_AUG] = aug
    return jnp.asarray(tab.reshape(S, n * slot), BF16)


def _inproj(x2, g1, w_in, g_q, g_ks, g_kw, S):
    T, D = x2.shape
    H, G = N_HEADS, N_KV_HEADS
    aw = H * HEAD_DIM
    kvw = G * HEAD_DIM
    o = 0
    wq = w_in[:, o:o + aw]; o += aw
    wkv = w_in[:, o:o + 6 * kvw]; o += 6 * kvw
    wng = w_in[:, o:o + 3 * H]; o += 3 * H
    wcv = w_in[:, o:o + 3 * CONV_WIDTH]; o += 3 * CONV_WIDTH
    wmg = w_in[:, o:o + 2 * D]
    wng = jnp.pad(wng, ((0, 0), (0, LANES - 3 * H)))
    wq, wkv, wng, wcv, wmg = (w.astype(BF16) for w in (wq, wkv, wng, wcv, wmg))
    gq = (jnp.tile(g_q, H) * (HEAD_DIM ** -0.5 * LOG2E)).reshape(1, aw)
    gk = jnp.concatenate([jnp.tile(g_ks, G), jnp.tile(g_kw, G)]).reshape(1, 2 * kvw)
    idx = np.arange(LANES) // HEAD_DIM
    bd = jnp.asarray((idx[:, None] == idx[None, :]).astype(np.float32) / HEAD_DIM, BF16)
    pos = np.arange(S)
    hi = (pos // 64 * 64).astype(np.float64)[:, None]
    lo = (pos % 64).astype(np.float64)[:, None]
    c = LOG2E * 2.0 ** (-8.0 * np.arange(1, H + 1) / H)[None, :]
    cs = np.broadcast_to(c, (S, H))
    aq = np.stack(_bf16_pieces(-c * hi) + _bf16_pieces(-c * lo) + _bf16_pieces(cs) + _bf16_pieces(cs)
                  + [np.ones((S, H), np.float32)], axis=-1)
    ak = np.broadcast_to(_key_aug(pos)[:, None, :], (S, G, N_AUG))
    qtab = _slot_table(aq, LANES)
    kwtab = _slot_table(ak, LANES)
    tm = ROW_TILE
    nst = S // tm
    row = lambda w: pl.BlockSpec((tm, w), lambda i: (i, 0))
    full = lambda a: pl.BlockSpec(a.shape, lambda i: (0,) * a.ndim)
    tab = lambda a: pl.BlockSpec((tm, a.shape[1]), lambda i: (i % nst, 0))
    ins = (x2, g1.reshape(1, D), wq, wkv, wng, wcv, wmg, gq, gk, bd, qtab, kwtab)
    widths = (H * LANES, G * LANES, G * LANES, LANES, 2 * CONV_WIDTH, 2 * D)
    dtypes = (BF16, BF16, BF16, F32, BF16, BF16)
    hw = G * CMP_STRIDE * HEAD_DIM
    nb = tm // CMP_STRIDE
    half = pl.BlockSpec((nb, hw), lambda i: (i, 0))
    half_shape = jax.ShapeDtypeStruct((T // CMP_STRIDE, hw), BF16)
    assert tm % KEY_TILE == 0 and tm % Q_TILE == 0

    def values(tile):
        return (pl.BlockSpec((1, G, tm // tile, V_ROWS, tile), lambda i: (i // nst, 0, i % nst, 0, 0)),
                jax.ShapeDtypeStruct((T // S, G, S // tile, V_ROWS, tile), BF16))

    rows = [(row(w), jax.ShapeDtypeStruct((T, w), dt)) for w, dt in zip(widths, dtypes)]
    outs = [rows[0], (half, half_shape), (half, half_shape), rows[1], rows[2], values(KEY_TILE),
            values(Q_TILE)] + rows[3:]
    return pl.pallas_call(
        _inproj_body,
        grid=(T // tm,),
        in_specs=[row(D)] + [full(a) for a in ins[1:10]] + [tab(a) for a in ins[10:]],
        out_specs=[o[0] for o in outs],
        out_shape=[o[1] for o in outs],
        scratch_shapes=[pltpu.VMEM((tm, LANES), F32)],
        compiler_params=_cparams(1),
        name="inproj",
    )(*ins)


def _compress_body(h_ref, w1_ref, pe_ref, b1_ref, w2_ref, b2_ref, g_ref, tab_ref, o_ref, *, for_keys):
    hb = h_ref[0]
    nc = hb.shape[0]
    a = _dot(hb, w1_ref[0])
    b = _dot(hb, w1_ref[1])
    c = _dot(pe_ref[0], w1_ref[0]) + _dot(pe_ref[1], w1_ref[1])
    pre = a + pltpu.roll(b, nc - 1, 0) + c[0:1, :] + b1_ref[...]
    hid = jax.nn.gelu(pre)
    out = _dot(hid.astype(BF16), w2_ref[...]) + b2_ref[...]
    if for_keys:
        ms = jnp.sum(out * out, axis=-1, keepdims=True) * (1.0 / HEAD_DIM)
        out = out * lax.rsqrt(ms + EPS) * g_ref[...]
        low = lax.broadcasted_iota(I32, out.shape, 1) < HEAD_DIM
        o_ref[0, 0] = jnp.where(low, out, tab_ref[...]).astype(BF16)
    else:
        o_ref[0, 0] = out.T[0:HEAD_DIM, :].astype(BF16)


def _compress(hh, pe, w1, b1, w2, b2, gain, for_keys):
    B, NC, _ = hh.shape
    G, HW = N_KV_HEADS, CMP_STRIDE * HEAD_DIM
    w1s = w1.reshape(2, HW, CMP_HID).astype(BF16)
    pes = jnp.broadcast_to(pe.reshape(2, 1, HW), (2, 8, HW)).astype(BF16)
    padl = lambda a: jnp.pad(a, ((0, 0), (0, LANES - HEAD_DIM)))
    tabn = np.zeros((NC, LANES), np.float32)
    tabn[:, HEAD_DIM:HEAD_DIM + N_AUG] = _key_aug(np.arange(NC) * CMP_STRIDE + (CMP_LEN - 1))
    tab = jnp.asarray(tabn)
    full = lambda a: pl.BlockSpec(a.shape, lambda b, g: (0,) * a.ndim)
    ins = (hh, w1s, pes, b1.reshape(1, CMP_HID), padl(w2).astype(BF16), padl(b2.reshape(1, HEAD_DIM)),
           padl(gain.reshape(1, HEAD_DIM)), tab)
    oshape = (B, G, NC, LANES) if for_keys else (B, G, HEAD_DIM, NC)
    return pl.pallas_call(
        functools.partial(_compress_body, for_keys=for_keys),
        grid=(B, G),
        in_specs=[pl.BlockSpec((1, NC, HW), lambda b, g: (b, 0, g))] + [full(a) for a in ins[1:]],
        out_specs=pl.BlockSpec((1, 1) + oshape[2:], lambda b, g: (b, g, 0, 0)),
        out_shape=jax.ShapeDtypeStruct(oshape, BF16),
        compiler_params=_cparams(2),
        name="compress_keys" if for_keys else "compress_values",
    )(*ins)


def _attn_body(qa_ref, g_ref, kca_ref, vct_ref, kas_ref, vst_ref, kaw_ref, vwt_ref, selmapt_ref, wbias_ref,
               tilemap_ref, cbias_ref, o_ref, selb_ref, m_ref, acc_ref, s0_ref, s1_ref, flagv_ref,
               flags_ref, list_ref, sem,
               *, n_sel):
    i = pl.program_id(2)
    q0 = i * Q_TILE
    gl = GROUP_LANES
    qa = jnp.concatenate([qa_ref[0, :, h * LANES:(h + 1) * LANES] for h in range(HEADS_PER_GROUP)], axis=0)

    nc = kca_ref.shape[2]
    s = _dot_nt(kca_ref[0, 0], qa)
    cb = cbias_ref[pl.ds(pl.multiple_of(nc - i * (Q_TILE // CMP_STRIDE), 8), nc), :]
    s = s + jnp.concatenate([cb] * HEADS_PER_GROUP, axis=1)
    m = jnp.max(s, axis=0, keepdims=True)
    p = jnp.exp2(s - m)
    l = jnp.sum(p, axis=0, keepdims=True)
    has_entry = (q0 + (lax.broadcasted_iota(I32, (1, gl), 1) & (Q_TILE - 1))) >= CMP_LEN - 1
    pc = p * jnp.where(has_entry, 1.0 / l, 0.0)
    o_c = _dot(vct_ref[0, 0], pc.astype(BF16))

    ps = pc[:, 0:Q_TILE]
    for h in range(1, HEADS_PER_GROUP):
        ps = ps + pc[:, h * Q_TILE:(h + 1) * Q_TILE]
    ps_hi = ps.astype(BF16)
    ps_lo = (ps - ps_hi.astype(F32)).astype(BF16)
    imp = _dot(selmapt_ref[...], ps_hi) + _dot(selmapt_ref[...], ps_lo)
    jb = lax.broadcasted_iota(I32, (LANES, Q_TILE), 0)
    cur = (q0 + lax.broadcasted_iota(I32, (LANES, Q_TILE), 1)) // SEL_BLK
    forced = (jb == 0) | (jb == cur) | (jb == cur - 1)
    score = jnp.where(forced, -jnp.inf, jnp.where(jb > cur, NEG, imp))
    jbf = jb.astype(F32)
    for _ in range(n_sel - N_FORCED):
        mx = jnp.max(score, axis=0, keepdims=True)
        first = jnp.min(jnp.where(score == mx, jbf, float(LANES)), axis=0, keepdims=True)
        score = jnp.where(jbf == first, -jnp.inf, score)
    picked = score == -jnp.inf
    bias_t = jnp.where(picked, 0.0, NEG)
    selb_ref[...] = jnp.concatenate([bias_t] * HEADS_PER_GROUP, axis=1)
    tile_hits = jnp.max(_dot(tilemap_ref[...], jnp.where(picked, 1.0, 0.0).astype(BF16)), axis=1, keepdims=True)
    flagv_ref[...] = jnp.broadcast_to(tile_hits, flagv_ref.shape).astype(I32)
    flag_copy = pltpu.make_async_copy(flagv_ref, flags_ref, sem)
    flag_copy.start()

    w0 = pl.multiple_of(jnp.maximum(q0 - WINDOW, 0), Q_TILE)
    sw = _dot_nt(kaw_ref[0, pl.ds(w0, WIN_KEYS), :], qa)
    wb = wbias_ref[jnp.minimum(i, WINDOW // Q_TILE)]
    sw = sw + jnp.concatenate([wb] * HEADS_PER_GROUP, axis=1)
    mw = jnp.max(sw, axis=0, keepdims=True)
    pw = jnp.exp2(sw - mw)
    c0 = w0 // Q_TILE
    vw = jnp.concatenate([vwt_ref[0, 0, c0 + j] for j in range(WIN_KEYS // Q_TILE)], axis=1)
    aw = _dot(vw, pw.astype(BF16))
    o_w = aw[0:HEAD_DIM] * (1.0 / aw[HEAD_DIM:HEAD_DIM + 1])

    def scores(kt):
        sc = _dot_nt(kas_ref[0, pl.ds(pl.multiple_of(kt * KEY_TILE, KEY_TILE), KEY_TILE), :], qa)
        blocks = KEY_TILE // SEL_BLK
        mask = [jnp.broadcast_to(selb_ref[pl.ds(kt * blocks + j, 1), :], (SEL_BLK, gl)) for j in range(blocks)]
        return sc + jnp.concatenate(mask, axis=0)

    n_full = q0 // KEY_TILE
    key = n_full * KEY_TILE + lax.broadcasted_iota(I32, (KEY_TILE, gl), 0)
    qry = q0 + (lax.broadcasted_iota(I32, (KEY_TILE, gl), 1) & (Q_TILE - 1))
    s0_ref[...] = jnp.where(key <= qry, scores(n_full), NEG)
    m_ref[...] = jnp.full(m_ref.shape, -3.0e38, F32)
    acc_ref[...] = jnp.zeros(acc_ref.shape, F32)
    list_ref[0] = n_full

    def absorb(s_ref, kt, live):
        sc = s_ref[...]
        m_old = m_ref[...]
        m_new = jnp.where(live, jnp.maximum(m_old, jnp.max(sc, axis=0, keepdims=True)), m_old)
        pv = _dot(vst_ref[0, 0, kt], jnp.exp2(sc - m_new).astype(BF16))
        acc_ref[...] = jnp.exp2(m_old - m_new) * acc_ref[...] + jnp.where(live, pv, 0.0)
        m_ref[...] = m_new

    flag_copy.wait()

    def compact(kt, n):
        active = flags_ref[kt, 0] > 0

        @pl.when(active)
        def _():
            list_ref[n] = kt

        return n + active.astype(I32)

    n_items = lax.fori_loop(0, n_full, compact, 1)

    last = n_items - 1

    def pair_body(j, carry):
        a = list_ref[2 * j]
        b = list_ref[jnp.minimum(2 * j + 1, last)]
        nxt = list_ref[jnp.minimum(2 * j + 2, last)]
        s1_ref[...] = scores(b)
        absorb(s0_ref, a, True)
        s0_ref[...] = scores(nxt)
        absorb(s1_ref, b, 2 * j + 1 <= last)
        return carry

    lax.fori_loop(0, (n_items + 1) // 2, pair_body, 0)

    o_s = acc_ref[0:HEAD_DIM, :] * (1.0 / acc_ref[HEAD_DIM:HEAD_DIM + 1, :])

    g = g_ref[0, 0]
    outs = []
    for h in range(HEADS_PER_GROUP):
        sl = slice(h * Q_TILE, (h + 1) * Q_TILE)
        outs.append(g[3 * h:3 * h + 1, :] * o_c[:, sl] + g[3 * h + 1:3 * h + 2, :] * o_s[:, sl]
                    + g[3 * h + 2:3 * h + 3, :] * o_w[:, sl])
    o_ref[0] = jnp.concatenate(outs, axis=0).T.astype(BF16)


def _attention(qa, gates_t, kca, vct, kas, vst, kaw, vwt, B, S):
    G, hd = N_KV_HEADS, HEAD_DIM
    NC = kca.shape[2]
    n_blk = S // SEL_BLK
    assert n_blk <= LANES and S % KEY_TILE == 0 and S >= WIN_KEYS
    n_sel = min(SEL_TOPN, n_blk)
    assert n_sel > N_FORCED
    ratio, span = SEL_BLK // CMP_STRIDE, CMP_LEN // CMP_STRIDE
    sm = np.zeros((LANES, NC), np.float32)
    for j in range(n_blk):
        for a in range(ratio):
            for b in range(span):
                n = ratio * j + a - b
                if 0 <= n < NC - 1:
                    sm[j, n] += 1.0
    selmapt = jnp.asarray(sm, BF16)
    c = np.arange(WIN_KEYS)[:, None]
    r = np.arange(Q_TILE)[None, :]
    offs = np.arange(WINDOW // Q_TILE + 1)[:, None, None] * Q_TILE
    wbias = jnp.asarray(np.where((c - r <= offs) & (c - r > offs - WINDOW), 0.0, NEG), F32)
    tilemap = jnp.asarray(np.arange(LANES)[None, :] // (KEY_TILE // SEL_BLK) == np.arange(N_TILE_ROWS)[:, None],
                          BF16)
    assert S // KEY_TILE <= N_TILE_ROWS
    u = np.arange(2 * NC)[:, None] - NC
    cbias = jnp.asarray(np.where(CMP_STRIDE * u + (CMP_LEN - 1) <= np.arange(Q_TILE)[None, :], 0.0, NEG), F32)

    hpg = HEADS_PER_GROUP
    grp = lambda *blk: pl.BlockSpec((1, 1) + blk, lambda b, g, i: (b, g) + (0,) * len(blk))
    seq = lambda w: pl.BlockSpec((1, S, w), lambda b, g, i: (b, 0, g))
    const = lambda a: pl.BlockSpec(a.shape, lambda b, g, i: (0,) * a.ndim)
    return pl.pallas_call(
        functools.partial(_attn_body, n_sel=n_sel),
        grid=(B, G, S // Q_TILE),
        in_specs=[pl.BlockSpec((1, Q_TILE, hpg * LANES), lambda b, g, i: (b, i, g)),
                  pl.BlockSpec((1, 1, 16, Q_TILE), lambda b, g, i: (b, g, 0, i)),
                  grp(NC, LANES), grp(hd, NC), seq(LANES), grp(S // KEY_TILE, V_ROWS, KEY_TILE),
                  seq(LANES), grp(S // Q_TILE, V_ROWS, Q_TILE), const(selmapt), const(wbias), const(tilemap),
                  const(cbias)],
        out_specs=pl.BlockSpec((1, Q_TILE, hpg * hd), lambda b, g, i: (b, i, g)),
        out_shape=jax.ShapeDtypeStruct((B, S, N_HEADS * hd), BF16),
        scratch_shapes=[pltpu.VMEM((LANES, GROUP_LANES), F32), pltpu.VMEM((1, GROUP_LANES), F32),
                        pltpu.VMEM((V_ROWS, GROUP_LANES), F32), pltpu.VMEM((KEY_TILE, GROUP_LANES), F32),
                        pltpu.VMEM((KEY_TILE, GROUP_LANES), F32),
                        pltpu.VMEM((N_TILE_ROWS, LANES), I32), pltpu.SMEM((N_TILE_ROWS, LANES), I32),
                        pltpu.SMEM((N_TILE_ROWS,), I32), pltpu.SemaphoreType.DMA(())],
        compiler_params=_cparams(3),
        name="nsa_attention",
    )(qa, gates_t, kca, vct, kas, vst, kaw, vwt, selmapt, wbias, tilemap, cbias)


def _mixer_out_body(x_ref, oa_ref, cbv_ref, halo_ref, gab_ref, cw_ref, wpa_ref, wpb_ref, wo_ref,
                    g2_ref, whi_ref, wlo_ref, br_ref, tri_ref, o_ref, h_out, mi_out, mf_out, cnt_out,
                    *, seq_len):
    i = pl.program_id(0)
    tm = x_ref.shape[0]
    cwd = CONV_WIDTH
    d = x_ref.shape[1]
    v = cbv_ref[:, cwd:2 * cwd].astype(F32)
    prev = halo_ref[:, cwd:2 * cwd].astype(F32)
    keep = ((i * tm) % seq_len != 0).astype(F32)
    p1 = prev[7:8, :] * keep
    p2 = prev[6:7, :] * keep
    ridx = lax.broadcasted_iota(I32, (tm, cwd), 0)
    v1 = jnp.where(ridx == 0, p1, pltpu.roll(v, 1, 0))
    v2 = jnp.where(ridx == 0, p2, jnp.where(ridx == 1, p1, pltpu.roll(v, 2, 0)))
    y = cw_ref[0:1, :] * v2 + cw_ref[1:2, :] * v1 + cw_ref[2:3, :] * v
    yb_in = (cbv_ref[:, 0:cwd].astype(F32) * y).astype(BF16)
    y_a = _dot(oa_ref[...], wpa_ref[...])
    y_b = _dot(yb_in, wpb_ref[...])
    merged = gab_ref[:, 0:d].astype(F32) * y_a + gab_ref[:, d:2 * d].astype(F32) * y_b
    x1 = x_ref[...] + _dot(merged.astype(BF16), wo_ref[...])
    o_ref[...] = x1
    _route(x1, g2_ref, whi_ref, wlo_ref, br_ref, tri_ref, h_out, mi_out, mf_out, cnt_out)


def _mixer_out(x2, oa, cbv, gab, conv_w, w_pa, w_pb, w_o, seq_len, g2, w_r, b_r):
    T, D = x2.shape
    assert D == ROW_TILES * LANES
    tm = ROW_TILE
    cw8 = jnp.pad(conv_w, ((0, 8 - CONV_K), (0, 0)))
    wpad = jnp.pad(w_r, ((0, 0), (0, LANES - N_EXPERTS)))
    whi = wpad.astype(BF16)
    wlo = (wpad - whi.astype(F32)).astype(BF16)
    br = jnp.pad(b_r, (0, LANES - N_EXPERTS)).reshape(1, LANES)
    tri = jnp.asarray(np.tril(np.ones((tm, tm), np.float32), -1), BF16)
    row = lambda w: pl.BlockSpec((tm, w), lambda i: (i, 0))
    full = lambda a: pl.BlockSpec(a.shape, lambda i: (0,) * a.ndim)
    halo = pl.BlockSpec((8, cbv.shape[1]), lambda i: (jnp.maximum(i * (tm // 8) - 1, 0), 0))
    wts = (cw8, w_pa.astype(BF16), w_pb.astype(BF16), w_o.astype(BF16), g2.reshape(1, D), whi, wlo, br, tri)
    return pl.pallas_call(
        functools.partial(_mixer_out_body, seq_len=seq_len),
        grid=(T // tm,),
        in_specs=[row(D), row(oa.shape[1]), row(cbv.shape[1]), halo, row(gab.shape[1])] + [full(a) for a in wts],
        out_specs=[row(D), pl.BlockSpec((tm * ROW_TILES, LANES), lambda i: (i, 0)), row(2 * TOP_K), row(LANES),
                   pl.BlockSpec((8, LANES), lambda i: (0, 0))],
        out_shape=[jax.ShapeDtypeStruct((T, D), F32), jax.ShapeDtypeStruct((T * ROW_TILES, LANES), F32),
                   jax.ShapeDtypeStruct((T, 2 * TOP_K), I32), jax.ShapeDtypeStruct((T, LANES), F32),
                   jax.ShapeDtypeStruct((8, LANES), F32)],
        compiler_params=_cparams(1),
        name="mixer_out_router",
    )(x2, oa, cbv, cbv, gab, *wts)


def _route(x, g2_ref, whi_ref, wlo_ref, br_ref, tri_ref, h_out, mi_out, mf_out, cnt_out):
    i = pl.program_id(0)

    @pl.when(i == 0)
    def _():
        cnt_out[...] = jnp.zeros(cnt_out.shape, F32)

    tm = x.shape[0]
    ms = jnp.mean(x * x, axis=-1, keepdims=True)
    h = x * lax.rsqrt(ms + EPS) * g2_ref[...]
    _store_row_tiles(h_out, h)
    h_hi = h.astype(BF16)
    h_lo = (h - h_hi.astype(F32)).astype(BF16)
    logits = (_dot(h_hi, whi_ref[...]) + _dot(h_lo, whi_ref[...]) + _dot(h_hi, wlo_ref[...])) + br_ref[...]
    lane = lax.broadcasted_iota(I32, (tm, LANES), 1)
    lanef = lane.astype(F32)
    work = jnp.where(lane < N_EXPERTS, logits, -jnp.inf)
    vals, hits = [], []
    for _ in range(TOP_K):
        mx = jnp.max(work, axis=-1, keepdims=True)
        first = jnp.min(jnp.where(work == mx, lanef, float(LANES)), axis=-1, keepdims=True)
        hit = lanef == first
        vals.append(mx)
        hits.append(hit)
        work = jnp.where(hit, -jnp.inf, work)
    ex = [jnp.exp(v - vals[0]) for v in vals]
    den = ex[0]
    for e in ex[1:]:
        den = den + e
    cnt = jnp.zeros((tm, LANES), F32)
    for hit in hits:
        cnt = cnt + hit.astype(F32)
    before = _dot(tri_ref[...], cnt.astype(BF16)) + cnt_out[0:1, :]
    mi = jnp.zeros((tm, LANES), F32)
    mf = jnp.zeros((tm, LANES), F32)
    for k, hit in enumerate(hits):
        e_k = jnp.sum(jnp.where(hit, lanef, 0.0), axis=-1, keepdims=True)
        r_k = jnp.sum(jnp.where(hit, before, 0.0), axis=-1, keepdims=True)
        mi = jnp.where(lane == k, e_k, jnp.where(lane == TOP_K + k, r_k, mi))
        mf = jnp.where(lane == k, ex[k] / den, mf)
    mi_out[...] = mi[:, 0:2 * TOP_K].astype(I32)
    mf_out[...] = mf
    cnt_out[...] = cnt_out[...] + jnp.sum(cnt, axis=0, keepdims=True)


DISPATCH_TILE = 512
DMA_UNROLL = 8


def _dispatch_body(dest_ref, last_ref, h_ref, o_hbm, zero_ref, sem, zsem):
    @pl.when(pl.program_id(0) == 0)
    def _():
        zero_ref[...] = jnp.zeros(zero_ref.shape, zero_ref.dtype)

        def clear(e):
            start = pl.multiple_of(last_ref[e] * ROW_TILES, MOE_CHUNK * ROW_TILES)
            return pltpu.make_async_copy(zero_ref, o_hbm.at[pl.ds(start, MOE_CHUNK * ROW_TILES)], zsem)

        for e in range(N_EXPERTS):
            @pl.when(last_ref[e] >= 0)
            def _():
                clear(e).start()
        for e in range(N_EXPERTS):
            @pl.when(last_ref[e] >= 0)
            def _():
                clear(e).wait()

    def row_copy(r, d):
        return pltpu.make_async_copy(h_ref.at[pl.ds(pl.multiple_of(r * ROW_TILES, ROW_TILES), ROW_TILES)],
                                     o_hbm.at[pl.ds(pl.multiple_of(d * ROW_TILES, ROW_TILES), ROW_TILES)], sem)

    def start(r, c):
        for k in range(TOP_K):
            row_copy(r, dest_ref[0, 0, r * TOP_K + k]).start(priority=k % 2)
        return c

    def wait(r, c):
        for k in range(TOP_K):
            row_copy(0, 0).wait()
        return c

    lax.fori_loop(0, DISPATCH_TILE, start, 0, unroll=DMA_UNROLL)
    lax.fori_loop(0, DISPATCH_TILE, wait, 0, unroll=DMA_UNROLL)


def _dispatch(h2, dest, last_chunk, n_rows):
    T = h2.shape[0] // ROW_TILES
    td = DISPATCH_TILE
    dest3 = dest.reshape(T // td, 1, td * TOP_K)
    return pl.pallas_call(
        _dispatch_body,
        grid=(T // td,),
        in_specs=[pl.BlockSpec((1, 1, td * TOP_K), lambda i: (i, 0, 0), memory_space=pltpu.SMEM),
                  pl.BlockSpec(memory_space=pltpu.SMEM),
                  pl.BlockSpec((td * ROW_TILES, LANES), lambda i: (i, 0))],
        out_specs=pl.BlockSpec(memory_space=pl.ANY),
        out_shape=jax.ShapeDtypeStruct((n_rows * ROW_TILES, LANES), h2.dtype),
        scratch_shapes=[pltpu.VMEM((MOE_CHUNK * ROW_TILES, LANES), h2.dtype), pltpu.SemaphoreType.DMA(()),
                        pltpu.SemaphoreType.DMA(())],
        compiler_params=_cparams(1),
        name="dispatch",
    )(dest3, last_chunk, h2)


def _expert_body(ce_ref, nu_ref, x_ref, wgu_ref, bgu_ref, wdn_ref, bdn_ref, o_ref, wgu_bf, wdn_bf):
    c = pl.program_id(0)
    dff = wdn_ref.shape[1]

    @pl.when((c == 0) | (ce_ref[c] != ce_ref[jnp.maximum(c - 1, 0)]))
    def _():
        wgu_bf[...] = wgu_ref[0].astype(BF16)
        wdn_bf[...] = wdn_ref[0].astype(BF16)

    @pl.when(c < nu_ref[0])
    def _():
        x = _load_row_tiles(x_ref, MOE_CHUNK)
        gu = _dot(x.astype(BF16), wgu_bf[...]) + bgu_ref[0]
        g = jnp.minimum(gu[:, 0:dff], SWIGLU_LIMIT)
        u = jnp.clip(gu[:, dff:2 * dff], -SWIGLU_LIMIT, SWIGLU_LIMIT)
        act = (u + 1.0) * (g * jax.nn.sigmoid(SWIGLU_ALPHA * g))
        _store_row_tiles(o_ref, _dot(act.astype(BF16), wdn_bf[...]) + bdn_ref[0])

    @pl.when(c >= nu_ref[0])
    def _():
        o_ref[...] = jnp.zeros(o_ref.shape, F32)


def _experts(hperm, chunk_e, n_used, w_gu, b_gu, w_dn, b_dn):
    E, D, F2 = w_gu.shape
    assert D == ROW_TILES * LANES
    P = hperm.shape[0] // ROW_TILES
    dff = F2 // 2
    n_chunks = P // MOE_CHUNK
    chunk = (MOE_CHUNK * ROW_TILES, LANES)
    grid_spec = pltpu.PrefetchScalarGridSpec(
        num_scalar_prefetch=2,
        grid=(n_chunks,),
        in_specs=[pl.BlockSpec(chunk, lambda c, ce, nu: (jnp.minimum(c, nu[0] - 1), 0)),
                  pl.BlockSpec((1, D, F2), lambda c, ce, nu: (ce[c], 0, 0)),
                  pl.BlockSpec((1, 1, F2), lambda c, ce, nu: (ce[c], 0, 0)),
                  pl.BlockSpec((1, dff, D), lambda c, ce, nu: (ce[c], 0, 0)),
                  pl.BlockSpec((1, 1, D), lambda c, ce, nu: (ce[c], 0, 0))],
        out_specs=pl.BlockSpec(chunk, lambda c, ce, nu: (c, 0)),
        scratch_shapes=[pltpu.VMEM((D, F2), BF16), pltpu.VMEM((dff, D), BF16)],
    )
    return pl.pallas_call(
        _expert_body,
        grid_spec=grid_spec,
        out_shape=jax.ShapeDtypeStruct(hperm.shape, F32),
        compiler_params=_cparams(1),
        name="experts",
    )(chunk_e, n_used, hperm, w_gu, b_gu.reshape(E, 1, F2), w_dn, b_dn.reshape(E, 1, D))


COMBINE_TILE = 512


def _combine_body(dest_ref, next_ref, x_ref, w_ref, y_hbm, o_ref, buf_ref, sems):
    i = pl.program_id(0)
    slot = i % 2

    def row_copy(s, r, k, d):
        return pltpu.make_async_copy(y_hbm.at[pl.ds(pl.multiple_of(d * ROW_TILES, ROW_TILES), ROW_TILES)],
                                     buf_ref.at[s, k, pl.ds(pl.multiple_of(r * ROW_TILES, ROW_TILES), ROW_TILES)],
                                     sems.at[s])

    def fetch(idx_ref, s):
        def start(r, c):
            for k in range(TOP_K):
                row_copy(s, r, k, idx_ref[0, 0, r * TOP_K + k]).start(priority=k % 2)
            return c

        lax.fori_loop(0, COMBINE_TILE, start, 0, unroll=DMA_UNROLL)

    @pl.when(i == 0)
    def _():
        fetch(dest_ref, slot)

    @pl.when(i + 1 < pl.num_programs(0))
    def _():
        fetch(next_ref, 1 - slot)

    def wait(r, c):
        for k in range(TOP_K):
            row_copy(slot, 0, 0, 0).wait()
        return c

    lax.fori_loop(0, COMBINE_TILE, wait, 0, unroll=DMA_UNROLL)
    gate = [jnp.broadcast_to(w_ref[:, k:k + 1], (COMBINE_TILE, LANES)) for k in range(TOP_K)]
    for s in range(ROW_TILES):
        sl = slice(s * LANES, (s + 1) * LANES)
        out = x_ref[:, sl]
        for k in range(TOP_K):
            out = out + gate[k] * buf_ref[slot, k, pl.ds(s, COMBINE_TILE, stride=ROW_TILES), :]
        o_ref[:, sl] = out


def _combine(x1, gate_w, dest, ys):
    T, D = x1.shape
    tc = COMBINE_TILE
    n = T // tc
    dest3 = dest.reshape(n, 1, tc * TOP_K)
    row = lambda w: pl.BlockSpec((tc, w), lambda i: (i, 0))
    idx = lambda f: pl.BlockSpec((1, 1, tc * TOP_K), f, memory_space=pltpu.SMEM)
    return pl.pallas_call(
        _combine_body,
        grid=(n,),
        in_specs=[idx(lambda i: (i, 0, 0)), idx(lambda i: (jnp.minimum(i + 1, n - 1), 0, 0)),
                  row(D), row(LANES), pl.BlockSpec(memory_space=pl.ANY)],
        out_specs=row(D),
        out_shape=jax.ShapeDtypeStruct((T, D), F32),
        scratch_shapes=[pltpu.VMEM((2, TOP_K, tc * ROW_TILES, LANES), F32), pltpu.SemaphoreType.DMA((2,))],
        compiler_params=_cparams(1),
        name="combine",
    )(dest3, dest3, x1, gate_w, ys)


def _mixer(x2, B, S, g_norm1, w_in, g_q, g_kc, g_ks, g_kw, pe_k, ck_w1, ck_b1, ck_w2, ck_b2,
           pe_v, cv_w1, cv_b1, cv_w2, cv_b2, conv_w, w_pa, w_pb, w_o, g_norm2, w_r, b_r):
    T, D = x2.shape
    G, H, hd = N_KV_HEADS, N_HEADS, HEAD_DIM
    qa, hk, hv, kas, kaw, vst, vwt, gates, cbv, gab = _inproj(x2, g_norm1, w_in, g_q, g_ks, g_kw, S)
    nh = S // CMP_STRIDE
    kca = _compress(hk.reshape(B, nh, -1), pe_k, ck_w1, ck_b1, ck_w2, ck_b2, g_kc, True)
    vct = _compress(hv.reshape(B, nh, -1), pe_v, cv_w1, cv_b1, cv_w2, cv_b2, jnp.ones((hd,), F32), False)
    gat = gates[:, :3 * H].reshape(B, S, G, 3 * HEADS_PER_GROUP).transpose(0, 2, 3, 1)
    gat = jnp.pad(gat, ((0, 0), (0, 0), (0, 16 - 3 * HEADS_PER_GROUP), (0, 0)))
    o = _attention(qa.reshape(B, S, -1), gat, kca, vct, kas.reshape(B, S, -1), vst, kaw.reshape(B, S, -1),
                   vwt, B, S)
    return _mixer_out(x2, o.reshape(T, H * hd), cbv, gab, conv_w, w_pa, w_pb, w_o, S, g_norm2, w_r, b_r)


def _moe(x1, h2, mi, mf, cnt, w_gu, b_gu, w_dn, b_dn):
    T, D = x1.shape
    top_e = mi[:, 0:TOP_K]
    rank = mi[:, TOP_K:2 * TOP_K]
    counts = cnt[0, :N_EXPERTS].astype(I32)
    padded = (counts + MOE_CHUNK - 1) // MOE_CHUNK * MOE_CHUNK
    pend = jnp.cumsum(padded)
    poffs = pend - padded
    dest = (poffs[top_e] + rank).reshape(-1)
    n_chunks = (T * TOP_K + MOE_CHUNK - 1) // MOE_CHUNK + N_EXPERTS
    chunk_start = jnp.arange(n_chunks, dtype=I32) * MOE_CHUNK
    chunk_e = jnp.minimum(jnp.sum((pend[None, :] <= chunk_start[:, None]).astype(I32), axis=1), N_EXPERTS - 1)
    n_used = (pend[-1:] // MOE_CHUNK).astype(I32)
    last_chunk = jnp.where(padded > 0, pend - MOE_CHUNK, -1).astype(I32)
    hperm = _dispatch(h2, dest, last_chunk, n_chunks * MOE_CHUNK)
    ys = _experts(hperm, chunk_e, n_used, w_gu, b_gu, w_dn, b_dn)
    return _combine(x1, mf, dest, ys)


def kernel(x, g_norm1, w_in, g_q, g_kc, g_ks, g_kw, pe_k, ck_w1, ck_b1, ck_w2, ck_b2, pe_v, cv_w1, cv_b1,
           cv_w2, cv_b2, conv_w, w_pa, w_pb, w_o, g_norm2, w_r, b_r, w_gu, b_gu, w_dn, b_dn):
    B, S, D = x.shape
    x2 = x.reshape(B * S, D)
    for l in range(g_norm1.shape[0]):
        routed = _mixer(x2, B, S, g_norm1[l], w_in[l], g_q[l], g_kc[l], g_ks[l], g_kw[l], pe_k[l], ck_w1[l],
                        ck_b1[l], ck_w2[l], ck_b2[l], pe_v[l], cv_w1[l], cv_b1[l], cv_w2[l], cv_b2[l],
                        conv_w[l], w_pa[l], w_pb[l], w_o[l], g_norm2[l], w_r[l], b_r[l])
        x2 = _moe(*routed, w_gu[l], b_gu[l], w_dn[l], b_dn[l])
    return x2.reshape(B, S, D)
```

```python
import functools

import numpy as np
import jax
import jax.numpy as jnp
from jax import lax
from jax.experimental import pallas as pl
from jax.experimental.pallas import tpu as pltpu

F32 = jnp.float32
BF16 = jnp.bfloat16
I32 = jnp.int32

N_HEADS = 8
HEAD_DIM = 64
N_KV_HEADS = 2
HEADS_PER_GROUP = N_HEADS // N_KV_HEADS
CMP_LEN = 32
CMP_STRIDE = 16
CMP_HID = 256
SEL_BLK = 64
SEL_TOPN = 16
WINDOW = 512
CONV_WIDTH = 512
CONV_K = 3
N_EXPERTS = 32
TOP_K = 4
SWIGLU_LIMIT = 7.0
SWIGLU_ALPHA = 1.702
MOE_CHUNK = 512
EPS = 1e-6
NEG = -1e30
N_FORCED = 3

LANES = 128
Q_TILE = 256
KEY_TILE = 512
LOG2E = float(np.log2(np.e))
N_SPLIT = 3
N_AUG = 4 * N_SPLIT + 1
ROW_TILE = 512
VMEM_LIMIT = 56 * 1024 * 1024
GROUP_LANES = HEADS_PER_GROUP * Q_TILE
WIN_KEYS = WINDOW + Q_TILE
N_TILE_ROWS = 16
V_ROWS = HEAD_DIM + 8


def _cparams(n_axes):
    return pltpu.CompilerParams(dimension_semantics=("arbitrary",) * n_axes,
                                vmem_limit_bytes=VMEM_LIMIT)


def _dot(a, b):
    return jnp.dot(a, b, preferred_element_type=F32)


def _dot_nt(a, b):
    return lax.dot_general(a, b, (((1,), (1,)), ((), ())), preferred_element_type=F32)


ROW_TILES = 8


def _store_row_tiles(ref, val):
    n = val.shape[0]
    for s in range(ROW_TILES):
        ref[pl.ds(s, n, stride=ROW_TILES), :] = val[:, s * LANES:(s + 1) * LANES]


def _load_row_tiles(ref, n):
    return jnp.concatenate([ref[pl.ds(s, n, stride=ROW_TILES), :] for s in range(ROW_TILES)], axis=1)


def _rms_pairs(v, bd):
    ss = _dot((v * v).astype(BF16), bd)
    return v * lax.rsqrt(ss + EPS)


def _inproj_body(x_ref, g1_ref, wq_ref, wkv_ref, wng_ref, wcv_ref, wmg_ref, gq_ref, gk_ref, bd_ref,
                 qtab_ref, kwtab_ref,
                 qa_out, hk_out, hv_out, kas_out, kaw_out, vst_out, vwt_out, gate_out, cbv_out, gab_out, raw_ref):
    x = x_ref[...]
    tm = x.shape[0]
    ms = jnp.mean(x * x, axis=-1, keepdims=True)
    h = (x * lax.rsqrt(ms + EPS) * g1_ref[...]).astype(BF16)
    bd = bd_ref[...]
    low = lax.broadcasted_iota(I32, (tm, LANES), 1) < HEAD_DIM

    def place(pair, tab_ref, out_ref, base, slot):
        for j, src in enumerate((pair, pltpu.roll(pair, HEAD_DIM, 1))):
            o = base + j * slot
            out_ref[:, o:o + LANES] = jnp.where(low, src, tab_ref[:, o:o + LANES].astype(F32)).astype(BF16)

    q = _dot(h, wq_ref[...])
    for c in range(N_HEADS * HEAD_DIM // LANES):
        sl = slice(c * LANES, (c + 1) * LANES)
        place(_rms_pairs(q[:, sl], bd) * gq_ref[:, sl], qtab_ref, qa_out, 2 * c * LANES, LANES)
    kv = _dot(h, wkv_ref[...])

    def emit_half_blocks(c, out_ref):
        nb = tm // CMP_STRIDE
        hw = CMP_STRIDE * HEAD_DIM
        raw_ref[...] = kv[:, c * LANES:(c + 1) * LANES]
        lo = lax.broadcasted_iota(I32, (nb, LANES), 1) < HEAD_DIM
        for u in range(CMP_STRIDE // 2):
            t0 = raw_ref[pl.ds(2 * u, nb, stride=CMP_STRIDE), :]
            t1 = raw_ref[pl.ds(2 * u + 1, nb, stride=CMP_STRIDE), :]
            out_ref[:, u * LANES:(u + 1) * LANES] = jnp.where(lo, t0, pltpu.roll(t1, HEAD_DIM, 1)).astype(BF16)
            out_ref[:, hw + u * LANES:hw + (u + 1) * LANES] = (
                jnp.where(lo, pltpu.roll(t0, HEAD_DIM, 1), t1).astype(BF16))

    emit_half_blocks(0, hk_out)
    emit_half_blocks(1, hv_out)
    place(_rms_pairs(kv[:, 256:384], bd) * gk_ref[:, 0:128], kwtab_ref, kas_out, 0, LANES)
    place(_rms_pairs(kv[:, 512:640], bd) * gk_ref[:, 128:256], kwtab_ref, kaw_out, 0, LANES)
    tail = jnp.where(lax.broadcasted_iota(I32, (V_ROWS - HEAD_DIM, tm), 0) == 0, 1.0, 0.0)

    def emit_values(c, out_ref, tile):
        vt = kv[:, c * LANES:(c + 1) * LANES].T
        for g in range(N_KV_HEADS):
            full = jnp.concatenate([vt[g * HEAD_DIM:(g + 1) * HEAD_DIM], tail], axis=0).astype(BF16)
            for j in range(tm // tile):
                out_ref[0, g, j] = full[:, j * tile:(j + 1) * tile]

    emit_values(3, vst_out, KEY_TILE)
    emit_values(5, vwt_out, Q_TILE)
    gate_out[...] = jax.nn.sigmoid(_dot(h, wng_ref[...]))
    cv = _dot(h, wcv_ref[...])
    cw = CONV_WIDTH
    cbv_out[:, 0:cw] = cv[:, 0:cw].astype(BF16)
    cbv_out[:, cw:2 * cw] = (cv[:, cw:2 * cw] * cv[:, 2 * cw:3 * cw]).astype(BF16)
    gab_out[...] = jax.nn.sigmoid(_dot(h, wmg_ref[...])).astype(BF16)


def _bf16_pieces(v):
    def round_bf16(a):
        u = np.ascontiguousarray(a, np.float32).view(np.uint32).astype(np.uint64)
        return ((u + 0x7FFF + ((u >> 16) & 1)) & 0xFFFF0000).astype(np.uint32).view(np.float32)

    pieces, rest = [], np.asarray(v, np.float32)
    for _ in range(N_SPLIT):
        pieces.append(round_bf16(rest))
        rest = rest - pieces[-1]
    return pieces


def _key_aug(pos):
    one = np.ones_like(pos, np.float32)
    hi = (pos // 64 * 64).astype(np.float32)
    lo = (pos % 64).astype(np.float32)
    return np.stack([one] * (2 * N_SPLIT) + [hi] * N_SPLIT + [lo] * N_SPLIT + [0 * one], axis=-1)


def _slot_table(aug, slot):
    S, n, _ = aug.shape
    tab = np.zeros((S, n, slot), np.float32)
    tab[:, :, HEAD_DIM:HEAD_DIM + N_AUG] = aug
    return jnp.asarray(tab.reshape(S, n * slot), BF16)


def _inproj(x2, g1, w_in, g_q, g_ks, g_kw, S):
    T, D = x2.shape
    H, G = N_HEADS, N_KV_HEADS
    aw = H * HEAD_DIM
    kvw = G * HEAD_DIM
    o = 0
    wq = w_in[:, o:o + aw]; o += aw
    wkv = w_in[:, o:o + 6 * kvw]; o += 6 * kvw
    wng = w_in[:, o:o + 3 * H]; o += 3 * H
    wcv = w_in[:, o:o + 3 * CONV_WIDTH]; o += 3 * CONV_WIDTH
    wmg = w_in[:, o:o + 2 * D]
    wng = jnp.pad(wng, ((0, 0), (0, LANES - 3 * H)))
    wq, wkv, wng, wcv, wmg = (w.astype(BF16) for w in (wq, wkv, wng, wcv, wmg))
    gq = (jnp.tile(g_q, H) * (HEAD_DIM ** -0.5 * LOG2E)).reshape(1, aw)
    gk = jnp.concatenate([jnp.tile(g_ks, G), jnp.tile(g_kw, G)]).reshape(1, 2 * kvw)
    idx = np.arange(LANES) // HEAD_DIM
    bd = jnp.asarray((idx[:, None] == idx[None, :]).astype(np.float32) / HEAD_DIM, BF16)
    pos = np.arange(S)
    hi = (pos // 64 * 64).astype(np.float64)[:, None]
    lo = (pos % 64).astype(np.float64)[:, None]
    c = LOG2E * 2.0 ** (-8.0 * np.arange(1, H + 1) / H)[None, :]
    cs = np.broadcast_to(c, (S, H))
    aq = np.stack(_bf16_pieces(-c * hi) + _bf16_pieces(-c * lo) + _bf16_pieces(cs) + _bf16_pieces(cs)
                  + [np.ones((S, H), np.float32)], axis=-1)
    ak = np.broadcast_to(_key_aug(pos)[:, None, :], (S, G, N_AUG))
    qtab = _slot_table(aq, LANES)
    kwtab = _slot_table(ak, LANES)
    tm = ROW_TILE
    nst = S // tm
    row = lambda w: pl.BlockSpec((tm, w), lambda i: (i, 0))
    full = lambda a: pl.BlockSpec(a.shape, lambda i: (0,) * a.ndim)
    tab = lambda a: pl.BlockSpec((tm, a.shape[1]), lambda i: (i % nst, 0))
    ins = (x2, g1.reshape(1, D), wq, wkv, wng, wcv, wmg, gq, gk, bd, qtab, kwtab)
    widths = (H * LANES, G * LANES, G * LANES, LANES, 2 * CONV_WIDTH, 2 * D)
    dtypes = (BF16, BF16, BF16, F32, BF16, BF16)
    hw = G * CMP_STRIDE * HEAD_DIM
    nb = tm // CMP_STRIDE
    half = pl.BlockSpec((nb, hw), lambda i: (i, 0))
    half_shape = jax.ShapeDtypeStruct((T // CMP_STRIDE, hw), BF16)
    assert tm % KEY_TILE == 0 and tm % Q_TILE == 0

    def values(tile):
        return (pl.BlockSpec((1, G, tm // tile, V_ROWS, tile), lambda i: (i // nst, 0, i % nst, 0, 0)),
                jax.ShapeDtypeStruct((T // S, G, S // tile, V_ROWS, tile), BF16))

    rows = [(row(w), jax.ShapeDtypeStruct((T, w), dt)) for w, dt in zip(widths, dtypes)]
    outs = [rows[0], (half, half_shape), (half, half_shape), rows[1], rows[2], values(KEY_TILE),
            values(Q_TILE)] + rows[3:]
    return pl.pallas_call(
        _inproj_body,
        grid=(T // tm,),
        in_specs=[row(D)] + [full(a) for a in ins[1:10]] + [tab(a) for a in ins[10:]],
        out_specs=[o[0] for o in outs],
        out_shape=[o[1] for o in outs],
        scratch_shapes=[pltpu.VMEM((tm, LANES), F32)],
        compiler_params=_cparams(1),
        name="inproj",
    )(*ins)


def _compress_body(h_ref, w1_ref, pe_ref, b1_ref, w2_ref, b2_ref, g_ref, tab_ref, o_ref, *, for_keys):
    hb = h_ref[0]
    nc = hb.shape[0]
    a = _dot(hb, w1_ref[0])
    b = _dot(hb, w1_ref[1])
    c = _dot(pe_ref[0], w1_ref[0]) + _dot(pe_ref[1], w1_ref[1])
    pre = a + pltpu.roll(b, nc - 1, 0) + c[0:1, :] + b1_ref[...]
    hid = jax.nn.gelu(pre)
    out = _dot(hid.astype(BF16), w2_ref[...]) + b2_ref[...]
    if for_keys:
        ms = jnp.sum(out * out, axis=-1, keepdims=True) * (1.0 / HEAD_DIM)
        out = out * lax.rsqrt(ms + EPS) * g_ref[...]
        low = lax.broadcasted_iota(I32, out.shape, 1) < HEAD_DIM
        o_ref[0, 0] = jnp.where(low, out, tab_ref[...]).astype(BF16)
    else:
        o_ref[0, 0] = out.T[0:HEAD_DIM, :].astype(BF16)


def _compress(hh, pe, w1, b1, w2, b2, gain, for_keys):
    B, NC, _ = hh.shape
    G, HW = N_KV_HEADS, CMP_STRIDE * HEAD_DIM
    w1s = w1.reshape(2, HW, CMP_HID).astype(BF16)
    pes = jnp.broadcast_to(pe.reshape(2, 1, HW), (2, 8, HW)).astype(BF16)
    padl = lambda a: jnp.pad(a, ((0, 0), (0, LANES - HEAD_DIM)))
    tabn = np.zeros((NC, LANES), np.float32)
    tabn[:, HEAD_DIM:HEAD_DIM + N_AUG] = _key_aug(np.arange(NC) * CMP_STRIDE + (CMP_LEN - 1))
    tab = jnp.asarray(tabn)
    full = lambda a: pl.BlockSpec(a.shape, lambda b, g: (0,) * a.ndim)
    ins = (hh, w1s, pes, b1.reshape(1, CMP_HID), padl(w2).astype(BF16), padl(b2.reshape(1, HEAD_DIM)),
           padl(gain.reshape(1, HEAD_DIM)), tab)
    oshape = (B, G, NC, LANES) if for_keys else (B, G, HEAD_DIM, NC)
    return pl.pallas_call(
        functools.partial(_compress_body, for_keys=for_keys),
        grid=(B, G),
        in_specs=[pl.BlockSpec((1, NC, HW), lambda b, g: (b, 0, g))] + [full(a) for a in ins[1:]],
        out_specs=pl.BlockSpec((1, 1) + oshape[2:], lambda b, g: (b, g, 0, 0)),
        out_shape=jax.ShapeDtypeStruct(oshape, BF16),
        compiler_params=_cparams(2),
        name="compress_keys" if for_keys else "compress_values",
    )(*ins)


def _attn_body(qa_ref, g_ref, kca_ref, vct_ref, kas_ref, vst_ref, kaw_ref, vwt_ref, selmapt_ref, wbias_ref,
               tilemap_ref, cbias_ref, o_ref, selb_ref, m_ref, acc_ref, s0_ref, s1_ref, flagv_ref,
               flags_ref, list_ref, sem,
               *, n_sel):
    i = pl.program_id(2)
    q0 = i * Q_TILE
    gl = GROUP_LANES
    qa = jnp.concatenate([qa_ref[0, :, h * LANES:(h + 1) * LANES] for h in range(HEADS_PER_GROUP)], axis=0)

    nc = kca_ref.shape[2]
    s = _dot_nt(kca_ref[0, 0], qa)
    cb = cbias_ref[pl.ds(pl.multiple_of(nc - i * (Q_TILE // CMP_STRIDE), 8), nc), :]
    s = s + jnp.concatenate([cb] * HEADS_PER_GROUP, axis=1)
    m = jnp.max(s, axis=0, keepdims=True)
    p = jnp.exp2(s - m)
    l = jnp.sum(p, axis=0, keepdims=True)
    has_entry = (q0 + (lax.broadcasted_iota(I32, (1, gl), 1) & (Q_TILE - 1))) >= CMP_LEN - 1
    pc = p * jnp.where(has_entry, 1.0 / l, 0.0)
    o_c = _dot(vct_ref[0, 0], pc.astype(BF16))

    ps = pc[:, 0:Q_TILE]
    for h in range(1, HEADS_PER_GROUP):
        ps = ps + pc[:, h * Q_TILE:(h + 1) * Q_TILE]
    ps_hi = ps.astype(BF16)
    ps_lo = (ps - ps_hi.astype(F32)).astype(BF16)
    imp = _dot(selmapt_ref[...], ps_hi) + _dot(selmapt_ref[...], ps_lo)
    jb = lax.broadcasted_iota(I32, (LANES, Q_TILE), 0)
    cur = (q0 + lax.broadcasted_iota(I32, (LANES, Q_TILE), 1)) // SEL_BLK
    forced = (jb == 0) | (jb == cur) | (jb == cur - 1)
    score = jnp.where(forced, -jnp.inf, jnp.where(jb > cur, NEG, imp))
    jbf = jb.astype(F32)
    for _ in range(n_sel - N_FORCED):
        mx = jnp.max(score, axis=0, keepdims=True)
        first = jnp.min(jnp.where(score == mx, jbf, float(LANES)), axis=0, keepdims=True)
        score = jnp.where(jbf == first, -jnp.inf, score)
    picked = score == -jnp.inf
    bias_t = jnp.where(picked, 0.0, NEG)
    selb_ref[...] = jnp.concatenate([bias_t] * HEADS_PER_GROUP, axis=1)
    tile_hits = jnp.max(_dot(tilemap_ref[...], jnp.where(picked, 1.0, 0.0).astype(BF16)), axis=1, keepdims=True)
    flagv_ref[...] = jnp.broadcast_to(tile_hits, flagv_ref.shape).astype(I32)
    flag_copy = pltpu.make_async_copy(flagv_ref, flags_ref, sem)
    flag_copy.start()

    w0 = pl.multiple_of(jnp.maximum(q0 - WINDOW, 0), Q_TILE)
    sw = _dot_nt(kaw_ref[0, pl.ds(w0, WIN_KEYS), :], qa)
    wb = wbias_ref[jnp.minimum(i, WINDOW // Q_TILE)]
    sw = sw + jnp.concatenate([wb] * HEADS_PER_GROUP, axis=1)
    mw = jnp.max(sw, axis=0, keepdims=True)
    pw = jnp.exp2(sw - mw)
    c0 = w0 // Q_TILE
    vw = jnp.concatenate([vwt_ref[0, 0, c0 + j] for j in range(WIN_KEYS // Q_TILE)], axis=1)
    aw = _dot(vw, pw.astype(BF16))
    o_w = aw[0:HEAD_DIM] * (1.0 / aw[HEAD_DIM:HEAD_DIM + 1])

    def scores(kt):
        sc = _dot_nt(kas_ref[0, pl.ds(pl.multiple_of(kt * KEY_TILE, KEY_TILE), KEY_TILE), :], qa)
        blocks = KEY_TILE // SEL_BLK
        mask = [jnp.broadcast_to(selb_ref[pl.ds(kt * blocks + j, 1), :], (SEL_BLK, gl)) for j in range(blocks)]
        return sc + jnp.concatenate(mask, axis=0)

    n_full = q0 // KEY_TILE
    key = n_full * KEY_TILE + lax.broadcasted_iota(I32, (KEY_TILE, gl), 0)
    qry = q0 + (lax.broadcasted_iota(I32, (KEY_TILE, gl), 1) & (Q_TILE - 1))
    s0_ref[...] = jnp.where(key <= qry, scores(n_full), NEG)
    m_ref[...] = jnp.full(m_ref.shape, -3.0e38, F32)
    acc_ref[...] = jnp.zeros(acc_ref.shape, F32)
    list_ref[0] = n_full

    def absorb(s_ref, kt, live):
        sc = s_ref[...]
        m_old = m_ref[...]
        m_new = jnp.where(live, jnp.maximum(m_old, jnp.max(sc, axis=0, keepdims=True)), m_old)
        pv = _dot(vst_ref[0, 0, kt], jnp.exp2(sc - m_new).astype(BF16))
        acc_ref[...] = jnp.exp2(m_old - m_new) * acc_ref[...] + jnp.where(live, pv, 0.0)
        m_ref[...] = m_new

    flag_copy.wait()

    def compact(kt, n):
        active = flags_ref[kt, 0] > 0

        @pl.when(active)
        def _():
            list_ref[n] = kt

        return n + active.astype(I32)

    n_items = lax.fori_loop(0, n_full, compact, 1)

    last = n_items - 1

    def pair_body(j, carry):
        a = list_ref[2 * j]
        b = list_ref[jnp.minimum(2 * j + 1, last)]
        nxt = list_ref[jnp.minimum(2 * j + 2, last)]
        s1_ref[...] = scores(b)
        absorb(s0_ref, a, True)
        s0_ref[...] = scores(nxt)
        absorb(s1_ref, b, 2 * j + 1 <= last)
        return carry

    lax.fori_loop(0, (n_items + 1) // 2, pair_body, 0)

    o_s = acc_ref[0:HEAD_DIM, :] * (1.0 / acc_ref[HEAD_DIM:HEAD_DIM + 1, :])

    g = g_ref[0, 0]
    outs = []
    for h in range(HEADS_PER_GROUP):
        sl = slice(h * Q_TILE, (h + 1) * Q_TILE)
        outs.append(g[3 * h:3 * h + 1, :] * o_c[:, sl] + g[3 * h + 1:3 * h + 2, :] * o_s[:, sl]
                    + g[3 * h + 2:3 * h + 3, :] * o_w[:, sl])
    o_ref[0] = jnp.concatenate(outs, axis=0).T.astype(BF16)


def _attention(qa, gates_t, kca, vct, kas, vst, kaw, vwt, B, S):
    G, hd = N_KV_HEADS, HEAD_DIM
    NC = kca.shape[2]
    n_blk = S // SEL_BLK
    assert n_blk <= LANES and S % KEY_TILE == 0 and S >= WIN_KEYS
    n_sel = min(SEL_TOPN, n_blk)
    assert n_sel > N_FORCED
    ratio, span = SEL_BLK // CMP_STRIDE, CMP_LEN // CMP_STRIDE
    sm = np.zeros((LANES, NC), np.float32)
    for j in range(n_blk):
        for a in range(ratio):
            for b in range(span):
                n = ratio * j + a - b
                if 0 <= n < NC - 1:
                    sm[j, n] += 1.0
    selmapt = jnp.asarray(sm, BF16)
    c = np.arange(WIN_KEYS)[:, None]
    r = np.arange(Q_TILE)[None, :]
    offs = np.arange(WINDOW // Q_TILE + 1)[:, None, None] * Q_TILE
    wbias = jnp.asarray(np.where((c - r <= offs) & (c - r > offs - WINDOW), 0.0, NEG), F32)
    tilemap = jnp.asarray(np.arange(LANES)[None, :] // (KEY_TILE // SEL_BLK) == np.arange(N_TILE_ROWS)[:, None],
                          BF16)
    assert S // KEY_TILE <= N_TILE_ROWS
    u = np.arange(2 * NC)[:, None] - NC
    cbias = jnp.asarray(np.where(CMP_STRIDE * u + (CMP_LEN - 1) <= np.arange(Q_TILE)[None, :], 0.0, NEG), F32)

    hpg = HEADS_PER_GROUP
    grp = lambda *blk: pl.BlockSpec((1, 1) + blk, lambda b, g, i: (b, g) + (0,) * len(blk))
    seq = lambda w: pl.BlockSpec((1, S, w), lambda b, g, i: (b, 0, g))
    const = lambda a: pl.BlockSpec(a.shape, lambda b, g, i: (0,) * a.ndim)
    return pl.pallas_call(
        functools.partial(_attn_body, n_sel=n_sel),
        grid=(B, G, S // Q_TILE),
        in_specs=[pl.BlockSpec((1, Q_TILE, hpg * LANES), lambda b, g, i: (b, i, g)),
                  pl.BlockSpec((1, 1, 16, Q_TILE), lambda b, g, i: (b, g, 0, i)),
                  grp(NC, LANES), grp(hd, NC), seq(LANES), grp(S // KEY_TILE, V_ROWS, KEY_TILE),
                  seq(LANES), grp(S // Q_TILE, V_ROWS, Q_TILE), const(selmapt), const(wbias), const(tilemap),
                  const(cbias)],
        out_specs=pl.BlockSpec((1, Q_TILE, hpg * hd), lambda b, g, i: (b, i, g)),
        out_shape=jax.ShapeDtypeStruct((B, S, N_HEADS * hd), BF16),
        scratch_shapes=[pltpu.VMEM((LANES, GROUP_LANES), F32), pltpu.VMEM((1, GROUP_LANES), F32),
                        pltpu.VMEM((V_ROWS, GROUP_LANES), F32), pltpu.VMEM((KEY_TILE, GROUP_LANES), F32),
                        pltpu.VMEM((KEY_TILE, GROUP_LANES), F32),
                        pltpu.VMEM((N_TILE_ROWS, LANES), I32), pltpu.SMEM((N_TILE_ROWS, LANES), I32),
                        pltpu.SMEM((N_TILE_ROWS,), I32), pltpu.SemaphoreType.DMA(())],
        compiler_params=_cparams(3),
        name="nsa_attention",
    )(qa, gates_t, kca, vct, kas, vst, kaw, vwt, selmapt, wbias, tilemap, cbias)


def _mixer_out_body(x_ref, oa_ref, cbv_ref, halo_ref, gab_ref, cw_ref, wpa_ref, wpb_ref, wo_ref,
                    g2_ref, whi_ref, wlo_ref, br_ref, tri_ref, o_ref, h_out, mi_out, mf_out, cnt_out,
                    *, seq_len):
    i = pl.program_id(0)
    tm = x_ref.shape[0]
    cwd = CONV_WIDTH
    d = x_ref.shape[1]
    v = cbv_ref[:, cwd:2 * cwd].astype(F32)
    prev = halo_ref[:, cwd:2 * cwd].astype(F32)
    keep = ((i * tm) % seq_len != 0).astype(F32)
    p1 = prev[7:8, :] * keep
    p2 = prev[6:7, :] * keep
    ridx = lax.broadcasted_iota(I32, (tm, cwd), 0)
    v1 = jnp.where(ridx == 0, p1, pltpu.roll(v, 1, 0))
    v2 = jnp.where(ridx == 0, p2, jnp.where(ridx == 1, p1, pltpu.roll(v, 2, 0)))
    y = cw_ref[0:1, :] * v2 + cw_ref[1:2, :] * v1 + cw_ref[2:3, :] * v
    yb_in = (cbv_ref[:, 0:cwd].astype(F32) * y).astype(BF16)
    y_a = _dot(oa_ref[...], wpa_ref[...])
    y_b = _dot(yb_in, wpb_ref[...])
    merged = gab_ref[:, 0:d].astype(F32) * y_a + gab_ref[:, d:2 * d].astype(F32) * y_b
    x1 = x_ref[...] + _dot(merged.astype(BF16), wo_ref[...])
    o_ref[...] = x1
    _route(x1, g2_ref, whi_ref, wlo_ref, br_ref, tri_ref, h_out, mi_out, mf_out, cnt_out)


def _mixer_out(x2, oa, cbv, gab, conv_w, w_pa, w_pb, w_o, seq_len, g2, w_r, b_r):
    T, D = x2.shape
    assert D == ROW_TILES * LANES
    tm = ROW_TILE
    cw8 = jnp.pad(conv_w, ((0, 8 - CONV_K), (0, 0)))
    wpad = jnp.pad(w_r, ((0, 0), (0, LANES - N_EXPERTS)))
    whi = wpad.astype(BF16)
    wlo = (wpad - whi.astype(F32)).astype(BF16)
    br = jnp.pad(b_r, (0, LANES - N_EXPERTS)).reshape(1, LANES)
    tri = jnp.asarray(np.tril(np.ones((tm, tm), np.float32), -1), BF16)
    row = lambda w: pl.BlockSpec((tm, w), lambda i: (i, 0))
    full = lambda a: pl.BlockSpec(a.shape, lambda i: (0,) * a.ndim)
    halo = pl.BlockSpec((8, cbv.shape[1]), lambda i: (jnp.maximum(i * (tm // 8) - 1, 0), 0))
    wts = (cw8, w_pa.astype(BF16), w_pb.astype(BF16), w_o.astype(BF16), g2.reshape(1, D), whi, wlo, br, tri)
    return pl.pallas_call(
        functools.partial(_mixer_out_body, seq_len=seq_len),
        grid=(T // tm,),
        in_specs=[row(D), row(oa.shape[1]), row(cbv.shape[1]), halo, row(gab.shape[1])] + [full(a) for a in wts],
        out_specs=[row(D), pl.BlockSpec((tm * ROW_TILES, LANES), lambda i: (i, 0)), row(2 * TOP_K), row(LANES),
                   pl.BlockSpec((8, LANES), lambda i: (0, 0))],
        out_shape=[jax.ShapeDtypeStruct((T, D), F32), jax.ShapeDtypeStruct((T * ROW_TILES, LANES), F32),
                   jax.ShapeDtypeStruct((T, 2 * TOP_K), I32), jax.ShapeDtypeStruct((T, LANES), F32),
                   jax.ShapeDtypeStruct((8, LANES), F32)],
        compiler_params=_cparams(1),
        name="mixer_out_router",
    )(x2, oa, cbv, cbv, gab, *wts)


def _route(x, g2_ref, whi_ref, wlo_ref, br_ref, tri_ref, h_out, mi_out, mf_out, cnt_out):
    i = pl.program_id(0)

    @pl.when(i == 0)
    def _():
        cnt_out[...] = jnp.zeros(cnt_out.shape, F32)

    tm = x.shape[0]
    ms = jnp.mean(x * x, axis=-1, keepdims=True)
    h = x * lax.rsqrt(ms + EPS) * g2_ref[...]
    _store_row_tiles(h_out, h)
    h_hi = h.astype(BF16)
    h_lo = (h - h_hi.astype(F32)).astype(BF16)
    logits = (_dot(h_hi, whi_ref[...]) + _dot(h_lo, whi_ref[...]) + _dot(h_hi, wlo_ref[...])) + br_ref[...]
    lane = lax.broadcasted_iota(I32, (tm, LANES), 1)
    lanef = lane.astype(F32)
    work = jnp.where(lane < N_EXPERTS, logits, -jnp.inf)
    vals, hits = [], []
    for _ in range(TOP_K):
        mx = jnp.max(work, axis=-1, keepdims=True)
        first = jnp.min(jnp.where(work == mx, lanef, float(LANES)), axis=-1, keepdims=True)
        hit = lanef == first
        vals.append(mx)
        hits.append(hit)
        work = jnp.where(hit, -jnp.inf, work)
    ex = [jnp.exp(v - vals[0]) for v in vals]
    den = ex[0]
    for e in ex[1:]:
        den = den + e
    cnt = jnp.zeros((tm, LANES), F32)
    for hit in hits:
        cnt = cnt + hit.astype(F32)
    before = _dot(tri_ref[...], cnt.astype(BF16)) + cnt_out[0:1, :]
    mi = jnp.zeros((tm, LANES), F32)
    mf = jnp.zeros((tm, LANES), F32)
    for k, hit in enumerate(hits):
        e_k = jnp.sum(jnp.where(hit, lanef, 0.0), axis=-1, keepdims=True)
        r_k = jnp.sum(jnp.where(hit, before, 0.0), axis=-1, keepdims=True)
        mi = jnp.where(lane == k, e_k, jnp.where(lane == TOP_K + k, r_k, mi))
        mf = jnp.where(lane == k, ex[k] / den, mf)
    mi_out[...] = mi[:, 0:2 * TOP_K].astype(I32)
    mf_out[...] = mf
    cnt_out[...] = cnt_out[...] + jnp.sum(cnt, axis=0, keepdims=True)


DISPATCH_TILE = 512
DMA_UNROLL = 8


def _dispatch_body(dest_ref, last_ref, h_ref, o_hbm, zero_ref, sem, zsem):
    @pl.when(pl.program_id(0) == 0)
    def _():
        zero_ref[...] = jnp.zeros(zero_ref.shape, zero_ref.dtype)

        def clear(e):
            start = pl.multiple_of(last_ref[e] * ROW_TILES, MOE_CHUNK * ROW_TILES)
            return pltpu.make_async_copy(zero_ref, o_hbm.at[pl.ds(start, MOE_CHUNK * ROW_TILES)], zsem)

        for e in range(N_EXPERTS):
            @pl.when(last_ref[e] >= 0)
            def _():
                clear(e).start()
        for e in range(N_EXPERTS):
            @pl.when(last_ref[e] >= 0)
            def _():
                clear(e).wait()

    def row_copy(r, d):
        return pltpu.make_async_copy(h_ref.at[pl.ds(pl.multiple_of(r * ROW_TILES, ROW_TILES), ROW_TILES)],
                                     o_hbm.at[pl.ds(pl.multiple_of(d * ROW_TILES, ROW_TILES), ROW_TILES)], sem)

    def start(r, c):
        for k in range(TOP_K):
            row_copy(r, dest_ref[0, 0, r * TOP_K + k]).start(priority=k % 2)
        return c

    def wait(r, c):
        for k in range(TOP_K):
            row_copy(0, 0).wait()
        return c

    lax.fori_loop(0, DISPATCH_TILE, start, 0, unroll=DMA_UNROLL)
    lax.fori_loop(0, DISPATCH_TILE, wait, 0, unroll=DMA_UNROLL)


def _dispatch(h2, dest, last_chunk, n_rows):
    T = h2.shape[0] // ROW_TILES
    td = DISPATCH_TILE
    dest3 = dest.reshape(T // td, 1, td * TOP_K)
    return pl.pallas_call(
        _dispatch_body,
        grid=(T // td,),
        in_specs=[pl.BlockSpec((1, 1, td * TOP_K), lambda i: (i, 0, 0), memory_space=pltpu.SMEM),
                  pl.BlockSpec(memory_space=pltpu.SMEM),
                  pl.BlockSpec((td * ROW_TILES, LANES), lambda i: (i, 0))],
        out_specs=pl.BlockSpec(memory_space=pl.ANY),
        out_shape=jax.ShapeDtypeStruct((n_rows * ROW_TILES, LANES), h2.dtype),
        scratch_shapes=[pltpu.VMEM((MOE_CHUNK * ROW_TILES, LANES), h2.dtype), pltpu.SemaphoreType.DMA(()),
                        pltpu.SemaphoreType.DMA(())],
        compiler_params=_cparams(1),
        name="dispatch",
    )(dest3, last_chunk, h2)


def _expert_body(ce_ref, nu_ref, x_ref, wgu_ref, bgu_ref, wdn_ref, bdn_ref, o_ref, wgu_bf, wdn_bf):
    c = pl.program_id(0)
    dff = wdn_ref.shape[1]

    @pl.when((c == 0) | (ce_ref[c] != ce_ref[jnp.maximum(c - 1, 0)]))
    def _():
        wgu_bf[...] = wgu_ref[0].astype(BF16)
        wdn_bf[...] = wdn_ref[0].astype(BF16)

    @pl.when(c < nu_ref[0])
    def _():
        x = _load_row_tiles(x_ref, MOE_CHUNK)
        gu = _dot(x.astype(BF16), wgu_bf[...]) + bgu_ref[0]
        g = jnp.minimum(gu[:, 0:dff], SWIGLU_LIMIT)
        u = jnp.clip(gu[:, dff:2 * dff], -SWIGLU_LIMIT, SWIGLU_LIMIT)
        act = (u + 1.0) * (g * jax.nn.sigmoid(SWIGLU_ALPHA * g))
        _store_row_tiles(o_ref, _dot(act.astype(BF16), wdn_bf[...]) + bdn_ref[0])

    @pl.when(c >= nu_ref[0])
    def _():
        o_ref[...] = jnp.zeros(o_ref.shape, F32)


def _experts(hperm, chunk_e, n_used, w_gu, b_gu, w_dn, b_dn):
    E, D, F2 = w_gu.shape
    assert D == ROW_TILES * LANES
    P = hperm.shape[0] // ROW_TILES
    dff = F2 // 2
    n_chunks = P // MOE_CHUNK
    chunk = (MOE_CHUNK * ROW_TILES, LANES)
    grid_spec = pltpu.PrefetchScalarGridSpec(
        num_scalar_prefetch=2,
        grid=(n_chunks,),
        in_specs=[pl.BlockSpec(chunk, lambda c, ce, nu: (jnp.minimum(c, nu[0] - 1), 0)),
                  pl.BlockSpec((1, D, F2), lambda c, ce, nu: (ce[c], 0, 0)),
                  pl.BlockSpec((1, 1, F2), lambda c, ce, nu: (ce[c], 0, 0)),
                  pl.BlockSpec((1, dff, D), lambda c, ce, nu: (ce[c], 0, 0)),
                  pl.BlockSpec((1, 1, D), lambda c, ce, nu: (ce[c], 0, 0))],
        out_specs=pl.BlockSpec(chunk, lambda c, ce, nu: (c, 0)),
        scratch_shapes=[pltpu.VMEM((D, F2), BF16), pltpu.VMEM((dff, D), BF16)],
    )
    return pl.pallas_call(
        _expert_body,
        grid_spec=grid_spec,
        out_shape=jax.ShapeDtypeStruct(hperm.shape, F32),
        compiler_params=_cparams(1),
        name="experts",
    )(chunk_e, n_used, hperm, w_gu, b_gu.reshape(E, 1, F2), w_dn, b_dn.reshape(E, 1, D))


COMBINE_TILE = 256


def _combine_body(dest_ref, next_ref, x_ref, w_ref, y_hbm, o_ref, buf_ref, sems):
    i = pl.program_id(0)
    slot = i % 2

    def row_copy(s, r, k, d):
        return pltpu.make_async_copy(y_hbm.at[pl.ds(pl.multiple_of(d * ROW_TILES, ROW_TILES), ROW_TILES)],
                                     buf_ref.at[s, k, pl.ds(pl.multiple_of(r * ROW_TILES, ROW_TILES), ROW_TILES)],
                                     sems.at[s])

    def fetch(idx_ref, s):
        def start(r, c):
            for k in range(TOP_K):
                row_copy(s, r, k, idx_ref[0, 0, r * TOP_K + k]).start(priority=k % 2)
            return c

        lax.fori_loop(0, COMBINE_TILE, start, 0, unroll=DMA_UNROLL)

    @pl.when(i == 0)
    def _():
        fetch(dest_ref, slot)

    @pl.when(i + 1 < pl.num_programs(0))
    def _():
        fetch(next_ref, 1 - slot)

    def wait(r, c):
        for k in range(TOP_K):
            row_copy(slot, 0, 0, 0).wait()
        return c

    lax.fori_loop(0, COMBINE_TILE, wait, 0, unroll=DMA_UNROLL)
    gate = [jnp.broadcast_to(w_ref[:, k:k + 1], (COMBINE_TILE, LANES)) for k in range(TOP_K)]
    for s in range(ROW_TILES):
        sl = slice(s * LANES, (s + 1) * LANES)
        out = x_ref[:, sl]
        for k in range(TOP_K):
            out = out + gate[k] * buf_ref[slot, k, pl.ds(s, COMBINE_TILE, stride=ROW_TILES), :]
        o_ref[:, sl] = out


def _combine(x1, gate_w, dest, ys):
    T, D = x1.shape
    tc = COMBINE_TILE
    n = T // tc
    dest3 = dest.reshape(n, 1, tc * TOP_K)
    row = lambda w: pl.BlockSpec((tc, w), lambda i: (i, 0))
    idx = lambda f: pl.BlockSpec((1, 1, tc * TOP_K), f, memory_space=pltpu.SMEM)
    return pl.pallas_call(
        _combine_body,
        grid=(n,),
        in_specs=[idx(lambda i: (i, 0, 0)), idx(lambda i: (jnp.minimum(i + 1, n - 1), 0, 0)),
                  row(D), row(LANES), pl.BlockSpec(memory_space=pl.ANY)],
        out_specs=row(D),
        out_shape=jax.ShapeDtypeStruct((T, D), F32),
        scratch_shapes=[pltpu.VMEM((2, TOP_K, tc * ROW_TILES, LANES), F32), pltpu.SemaphoreType.DMA((2,))],
        compiler_params=_cparams(1),
        name="combine",
    )(dest3, dest3, x1, gate_w, ys)


def _mixer(x2, B, S, g_norm1, w_in, g_q, g_kc, g_ks, g_kw, pe_k, ck_w1, ck_b1, ck_w2, ck_b2,
           pe_v, cv_w1, cv_b1, cv_w2, cv_b2, conv_w, w_pa, w_pb, w_o, g_norm2, w_r, b_r):
    T, D = x2.shape
    G, H, hd = N_KV_HEADS, N_HEADS, HEAD_DIM
    qa, hk, hv, kas, kaw, vst, vwt, gates, cbv, gab = _inproj(x2, g_norm1, w_in, g_q, g_ks, g_kw, S)
    nh = S // CMP_STRIDE
    kca = _compress(hk.reshape(B, nh, -1), pe_k, ck_w1, ck_b1, ck_w2, ck_b2, g_kc, True)
    vct = _compress(hv.reshape(B, nh, -1), pe_v, cv_w1, cv_b1, cv_w2, cv_b2, jnp.ones((hd,), F32), False)
    gat = gates[:, :3 * H].reshape(B, S, G, 3 * HEADS_PER_GROUP).transpose(0, 2, 3, 1)
    gat = jnp.pad(gat, ((0, 0), (0, 0), (0, 16 - 3 * HEADS_PER_GROUP), (0, 0)))
    o = _attention(qa.reshape(B, S, -1), gat, kca, vct, kas.reshape(B, S, -1), vst, kaw.reshape(B, S, -1),
                   vwt, B, S)
    return _mixer_out(x2, o.reshape(T, H * hd), cbv, gab, conv_w, w_pa, w_pb, w_o, S, g_norm2, w_r, b_r)


def _moe(x1, h2, mi, mf, cnt, w_gu, b_gu, w_dn, b_dn):
    T, D = x1.shape
    top_e = mi[:, 0:TOP_K]
    rank = mi[:, TOP_K:2 * TOP_K]
    counts = cnt[0, :N_EXPERTS].astype(I32)
    padded = (counts + MOE_CHUNK - 1) // MOE_CHUNK * MOE_CHUNK
    pend = jnp.cumsum(padded)
    poffs = pend - padded
    dest = (poffs[top_e] + rank).reshape(-1)
    n_chunks = (T * TOP_K + MOE_CHUNK - 1) // MOE_CHUNK + N_EXPERTS
    chunk_start = jnp.arange(n_chunks, dtype=I32) * MOE_CHUNK
    chunk_e = jnp.minimum(jnp.sum((pend[None, :] <= chunk_start[:, None]).astype(I32), axis=1), N_EXPERTS - 1)
    n_used = (pend[-1:] // MOE_CHUNK).astype(I32)
    last_chunk = jnp.where(padded > 0, pend - MOE_CHUNK, -1).astype(I32)
    hperm = _dispatch(h2, dest, last_chunk, n_chunks * MOE_CHUNK)
    ys = _experts(hperm, chunk_e, n_used, w_gu, b_gu, w_dn, b_dn)
    return _combine(x1, mf, dest, ys)


def kernel(x, g_norm1, w_in, g_q, g_kc, g_ks, g_kw, pe_k, ck_w1, ck_b1, ck_w2, ck_b2, pe_v, cv_w1, cv_b1,
           cv_w2, cv_b2, conv_w, w_pa, w_pb, w_o, g_norm2, w_r, b_r, w_gu, b_gu, w_dn, b_dn):
    B, S, D = x.shape
    x2 = x.reshape(B * S, D)
    for l in range(g_norm1.shape[0]):
        routed = _mixer(x2, B, S, g_norm1[l], w_in[l], g_q[l], g_kc[l], g_ks[l], g_kw[l], pe_k[l], ck_w1[l],
                        ck_b1[l], ck_w2[l], ck_b2[l], pe_v[l], cv_w1[l], cv_b1[l], cv_w2[l], cv_b2[l],
                        conv_w[l], w_pa[l], w_pb[l], w_o[l], g_norm2[l], w_r[l], b_r[l])
        x2 = _moe(*routed, w_gu[l], b_gu[l], w_dn[l], b_dn[l])
    return x2.reshape(B, S, D)
```

```python
import functools

import numpy as np
import jax
import jax.numpy as jnp
from jax import lax
from jax.experimental import pallas as pl
from jax.experimental.pallas import tpu as pltpu

F32 = jnp.float32
BF16 = jnp.bfloat16
I32 = jnp.int32

N_HEADS = 8
HEAD_DIM = 64
N_KV_HEADS = 2
HEADS_PER_GROUP = N_HEADS // N_KV_HEADS
CMP_LEN = 32
CMP_STRIDE = 16
CMP_HID = 256
SEL_BLK = 64
SEL_TOPN = 16
WINDOW = 512
CONV_WIDTH = 512
CONV_K = 3
N_EXPERTS = 32
TOP_K = 4
SWIGLU_LIMIT = 7.0
SWIGLU_ALPHA = 1.702
MOE_CHUNK = 512
EPS = 1e-6
NEG = -1e30
N_FORCED = 3

LANES = 128
Q_TILE = 256
KEY_TILE = 512
LOG2E = float(np.log2(np.e))
N_SPLIT = 3
N_AUG = 4 * N_SPLIT + 1
ROW_TILE = 512
VMEM_LIMIT = 56 * 1024 * 1024
GROUP_LANES = HEADS_PER_GROUP * Q_TILE
WIN_KEYS = WINDOW + Q_TILE
N_TILE_ROWS = 16
V_ROWS = HEAD_DIM + 8


def _cparams(n_axes):
    return pltpu.CompilerParams(dimension_semantics=("arbitrary",) * n_axes,
                                vmem_limit_bytes=VMEM_LIMIT)


def _dot(a, b):
    return jnp.dot(a, b, preferred_element_type=F32)


def _dot_nt(a, b):
    return lax.dot_general(a, b, (((1,), (1,)), ((), ())), preferred_element_type=F32)


ROW_TILES = 8


def _store_row_tiles(ref, val):
    n = val.shape[0]
    for s in range(ROW_TILES):
        ref[pl.ds(s, n, stride=ROW_TILES), :] = val[:, s * LANES:(s + 1) * LANES]


def _load_row_tiles(ref, n):
    return jnp.concatenate([ref[pl.ds(s, n, stride=ROW_TILES), :] for s in range(ROW_TILES)], axis=1)


def _rms_pairs(v, bd):
    ss = _dot((v * v).astype(BF16), bd)
    return v * lax.rsqrt(ss + EPS)


def _inproj_body(x_ref, g1_ref, wq_ref, wkv_ref, wng_ref, wcv_ref, wmg_ref, gq_ref, gk_ref, bd_ref,
                 qtab_ref, kwtab_ref,
                 qa_out, hk_out, hv_out, kas_out, kaw_out, vst_out, vwt_out, gate_out, cbv_out, gab_out, raw_ref):
    x = x_ref[...]
    tm = x.shape[0]
    ms = jnp.mean(x * x, axis=-1, keepdims=True)
    h = (x * lax.rsqrt(ms + EPS) * g1_ref[...]).astype(BF16)
    bd = bd_ref[...]
    low = lax.broadcasted_iota(I32, (tm, LANES), 1) < HEAD_DIM

    def place(pair, tab_ref, out_ref, base, slot):
        for j, src in enumerate((pair, pltpu.roll(pair, HEAD_DIM, 1))):
            o = base + j * slot
            out_ref[:, o:o + LANES] = jnp.where(low, src, tab_ref[:, o:o + LANES].astype(F32)).astype(BF16)

    q = _dot(h, wq_ref[...])
    for c in range(N_HEADS * HEAD_DIM // LANES):
        sl = slice(c * LANES, (c + 1) * LANES)
        place(_rms_pairs(q[:, sl], bd) * gq_ref[:, sl], qtab_ref, qa_out, 2 * c * LANES, LANES)
    kv = _dot(h, wkv_ref[...])

    def emit_half_blocks(c, out_ref):
        nb = tm // CMP_STRIDE
        hw = CMP_STRIDE * HEAD_DIM
        raw_ref[...] = kv[:, c * LANES:(c + 1) * LANES]
        lo = lax.broadcasted_iota(I32, (nb, LANES), 1) < HEAD_DIM
        for u in range(CMP_STRIDE // 2):
            t0 = raw_ref[pl.ds(2 * u, nb, stride=CMP_STRIDE), :]
            t1 = raw_ref[pl.ds(2 * u + 1, nb, stride=CMP_STRIDE), :]
            out_ref[:, u * LANES:(u + 1) * LANES] = jnp.where(lo, t0, pltpu.roll(t1, HEAD_DIM, 1)).astype(BF16)
            out_ref[:, hw + u * LANES:hw + (u + 1) * LANES] = (
                jnp.where(lo, pltpu.roll(t0, HEAD_DIM, 1), t1).astype(BF16))

    emit_half_blocks(0, hk_out)
    emit_half_blocks(1, hv_out)
    place(_rms_pairs(kv[:, 256:384], bd) * gk_ref[:, 0:128], kwtab_ref, kas_out, 0, LANES)
    place(_rms_pairs(kv[:, 512:640], bd) * gk_ref[:, 128:256], kwtab_ref, kaw_out, 0, LANES)
    tail = jnp.where(lax.broadcasted_iota(I32, (V_ROWS - HEAD_DIM, tm), 0) == 0, 1.0, 0.0)

    def emit_values(c, out_ref, tile):
        vt = kv[:, c * LANES:(c + 1) * LANES].T
        for g in range(N_KV_HEADS):
            full = jnp.concatenate([vt[g * HEAD_DIM:(g + 1) * HEAD_DIM], tail], axis=0).astype(BF16)
            for j in range(tm // tile):
                out_ref[0, g, j] = full[:, j * tile:(j + 1) * tile]

    emit_values(3, vst_out, KEY_TILE)
    emit_values(5, vwt_out, Q_TILE)
    gate_out[...] = jax.nn.sigmoid(_dot(h, wng_ref[...]))
    cv = _dot(h, wcv_ref[...])
    cw = CONV_WIDTH
    cbv_out[:, 0:cw] = cv[:, 0:cw].astype(BF16)
    cbv_out[:, cw:2 * cw] = (cv[:, cw:2 * cw] * cv[:, 2 * cw:3 * cw]).astype(BF16)
    gab_out[...] = jax.nn.sigmoid(_dot(h, wmg_ref[...])).astype(BF16)


def _bf16_pieces(v):
    def round_bf16(a):
        u = np.ascontiguousarray(a, np.float32).view(np.uint32).astype(np.uint64)
        return ((u + 0x7FFF + ((u >> 16) & 1)) & 0xFFFF0000).astype(np.uint32).view(np.float32)

    pieces, rest = [], np.asarray(v, np.float32)
    for _ in range(N_SPLIT):
        pieces.append(round_bf16(rest))
        rest = rest - pieces[-1]
    return pieces


def _key_aug(pos):
    one = np.ones_like(pos, np.float32)
    hi = (pos // 64 * 64).astype(np.float32)
    lo = (pos % 64).astype(np.float32)
    return np.stack([one] * (2 * N_SPLIT) + [hi] * N_SPLIT + [lo] * N_SPLIT + [0 * one], axis=-1)


def _slot_table(aug, slot):
    S, n, _ = aug.shape
    tab = np.zeros((S, n, slot), np.float32)
    tab[:, :, HEAD_DIM:HEAD_DIM + N_AUG] = aug
    return jnp.asarray(tab.reshape(S, n * slot), BF16)


def _inproj(x2, g1, w_in, g_q, g_ks, g_kw, S):
    T, D = x2.shape
    H, G = N_HEADS, N_KV_HEADS
    aw = H * HEAD_DIM
    kvw = G * HEAD_DIM
    o = 0
    wq = w_in[:, o:o + aw]; o += aw
    wkv = w_in[:, o:o + 6 * kvw]; o += 6 * kvw
    wng = w_in[:, o:o + 3 * H]; o += 3 * H
    wcv = w_in[:, o:o + 3 * CONV_WIDTH]; o += 3 * CONV_WIDTH
    wmg = w_in[:, o:o + 2 * D]
    wng = jnp.pad(wng, ((0, 0), (0, LANES - 3 * H)))
    wq, wkv, wng, wcv, wmg = (w.astype(BF16) for w in (wq, wkv, wng, wcv, wmg))
    gq = (jnp.tile(g_q, H) * (HEAD_DIM ** -0.5 * LOG2E)).reshape(1, aw)
    gk = jnp.concatenate([jnp.tile(g_ks, G), jnp.tile(g_kw, G)]).reshape(1, 2 * kvw)
    idx = np.arange(LANES) // HEAD_DIM
    bd = jnp.asarray((idx[:, None] == idx[None, :]).astype(np.float32) / HEAD_DIM, BF16)
    pos = np.arange(S)
    hi = (pos // 64 * 64).astype(np.float64)[:, None]
    lo = (pos % 64).astype(np.float64)[:, None]
    c = LOG2E * 2.0 ** (-8.0 * np.arange(1, H + 1) / H)[None, :]
    cs = np.broadcast_to(c, (S, H))
    aq = np.stack(_bf16_pieces(-c * hi) + _bf16_pieces(-c * lo) + _bf16_pieces(cs) + _bf16_pieces(cs)
                  + [np.ones((S, H), np.float32)], axis=-1)
    ak = np.broadcast_to(_key_aug(pos)[:, None, :], (S, G, N_AUG))
    qtab = _slot_table(aq, LANES)
    kwtab = _slot_table(ak, LANES)
    tm = ROW_TILE
    nst = S // tm
    row = lambda w: pl.BlockSpec((tm, w), lambda i: (i, 0))
    full = lambda a: pl.BlockSpec(a.shape, lambda i: (0,) * a.ndim)
    tab = lambda a: pl.BlockSpec((tm, a.shape[1]), lambda i: (i % nst, 0))
    ins = (x2, g1.reshape(1, D), wq, wkv, wng, wcv, wmg, gq, gk, bd, qtab, kwtab)
    widths = (H * LANES, G * LANES, G * LANES, LANES, 2 * CONV_WIDTH, 2 * D)
    dtypes = (BF16, BF16, BF16, F32, BF16, BF16)
    hw = G * CMP_STRIDE * HEAD_DIM
    nb = tm // CMP_STRIDE
    half = pl.BlockSpec((nb, hw), lambda i: (i, 0))
    half_shape = jax.ShapeDtypeStruct((T // CMP_STRIDE, hw), BF16)
    assert tm % KEY_TILE == 0 and tm % Q_TILE == 0

    def values(tile):
        return (pl.BlockSpec((1, G, tm // tile, V_ROWS, tile), lambda i: (i // nst, 0, i % nst, 0, 0)),
                jax.ShapeDtypeStruct((T // S, G, S // tile, V_ROWS, tile), BF16))

    rows = [(row(w), jax.ShapeDtypeStruct((T, w), dt)) for w, dt in zip(widths, dtypes)]
    outs = [rows[0], (half, half_shape), (half, half_shape), rows[1], rows[2], values(KEY_TILE),
            values(Q_TILE)] + rows[3:]
    return pl.pallas_call(
        _inproj_body,
        grid=(T // tm,),
        in_specs=[row(D)] + [full(a) for a in ins[1:10]] + [tab(a) for a in ins[10:]],
        out_specs=[o[0] for o in outs],
        out_shape=[o[1] for o in outs],
        scratch_shapes=[pltpu.VMEM((tm, LANES), F32)],
        compiler_params=_cparams(1),
        name="inproj",
    )(*ins)


def _compress_body(h_ref, w1_ref, pe_ref, b1_ref, w2_ref, b2_ref, g_ref, tab_ref, o_ref, *, for_keys):
    hb = h_ref[0]
    nc = hb.shape[0]
    a = _dot(hb, w1_ref[0])
    b = _dot(hb, w1_ref[1])
    c = _dot(pe_ref[0], w1_ref[0]) + _dot(pe_ref[1], w1_ref[1])
    pre = a + pltpu.roll(b, nc - 1, 0) + c[0:1, :] + b1_ref[...]
    hid = jax.nn.gelu(pre)
    out = _dot(hid.astype(BF16), w2_ref[...]) + b2_ref[...]
    if for_keys:
        ms = jnp.sum(out * out, axis=-1, keepdims=True) * (1.0 / HEAD_DIM)
        out = out * lax.rsqrt(ms + EPS) * g_ref[...]
        low = lax.broadcasted_iota(I32, out.shape, 1) < HEAD_DIM
        o_ref[0, 0] = jnp.where(low, out, tab_ref[...]).astype(BF16)
    else:
        o_ref[0, 0] = out.T[0:HEAD_DIM, :].astype(BF16)


def _compress(hh, pe, w1, b1, w2, b2, gain, for_keys):
    B, NC, _ = hh.shape
    G, HW = N_KV_HEADS, CMP_STRIDE * HEAD_DIM
    w1s = w1.reshape(2, HW, CMP_HID).astype(BF16)
    pes = jnp.broadcast_to(pe.reshape(2, 1, HW), (2, 8, HW)).astype(BF16)
    padl = lambda a: jnp.pad(a, ((0, 0), (0, LANES - HEAD_DIM)))
    tabn = np.zeros((NC, LANES), np.float32)
    tabn[:, HEAD_DIM:HEAD_DIM + N_AUG] = _key_aug(np.arange(NC) * CMP_STRIDE + (CMP_LEN - 1))
    tab = jnp.asarray(tabn)
    full = lambda a: pl.BlockSpec(a.shape, lambda b, g: (0,) * a.ndim)
    ins = (hh, w1s, pes, b1.reshape(1, CMP_HID), padl(w2).astype(BF16), padl(b2.reshape(1, HEAD_DIM)),
           padl(gain.reshape(1, HEAD_DIM)), tab)
    oshape = (B, G, NC, LANES) if for_keys else (B, G, HEAD_DIM, NC)
    return pl.pallas_call(
        functools.partial(_compress_body, for_keys=for_keys),
        grid=(B, G),
        in_specs=[pl.BlockSpec((1, NC, HW), lambda b, g: (b, 0, g))] + [full(a) for a in ins[1:]],
        out_specs=pl.BlockSpec((1, 1) + oshape[2:], lambda b, g: (b, g, 0, 0)),
        out_shape=jax.ShapeDtypeStruct(oshape, BF16),
        compiler_params=_cparams(2),
        name="compress_keys" if for_keys else "compress_values",
    )(*ins)


def _attn_body(qa_ref, g_ref, kca_ref, vct_ref, kas_ref, vst_ref, kaw_ref, vwt_ref, selmapt_ref, wbias_ref,
               tilemap_ref, cbias_ref, o_ref, selb_ref, m_ref, acc_ref, s0_ref, s1_ref, flagv_ref,
               flags_ref, list_ref, sem,
               *, n_sel):
    i = pl.program_id(2)
    q0 = i * Q_TILE
    gl = GROUP_LANES
    qa = jnp.concatenate([qa_ref[0, :, h * LANES:(h + 1) * LANES] for h in range(HEADS_PER_GROUP)], axis=0)

    nc = kca_ref.shape[2]
    s = _dot_nt(kca_ref[0, 0], qa)
    cb = cbias_ref[pl.ds(pl.multiple_of(nc - i * (Q_TILE // CMP_STRIDE), 8), nc), :]
    s = s + jnp.concatenate([cb] * HEADS_PER_GROUP, axis=1)
    m = jnp.max(s, axis=0, keepdims=True)
    p = jnp.exp2(s - m)
    l = jnp.sum(p, axis=0, keepdims=True)
    has_entry = (q0 + (lax.broadcasted_iota(I32, (1, gl), 1) & (Q_TILE - 1))) >= CMP_LEN - 1
    pc = p * jnp.where(has_entry, 1.0 / l, 0.0)
    o_c = _dot(vct_ref[0, 0], pc.astype(BF16))

    ps = pc[:, 0:Q_TILE]
    for h in range(1, HEADS_PER_GROUP):
        ps = ps + pc[:, h * Q_TILE:(h + 1) * Q_TILE]
    ps_hi = ps.astype(BF16)
    ps_lo = (ps - ps_hi.astype(F32)).astype(BF16)
    imp = _dot(selmapt_ref[...], ps_hi) + _dot(selmapt_ref[...], ps_lo)
    jb = lax.broadcasted_iota(I32, (LANES, Q_TILE), 0)
    cur = (q0 + lax.broadcasted_iota(I32, (LANES, Q_TILE), 1)) // SEL_BLK
    forced = (jb == 0) | (jb == cur) | (jb == cur - 1)
    score = jnp.where(forced, -jnp.inf, jnp.where(jb > cur, NEG, imp))
    jbf = jb.astype(F32)
    for _ in range(n_sel - N_FORCED):
        mx = jnp.max(score, axis=0, keepdims=True)
        first = jnp.min(jnp.where(score == mx, jbf, float(LANES)), axis=0, keepdims=True)
        score = jnp.where(jbf == first, -jnp.inf, score)
    picked = score == -jnp.inf
    bias_t = jnp.where(picked, 0.0, NEG)
    selb_ref[...] = jnp.concatenate([bias_t] * HEADS_PER_GROUP, axis=1)
    tile_hits = jnp.max(_dot(tilemap_ref[...], jnp.where(picked, 1.0, 0.0).astype(BF16)), axis=1, keepdims=True)
    flagv_ref[...] = jnp.broadcast_to(tile_hits, flagv_ref.shape).astype(I32)
    flag_copy = pltpu.make_async_copy(flagv_ref, flags_ref, sem)
    flag_copy.start()

    w0 = pl.multiple_of(jnp.maximum(q0 - WINDOW, 0), Q_TILE)
    sw = _dot_nt(kaw_ref[0, pl.ds(w0, WIN_KEYS), :], qa)
    wb = wbias_ref[jnp.minimum(i, WINDOW // Q_TILE)]
    sw = sw + jnp.concatenate([wb] * HEADS_PER_GROUP, axis=1)
    mw = jnp.max(sw, axis=0, keepdims=True)
    pw = jnp.exp2(sw - mw)
    c0 = w0 // Q_TILE
    vw = jnp.concatenate([vwt_ref[0, 0, c0 + j] for j in range(WIN_KEYS // Q_TILE)], axis=1)
    aw = _dot(vw, pw.astype(BF16))
    o_w = aw[0:HEAD_DIM] * (1.0 / aw[HEAD_DIM:HEAD_DIM + 1])

    def scores(kt):
        sc = _dot_nt(kas_ref[0, pl.ds(pl.multiple_of(kt * KEY_TILE, KEY_TILE), KEY_TILE), :], qa)
        blocks = KEY_TILE // SEL_BLK
        mask = [jnp.broadcast_to(selb_ref[pl.ds(kt * blocks + j, 1), :], (SEL_BLK, gl)) for j in range(blocks)]
        return sc + jnp.concatenate(mask, axis=0)

    n_full = q0 // KEY_TILE
    key = n_full * KEY_TILE + lax.broadcasted_iota(I32, (KEY_TILE, gl), 0)
    qry = q0 + (lax.broadcasted_iota(I32, (KEY_TILE, gl), 1) & (Q_TILE - 1))
    s0_ref[...] = jnp.where(key <= qry, scores(n_full), NEG)
    m_ref[...] = jnp.full(m_ref.shape, -3.0e38, F32)
    acc_ref[...] = jnp.zeros(acc_ref.shape, F32)
    list_ref[0] = n_full

    def absorb(s_ref, kt, live):
        sc = s_ref[...]
        m_old = m_ref[...]
        m_new = jnp.where(live, jnp.maximum(m_old, jnp.max(sc, axis=0, keepdims=True)), m_old)
        pv = _dot(vst_ref[0, 0, kt], jnp.exp2(sc - m_new).astype(BF16))
        acc_ref[...] = jnp.exp2(m_old - m_new) * acc_ref[...] + jnp.where(live, pv, 0.0)
        m_ref[...] = m_new

    flag_copy.wait()

    def compact(kt, n):
        active = flags_ref[kt, 0] > 0

        @pl.when(active)
        def _():
            list_ref[n] = kt

        return n + active.astype(I32)

    n_items = lax.fori_loop(0, n_full, compact, 1)

    last = n_items - 1

    def pair_body(j, carry):
        a = list_ref[2 * j]
        b = list_ref[jnp.minimum(2 * j + 1, last)]
        nxt = list_ref[jnp.minimum(2 * j + 2, last)]
        s1_ref[...] = scores(b)
        absorb(s0_ref, a, True)
        s0_ref[...] = scores(nxt)
        absorb(s1_ref, b, 2 * j + 1 <= last)
        return carry

    lax.fori_loop(0, (n_items + 1) // 2, pair_body, 0)

    o_s = acc_ref[0:HEAD_DIM, :] * (1.0 / acc_ref[HEAD_DIM:HEAD_DIM + 1, :])

    g = g_ref[0, 0]
    outs = []
    for h in range(HEADS_PER_GROUP):
        sl = slice(h * Q_TILE, (h + 1) * Q_TILE)
        outs.append(g[3 * h:3 * h + 1, :] * o_c[:, sl] + g[3 * h + 1:3 * h + 2, :] * o_s[:, sl]
                    + g[3 * h + 2:3 * h + 3, :] * o_w[:, sl])
    o_ref[0] = jnp.concatenate(outs, axis=0).T.astype(BF16)


def _attention(qa, gates_t, kca, vct, kas, vst, kaw, vwt, B, S):
    G, hd = N_KV_HEADS, HEAD_DIM
    NC = kca.shape[2]
    n_blk = S // SEL_BLK
    assert n_blk <= LANES and S % KEY_TILE == 0 and S >= WIN_KEYS
    n_sel = min(SEL_TOPN, n_blk)
    assert n_sel > N_FORCED
    ratio, span = SEL_BLK // CMP_STRIDE, CMP_LEN // CMP_STRIDE
    sm = np.zeros((LANES, NC), np.float32)
    for j in range(n_blk):
        for a in range(ratio):
            for b in range(span):
                n = ratio * j + a - b
                if 0 <= n < NC - 1:
                    sm[j, n] += 1.0
    selmapt = jnp.asarray(sm, BF16)
    c = np.arange(WIN_KEYS)[:, None]
    r = np.arange(Q_TILE)[None, :]
    offs = np.arange(WINDOW // Q_TILE + 1)[:, None, None] * Q_TILE
    wbias = jnp.asarray(np.where((c - r <= offs) & (c - r > offs - WINDOW), 0.0, NEG), F32)
    tilemap = jnp.asarray(np.arange(LANES)[None, :] // (KEY_TILE // SEL_BLK) == np.arange(N_TILE_ROWS)[:, None],
                          BF16)
    assert S // KEY_TILE <= N_TILE_ROWS
    u = np.arange(2 * NC)[:, None] - NC
    cbias = jnp.asarray(np.where(CMP_STRIDE * u + (CMP_LEN - 1) <= np.arange(Q_TILE)[None, :], 0.0, NEG), F32)

    hpg = HEADS_PER_GROUP
    grp = lambda *blk: pl.BlockSpec((1, 1) + blk, lambda b, g, i: (b, g) + (0,) * len(blk))
    seq = lambda w: pl.BlockSpec((1, S, w), lambda b, g, i: (b, 0, g))
    const = lambda a: pl.BlockSpec(a.shape, lambda b, g, i: (0,) * a.ndim)
    return pl.pallas_call(
        functools.partial(_attn_body, n_sel=n_sel),
        grid=(B, G, S // Q_TILE),
        in_specs=[pl.BlockSpec((1, Q_TILE, hpg * LANES), lambda b, g, i: (b, i, g)),
                  pl.BlockSpec((1, 1, 16, Q_TILE), lambda b, g, i: (b, g, 0, i)),
                  grp(NC, LANES), grp(hd, NC), seq(LANES), grp(S // KEY_TILE, V_ROWS, KEY_TILE),
                  seq(LANES), grp(S // Q_TILE, V_ROWS, Q_TILE), const(selmapt), const(wbias), const(tilemap),
                  const(cbias)],
        out_specs=pl.BlockSpec((1, Q_TILE, hpg * hd), lambda b, g, i: (b, i, g)),
        out_shape=jax.ShapeDtypeStruct((B, S, N_HEADS * hd), BF16),
        scratch_shapes=[pltpu.VMEM((LANES, GROUP_LANES), F32), pltpu.VMEM((1, GROUP_LANES), F32),
                        pltpu.VMEM((V_ROWS, GROUP_LANES), F32), pltpu.VMEM((KEY_TILE, GROUP_LANES), F32),
                        pltpu.VMEM((KEY_TILE, GROUP_LANES), F32),
                        pltpu.VMEM((N_TILE_ROWS, LANES), I32), pltpu.SMEM((N_TILE_ROWS, LANES), I32),
                        pltpu.SMEM((N_TILE_ROWS,), I32), pltpu.SemaphoreType.DMA(())],
        compiler_params=_cparams(3),
        name="nsa_attention",
    )(qa, gates_t, kca, vct, kas, vst, kaw, vwt, selmapt, wbias, tilemap, cbias)


def _mixer_out_body(x_ref, oa_ref, cbv_ref, halo_ref, gab_ref, cw_ref, wpa_ref, wpb_ref, wo_ref,
                    g2_ref, whi_ref, wlo_ref, br_ref, tri_ref, o_ref, h_out, mi_out, mf_out, cnt_out,
                    *, seq_len):
    i = pl.program_id(0)
    tm = x_ref.shape[0]
    cwd = CONV_WIDTH
    d = x_ref.shape[1]
    v = cbv_ref[:, cwd:2 * cwd].astype(F32)
    prev = halo_ref[:, cwd:2 * cwd].astype(F32)
    keep = ((i * tm) % seq_len != 0).astype(F32)
    p1 = prev[7:8, :] * keep
    p2 = prev[6:7, :] * keep
    ridx = lax.broadcasted_iota(I32, (tm, cwd), 0)
    v1 = jnp.where(ridx == 0, p1, pltpu.roll(v, 1, 0))
    v2 = jnp.where(ridx == 0, p2, jnp.where(ridx == 1, p1, pltpu.roll(v, 2, 0)))
    y = cw_ref[0:1, :] * v2 + cw_ref[1:2, :] * v1 + cw_ref[2:3, :] * v
    yb_in = (cbv_ref[:, 0:cwd].astype(F32) * y).astype(BF16)
    y_a = _dot(oa_ref[...], wpa_ref[...])
    y_b = _dot(yb_in, wpb_ref[...])
    merged = gab_ref[:, 0:d].astype(F32) * y_a + gab_ref[:, d:2 * d].astype(F32) * y_b
    x1 = x_ref[...] + _dot(merged.astype(BF16), wo_ref[...])
    o_ref[...] = x1
    _route(x1, g2_ref, whi_ref, wlo_ref, br_ref, tri_ref, h_out, mi_out, mf_out, cnt_out)


def _mixer_out(x2, oa, cbv, gab, conv_w, w_pa, w_pb, w_o, seq_len, g2, w_r, b_r):
    T, D = x2.shape
    assert D == ROW_TILES * LANES
    tm = ROW_TILE
    cw8 = jnp.pad(conv_w, ((0, 8 - CONV_K), (0, 0)))
    wpad = jnp.pad(w_r, ((0, 0), (0, LANES - N_EXPERTS)))
    whi = wpad.astype(BF16)
    wlo = (wpad - whi.astype(F32)).astype(BF16)
    br = jnp.pad(b_r, (0, LANES - N_EXPERTS)).reshape(1, LANES)
    tri = jnp.asarray(np.tril(np.ones((tm, tm), np.float32), -1), BF16)
    row = lambda w: pl.BlockSpec((tm, w), lambda i: (i, 0))
    full = lambda a: pl.BlockSpec(a.shape, lambda i: (0,) * a.ndim)
    halo = pl.BlockSpec((8, cbv.shape[1]), lambda i: (jnp.maximum(i * (tm // 8) - 1, 0), 0))
    wts = (cw8, w_pa.astype(BF16), w_pb.astype(BF16), w_o.astype(BF16), g2.reshape(1, D), whi, wlo, br, tri)
    return pl.pallas_call(
        functools.partial(_mixer_out_body, seq_len=seq_len),
        grid=(T // tm,),
        in_specs=[row(D), row(oa.shape[1]), row(cbv.shape[1]), halo, row(gab.shape[1])] + [full(a) for a in wts],
        out_specs=[row(D), pl.BlockSpec((tm * ROW_TILES, LANES), lambda i: (i, 0)), row(2 * TOP_K), row(LANES),
                   pl.BlockSpec((8, LANES), lambda i: (0, 0))],
        out_shape=[jax.ShapeDtypeStruct((T, D), F32), jax.ShapeDtypeStruct((T * ROW_TILES, LANES), F32),
                   jax.ShapeDtypeStruct((T, 2 * TOP_K), I32), jax.ShapeDtypeStruct((T, LANES), F32),
                   jax.ShapeDtypeStruct((8, LANES), F32)],
        compiler_params=_cparams(1),
        name="mixer_out_router",
    )(x2, oa, cbv, cbv, gab, *wts)


def _route(x, g2_ref, whi_ref, wlo_ref, br_ref, tri_ref, h_out, mi_out, mf_out, cnt_out):
    i = pl.program_id(0)

    @pl.when(i == 0)
    def _():
        cnt_out[...] = jnp.zeros(cnt_out.shape, F32)

    tm = x.shape[0]
    ms = jnp.mean(x * x, axis=-1, keepdims=True)
    h = x * lax.rsqrt(ms + EPS) * g2_ref[...]
    _store_row_tiles(h_out, h)
    h_hi = h.astype(BF16)
    h_lo = (h - h_hi.astype(F32)).astype(BF16)
    logits = (_dot(h_hi, whi_ref[...]) + _dot(h_lo, whi_ref[...]) + _dot(h_hi, wlo_ref[...])) + br_ref[...]
    lane = lax.broadcasted_iota(I32, (tm, LANES), 1)
    lanef = lane.astype(F32)
    work = jnp.where(lane < N_EXPERTS, logits, -jnp.inf)
    vals, hits = [], []
    for _ in range(TOP_K):
        mx = jnp.max(work, axis=-1, keepdims=True)
        first = jnp.min(jnp.where(work == mx, lanef, float(LANES)), axis=-1, keepdims=True)
        hit = lanef == first
        vals.append(mx)
        hits.append(hit)
        work = jnp.where(hit, -jnp.inf, work)
    ex = [jnp.exp(v - vals[0]) for v in vals]
    den = ex[0]
    for e in ex[1:]:
        den = den + e
    cnt = jnp.zeros((tm, LANES), F32)
    for hit in hits:
        cnt = cnt + hit.astype(F32)
    before = _dot(tri_ref[...], cnt.astype(BF16)) + cnt_out[0:1, :]
    mi = jnp.zeros((tm, LANES), F32)
    mf = jnp.zeros((tm, LANES), F32)
    for k, hit in enumerate(hits):
        e_k = jnp.sum(jnp.where(hit, lanef, 0.0), axis=-1, keepdims=True)
        r_k = jnp.sum(jnp.where(hit, before, 0.0), axis=-1, keepdims=True)
        mi = jnp.where(lane == k, e_k, jnp.where(lane == TOP_K + k, r_k, mi))
        mf = jnp.where(lane == k, ex[k] / den, mf)
    mi_out[...] = mi[:, 0:2 * TOP_K].astype(I32)
    mf_out[...] = mf
    cnt_out[...] = cnt_out[...] + jnp.sum(cnt, axis=0, keepdims=True)


DISPATCH_TILE = 512
DMA_UNROLL = 8


def _dispatch_body(dest_ref, last_ref, h_ref, o_hbm, zero_ref, sem, zsem):
    @pl.when(pl.program_id(0) == 0)
    def _():
        zero_ref[...] = jnp.zeros(zero_ref.shape, zero_ref.dtype)

        def clear(e):
            start = pl.multiple_of(last_ref[e] * ROW_TILES, MOE_CHUNK * ROW_TILES)
            return pltpu.make_async_copy(zero_ref, o_hbm.at[pl.ds(start, MOE_CHUNK * ROW_TILES)], zsem)

        for e in range(N_EXPERTS):
            @pl.when(last_ref[e] >= 0)
            def _():
                clear(e).start()
        for e in range(N_EXPERTS):
            @pl.when(last_ref[e] >= 0)
            def _():
                clear(e).wait()

    def row_copy(r, d):
        return pltpu.make_async_copy(h_ref.at[pl.ds(pl.multiple_of(r * ROW_TILES, ROW_TILES), ROW_TILES)],
                                     o_hbm.at[pl.ds(pl.multiple_of(d * ROW_TILES, ROW_TILES), ROW_TILES)], sem)

    def start(r, c):
        for k in range(TOP_K):
            row_copy(r, dest_ref[0, 0, r * TOP_K + k]).start(priority=k % 2)
        return c

    def wait(r, c):
        for k in range(TOP_K):
            row_copy(0, 0).wait()
        return c

    lax.fori_loop(0, DISPATCH_TILE, start, 0, unroll=DMA_UNROLL)
    lax.fori_loop(0, DISPATCH_TILE, wait, 0, unroll=DMA_UNROLL)


def _dispatch(h2, dest, last_chunk, n_rows):
    T = h2.shape[0] // ROW_TILES
    td = DISPATCH_TILE
    dest3 = dest.reshape(T // td, 1, td * TOP_K)
    return pl.pallas_call(
        _dispatch_body,
        grid=(T // td,),
        in_specs=[pl.BlockSpec((1, 1, td * TOP_K), lambda i: (i, 0, 0), memory_space=pltpu.SMEM),
                  pl.BlockSpec(memory_space=pltpu.SMEM),
                  pl.BlockSpec((td * ROW_TILES, LANES), lambda i: (i, 0))],
        out_specs=pl.BlockSpec(memory_space=pl.ANY),
        out_shape=jax.ShapeDtypeStruct((n_rows * ROW_TILES, LANES), h2.dtype),
        scratch_shapes=[pltpu.VMEM((MOE_CHUNK * ROW_TILES, LANES), h2.dtype), pltpu.SemaphoreType.DMA(()),
                        pltpu.SemaphoreType.DMA(())],
        compiler_params=_cparams(1),
        name="dispatch",
    )(dest3, last_chunk, h2)


def _expert_body(ce_ref, nu_ref, x_ref, wgu_ref, bgu_ref, wdn_ref, bdn_ref, o_ref, wgu_bf, wdn_bf):
    c = pl.program_id(0)
    dff = wdn_ref.shape[1]

    @pl.when((c == 0) | (ce_ref[c] != ce_ref[jnp.maximum(c - 1, 0)]))
    def _():
        wgu_bf[...] = wgu_ref[0].astype(BF16)
        wdn_bf[...] = wdn_ref[0].astype(BF16)

    @pl.when(c < nu_ref[0])
    def _():
        x = _load_row_tiles(x_ref, MOE_CHUNK)
        gu = _dot(x.astype(BF16), wgu_bf[...]) + bgu_ref[0]
        g = jnp.minimum(gu[:, 0:dff], SWIGLU_LIMIT)
        u = jnp.clip(gu[:, dff:2 * dff], -SWIGLU_LIMIT, SWIGLU_LIMIT)
        act = (u + 1.0) * (g * jax.nn.sigmoid(SWIGLU_ALPHA * g))
        _store_row_tiles(o_ref, _dot(act.astype(BF16), wdn_bf[...]) + bdn_ref[0])

    @pl.when(c >= nu_ref[0])
    def _():
        o_ref[...] = jnp.zeros(o_ref.shape, F32)


def _experts(hperm, chunk_e, n_used, w_gu, b_gu, w_dn, b_dn):
    E, D, F2 = w_gu.shape
    assert D == ROW_TILES * LANES
    P = hperm.shape[0] // ROW_TILES
    dff = F2 // 2
    n_chunks = P // MOE_CHUNK
    chunk = (MOE_CHUNK * ROW_TILES, LANES)
    grid_spec = pltpu.PrefetchScalarGridSpec(
        num_scalar_prefetch=2,
        grid=(n_chunks,),
        in_specs=[pl.BlockSpec(chunk, lambda c, ce, nu: (jnp.minimum(c, nu[0] - 1), 0)),
                  pl.BlockSpec((1, D, F2), lambda c, ce, nu: (ce[c], 0, 0)),
                  pl.BlockSpec((1, 1, F2), lambda c, ce, nu: (ce[c], 0, 0)),
                  pl.BlockSpec((1, dff, D), lambda c, ce, nu: (ce[c], 0, 0)),
                  pl.BlockSpec((1, 1, D), lambda c, ce, nu: (ce[c], 0, 0))],
        out_specs=pl.BlockSpec(chunk, lambda c, ce, nu: (c, 0)),
        scratch_shapes=[pltpu.VMEM((D, F2), BF16), pltpu.VMEM((dff, D), BF16)],
    )
    return pl.pallas_call(
        _expert_body,
        grid_spec=grid_spec,
        out_shape=jax.ShapeDtypeStruct(hperm.shape, F32),
        compiler_params=_cparams(1),
        name="experts",
    )(chunk_e, n_used, hperm, w_gu, b_gu.reshape(E, 1, F2), w_dn, b_dn.reshape(E, 1, D))


COMBINE_TILE = 256


def _combine_body(dest_ref, next_ref, x_ref, w_ref, y_hbm, o_ref, buf_ref, sems):
    i = pl.program_id(0)
    slot = i % 2

    def row_copy(s, r, k, d):
        return pltpu.make_async_copy(y_hbm.at[pl.ds(pl.multiple_of(d * ROW_TILES, ROW_TILES), ROW_TILES)],
                                     buf_ref.at[s, k, pl.ds(pl.multiple_of(r * ROW_TILES, ROW_TILES), ROW_TILES)],
                                     sems.at[s])

    def fetch(idx_ref, s):
        def start(r, c):
            for k in range(TOP_K):
                row_copy(s, r, k, idx_ref[0, 0, r * TOP_K + k]).start(priority=k % 2)
            return c

        lax.fori_loop(0, COMBINE_TILE, start, 0, unroll=DMA_UNROLL)

    @pl.when(i == 0)
    def _():
        fetch(dest_ref, slot)

    def wait(r, c):
        for k in range(TOP_K):
            row_copy(slot, 0, 0, 0).wait()
        return c

    lax.fori_loop(0, COMBINE_TILE, wait, 0, unroll=DMA_UNROLL)

    def sum_rows(g):
        r0 = pl.multiple_of(g * DMA_UNROLL, DMA_UNROLL)
        gate = [jnp.broadcast_to(w_ref[pl.ds(r0, DMA_UNROLL), k:k + 1], (DMA_UNROLL, LANES)) for k in range(TOP_K)]
        for s in range(ROW_TILES):
            sl = slice(s * LANES, (s + 1) * LANES)
            out = x_ref[pl.ds(r0, DMA_UNROLL), sl]
            for k in range(TOP_K):
                out = out + gate[k] * buf_ref[slot, k, pl.ds(r0 * ROW_TILES + s, DMA_UNROLL, stride=ROW_TILES), :]
            o_ref[pl.ds(r0, DMA_UNROLL), sl] = out

    @pl.when(i + 1 < pl.num_programs(0))
    def _():
        def group(g, c):
            sum_rows(g)
            for t in range(DMA_UNROLL):
                r = g * DMA_UNROLL + t
                for k in range(TOP_K):
                    row_copy(1 - slot, r, k, next_ref[0, 0, r * TOP_K + k]).start(priority=k % 2)
            return c

        lax.fori_loop(0, COMBINE_TILE // DMA_UNROLL, group, 0)

    @pl.when(i + 1 == pl.num_programs(0))
    def _():
        def group(g, c):
            sum_rows(g)
            return c

        lax.fori_loop(0, COMBINE_TILE // DMA_UNROLL, group, 0)


def _combine(x1, gate_w, dest, ys):
    T, D = x1.shape
    tc = COMBINE_TILE
    n = T // tc
    dest3 = dest.reshape(n, 1, tc * TOP_K)
    row = lambda w: pl.BlockSpec((tc, w), lambda i: (i, 0))
    idx = lambda f: pl.BlockSpec((1, 1, tc * TOP_K), f, memory_space=pltpu.SMEM)
    return pl.pallas_call(
        _combine_body,
        grid=(n,),
        in_specs=[idx(lambda i: (i, 0, 0)), idx(lambda i: (jnp.minimum(i + 1, n - 1), 0, 0)),
                  row(D), row(LANES), pl.BlockSpec(memory_space=pl.ANY)],
        out_specs=row(D),
        out_shape=jax.ShapeDtypeStruct((T, D), F32),
        scratch_shapes=[pltpu.VMEM((2, TOP_K, tc * ROW_TILES, LANES), F32), pltpu.SemaphoreType.DMA((2,))],
        compiler_params=_cparams(1),
        name="combine",
    )(dest3, dest3, x1, gate_w, ys)


def _mixer(x2, B, S, g_norm1, w_in, g_q, g_kc, g_ks, g_kw, pe_k, ck_w1, ck_b1, ck_w2, ck_b2,
           pe_v, cv_w1, cv_b1, cv_w2, cv_b2, conv_w, w_pa, w_pb, w_o, g_norm2, w_r, b_r):
    T, D = x2.shape
    G, H, hd = N_KV_HEADS, N_HEADS, HEAD_DIM
    qa, hk, hv, kas, kaw, vst, vwt, gates, cbv, gab = _inproj(x2, g_norm1, w_in, g_q, g_ks, g_kw, S)
    nh = S // CMP_STRIDE
    kca = _compress(hk.reshape(B, nh, -1), pe_k, ck_w1, ck_b1, ck_w2, ck_b2, g_kc, True)
    vct = _compress(hv.reshape(B, nh, -1), pe_v, cv_w1, cv_b1, cv_w2, cv_b2, jnp.ones((hd,), F32), False)
    gat = gates[:, :3 * H].reshape(B, S, G, 3 * HEADS_PER_GROUP).transpose(0, 2, 3, 1)
    gat = jnp.pad(gat, ((0, 0), (0, 0), (0, 16 - 3 * HEADS_PER_GROUP), (0, 0)))
    o = _attention(qa.reshape(B, S, -1), gat, kca, vct, kas.reshape(B, S, -1), vst, kaw.reshape(B, S, -1),
                   vwt, B, S)
    return _mixer_out(x2, o.reshape(T, H * hd), cbv, gab, conv_w, w_pa, w_pb, w_o, S, g_norm2, w_r, b_r)


def _moe(x1, h2, mi, mf, cnt, w_gu, b_gu, w_dn, b_dn):
    T, D = x1.shape
    top_e = mi[:, 0:TOP_K]
    rank = mi[:, TOP_K:2 * TOP_K]
    counts = cnt[0, :N_EXPERTS].astype(I32)
    padded = (counts + MOE_CHUNK - 1) // MOE_CHUNK * MOE_CHUNK
    pend = jnp.cumsum(padded)
    poffs = pend - padded
    dest = (poffs[top_e] + rank).reshape(-1)
    n_chunks = (T * TOP_K + MOE_CHUNK - 1) // MOE_CHUNK + N_EXPERTS
    chunk_start = jnp.arange(n_chunks, dtype=I32) * MOE_CHUNK
    chunk_e = jnp.minimum(jnp.sum((pend[None, :] <= chunk_start[:, None]).astype(I32), axis=1), N_EXPERTS - 1)
    n_used = (pend[-1:] // MOE_CHUNK).astype(I32)
    last_chunk = jnp.where(padded > 0, pend - MOE_CHUNK, -1).astype(I32)
    hperm = _dispatch(h2, dest, last_chunk, n_chunks * MOE_CHUNK)
    ys = _experts(hperm, chunk_e, n_used, w_gu, b_gu, w_dn, b_dn)
    return _combine(x1, mf, dest, ys)


def kernel(x, g_norm1, w_in, g_q, g_kc, g_ks, g_kw, pe_k, ck_w1, ck_b1, ck_w2, ck_b2, pe_v, cv_w1, cv_b1,
           cv_w2, cv_b2, conv_w, w_pa, w_pb, w_o, g_norm2, w_r, b_r, w_gu, b_gu, w_dn, b_dn):
    B, S, D = x.shape
    x2 = x.reshape(B * S, D)
    for l in range(g_norm1.shape[0]):
        routed = _mixer(x2, B, S, g_norm1[l], w_in[l], g_q[l], g_kc[l], g_ks[l], g_kw[l], pe_k[l], ck_w1[l],
                        ck_b1[l], ck_w2[l], ck_b2[l], pe_v[l], cv_w1[l], cv_b1[l], cv_w2[l], cv_b2[l],
                        conv_w[l], w_pa[l], w_pb[l], w_o[l], g_norm2[l], w_r[l], b_r[l])
        x2 = _moe(*routed, w_gu[l], b_gu[l], w_dn[l], b_dn[l])
    return x2.reshape(B, S, D)
```

```python
import functools

import numpy as np
import jax
import jax.numpy as jnp
from jax import lax
from jax.experimental import pallas as pl
from jax.experimental.pallas import tpu as pltpu

F32 = jnp.float32
BF16 = jnp.bfloat16
I32 = jnp.int32

N_HEADS = 8
HEAD_DIM = 64
N_KV_HEADS = 2
HEADS_PER_GROUP = N_HEADS // N_KV_HEADS
CMP_LEN = 32
CMP_STRIDE = 16
CMP_HID = 256
SEL_BLK = 64
SEL_TOPN = 16
WINDOW = 512
CONV_WIDTH = 512
CONV_K = 3
N_EXPERTS = 32
TOP_K = 4
SWIGLU_LIMIT = 7.0
SWIGLU_ALPHA = 1.702
MOE_CHUNK = 512
EPS = 1e-6
NEG = -1e30
N_FORCED = 3

LANES = 128
Q_TILE = 256
KEY_TILE = 512
LOG2E = float(np.log2(np.e))
N_SPLIT = 3
N_AUG = 4 * N_SPLIT + 1
ROW_TILE = 512
VMEM_LIMIT = 56 * 1024 * 1024
GROUP_LANES = HEADS_PER_GROUP * Q_TILE
WIN_KEYS = WINDOW + Q_TILE
N_TILE_ROWS = 16
V_ROWS = HEAD_DIM + 8


def _cparams(n_axes):
    return pltpu.CompilerParams(dimension_semantics=("arbitrary",) * n_axes,
                                vmem_limit_bytes=VMEM_LIMIT)


def _dot(a, b):
    return jnp.dot(a, b, preferred_element_type=F32)


def _dot_nt(a, b):
    return lax.dot_general(a, b, (((1,), (1,)), ((), ())), preferred_element_type=F32)


ROW_TILES = 8


def _store_row_tiles(ref, val):
    n = val.shape[0]
    for s in range(ROW_TILES):
        ref[pl.ds(s, n, stride=ROW_TILES), :] = val[:, s * LANES:(s + 1) * LANES]


def _load_row_tiles(ref, n):
    return jnp.concatenate([ref[pl.ds(s, n, stride=ROW_TILES), :] for s in range(ROW_TILES)], axis=1)


def _rms_pairs(v, bd):
    ss = _dot((v * v).astype(BF16), bd)
    return v * lax.rsqrt(ss + EPS)


def _inproj_body(x_ref, g1_ref, wq_ref, wkv_ref, wng_ref, wcv_ref, wmg_ref, gq_ref, gk_ref, bd_ref,
                 qtab_ref, kwtab_ref,
                 qa_out, hk_out, hv_out, kas_out, kaw_out, vst_out, vwt_out, gate_out, cbv_out, gab_out, raw_ref):
    x = x_ref[...]
    tm = x.shape[0]
    ms = jnp.mean(x * x, axis=-1, keepdims=True)
    h = (x * lax.rsqrt(ms + EPS) * g1_ref[...]).astype(BF16)
    bd = bd_ref[...]
    low = lax.broadcasted_iota(I32, (tm, LANES), 1) < HEAD_DIM

    def place(pair, tab_ref, out_ref, base, slot):
        for j, src in enumerate((pair, pltpu.roll(pair, HEAD_DIM, 1))):
            o = base + j * slot
            out_ref[:, o:o + LANES] = jnp.where(low, src, tab_ref[:, o:o + LANES].astype(F32)).astype(BF16)

    q = _dot(h, wq_ref[...])
    for c in range(N_HEADS * HEAD_DIM // LANES):
        sl = slice(c * LANES, (c + 1) * LANES)
        place(_rms_pairs(q[:, sl], bd) * gq_ref[:, sl], qtab_ref, qa_out, 2 * c * LANES, LANES)
    kv = _dot(h, wkv_ref[...])

    def emit_half_blocks(c, out_ref):
        nb = tm // CMP_STRIDE
        hw = CMP_STRIDE * HEAD_DIM
        raw_ref[...] = kv[:, c * LANES:(c + 1) * LANES]
        lo = lax.broadcasted_iota(I32, (nb, LANES), 1) < HEAD_DIM
        for u in range(CMP_STRIDE // 2):
            t0 = raw_ref[pl.ds(2 * u, nb, stride=CMP_STRIDE), :]
            t1 = raw_ref[pl.ds(2 * u + 1, nb, stride=CMP_STRIDE), :]
            out_ref[:, u * LANES:(u + 1) * LANES] = jnp.where(lo, t0, pltpu.roll(t1, HEAD_DIM, 1)).astype(BF16)
            out_ref[:, hw + u * LANES:hw + (u + 1) * LANES] = (
                jnp.where(lo, pltpu.roll(t0, HEAD_DIM, 1), t1).astype(BF16))

    emit_half_blocks(0, hk_out)
    emit_half_blocks(1, hv_out)
    place(_rms_pairs(kv[:, 256:384], bd) * gk_ref[:, 0:128], kwtab_ref, kas_out, 0, LANES)
    place(_rms_pairs(kv[:, 512:640], bd) * gk_ref[:, 128:256], kwtab_ref, kaw_out, 0, LANES)
    tail = jnp.where(lax.broadcasted_iota(I32, (V_ROWS - HEAD_DIM, tm), 0) == 0, 1.0, 0.0)

    def emit_values(c, out_ref, tile):
        vt = kv[:, c * LANES:(c + 1) * LANES].T
        for g in range(N_KV_HEADS):
            full = jnp.concatenate([vt[g * HEAD_DIM:(g + 1) * HEAD_DIM], tail], axis=0).astype(BF16)
            for j in range(tm // tile):
                out_ref[0, g, j] = full[:, j * tile:(j + 1) * tile]

    emit_values(3, vst_out, KEY_TILE)
    emit_values(5, vwt_out, Q_TILE)
    gate_out[...] = jax.nn.sigmoid(_dot(h, wng_ref[...]))
    cv = _dot(h, wcv_ref[...])
    cw = CONV_WIDTH
    cbv_out[:, 0:cw] = cv[:, 0:cw].astype(BF16)
    cbv_out[:, cw:2 * cw] = (cv[:, cw:2 * cw] * cv[:, 2 * cw:3 * cw]).astype(BF16)
    gab_out[...] = jax.nn.sigmoid(_dot(h, wmg_ref[...])).astype(BF16)


def _bf16_pieces(v):
    def round_bf16(a):
        u = np.ascontiguousarray(a, np.float32).view(np.uint32).astype(np.uint64)
        return ((u + 0x7FFF + ((u >> 16) & 1)) & 0xFFFF0000).astype(np.uint32).view(np.float32)

    pieces, rest = [], np.asarray(v, np.float32)
    for _ in range(N_SPLIT):
        pieces.append(round_bf16(rest))
        rest = rest - pieces[-1]
    return pieces


def _key_aug(pos):
    one = np.ones_like(pos, np.float32)
    hi = (pos // 64 * 64).astype(np.float32)
    lo = (pos % 64).astype(np.float32)
    return np.stack([one] * (2 * N_SPLIT) + [hi] * N_SPLIT + [lo] * N_SPLIT + [0 * one], axis=-1)


def _slot_table(aug, slot):
    S, n, _ = aug.shape
    tab = np.zeros((S, n, slot), np.float32)
    tab[:, :, HEAD_DIM:HEAD_DIM + N_AUG] = aug
    return jnp.asarray(tab.reshape(S, n * slot), BF16)


def _inproj(x2, g1, w_in, g_q, g_ks, g_kw, S):
    T, D = x2.shape
    H, G = N_HEADS, N_KV_HEADS
    aw = H * HEAD_DIM
    kvw = G * HEAD_DIM
    o = 0
    wq = w_in[:, o:o + aw]; o += aw
    wkv = w_in[:, o:o + 6 * kvw]; o += 6 * kvw
    wng = w_in[:, o:o + 3 * H]; o += 3 * H
    wcv = w_in[:, o:o + 3 * CONV_WIDTH]; o += 3 * CONV_WIDTH
    wmg = w_in[:, o:o + 2 * D]
    wng = jnp.pad(wng, ((0, 0), (0, LANES - 3 * H)))
    wq, wkv, wng, wcv, wmg = (w.astype(BF16) for w in (wq, wkv, wng, wcv, wmg))
    gq = (jnp.tile(g_q, H) * (HEAD_DIM ** -0.5 * LOG2E)).reshape(1, aw)
    gk = jnp.concatenate([jnp.tile(g_ks, G), jnp.tile(g_kw, G)]).reshape(1, 2 * kvw)
    idx = np.arange(LANES) // HEAD_DIM
    bd = jnp.asarray((idx[:, None] == idx[None, :]).astype(np.float32) / HEAD_DIM, BF16)
    pos = np.arange(S)
    hi = (pos // 64 * 64).astype(np.float64)[:, None]
    lo = (pos % 64).astype(np.float64)[:, None]
    c = LOG2E * 2.0 ** (-8.0 * np.arange(1, H + 1) / H)[None, :]
    cs = np.broadcast_to(c, (S, H))
    aq = np.stack(_bf16_pieces(-c * hi) + _bf16_pieces(-c * lo) + _bf16_pieces(cs) + _bf16_pieces(cs)
                  + [np.ones((S, H), np.float32)], axis=-1)
    ak = np.broadcast_to(_key_aug(pos)[:, None, :], (S, G, N_AUG))
    qtab = _slot_table(aq, LANES)
    kwtab = _slot_table(ak, LANES)
    tm = ROW_TILE
    nst = S // tm
    row = lambda w: pl.BlockSpec((tm, w), lambda i: (i, 0))
    full = lambda a: pl.BlockSpec(a.shape, lambda i: (0,) * a.ndim)
    tab = lambda a: pl.BlockSpec((tm, a.shape[1]), lambda i: (i % nst, 0))
    ins = (x2, g1.reshape(1, D), wq, wkv, wng, wcv, wmg, gq, gk, bd, qtab, kwtab)
    widths = (H * LANES, G * LANES, G * LANES, LANES, 2 * CONV_WIDTH, 2 * D)
    dtypes = (BF16, BF16, BF16, F32, BF16, BF16)
    hw = G * CMP_STRIDE * HEAD_DIM
    nb = tm // CMP_STRIDE
    half = pl.BlockSpec((nb, hw), lambda i: (i, 0))
    half_shape = jax.ShapeDtypeStruct((T // CMP_STRIDE, hw), BF16)
    assert tm % KEY_TILE == 0 and tm % Q_TILE == 0

    def values(tile):
        return (pl.BlockSpec((1, G, tm // tile, V_ROWS, tile), lambda i: (i // nst, 0, i % nst, 0, 0)),
                jax.ShapeDtypeStruct((T // S, G, S // tile, V_ROWS, tile), BF16))

    rows = [(row(w), jax.ShapeDtypeStruct((T, w), dt)) for w, dt in zip(widths, dtypes)]
    outs = [rows[0], (half, half_shape), (half, half_shape), rows[1], rows[2], values(KEY_TILE),
            values(Q_TILE)] + rows[3:]
    return pl.pallas_call(
        _inproj_body,
        grid=(T // tm,),
        in_specs=[row(D)] + [full(a) for a in ins[1:10]] + [tab(a) for a in ins[10:]],
        out_specs=[o[0] for o in outs],
        out_shape=[o[1] for o in outs],
        scratch_shapes=[pltpu.VMEM((tm, LANES), F32)],
        compiler_params=_cparams(1),
        name="inproj",
    )(*ins)


def _compress_body(h_ref, w1_ref, pe_ref, b1_ref, w2_ref, b2_ref, g_ref, tab_ref, o_ref, *, for_keys):
    hb = h_ref[0]
    nc = hb.shape[0]
    a = _dot(hb, w1_ref[0])
    b = _dot(hb, w1_ref[1])
    c = _dot(pe_ref[0], w1_ref[0]) + _dot(pe_ref[1], w1_ref[1])
    pre = a + pltpu.roll(b, nc - 1, 0) + c[0:1, :] + b1_ref[...]
    hid = jax.nn.gelu(pre)
    out = _dot(hid.astype(BF16), w2_ref[...]) + b2_ref[...]
    if for_keys:
        ms = jnp.sum(out * out, axis=-1, keepdims=True) * (1.0 / HEAD_DIM)
        out = out * lax.rsqrt(ms + EPS) * g_ref[...]
        low = lax.broadcasted_iota(I32, out.shape, 1) < HEAD_DIM
        o_ref[0, 0] = jnp.where(low, out, tab_ref[...]).astype(BF16)
    else:
        o_ref[0, 0] = out.T[0:HEAD_DIM, :].astype(BF16)


def _compress(hh, pe, w1, b1, w2, b2, gain, for_keys):
    B, NC, _ = hh.shape
    G, HW = N_KV_HEADS, CMP_STRIDE * HEAD_DIM
    w1s = w1.reshape(2, HW, CMP_HID).astype(BF16)
    pes = jnp.broadcast_to(pe.reshape(2, 1, HW), (2, 8, HW)).astype(BF16)
    padl = lambda a: jnp.pad(a, ((0, 0), (0, LANES - HEAD_DIM)))
    tabn = np.zeros((NC, LANES), np.float32)
    tabn[:, HEAD_DIM:HEAD_DIM + N_AUG] = _key_aug(np.arange(NC) * CMP_STRIDE + (CMP_LEN - 1))
    tab = jnp.asarray(tabn)
    full = lambda a: pl.BlockSpec(a.shape, lambda b, g: (0,) * a.ndim)
    ins = (hh, w1s, pes, b1.reshape(1, CMP_HID), padl(w2).astype(BF16), padl(b2.reshape(1, HEAD_DIM)),
           padl(gain.reshape(1, HEAD_DIM)), tab)
    oshape = (B, G, NC, LANES) if for_keys else (B, G, HEAD_DIM, NC)
    return pl.pallas_call(
        functools.partial(_compress_body, for_keys=for_keys),
        grid=(B, G),
        in_specs=[pl.BlockSpec((1, NC, HW), lambda b, g: (b, 0, g))] + [full(a) for a in ins[1:]],
        out_specs=pl.BlockSpec((1, 1) + oshape[2:], lambda b, g: (b, g, 0, 0)),
        out_shape=jax.ShapeDtypeStruct(oshape, BF16),
        compiler_params=_cparams(2),
        name="compress_keys" if for_keys else "compress_values",
    )(*ins)


def _attn_body(qa_ref, g_ref, kca_ref, vct_ref, kas_ref, vst_ref, kaw_ref, vwt_ref, selmapt_ref, wbias_ref,
               tilemap_ref, cbias_ref, o_ref, selb_ref, m_ref, acc_ref, s0_ref, s1_ref, flagv_ref,
               flags_ref, list_ref, sem,
               *, n_sel):
    i = pl.program_id(2)
    q0 = i * Q_TILE
    gl = GROUP_LANES
    qa = jnp.concatenate([qa_ref[0, :, h * LANES:(h + 1) * LANES] for h in range(HEADS_PER_GROUP)], axis=0)

    nc = kca_ref.shape[2]
    s = _dot_nt(kca_ref[0, 0], qa)
    cb = cbias_ref[pl.ds(pl.multiple_of(nc - i * (Q_TILE // CMP_STRIDE), 8), nc), :]
    s = s + jnp.concatenate([cb] * HEADS_PER_GROUP, axis=1)
    m = jnp.max(s, axis=0, keepdims=True)
    p = jnp.exp2(s - m)
    l = jnp.sum(p, axis=0, keepdims=True)
    has_entry = (q0 + (lax.broadcasted_iota(I32, (1, gl), 1) & (Q_TILE - 1))) >= CMP_LEN - 1
    pc = p * jnp.where(has_entry, 1.0 / l, 0.0)
    o_c = _dot(vct_ref[0, 0], pc.astype(BF16))

    ps = pc[:, 0:Q_TILE]
    for h in range(1, HEADS_PER_GROUP):
        ps = ps + pc[:, h * Q_TILE:(h + 1) * Q_TILE]
    ps_hi = ps.astype(BF16)
    ps_lo = (ps - ps_hi.astype(F32)).astype(BF16)
    imp = _dot(selmapt_ref[...], ps_hi) + _dot(selmapt_ref[...], ps_lo)
    jb = lax.broadcasted_iota(I32, (LANES, Q_TILE), 0)
    cur = (q0 + lax.broadcasted_iota(I32, (LANES, Q_TILE), 1)) // SEL_BLK
    forced = (jb == 0) | (jb == cur) | (jb == cur - 1)
    score = jnp.where(forced, -jnp.inf, jnp.where(jb > cur, NEG, imp))
    jbf = jb.astype(F32)
    for _ in range(n_sel - N_FORCED):
        mx = jnp.max(score, axis=0, keepdims=True)
        first = jnp.min(jnp.where(score == mx, jbf, float(LANES)), axis=0, keepdims=True)
        score = jnp.where(jbf == first, -jnp.inf, score)
    picked = score == -jnp.inf
    bias_t = jnp.where(picked, 0.0, NEG)
    selb_ref[...] = jnp.concatenate([bias_t] * HEADS_PER_GROUP, axis=1)
    tile_hits = jnp.max(_dot(tilemap_ref[...], jnp.where(picked, 1.0, 0.0).astype(BF16)), axis=1, keepdims=True)
    flagv_ref[...] = jnp.broadcast_to(tile_hits, flagv_ref.shape).astype(I32)
    flag_copy = pltpu.make_async_copy(flagv_ref, flags_ref, sem)
    flag_copy.start()

    w0 = pl.multiple_of(jnp.maximum(q0 - WINDOW, 0), Q_TILE)
    sw = _dot_nt(kaw_ref[0, pl.ds(w0, WIN_KEYS), :], qa)
    wb = wbias_ref[jnp.minimum(i, WINDOW // Q_TILE)]
    sw = sw + jnp.concatenate([wb] * HEADS_PER_GROUP, axis=1)
    mw = jnp.max(sw, axis=0, keepdims=True)
    pw = jnp.exp2(sw - mw)
    c0 = w0 // Q_TILE
    vw = jnp.concatenate([vwt_ref[0, 0, c0 + j] for j in range(WIN_KEYS // Q_TILE)], axis=1)
    aw = _dot(vw, pw.astype(BF16))
    o_w = aw[0:HEAD_DIM] * (1.0 / aw[HEAD_DIM:HEAD_DIM + 1])

    def scores(kt):
        sc = _dot_nt(kas_ref[0, pl.ds(pl.multiple_of(kt * KEY_TILE, KEY_TILE), KEY_TILE), :], qa)
        blocks = KEY_TILE // SEL_BLK
        mask = [jnp.broadcast_to(selb_ref[pl.ds(kt * blocks + j, 1), :], (SEL_BLK, gl)) for j in range(blocks)]
        return sc + jnp.concatenate(mask, axis=0)

    n_full = q0 // KEY_TILE
    key = n_full * KEY_TILE + lax.broadcasted_iota(I32, (KEY_TILE, gl), 0)
    qry = q0 + (lax.broadcasted_iota(I32, (KEY_TILE, gl), 1) & (Q_TILE - 1))
    s0_ref[...] = jnp.where(key <= qry, scores(n_full), NEG)
    m_ref[...] = jnp.full(m_ref.shape, -3.0e38, F32)
    acc_ref[...] = jnp.zeros(acc_ref.shape, F32)
    list_ref[0] = n_full

    def absorb(s_ref, kt):
        sc = s_ref[...]
        m_old = m_ref[...]
        m_new = jnp.maximum(m_old, jnp.max(sc, axis=0, keepdims=True))
        pv = _dot(vst_ref[0, 0, kt], jnp.exp2(sc - m_new).astype(BF16))
        acc_ref[...] = jnp.exp2(m_old - m_new) * acc_ref[...] + pv
        m_ref[...] = m_new

    flag_copy.wait()

    def compact(kt, n):
        active = flags_ref[kt, 0] > 0

        @pl.when(active)
        def _():
            list_ref[n] = kt

        return n + active.astype(I32)

    n_items = lax.fori_loop(0, n_full, compact, 1)

    last = n_items - 1

    def pair_body(j, carry):
        b = list_ref[2 * j + 1]
        s1_ref[...] = scores(b)
        absorb(s0_ref, list_ref[2 * j])
        s0_ref[...] = scores(list_ref[jnp.minimum(2 * j + 2, last)])
        absorb(s1_ref, b)
        return carry

    lax.fori_loop(0, n_items // 2, pair_body, 0)

    @pl.when(n_items % 2 == 1)
    def _():
        absorb(s0_ref, list_ref[last])

    o_s = acc_ref[0:HEAD_DIM, :] * (1.0 / acc_ref[HEAD_DIM:HEAD_DIM + 1, :])

    g = g_ref[0, 0]
    outs = []
    for h in range(HEADS_PER_GROUP):
        sl = slice(h * Q_TILE, (h + 1) * Q_TILE)
        outs.append(g[3 * h:3 * h + 1, :] * o_c[:, sl] + g[3 * h + 1:3 * h + 2, :] * o_s[:, sl]
                    + g[3 * h + 2:3 * h + 3, :] * o_w[:, sl])
    o_ref[0] = jnp.concatenate(outs, axis=0).T.astype(BF16)


def _attention(qa, gates_t, kca, vct, kas, vst, kaw, vwt, B, S):
    G, hd = N_KV_HEADS, HEAD_DIM
    NC = kca.shape[2]
    n_blk = S // SEL_BLK
    assert n_blk <= LANES and S % KEY_TILE == 0 and S >= WIN_KEYS
    n_sel = min(SEL_TOPN, n_blk)
    assert n_sel > N_FORCED
    ratio, span = SEL_BLK // CMP_STRIDE, CMP_LEN // CMP_STRIDE
    sm = np.zeros((LANES, NC), np.float32)
    for j in range(n_blk):
        for a in range(ratio):
            for b in range(span):
                n = ratio * j + a - b
                if 0 <= n < NC - 1:
                    sm[j, n] += 1.0
    selmapt = jnp.asarray(sm, BF16)
    c = np.arange(WIN_KEYS)[:, None]
    r = np.arange(Q_TILE)[None, :]
    offs = np.arange(WINDOW // Q_TILE + 1)[:, None, None] * Q_TILE
    wbias = jnp.asarray(np.where((c - r <= offs) & (c - r > offs - WINDOW), 0.0, NEG), F32)
    tilemap = jnp.asarray(np.arange(LANES)[None, :] // (KEY_TILE // SEL_BLK) == np.arange(N_TILE_ROWS)[:, None],
                          BF16)
    assert S // KEY_TILE <= N_TILE_ROWS
    u = np.arange(2 * NC)[:, None] - NC
    cbias = jnp.asarray(np.where(CMP_STRIDE * u + (CMP_LEN - 1) <= np.arange(Q_TILE)[None, :], 0.0, NEG), F32)

    hpg = HEADS_PER_GROUP
    grp = lambda *blk: pl.BlockSpec((1, 1) + blk, lambda b, g, i: (b, g) + (0,) * len(blk))
    seq = lambda w: pl.BlockSpec((1, S, w), lambda b, g, i: (b, 0, g))
    const = lambda a: pl.BlockSpec(a.shape, lambda b, g, i: (0,) * a.ndim)
    return pl.pallas_call(
        functools.partial(_attn_body, n_sel=n_sel),
        grid=(B, G, S // Q_TILE),
        in_specs=[pl.BlockSpec((1, Q_TILE, hpg * LANES), lambda b, g, i: (b, i, g)),
                  pl.BlockSpec((1, 1, 16, Q_TILE), lambda b, g, i: (b, g, 0, i)),
                  grp(NC, LANES), grp(hd, NC), seq(LANES), grp(S // KEY_TILE, V_ROWS, KEY_TILE),
                  seq(LANES), grp(S // Q_TILE, V_ROWS, Q_TILE), const(selmapt), const(wbias), const(tilemap),
                  const(cbias)],
        out_specs=pl.BlockSpec((1, Q_TILE, hpg * hd), lambda b, g, i: (b, i, g)),
        out_shape=jax.ShapeDtypeStruct((B, S, N_HEADS * hd), BF16),
        scratch_shapes=[pltpu.VMEM((LANES, GROUP_LANES), F32), pltpu.VMEM((1, GROUP_LANES), F32),
                        pltpu.VMEM((V_ROWS, GROUP_LANES), F32), pltpu.VMEM((KEY_TILE, GROUP_LANES), F32),
                        pltpu.VMEM((KEY_TILE, GROUP_LANES), F32),
                        pltpu.VMEM((N_TILE_ROWS, LANES), I32), pltpu.SMEM((N_TILE_ROWS, LANES), I32),
                        pltpu.SMEM((N_TILE_ROWS,), I32), pltpu.SemaphoreType.DMA(())],
        compiler_params=_cparams(3),
        name="nsa_attention",
    )(qa, gates_t, kca, vct, kas, vst, kaw, vwt, selmapt, wbias, tilemap, cbias)


def _mixer_out_body(x_ref, oa_ref, cbv_ref, halo_ref, gab_ref, cw_ref, wpa_ref, wpb_ref, wo_ref,
                    g2_ref, whi_ref, wlo_ref, br_ref, tri_ref, o_ref, h_out, mi_out, mf_out, cnt_out,
                    *, seq_len):
    i = pl.program_id(0)
    tm = x_ref.shape[0]
    cwd = CONV_WIDTH
    d = x_ref.shape[1]
    v = cbv_ref[:, cwd:2 * cwd].astype(F32)
    prev = halo_ref[:, cwd:2 * cwd].astype(F32)
    keep = ((i * tm) % seq_len != 0).astype(F32)
    p1 = prev[7:8, :] * keep
    p2 = prev[6:7, :] * keep
    ridx = lax.broadcasted_iota(I32, (tm, cwd), 0)
    v1 = jnp.where(ridx == 0, p1, pltpu.roll(v, 1, 0))
    v2 = jnp.where(ridx == 0, p2, jnp.where(ridx == 1, p1, pltpu.roll(v, 2, 0)))
    y = cw_ref[0:1, :] * v2 + cw_ref[1:2, :] * v1 + cw_ref[2:3, :] * v
    yb_in = (cbv_ref[:, 0:cwd].astype(F32) * y).astype(BF16)
    y_a = _dot(oa_ref[...], wpa_ref[...])
    y_b = _dot(yb_in, wpb_ref[...])
    merged = gab_ref[:, 0:d].astype(F32) * y_a + gab_ref[:, d:2 * d].astype(F32) * y_b
    x1 = x_ref[...] + _dot(merged.astype(BF16), wo_ref[...])
    o_ref[...] = x1
    _route(x1, g2_ref, whi_ref, wlo_ref, br_ref, tri_ref, h_out, mi_out, mf_out, cnt_out)


def _mixer_out(x2, oa, cbv, gab, conv_w, w_pa, w_pb, w_o, seq_len, g2, w_r, b_r):
    T, D = x2.shape
    assert D == ROW_TILES * LANES
    tm = ROW_TILE
    cw8 = jnp.pad(conv_w, ((0, 8 - CONV_K), (0, 0)))
    wpad = jnp.pad(w_r, ((0, 0), (0, LANES - N_EXPERTS)))
    whi = wpad.astype(BF16)
    wlo = (wpad - whi.astype(F32)).astype(BF16)
    br = jnp.pad(b_r, (0, LANES - N_EXPERTS)).reshape(1, LANES)
    tri = jnp.asarray(np.tril(np.ones((tm, tm), np.float32), -1), BF16)
    row = lambda w: pl.BlockSpec((tm, w), lambda i: (i, 0))
    full = lambda a: pl.BlockSpec(a.shape, lambda i: (0,) * a.ndim)
    halo = pl.BlockSpec((8, cbv.shape[1]), lambda i: (jnp.maximum(i * (tm // 8) - 1, 0), 0))
    wts = (cw8, w_pa.astype(BF16), w_pb.astype(BF16), w_o.astype(BF16), g2.reshape(1, D), whi, wlo, br, tri)
    return pl.pallas_call(
        functools.partial(_mixer_out_body, seq_len=seq_len),
        grid=(T // tm,),
        in_specs=[row(D), row(oa.shape[1]), row(cbv.shape[1]), halo, row(gab.shape[1])] + [full(a) for a in wts],
        out_specs=[row(D), pl.BlockSpec((tm * ROW_TILES, LANES), lambda i: (i, 0)), row(2 * TOP_K), row(LANES),
                   pl.BlockSpec((8, LANES), lambda i: (0, 0))],
        out_shape=[jax.ShapeDtypeStruct((T, D), F32), jax.ShapeDtypeStruct((T * ROW_TILES, LANES), F32),
                   jax.ShapeDtypeStruct((T, 2 * TOP_K), I32), jax.ShapeDtypeStruct((T, LANES), F32),
                   jax.ShapeDtypeStruct((8, LANES), F32)],
        compiler_params=_cparams(1),
        name="mixer_out_router",
    )(x2, oa, cbv, cbv, gab, *wts)


def _route(x, g2_ref, whi_ref, wlo_ref, br_ref, tri_ref, h_out, mi_out, mf_out, cnt_out):
    i = pl.program_id(0)

    @pl.when(i == 0)
    def _():
        cnt_out[...] = jnp.zeros(cnt_out.shape, F32)

    tm = x.shape[0]
    ms = jnp.mean(x * x, axis=-1, keepdims=True)
    h = x * lax.rsqrt(ms + EPS) * g2_ref[...]
    _store_row_tiles(h_out, h)
    h_hi = h.astype(BF16)
    h_lo = (h - h_hi.astype(F32)).astype(BF16)
    logits = (_dot(h_hi, whi_ref[...]) + _dot(h_lo, whi_ref[...]) + _dot(h_hi, wlo_ref[...])) + br_ref[...]
    lane = lax.broadcasted_iota(I32, (tm, LANES), 1)
    lanef = lane.astype(F32)
    work = jnp.where(lane < N_EXPERTS, logits, -jnp.inf)
    vals, hits = [], []
    for _ in range(TOP_K):
        mx = jnp.max(work, axis=-1, keepdims=True)
        first = jnp.min(jnp.where(work == mx, lanef, float(LANES)), axis=-1, keepdims=True)
        hit = lanef == first
        vals.append(mx)
        hits.append(hit)
        work = jnp.where(hit, -jnp.inf, work)
    ex = [jnp.exp(v - vals[0]) for v in vals]
    den = ex[0]
    for e in ex[1:]:
        den = den + e
    cnt = jnp.zeros((tm, LANES), F32)
    for hit in hits:
        cnt = cnt + hit.astype(F32)
    before = _dot(tri_ref[...], cnt.astype(BF16)) + cnt_out[0:1, :]
    mi = jnp.zeros((tm, LANES), F32)
    mf = jnp.zeros((tm, LANES), F32)
    for k, hit in enumerate(hits):
        e_k = jnp.sum(jnp.where(hit, lanef, 0.0), axis=-1, keepdims=True)
        r_k = jnp.sum(jnp.where(hit, before, 0.0), axis=-1, keepdims=True)
        mi = jnp.where(lane == k, e_k, jnp.where(lane == TOP_K + k, r_k, mi))
        mf = jnp.where(lane == k, ex[k] / den, mf)
    mi_out[...] = mi[:, 0:2 * TOP_K].astype(I32)
    mf_out[...] = mf
    cnt_out[...] = cnt_out[...] + jnp.sum(cnt, axis=0, keepdims=True)


DISPATCH_TILE = 512
DMA_UNROLL = 8


def _dispatch_body(dest_ref, last_ref, h_ref, o_hbm, zero_ref, sem, zsem):
    @pl.when(pl.program_id(0) == 0)
    def _():
        zero_ref[...] = jnp.zeros(zero_ref.shape, zero_ref.dtype)

        def clear(e):
            start = pl.multiple_of(last_ref[e] * ROW_TILES, MOE_CHUNK * ROW_TILES)
            return pltpu.make_async_copy(zero_ref, o_hbm.at[pl.ds(start, MOE_CHUNK * ROW_TILES)], zsem)

        for e in range(N_EXPERTS):
            @pl.when(last_ref[e] >= 0)
            def _():
                clear(e).start()
        for e in range(N_EXPERTS):
            @pl.when(last_ref[e] >= 0)
            def _():
                clear(e).wait()

    def row_copy(r, d):
        return pltpu.make_async_copy(h_ref.at[pl.ds(pl.multiple_of(r * ROW_TILES, ROW_TILES), ROW_TILES)],
                                     o_hbm.at[pl.ds(pl.multiple_of(d * ROW_TILES, ROW_TILES), ROW_TILES)], sem)

    def start(r, c):
        for k in range(TOP_K):
            row_copy(r, dest_ref[0, 0, r * TOP_K + k]).start(priority=k % 2)
        return c

    def wait(r, c):
        for k in range(TOP_K):
            row_copy(0, 0).wait()
        return c

    lax.fori_loop(0, DISPATCH_TILE, start, 0, unroll=DMA_UNROLL)
    lax.fori_loop(0, DISPATCH_TILE, wait, 0, unroll=DMA_UNROLL)


def _dispatch(h2, dest, last_chunk, n_rows):
    T = h2.shape[0] // ROW_TILES
    td = DISPATCH_TILE
    dest3 = dest.reshape(T // td, 1, td * TOP_K)
    return pl.pallas_call(
        _dispatch_body,
        grid=(T // td,),
        in_specs=[pl.BlockSpec((1, 1, td * TOP_K), lambda i: (i, 0, 0), memory_space=pltpu.SMEM),
                  pl.BlockSpec(memory_space=pltpu.SMEM),
                  pl.BlockSpec((td * ROW_TILES, LANES), lambda i: (i, 0))],
        out_specs=pl.BlockSpec(memory_space=pl.ANY),
        out_shape=jax.ShapeDtypeStruct((n_rows * ROW_TILES, LANES), h2.dtype),
        scratch_shapes=[pltpu.VMEM((MOE_CHUNK * ROW_TILES, LANES), h2.dtype), pltpu.SemaphoreType.DMA(()),
                        pltpu.SemaphoreType.DMA(())],
        compiler_params=_cparams(1),
        name="dispatch",
    )(dest3, last_chunk, h2)


def _expert_body(ce_ref, nu_ref, x_ref, wgu_ref, bgu_ref, wdn_ref, bdn_ref, o_ref, wgu_bf, wdn_bf):
    c = pl.program_id(0)
    dff = wdn_ref.shape[1]

    @pl.when((c == 0) | (ce_ref[c] != ce_ref[jnp.maximum(c - 1, 0)]))
    def _():
        wgu_bf[...] = wgu_ref[0].astype(BF16)
        wdn_bf[...] = wdn_ref[0].astype(BF16)

    @pl.when(c < nu_ref[0])
    def _():
        x = _load_row_tiles(x_ref, MOE_CHUNK)
        gu = _dot(x.astype(BF16), wgu_bf[...]) + bgu_ref[0]
        g = jnp.minimum(gu[:, 0:dff], SWIGLU_LIMIT)
        u = jnp.clip(gu[:, dff:2 * dff], -SWIGLU_LIMIT, SWIGLU_LIMIT)
        act = (u + 1.0) * (g * jax.nn.sigmoid(SWIGLU_ALPHA * g))
        _store_row_tiles(o_ref, _dot(act.astype(BF16), wdn_bf[...]) + bdn_ref[0])

    @pl.when(c >= nu_ref[0])
    def _():
        o_ref[...] = jnp.zeros(o_ref.shape, F32)


def _experts(hperm, chunk_e, n_used, w_gu, b_gu, w_dn, b_dn):
    E, D, F2 = w_gu.shape
    assert D == ROW_TILES * LANES
    P = hperm.shape[0] // ROW_TILES
    dff = F2 // 2
    n_chunks = P // MOE_CHUNK
    chunk = (MOE_CHUNK * ROW_TILES, LANES)
    grid_spec = pltpu.PrefetchScalarGridSpec(
        num_scalar_prefetch=2,
        grid=(n_chunks,),
        in_specs=[pl.BlockSpec(chunk, lambda c, ce, nu: (jnp.minimum(c, nu[0] - 1), 0)),
                  pl.BlockSpec((1, D, F2), lambda c, ce, nu: (ce[c], 0, 0)),
                  pl.BlockSpec((1, 1, F2), lambda c, ce, nu: (ce[c], 0, 0)),
                  pl.BlockSpec((1, dff, D), lambda c, ce, nu: (ce[c], 0, 0)),
                  pl.BlockSpec((1, 1, D), lambda c, ce, nu: (ce[c], 0, 0))],
        out_specs=pl.BlockSpec(chunk, lambda c, ce, nu: (c, 0)),
        scratch_shapes=[pltpu.VMEM((D, F2), BF16), pltpu.VMEM((dff, D), BF16)],
    )
    return pl.pallas_call(
        _expert_body,
        grid_spec=grid_spec,
        out_shape=jax.ShapeDtypeStruct(hperm.shape, F32),
        compiler_params=_cparams(1),
        name="experts",
    )(chunk_e, n_used, hperm, w_gu, b_gu.reshape(E, 1, F2), w_dn, b_dn.reshape(E, 1, D))


COMBINE_TILE = 256


def _combine_body(dest_ref, next_ref, x_ref, w_ref, y_hbm, o_ref, buf_ref, sems):
    i = pl.program_id(0)
    slot = i % 2

    def row_copy(s, r, k, d):
        return pltpu.make_async_copy(y_hbm.at[pl.ds(pl.multiple_of(d * ROW_TILES, ROW_TILES), ROW_TILES)],
                                     buf_ref.at[s, k, pl.ds(pl.multiple_of(r * ROW_TILES, ROW_TILES), ROW_TILES)],
                                     sems.at[s])

    def fetch(idx_ref, s):
        def start(r, c):
            for k in range(TOP_K):
                row_copy(s, r, k, idx_ref[0, 0, r * TOP_K + k]).start(priority=k % 2)
            return c

        lax.fori_loop(0, COMBINE_TILE, start, 0, unroll=DMA_UNROLL)

    @pl.when(i == 0)
    def _():
        fetch(dest_ref, slot)

    @pl.when(i + 1 < pl.num_programs(0))
    def _():
        fetch(next_ref, 1 - slot)

    def wait(r, c):
        for k in range(TOP_K):
            row_copy(slot, 0, 0, 0).wait()
        return c

    lax.fori_loop(0, COMBINE_TILE, wait, 0, unroll=DMA_UNROLL)
    gate = [jnp.broadcast_to(w_ref[:, k:k + 1], (COMBINE_TILE, LANES)) for k in range(TOP_K)]
    for s in range(ROW_TILES):
        sl = slice(s * LANES, (s + 1) * LANES)
        out = x_ref[:, sl]
        for k in range(TOP_K):
            out = out + gate[k] * buf_ref[slot, k, pl.ds(s, COMBINE_TILE, stride=ROW_TILES), :]
        o_ref[:, sl] = out


def _combine(x1, gate_w, dest, ys):
    T, D = x1.shape
    tc = COMBINE_TILE
    n = T // tc
    dest3 = dest.reshape(n, 1, tc * TOP_K)
    row = lambda w: pl.BlockSpec((tc, w), lambda i: (i, 0))
    idx = lambda f: pl.BlockSpec((1, 1, tc * TOP_K), f, memory_space=pltpu.SMEM)
    return pl.pallas_call(
        _combine_body,
        grid=(n,),
        in_specs=[idx(lambda i: (i, 0, 0)), idx(lambda i: (jnp.minimum(i + 1, n - 1), 0, 0)),
                  row(D), row(LANES), pl.BlockSpec(memory_space=pl.ANY)],
        out_specs=row(D),
        out_shape=jax.ShapeDtypeStruct((T, D), F32),
        scratch_shapes=[pltpu.VMEM((2, TOP_K, tc * ROW_TILES, LANES), F32), pltpu.SemaphoreType.DMA((2,))],
        compiler_params=_cparams(1),
        name="combine",
    )(dest3, dest3, x1, gate_w, ys)


def _mixer(x2, B, S, g_norm1, w_in, g_q, g_kc, g_ks, g_kw, pe_k, ck_w1, ck_b1, ck_w2, ck_b2,
           pe_v, cv_w1, cv_b1, cv_w2, cv_b2, conv_w, w_pa, w_pb, w_o, g_norm2, w_r, b_r):
    T, D = x2.shape
    G, H, hd = N_KV_HEADS, N_HEADS, HEAD_DIM
    qa, hk, hv, kas, kaw, vst, vwt, gates, cbv, gab = _inproj(x2, g_norm1, w_in, g_q, g_ks, g_kw, S)
    nh = S // CMP_STRIDE
    kca = _compress(hk.reshape(B, nh, -1), pe_k, ck_w1, ck_b1, ck_w2, ck_b2, g_kc, True)
    vct = _compress(hv.reshape(B, nh, -1), pe_v, cv_w1, cv_b1, cv_w2, cv_b2, jnp.ones((hd,), F32), False)
    gat = gates[:, :3 * H].reshape(B, S, G, 3 * HEADS_PER_GROUP).transpose(0, 2, 3, 1)
    gat = jnp.pad(gat, ((0, 0), (0, 0), (0, 16 - 3 * HEADS_PER_GROUP), (0, 0)))
    o = _attention(qa.reshape(B, S, -1), gat, kca, vct, kas.reshape(B, S, -1), vst, kaw.reshape(B, S, -1),
                   vwt, B, S)
    return _mixer_out(x2, o.reshape(T, H * hd), cbv, gab, conv_w, w_pa, w_pb, w_o, S, g_norm2, w_r, b_r)


def _moe(x1, h2, mi, mf, cnt, w_gu, b_gu, w_dn, b_dn):
    T, D = x1.shape
    top_e = mi[:, 0:TOP_K]
    rank = mi[:, TOP_K:2 * TOP_K]
    counts = cnt[0, :N_EXPERTS].astype(I32)
    padded = (counts + MOE_CHUNK - 1) // MOE_CHUNK * MOE_CHUNK
    pend = jnp.cumsum(padded)
    poffs = pend - padded
    dest = (poffs[top_e] + rank).reshape(-1)
    n_chunks = (T * TOP_K + MOE_CHUNK - 1) // MOE_CHUNK + N_EXPERTS
    chunk_start = jnp.arange(n_chunks, dtype=I32) * MOE_CHUNK
    chunk_e = jnp.minimum(jnp.sum((pend[None, :] <= chunk_start[:, None]).astype(I32), axis=1), N_EXPERTS - 1)
    n_used = (pend[-1:] // MOE_CHUNK).astype(I32)
    last_chunk = jnp.where(padded > 0, pend - MOE_CHUNK, -1).astype(I32)
    hperm = _dispatch(h2, dest, last_chunk, n_chunks * MOE_CHUNK)
    ys = _experts(hperm, chunk_e, n_used, w_gu, b_gu, w_dn, b_dn)
    return _combine(x1, mf, dest, ys)


def kernel(x, g_norm1, w_in, g_q, g_kc, g_ks, g_kw, pe_k, ck_w1, ck_b1, ck_w2, ck_b2, pe_v, cv_w1, cv_b1,
           cv_w2, cv_b2, conv_w, w_pa, w_pb, w_o, g_norm2, w_r, b_r, w_gu, b_gu, w_dn, b_dn):
    B, S, D = x.shape
    x2 = x.reshape(B * S, D)
    for l in range(g_norm1.shape[0]):
        routed = _mixer(x2, B, S, g_norm1[l], w_in[l], g_q[l], g_kc[l], g_ks[l], g_kw[l], pe_k[l], ck_w1[l],
                        ck_b1[l], ck_w2[l], ck_b2[l], pe_v[l], cv_w1[l], cv_b1[l], cv_w2[l], cv_b2[l],
                        conv_w[l], w_pa[l], w_pb[l], w_o[l], g_norm2[l], w_r[l], b_r[l])
        x2 = _moe(*routed, w_gu[l], b_gu[l], w_dn[l], b_dn[l])
    return x2.reshape(B, S, D)
```
